```python
import math, functools
import jax, jax.numpy as jnp
from jax import lax
import numpy as np

D_MODEL = 1024
BATCH = 4
SEQ = 8192
DEPTH = 4

N_MIXERS = 3
MEM_LEN = 256
BLOCK = 128
ROPE_THETA = 10000.0
NEG = -1e30
LN_EPS = 1e-5
RMS_EPS = 1e-6

A_HEADS = 16
A_KV_HEADS = 4
A_HEAD_DIM = 64
A_WINDOW = 128

LRU_WIDTH = D_MODEL
LRU_BLOCKS = 4
LRU_BLOCK_W = LRU_WIDTH // LRU_BLOCKS
LRU_CONV = 4
LRU_C = 8.0

C_HEADS = 8
C_NOPE = 128
C_ROPE = 64
C_V = 128
C_Q_RANK = 384
C_KV_RANK = 256

X_HEADS = 4
X_HEAD_DIM = D_MODEL // X_HEADS

D_FF = 2816
FFN_CONV = 3

ALPHA = (2.0 * DEPTH) ** 0.25
BETA = (8.0 * DEPTH) ** -0.25

N_A = (DEPTH + 2) // 3
N_B = (DEPTH + 1) // 3
N_C = DEPTH // 3

kernel_name = "interleaved_swa_rglru_mla_deepnorm_trunk"


def layer_norm(x, g, b):
    xf = x.astype(jnp.float32)
    mu = jnp.mean(xf, axis=-1, keepdims=True)
    var = jnp.mean(jnp.square(xf - mu), axis=-1, keepdims=True)
    y = (xf - mu) * lax.rsqrt(var + LN_EPS) * g.astype(jnp.float32) + b.astype(jnp.float32)
    return y.astype(x.dtype)


def rms_norm(x, g):
    xf = x.astype(jnp.float32)
    y = xf * lax.rsqrt(jnp.mean(jnp.square(xf), axis=-1, keepdims=True) + RMS_EPS)
    return (y * g.astype(jnp.float32)).astype(x.dtype)


def rope_tables(seq, dim):
    inv = 1.0 / (ROPE_THETA ** (jnp.arange(0, dim, 2, dtype=jnp.float32) / dim))
    ang = jnp.arange(seq, dtype=jnp.float32)[:, None] * inv[None, :]
    return jnp.cos(ang), jnp.sin(ang)


def apply_rope(x, cos, sin):
    c = cos[None, :, None, :].astype(x.dtype)
    s = sin[None, :, None, :].astype(x.dtype)
    x1, x2 = jnp.split(x, 2, axis=-1)
    return jnp.concatenate([x1 * c - x2 * s, x2 * c + x1 * s], axis=-1)


def causal_depthwise_conv(x, w, b):
    k_width = w.shape[0]
    s = x.shape[1]
    xp = jnp.pad(x, ((0, 0), (k_width - 1, 0), (0, 0)))
    y = xp[:, 0:s] * w[0]
    for k in range(1, k_width):
        y = y + xp[:, k:k + s] * w[k]
    return y + b


def swa_sink_attention(x, w_qkv, sinks, w_o, cos, sin):
    bsz, s, _ = x.shape
    grp = A_HEADS // A_KV_HEADS
    nb = s // BLOCK
    qkv = x @ w_qkv
    q, k, v = jnp.split(qkv, [A_HEADS * A_HEAD_DIM, (A_HEADS + A_KV_HEADS) * A_HEAD_DIM], axis=-1)
    q = apply_rope(q.reshape(bsz, s, A_HEADS, A_HEAD_DIM), cos, sin)
    k = apply_rope(k.reshape(bsz, s, A_KV_HEADS, A_HEAD_DIM), cos, sin)
    v = v.reshape(bsz, s, A_KV_HEADS, A_HEAD_DIM)
    qb = q.reshape(bsz, nb, BLOCK, A_KV_HEADS, grp, A_HEAD_DIM)

    def with_prev(t):
        tb = t.reshape(bsz, nb, BLOCK, A_KV_HEADS, A_HEAD_DIM)
        prev = jnp.pad(tb[:, :-1], ((0, 0), (1, 0), (0, 0), (0, 0), (0, 0)))
        return jnp.concatenate([prev, tb], axis=2)

    kb, vb = with_prev(k), with_prev(v)
    scores = jnp.einsum('bnqhgd,bnkhd->bnhgqk', qb, kb).astype(jnp.float32) * (A_HEAD_DIM ** -0.5)
    qi = jnp.arange(BLOCK)[:, None]
    kj = jnp.arange(2 * BLOCK)[None, :]
    dist = qi + BLOCK - kj
    band = (dist >= 0) & (dist < A_WINDOW)
    real_key = (jnp.arange(nb)[:, None, None] > 0) | (kj >= BLOCK)[None]
    valid = band[None] & real_key
    scores = jnp.where(valid[None, :, None, None], scores, NEG)
    sink = sinks.astype(jnp.float32).reshape(A_KV_HEADS, grp)[None, None, :, :, None, None]
    sink = jnp.broadcast_to(sink, scores.shape[:-1] + (1,))
    probs = jax.nn.softmax(jnp.concatenate([scores, sink], axis=-1), axis=-1)[..., :-1]
    out = jnp.einsum('bnhgqk,bnkhd->bnqhgd', probs.astype(v.dtype), vb)
    return out.reshape(bsz, s, A_HEADS * A_HEAD_DIM) @ w_o


def rglru_block(x, w_in, conv_w, conv_b, w_rgate, b_rgate, w_igate, b_igate, lam, w_o):
    bsz, s, _ = x.shape
    gate, u = jnp.split(x @ w_in, 2, axis=-1)
    u = causal_depthwise_conv(u, conv_w, conv_b)
    ub = u.reshape(bsz, s, LRU_BLOCKS, LRU_BLOCK_W)
    r = jax.nn.sigmoid(jnp.einsum('bshi,hij->bshj', ub, w_rgate).reshape(bsz, s, LRU_WIDTH) + b_rgate)
    i = jax.nn.sigmoid(jnp.einsum('bshi,hij->bshj', ub, w_igate).reshape(bsz, s, LRU_WIDTH) + b_igate)
    log_a = -LRU_C * r.astype(jnp.float32) * jax.nn.softplus(-lam.astype(jnp.float32))
    a = jnp.exp(log_a)
    b_in = jnp.sqrt(-jnp.expm1(2.0 * log_a)) * (i * u).astype(jnp.float32)

    def combine(c1, c2):
        a1, b1 = c1
        a2, b2 = c2
        return a1 * a2, a2 * b1 + b2

    _, h = lax.associative_scan(combine, (a, b_in), axis=1)
    y = h.astype(x.dtype) * jax.nn.gelu(gate)
    return y @ w_o


def mla_attention(x, w_down, q_norm, kv_norm, w_uq, w_ukv, w_o, cos_r, sin_r):
    bsz, s, _ = x.shape
    nb = s // BLOCK
    c = x @ w_down
    cq, ckv, k_rope = jnp.split(c, [C_Q_RANK, C_Q_RANK + C_KV_RANK], axis=-1)
    cq = rms_norm(cq, q_norm)
    ckv = rms_norm(ckv, kv_norm)
    q = (cq @ w_uq).reshape(bsz, s, C_HEADS, C_NOPE + C_ROPE)
    q_nope, q_rope = jnp.split(q, [C_NOPE], axis=-1)
    q_rope = apply_rope(q_rope, cos_r, sin_r)
    k_rope = apply_rope(k_rope[:, :, None, :], cos_r, sin_r)[:, :, 0]
    kv = (ckv @ w_ukv).reshape(bsz, s, C_HEADS, C_NOPE + C_V)
    k_nope, v = jnp.split(kv, [C_NOPE], axis=-1)
    scale = (C_NOPE + C_ROPE) ** -0.5
    qn = q_nope.reshape(bsz, nb, BLOCK, C_HEADS, C_NOPE).transpose(1, 0, 2, 3, 4)
    qr = q_rope.reshape(bsz, nb, BLOCK, C_HEADS, C_ROPE).transpose(1, 0, 2, 3, 4)
    key_pos = jnp.arange(s)

    def attend(args):
        n, qn_b, qr_b = args
        sc = (jnp.einsum('bqhd,bkhd->bhqk', qn_b, k_nope)
              + jnp.einsum('bqhd,bkd->bhqk', qr_b, k_rope)).astype(jnp.float32) * scale
        q_pos = n * BLOCK + jnp.arange(BLOCK)
        sc = jnp.where(key_pos[None, :] <= q_pos[:, None], sc, NEG)
        p = jax.nn.softmax(sc, axis=-1)
        return jnp.einsum('bhqk,bkhd->bqhd', p.astype(v.dtype), v)

    out = lax.map(attend, (jnp.arange(nb), qn, qr))
    out = out.transpose(1, 0, 2, 3, 4).reshape(bsz, s, C_HEADS * C_V)
    return out @ w_o


def memory_cross_attention(x, mem_k, mem_v, w_q, w_o):
    bsz, s, _ = x.shape
    q = (x @ w_q).reshape(bsz, s, X_HEADS, X_HEAD_DIM)
    sc = jnp.einsum('bshd,bmhd->bhsm', q, mem_k).astype(jnp.float32) * (X_HEAD_DIM ** -0.5)
    p = jax.nn.softmax(sc, axis=-1)
    o = jnp.einsum('bhsm,bmhd->bshd', p.astype(mem_v.dtype), mem_v).reshape(bsz, s, D_MODEL)
    return o @ w_o


def conv_glu_ffn(x, w_up, conv_w, conv_b, w_down):
    h = causal_depthwise_conv(x @ w_up, conv_w, conv_b)
    g, u = jnp.split(h, 2, axis=-1)
    return (jax.nn.silu(g) * u) @ w_down


def setup_inputs(seed: int = 0) -> dict:
    key = jax.random.key(seed)
    ks = iter(jax.random.split(key, 40))

    def dense(shape, fan_in, scale=1.0):
        return jax.random.normal(next(ks), shape, jnp.float32) * (scale * fan_in ** -0.5)

    def small(shape, scale=0.01):
        return jax.random.normal(next(ks), shape, jnp.float32) * scale

    def gain(shape):
        return 1.0 + small(shape)

    qkv_w = (A_HEADS + 2 * A_KV_HEADS) * A_HEAD_DIM
    a0 = jax.random.uniform(next(ks), (N_B, LRU_WIDTH), jnp.float32, 0.9, 0.999) ** (1.0 / LRU_C)
    return {
        "x": jax.random.normal(next(ks), (BATCH, SEQ, D_MODEL), jnp.float32),
        "mem": jax.random.normal(next(ks), (BATCH, MEM_LEN, D_MODEL), jnp.float32),
        "a_w_qkv": dense((N_A, D_MODEL, qkv_w), D_MODEL),
        "a_sinks": small((N_A, A_HEADS), 1.0),
        "a_w_o": dense((N_A, A_HEADS * A_HEAD_DIM, D_MODEL), A_HEADS * A_HEAD_DIM, BETA),
        "b_w_in": dense((N_B, D_MODEL, 2 * LRU_WIDTH), D_MODEL),
        "b_conv_w": dense((N_B, LRU_CONV, LRU_WIDTH), LRU_CONV),
        "b_conv_b": small((N_B, LRU_WIDTH)),
        "b_w_rgate": dense((N_B, LRU_BLOCKS, LRU_BLOCK_W, LRU_BLOCK_W), LRU_BLOCK_W),
        "b_b_rgate": small((N_B, LRU_WIDTH)),
        "b_w_igate": dense((N_B, LRU_BLOCKS, LRU_BLOCK_W, LRU_BLOCK_W), LRU_BLOCK_W),
        "b_b_igate": small((N_B, LRU_WIDTH)),
        "b_lambda": jnp.log(a0) - jnp.log1p(-a0),
        "b_w_o": dense((N_B, LRU_WIDTH, D_MODEL), LRU_WIDTH, BETA),
        "c_w_down": dense((N_C, D_MODEL, C_Q_RANK + C_KV_RANK + C_ROPE), D_MODEL),
        "c_q_norm": gain((N_C, C_Q_RANK)),
        "c_kv_norm": gain((N_C, C_KV_RANK)),
        "c_w_uq": dense((N_C, C_Q_RANK, C_HEADS * (C_NOPE + C_ROPE)), C_Q_RANK),
        "c_w_ukv": dense((N_C, C_KV_RANK, C_HEADS * (C_NOPE + C_V)), C_KV_RANK),
        "c_w_o": dense((N_C, C_HEADS * C_V, D_MODEL), C_HEADS * C_V, BETA),
        "mem_w_kv": dense((D_MODEL, 2 * D_MODEL), D_MODEL),
        "x_w_q": dense((DEPTH, D_MODEL, D_MODEL), D_MODEL),
        "x_w_o": dense((DEPTH, D_MODEL, D_MODEL), D_MODEL, BETA),
        "f_w_up": dense((DEPTH, D_MODEL, 2 * D_FF), D_MODEL),
        "f_conv_w": dense((DEPTH, FFN_CONV, 2 * D_FF), FFN_CONV),
        "f_conv_b": small((DEPTH, 2 * D_FF)),
        "f_w_down": dense((DEPTH, D_FF, D_MODEL), D_FF, BETA),
        "ln_g": gain((DEPTH, 3, D_MODEL)),
        "ln_b": small((DEPTH, 3, D_MODEL)),
    }


def reference(x, mem, a_w_qkv, a_sinks, a_w_o, b_w_in, b_conv_w, b_conv_b, b_w_rgate, b_b_rgate,
              b_w_igate, b_b_igate, b_lambda, b_w_o, c_w_down, c_q_norm, c_kv_norm, c_w_uq, c_w_ukv,
              c_w_o, mem_w_kv, x_w_q, x_w_o, f_w_up, f_conv_w, f_conv_b, f_w_down, ln_g, ln_b):
    bsz, s, _ = x.shape
    cos_a, sin_a = rope_tables(s, A_HEAD_DIM)
    cos_c, sin_c = rope_tables(s, C_ROPE)
    mem_k, mem_v = jnp.split(mem @ mem_w_kv, 2, axis=-1)
    mem_k = mem_k.reshape(bsz, MEM_LEN, X_HEADS, X_HEAD_DIM)
    mem_v = mem_v.reshape(bsz, MEM_LEN, X_HEADS, X_HEAD_DIM)
    for i in range(DEPTH):
        kind, j = i % N_MIXERS, i // N_MIXERS
        if kind == 0:
            y = swa_sink_attention(x, a_w_qkv[j], a_sinks[j], a_w_o[j], cos_a, sin_a)
        elif kind == 1:
            y = rglru_block(x, b_w_in[j], b_conv_w[j], b_conv_b[j], b_w_rgate[j], b_b_rgate[j],
                            b_w_igate[j], b_b_igate[j], b_lambda[j], b_w_o[j])
        else:
            y = mla_attention(x, c_w_down[j], c_q_norm[j], c_kv_norm[j], c_w_uq[j], c_w_ukv[j],
                              c_w_o[j], cos_c, sin_c)
        x = layer_norm(ALPHA * x + y, ln_g[i, 0], ln_b[i, 0])
        x = layer_norm(ALPHA * x + memory_cross_attention(x, mem_k, mem_v, x_w_q[i], x_w_o[i]),
                       ln_g[i, 1], ln_b[i, 1])
        x = layer_norm(ALPHA * x + conv_glu_ffn(x, f_w_up[i], f_conv_w[i], f_conv_b[i], f_w_down[i]),
                       ln_g[i, 2], ln_b[i, 2])
    return x
```

```python
import functools
import math

import jax
import jax.numpy as jnp
import numpy as np
from jax import lax
from jax.experimental import pallas as pl
from jax.experimental.pallas import tpu as pltpu

D_MODEL = 1024
DEPTH = 4
N_MIXERS = 3
MEM_LEN = 256
BLOCK = 128
ROPE_THETA = 10000.0
NEG = -1e30
LN_EPS = 1e-5
RMS_EPS = 1e-6

A_HEADS = 16
A_KV_HEADS = 4
A_HEAD_DIM = 64

LRU_WIDTH = D_MODEL
LRU_BLOCKS = 4
LRU_BLOCK_W = LRU_WIDTH // LRU_BLOCKS
LRU_CONV = 4
LRU_C = 8.0

C_HEADS = 8
C_NOPE = 128
C_ROPE = 64
C_V = 128
C_Q_RANK = 384
C_KV_RANK = 256
C_QK_PAD = 256

X_HEADS = 4
X_HEAD_DIM = D_MODEL // X_HEADS

D_FF = 2816
FFN_CONV = 3
FFN_CHUNK = 256
FFN_NCHUNK = D_FF // FFN_CHUNK

ALPHA = (2.0 * DEPTH) ** 0.25

LANES = 128
SUBLANES = 8
VMEM_LIMIT = 56 * 1024 * 1024

BF16 = jnp.bfloat16
F32 = jnp.float32

NT_DIMS = (((1,), (1,)), ((), ()))


def _dot(a, b):
    return jnp.dot(a, b, preferred_element_type=F32)


def _dot_nt(a, b):
    return lax.dot_general(a, b, NT_DIMS, preferred_element_type=F32)


def _layer_norm(z, g, b):
    mu = jnp.mean(z, axis=-1, keepdims=True)
    zc = z - mu
    var = jnp.mean(zc * zc, axis=-1, keepdims=True)
    return zc * lax.rsqrt(var + LN_EPS) * g + b


def _sigmoid(z):
    return 1.0 / (1.0 + jnp.exp(-z))


def _resident(shape):
    nd = len(shape)
    return pl.BlockSpec(shape, lambda *_: (0,) * nd, pipeline_mode=pl.Buffered(1))


def _params(sem):
    return pltpu.CompilerParams(dimension_semantics=sem, vmem_limit_bytes=VMEM_LIMIT)


def _matmul_kernel(a_ref, w_ref, o_ref):
    o_ref[...] = _dot(a_ref[...].astype(BF16), w_ref[...]).astype(o_ref.dtype)


def _matmul(a, w, tm, out_dtype):
    t, k = a.shape
    n = w.shape[1]
    return pl.pallas_call(
        _matmul_kernel,
        grid=(t // tm,),
        in_specs=[pl.BlockSpec((tm, k), lambda i: (i, 0)), _resident((k, n))],
        out_specs=pl.BlockSpec((tm, n), lambda i: (i, 0)),
        out_shape=jax.ShapeDtypeStruct((t, n), out_dtype),
        compiler_params=_params(("parallel",)),
        name="matmul",
    )(a, w)


def _proj_ln_kernel(a_ref, r_ref, w_ref, g_ref, b_ref, o_ref):
    y = _dot(a_ref[...], w_ref[...])
    o_ref[...] = _layer_norm(ALPHA * r_ref[...] + y, g_ref[...], b_ref[...])


def _proj_ln(a, resid, w, g, b, tm=512):
    t, k = a.shape
    d = w.shape[1]
    return pl.pallas_call(
        _proj_ln_kernel,
        grid=(t // tm,),
        in_specs=[pl.BlockSpec((tm, k), lambda i: (i, 0)),
                  pl.BlockSpec((tm, d), lambda i: (i, 0)),
                  _resident((k, d)), _resident((1, d)), _resident((1, d))],
        out_specs=pl.BlockSpec((tm, d), lambda i: (i, 0)),
        out_shape=jax.ShapeDtypeStruct((t, d), F32),
        compiler_params=_params(("parallel",)),
        name="proj_ln",
    )(a, resid, w, g, b)


def _rope_slab(slab, cos, sin_signed, first_half):
    rot = jnp.where(first_half, pltpu.roll(slab, LANES - 32, 1), pltpu.roll(slab, 32, 1))
    return slab * cos + rot * sin_signed


def _rope_tables(seq):
    inv = 1.0 / (ROPE_THETA ** (jnp.arange(0, A_HEAD_DIM, 2, dtype=F32) / A_HEAD_DIM))
    ang = jnp.arange(seq, dtype=F32)[:, None] * inv[None, :]
    cos, sin = jnp.cos(ang), jnp.sin(ang)
    return (jnp.concatenate([cos, cos, cos, cos], axis=-1),
            jnp.concatenate([-sin, sin, -sin, sin], axis=-1))


SWA_TM = 256
SWA_NBLK = SWA_TM // BLOCK


def _swa_kernel(sinks_ref, x_ref, cos_ref, sin_ref, wqkv_ref, wo_ref, g_ref, b_ref, o_ref,
                kext_scr, vext_scr, q_scr, o_scr):
    i = pl.program_id(1)
    tm = SWA_TM

    @pl.when(i == 0)
    def _():
        kext_scr[:, 0:BLOCK, :] = jnp.zeros((2 * A_KV_HEADS, BLOCK, LANES), BF16)
        vext_scr[:, 0:BLOCK, :] = jnp.zeros((2 * A_KV_HEADS, BLOCK, LANES), BF16)

    @pl.when(i > 0)
    def _():
        kext_scr[:, 0:BLOCK, :] = kext_scr[:, tm:tm + BLOCK, :]
        vext_scr[:, 0:BLOCK, :] = vext_scr[:, tm:tm + BLOCK, :]

    x = x_ref[0]
    qkv = _dot(x.astype(BF16), wqkv_ref[...])
    cos = cos_ref[...]
    sin = sin_ref[...]
    lane = lax.broadcasted_iota(jnp.int32, (tm, LANES), 1)
    first_half = (lane & 32) == 0
    low = lane < 64

    nq = A_HEADS * A_HEAD_DIM
    nkv = A_KV_HEADS * A_HEAD_DIM
    for j in range(nq // LANES):
        slab = _rope_slab(qkv[:, j * LANES:(j + 1) * LANES], cos, sin, first_half)
        q_scr[:, j * LANES:(j + 1) * LANES] = (slab * (A_HEAD_DIM ** -0.5)).astype(BF16)

    for m in range(nkv // LANES):
        kslab = _rope_slab(qkv[:, nq + m * LANES:nq + (m + 1) * LANES], cos, sin, first_half)
        vslab = qkv[:, nq + nkv + m * LANES:nq + nkv + (m + 1) * LANES]
        for slab, ext in ((kslab, kext_scr), (vslab, vext_scr)):
            swapped = pltpu.roll(slab, 64, 1)
            h0, h1 = 2 * m, 2 * m + 1
            ext[2 * h0, BLOCK:BLOCK + tm, :] = jnp.where(low, slab, 0.0).astype(BF16)
            ext[2 * h0 + 1, BLOCK:BLOCK + tm, :] = jnp.where(low, 0.0, swapped).astype(BF16)
            ext[2 * h1, BLOCK:BLOCK + tm, :] = jnp.where(low, swapped, 0.0).astype(BF16)
            ext[2 * h1 + 1, BLOCK:BLOCK + tm, :] = jnp.where(low, 0.0, slab).astype(BF16)

    qi = lax.broadcasted_iota(jnp.int32, (2 * BLOCK, 2 * BLOCK), 0) & (BLOCK - 1)
    kj = lax.broadcasted_iota(jnp.int32, (2 * BLOCK, 2 * BLOCK), 1)
    dist = qi + BLOCK - kj
    band = (dist >= 0) & (dist < BLOCK)
    band_first = band & ((kj >= BLOCK) | (i > 0))
    top = lax.broadcasted_iota(jnp.int32, (2 * BLOCK, 1), 0) < BLOCK

    for n in range(SWA_NBLK):
        mask = band_first if n == 0 else band
        r0 = n * BLOCK
        for h in range(A_KV_HEADS):
            qs = jnp.concatenate([q_scr[r0:r0 + BLOCK, (2 * h) * LANES:(2 * h + 1) * LANES],
                                  q_scr[r0:r0 + BLOCK, (2 * h + 1) * LANES:(2 * h + 2) * LANES]], axis=0)
            acc = None
            for half in range(2):
                kk = kext_scr[2 * h + half, r0:r0 + 2 * BLOCK, :]
                vv = vext_scr[2 * h + half, r0:r0 + 2 * BLOCK, :]
                s = jnp.where(mask, _dot_nt(qs, kk), NEG)
                sink = jnp.where(top, sinks_ref[4 * h + half], sinks_ref[4 * h + 2 + half])
                mx = jnp.maximum(jnp.max(s, axis=-1, keepdims=True), sink)
                p = jnp.exp(s - mx)
                denom = jnp.sum(p, axis=-1, keepdims=True) + jnp.exp(sink - mx)
                o = _dot(p.astype(BF16), vv) / denom
                acc = o if acc is None else acc + o
            o_scr[r0:r0 + BLOCK, (2 * h) * LANES:(2 * h + 1) * LANES] = acc[:BLOCK].astype(BF16)
            o_scr[r0:r0 + BLOCK, (2 * h + 1) * LANES:(2 * h + 2) * LANES] = acc[BLOCK:].astype(BF16)

    y = _dot(o_scr[...], wo_ref[...])
    o_ref[0] = _layer_norm(ALPHA * x + y, g_ref[...], b_ref[...])


def _swa_layer(x, sinks, cos, sin, wqkv, wo, g, b):
    bsz, s, d = x.shape
    tm = SWA_TM
    nqkv = wqkv.shape[1]
    return pl.pallas_call(
        _swa_kernel,
        grid=(bsz, s // tm),
        in_specs=[pl.BlockSpec(memory_space=pltpu.SMEM),
                  pl.BlockSpec((1, tm, d), lambda bi, i: (bi, i, 0)),
                  pl.BlockSpec((tm, LANES), lambda bi, i: (i, 0)),
                  pl.BlockSpec((tm, LANES), lambda bi, i: (i, 0)),
                  _resident((d, nqkv)), _resident((d, d)), _resident((1, d)), _resident((1, d))],
        out_specs=pl.BlockSpec((1, tm, d), lambda bi, i: (bi, i, 0)),
        out_shape=jax.ShapeDtypeStruct((bsz, s, d), F32),
        scratch_shapes=[pltpu.VMEM((2 * A_KV_HEADS, tm + BLOCK, LANES), BF16),
                        pltpu.VMEM((2 * A_KV_HEADS, tm + BLOCK, LANES), BF16),
                        pltpu.VMEM((tm, d), BF16),
                        pltpu.VMEM((tm, d), BF16)],
        compiler_params=_params(("arbitrary", "arbitrary")),
        name="swa_layer",
    )(sinks, x, cos, sin, wqkv, wo, g, b)


LRU_TM = 256


def _rglru_kernel(x_ref, win_ref, cw_ref, cb_ref, wr_ref, br_ref, wi_ref, bi_ref, lam_ref, wo_ref,
                  g_ref, b_ref, o_ref, uprev_scr, hcar_scr, a_scr, b_scr, h_scr):
    i = pl.program_id(1)
    tm = LRU_TM
    w = LRU_WIDTH

    @pl.when(i == 0)
    def _():
        uprev_scr[...] = jnp.zeros_like(uprev_scr)
        hcar_scr[...] = jnp.zeros_like(hcar_scr)

    x = x_ref[0]
    gu = _dot(x.astype(BF16), win_ref[...])
    gate = gu[:, :w]
    u = gu[:, w:]
    full = jnp.concatenate([uprev_scr[...], u], axis=0)
    uprev_scr[...] = u[tm - SUBLANES:, :]
    cw = cw_ref[...]
    uc = full[SUBLANES - 3:SUBLANES - 3 + tm] * cw[0:1]
    uc = uc + full[SUBLANES - 2:SUBLANES - 2 + tm] * cw[1:2]
    uc = uc + full[SUBLANES - 1:SUBLANES - 1 + tm] * cw[2:3]
    uc = uc + u * cw[3:4]
    uc = uc + cb_ref[...]

    ucb = uc.astype(BF16)
    rs, igs = [], []
    for h in range(LRU_BLOCKS):
        blk = ucb[:, h * LRU_BLOCK_W:(h + 1) * LRU_BLOCK_W]
        rs.append(_dot(blk, wr_ref[h]))
        igs.append(_dot(blk, wi_ref[h]))
    r = _sigmoid(jnp.concatenate(rs, axis=-1) + br_ref[...])
    ig = _sigmoid(jnp.concatenate(igs, axis=-1) + bi_ref[...])
    z = -lam_ref[...]
    softplus = jnp.maximum(z, 0.0) + jnp.log1p(jnp.exp(-jnp.abs(z)))
    log_a = (-LRU_C * r) * softplus
    a_scr[...] = jnp.exp(log_a)
    th = jnp.tanh(log_a)
    b_scr[...] = jnp.sqrt((-2.0 * th) / (1.0 - th)) * (ig * uc)

    row = lax.broadcasted_iota(jnp.int32, (SUBLANES, w), 0)

    def group(gi, hprev):
        r0 = pl.multiple_of(gi * SUBLANES, SUBLANES)
        a = a_scr[pl.ds(r0, SUBLANES), :]
        bb = b_scr[pl.ds(r0, SUBLANES), :]
        for d in (1, 2, 4):
            keep = row >= d
            a_s = jnp.where(keep, pltpu.roll(a, d, 0), 1.0)
            b_s = jnp.where(keep, pltpu.roll(bb, d, 0), 0.0)
            bb = a * b_s + bb
            a = a * a_s
        h = a * hprev + bb
        h_scr[pl.ds(r0, SUBLANES), :] = h
        return jnp.broadcast_to(h[SUBLANES - 1:SUBLANES, :], (SUBLANES, w))

    hlast = lax.fori_loop(0, tm // SUBLANES, group, hcar_scr[...])
    hcar_scr[...] = hlast

    c0 = math.sqrt(2.0 / math.pi)
    gelu = gate * (0.5 * (1.0 + jnp.tanh(c0 * (gate + 0.044715 * (gate * gate * gate)))))
    y = _dot((h_scr[...] * gelu).astype(BF16), wo_ref[...])
    o_ref[0] = _layer_norm(ALPHA * x + y, g_ref[...], b_ref[...])


def _rglru_layer(x, win, cw, cb, wr, br, wi, bi, lam, wo, g, b):
    bsz, s, d = x.shape
    tm = LRU_TM
    w = LRU_WIDTH
    return pl.pallas_call(
        _rglru_kernel,
        grid=(bsz, s // tm),
        in_specs=[pl.BlockSpec((1, tm, d), lambda bi_, i: (bi_, i, 0)),
                  _resident((d, 2 * w)), _resident((LRU_CONV, w)), _resident((1, w)),
                  _resident((LRU_BLOCKS, LRU_BLOCK_W, LRU_BLOCK_W)), _resident((1, w)),
                  _resident((LRU_BLOCKS, LRU_BLOCK_W, LRU_BLOCK_W)), _resident((1, w)),
                  _resident((1, w)), _resident((w, d)), _resident((1, d)), _resident((1, d))],
        out_specs=pl.BlockSpec((1, tm, d), lambda bi_, i: (bi_, i, 0)),
        out_shape=jax.ShapeDtypeStruct((bsz, s, d), F32),
        scratch_shapes=[pltpu.VMEM((SUBLANES, w), F32), pltpu.VMEM((SUBLANES, w), F32),
                        pltpu.VMEM((tm, w), F32), pltpu.VMEM((tm, w), F32), pltpu.VMEM((tm, w), F32)],
        compiler_params=_params(("arbitrary", "arbitrary")),
        name="rglru_layer",
    )(x, win, cw, cb, wr, br, wi, bi, lam, wo, g, b)


MLA_TM = 512
MLA_TQ = 512
MLA_TK = 512
C_DOWN_PAD = 768


def _mla_proj_kernel(x_ref, cos_ref, sin_ref, wd_ref, qn_ref, kvn_ref, wuq_ref, wukv_ref,
                     q_ref, k_ref, v_ref):
    tm = MLA_TM
    xb = x_ref[0].astype(BF16)
    c = _dot(xb, wd_ref[...])
    cq = c[:, :C_Q_RANK]
    ckv = c[:, C_Q_RANK:C_Q_RANK + C_KV_RANK]
    kr_slab = c[:, C_Q_RANK + C_KV_RANK:]
    cq = cq * lax.rsqrt(jnp.mean(cq * cq, axis=-1, keepdims=True) + RMS_EPS) * qn_ref[...]
    ckv = ckv * lax.rsqrt(jnp.mean(ckv * ckv, axis=-1, keepdims=True) + RMS_EPS) * kvn_ref[...]
    q = _dot(cq.astype(BF16), wuq_ref[...])
    kv = _dot(ckv.astype(BF16), wukv_ref[...])

    cos = cos_ref[...]
    sin = sin_ref[...]
    lane = lax.broadcasted_iota(jnp.int32, (tm, LANES), 1)
    first_half = (lane & 32) == 0
    low = lane < 64
    nn = C_HEADS * C_NOPE

    kr = _rope_slab(kr_slab, cos, sin, first_half)
    kr_pair = (kr.astype(BF16), pltpu.roll(kr, 64, 1).astype(BF16))
    for pair in range(C_HEADS // 2):
        qr = _rope_slab(q[:, nn + pair * LANES:nn + (pair + 1) * LANES], cos, sin, first_half)
        for half in range(2):
            h = 2 * pair + half
            qr_h = jnp.where(low, qr, 0.0) if half == 0 else jnp.where(low, 0.0, qr)
            q_ref[0, h, :, 0:C_NOPE] = q[:, h * C_NOPE:(h + 1) * C_NOPE].astype(BF16)
            q_ref[0, h, :, C_NOPE:C_QK_PAD] = qr_h.astype(BF16)
            k_ref[0, h, :, 0:C_NOPE] = kv[:, h * C_NOPE:(h + 1) * C_NOPE].astype(BF16)
            k_ref[0, h, :, C_NOPE:C_QK_PAD] = kr_pair[half]
            v_ref[0, h] = kv[:, nn + h * C_V:nn + (h + 1) * C_V].astype(BF16)


def _mla_proj(x, cos, sin, wd, qn, kvn, wuq, wukv):
    bsz, s, d = x.shape
    tm = MLA_TM
    qk_shape = jax.ShapeDtypeStruct((bsz, C_HEADS, s, C_QK_PAD), BF16)
    v_shape = jax.ShapeDtypeStruct((bsz, C_HEADS, s, C_V), BF16)
    return pl.pallas_call(
        _mla_proj_kernel,
        grid=(bsz, s // tm),
        in_specs=[pl.BlockSpec((1, tm, d), lambda bi, i: (bi, i, 0)),
                  pl.BlockSpec((tm, LANES), lambda bi, i: (i, 0)),
                  pl.BlockSpec((tm, LANES), lambda bi, i: (i, 0)),
                  _resident(wd.shape), _resident(qn.shape), _resident(kvn.shape),
                  _resident(wuq.shape), _resident(wukv.shape)],
        out_specs=[pl.BlockSpec((1, C_HEADS, tm, C_QK_PAD), lambda bi, i: (bi, 0, i, 0)),
                   pl.BlockSpec((1, C_HEADS, tm, C_QK_PAD), lambda bi, i: (bi, 0, i, 0)),
                   pl.BlockSpec((1, C_HEADS, tm, C_V), lambda bi, i: (bi, 0, i, 0))],
        out_shape=[qk_shape, qk_shape, v_shape],
        compiler_params=_params(("parallel", "parallel")),
        name="mla_proj",
    )(x, cos, sin, wd, qn, kvn, wuq, wukv)


def _mla_attn_kernel(q_ref, k_ref, v_ref, o_ref, m_scr, l_scr, acc_scr):
    i = pl.program_id(2)
    tq, tk = MLA_TQ, MLA_TK
    scale = (C_NOPE + C_ROPE) ** -0.5
    q = q_ref[0, 0]

    m_scr[...] = jnp.full_like(m_scr, -jnp.inf)
    l_scr[...] = jnp.zeros_like(l_scr)
    acc_scr[...] = jnp.zeros_like(acc_scr)

    def step(j, diagonal):
        k0 = pl.multiple_of(j * tk, tk)
        kk = k_ref[0, 0, pl.ds(k0, tk), :]
        vv = v_ref[0, 0, pl.ds(k0, tk), :]
        s = _dot_nt(q, kk) * scale
        if diagonal:
            qpos = lax.broadcasted_iota(jnp.int32, (tq, tk), 0)
            kpos = lax.broadcasted_iota(jnp.int32, (tq, tk), 1)
            s = jnp.where(kpos <= qpos, s, NEG)
        m_prev = m_scr[...]
        m_next = jnp.maximum(m_prev, jnp.max(s, axis=-1, keepdims=True))
        p = jnp.exp(s - jnp.tile(m_next, (1, tk // LANES)))
        corr = jnp.exp(m_prev - m_next)
        l_scr[...] = corr * l_scr[...] + jnp.sum(p, axis=-1, keepdims=True)
        acc_scr[...] = corr * acc_scr[...] + _dot(p.astype(BF16), vv)
        m_scr[...] = m_next

    def body(j, carry):
        step(j, False)
        return carry

    lax.fori_loop(0, i, body, 0)
    step(i, True)
    o_ref[0] = (acc_scr[...] / l_scr[...]).astype(BF16)


def _mla_attn(q, k, v):
    bsz, nh, s, _ = q.shape
    tq = MLA_TQ
    return pl.pallas_call(
        _mla_attn_kernel,
        grid=(bsz, nh, s // tq),
        in_specs=[pl.BlockSpec((1, 1, tq, C_QK_PAD), lambda b, h, i: (b, h, i, 0)),
                  pl.BlockSpec((1, 1, s, C_QK_PAD), lambda b, h, i: (b, h, 0, 0)),
                  pl.BlockSpec((1, 1, s, C_V), lambda b, h, i: (b, h, 0, 0))],
        out_specs=pl.BlockSpec((1, tq, C_V), lambda b, h, i: (b, i, h)),
        out_shape=jax.ShapeDtypeStruct((bsz, s, nh * C_V), BF16),
        scratch_shapes=[pltpu.VMEM((tq, LANES), F32), pltpu.VMEM((tq, LANES), F32),
                        pltpu.VMEM((tq, C_V), F32)],
        compiler_params=_params(("parallel", "parallel", "arbitrary")),
        name="mla_attn",
    )(q, k, v)


X_TM = 512


def _xattn_kernel(x_ref, wq_ref, mkv_ref, wo_ref, g_ref, b_ref, o_ref):
    x = x_ref[0]
    xb = x.astype(BF16)
    outs = []
    for h in range(X_HEADS):
        lo, hi = h * X_HEAD_DIM, (h + 1) * X_HEAD_DIM
        q = _dot(xb, wq_ref[:, lo:hi]) * (X_HEAD_DIM ** -0.5)
        s = _dot_nt(q.astype(BF16), mkv_ref[0, :, lo:hi])
        p = jnp.exp(s - jnp.max(s, axis=-1, keepdims=True))
        denom = jnp.sum(p, axis=-1, keepdims=True)
        o = _dot(p.astype(BF16), mkv_ref[0, :, D_MODEL + lo:D_MODEL + hi]) / denom
        outs.append(o.astype(BF16))
    y = _dot(jnp.concatenate(outs, axis=-1), wo_ref[...])
    o_ref[0] = _layer_norm(ALPHA * x + y, g_ref[...], b_ref[...])


def _xattn_layer(x, wq, mkv, wo, g, b):
    bsz, s, d = x.shape
    tm = X_TM
    return pl.pallas_call(
        _xattn_kernel,
        grid=(bsz, s // tm),
        in_specs=[pl.BlockSpec((1, tm, d), lambda bi, i: (bi, i, 0)),
                  _resident((d, d)),
                  pl.BlockSpec((1, MEM_LEN, 2 * d), lambda bi, i: (bi, 0, 0)),
                  _resident((d, d)), _resident((1, d)), _resident((1, d))],
        out_specs=pl.BlockSpec((1, tm, d), lambda bi, i: (bi, i, 0)),
        out_shape=jax.ShapeDtypeStruct((bsz, s, d), F32),
        compiler_params=_params(("parallel", "parallel")),
        name="xattn_layer",
    )(x, wq, mkv, wo, g, b)


FFN_TM = 512


def _ffn_kernel(x_ref, wup_ref, cw_ref, cb_ref, wdn_ref, g_ref, b_ref, o_ref, hprev_scr, act_scr):
    i = pl.program_id(1)
    tm = FFN_TM
    cc = 2 * FFN_CHUNK

    @pl.when(i == 0)
    def _():
        hprev_scr[...] = jnp.zeros_like(hprev_scr)

    x = x_ref[0]
    xb = x.astype(BF16)
    for c in range(FFN_NCHUNK):
        h = _dot(xb, wup_ref[:, c * cc:(c + 1) * cc])
        full = jnp.concatenate([hprev_scr[:, c * cc:(c + 1) * cc], h], axis=0)
        hprev_scr[:, c * cc:(c + 1) * cc] = h[tm - SUBLANES:, :]
        cw = cw_ref[:, c * cc:(c + 1) * cc]
        y = full[SUBLANES - 2:SUBLANES - 2 + tm] * cw[0:1]
        y = y + full[SUBLANES - 1:SUBLANES - 1 + tm] * cw[1:2]
        y = y + h * cw[2:3]
        y = y + cb_ref[:, c * cc:(c + 1) * cc]
        gt = y[:, :FFN_CHUNK]
        act = (gt * _sigmoid(gt)) * y[:, FFN_CHUNK:]
        act_scr[:, c * FFN_CHUNK:(c + 1) * FFN_CHUNK] = act.astype(BF16)
    out = _dot(act_scr[...], wdn_ref[...])
    o_ref[0] = _layer_norm(ALPHA * x + out, g_ref[...], b_ref[...])


def _ffn_layer(x, wup, cw, cb, wdn, g, b):
    bsz, s, d = x.shape
    tm = FFN_TM
    return pl.pallas_call(
        _ffn_kernel,
        grid=(bsz, s // tm),
        in_specs=[pl.BlockSpec((1, tm, d), lambda bi, i: (bi, i, 0)),
                  _resident((d, 2 * D_FF)), _resident((FFN_CONV, 2 * D_FF)), _resident((1, 2 * D_FF)),
                  _resident((D_FF, d)), _resident((1, d)), _resident((1, d))],
        out_specs=pl.BlockSpec((1, tm, d), lambda bi, i: (bi, i, 0)),
        out_shape=jax.ShapeDtypeStruct((bsz, s, d), F32),
        scratch_shapes=[pltpu.VMEM((SUBLANES, 2 * D_FF), F32), pltpu.VMEM((tm, D_FF), BF16)],
        compiler_params=_params(("arbitrary", "arbitrary")),
        name="ffn_layer",
    )(x, wup, cw, cb, wdn, g, b)


def _chunk_interleave(w):
    lead = w.shape[:-1]
    w = w.reshape(lead + (2, FFN_NCHUNK, FFN_CHUNK))
    w = jnp.swapaxes(w, -3, -2)
    return w.reshape(lead + (2 * D_FF,))


def kernel(x, mem, a_w_qkv, a_sinks, a_w_o, b_w_in, b_conv_w, b_conv_b, b_w_rgate, b_b_rgate, b_w_igate, b_b_igate, b_lambda, b_w_o, c_w_down, c_q_norm, c_kv_norm, c_w_uq, c_w_ukv, c_w_o, mem_w_kv, x_w_q, x_w_o, f_w_up, f_conv_w, f_conv_b, f_w_down, ln_g, ln_b):
    bsz, s, d = x.shape
    cos, sin = _rope_tables(s)

    mkv = _matmul(mem.reshape(bsz * MEM_LEN, d), mem_w_kv.astype(BF16), 512, BF16)
    mkv = mkv.reshape(bsz, MEM_LEN, 2 * d)

    def row(v):
        return v.reshape(1, -1)

    for i in range(DEPTH):
        kind, j = i % N_MIXERS, i // N_MIXERS
        g0, b0 = row(ln_g[i, 0]), row(ln_b[i, 0])
        if kind == 0:
            x = _swa_layer(x, a_sinks[j], cos, sin, a_w_qkv[j].astype(BF16), a_w_o[j].astype(BF16), g0, b0)
        elif kind == 1:
            x = _rglru_layer(x, b_w_in[j].astype(BF16), b_conv_w[j], row(b_conv_b[j]),
                             b_w_rgate[j].astype(BF16), row(b_b_rgate[j]),
                             b_w_igate[j].astype(BF16), row(b_b_igate[j]),
                             row(b_lambda[j]), b_w_o[j].astype(BF16), g0, b0)
        else:
            wd = jnp.pad(c_w_down[j], ((0, 0), (0, C_DOWN_PAD - c_w_down.shape[-1]))).astype(BF16)
            wuq = c_w_uq[j].reshape(C_Q_RANK, C_HEADS, C_NOPE + C_ROPE)
            wuq = jnp.concatenate([wuq[:, :, :C_NOPE].reshape(C_Q_RANK, -1),
                                   wuq[:, :, C_NOPE:].reshape(C_Q_RANK, -1)], axis=-1).astype(BF16)
            wukv = c_w_ukv[j].reshape(C_KV_RANK, C_HEADS, C_NOPE + C_V)
            wukv = jnp.concatenate([wukv[:, :, :C_NOPE].reshape(C_KV_RANK, -1),
                                    wukv[:, :, C_NOPE:].reshape(C_KV_RANK, -1)], axis=-1).astype(BF16)
            q, k, v = _mla_proj(x, cos, sin, wd, row(c_q_norm[j]), row(c_kv_norm[j]), wuq, wukv)
            o = _mla_attn(q, k, v)
            x = _proj_ln(o.reshape(bsz * s, d), x.reshape(bsz * s, d), c_w_o[j].astype(BF16), g0, b0)
            x = x.reshape(bsz, s, d)
        x = _xattn_layer(x, x_w_q[i].astype(BF16), mkv, x_w_o[i].astype(BF16),
                         row(ln_g[i, 1]), row(ln_b[i, 1]))
        x = _ffn_layer(x, _chunk_interleave(f_w_up[i]).astype(BF16), _chunk_interleave(f_conv_w[i]),
                       row(_chunk_interleave(f_conv_b[i])), f_w_down[i].astype(BF16),
                       row(ln_g[i, 2]), row(ln_b[i, 2]))
    return x
```

```python
import functools
import math

import jax
import jax.numpy as jnp
import numpy as np
from jax import lax
from jax.experimental import pallas as pl
from jax.experimental.pallas import tpu as pltpu

D_MODEL = 1024
DEPTH = 4
N_MIXERS = 3
MEM_LEN = 256
BLOCK = 128
ROPE_THETA = 10000.0
NEG = -1e30
LN_EPS = 1e-5
RMS_EPS = 1e-6

A_HEADS = 16
A_KV_HEADS = 4
A_HEAD_DIM = 64

LRU_WIDTH = D_MODEL
LRU_BLOCKS = 4
LRU_BLOCK_W = LRU_WIDTH // LRU_BLOCKS
LRU_CONV = 4
LRU_C = 8.0

C_HEADS = 8
C_NOPE = 128
C_ROPE = 64
C_V = 128
C_Q_RANK = 384
C_KV_RANK = 256
C_QK_PAD = 256

X_HEADS = 4
X_HEAD_DIM = D_MODEL // X_HEADS

D_FF = 2816
FFN_CONV = 3
FFN_CHUNK = 256
FFN_NCHUNK = D_FF // FFN_CHUNK

ALPHA = (2.0 * DEPTH) ** 0.25

LANES = 128
SUBLANES = 8
VMEM_LIMIT = 56 * 1024 * 1024

BF16 = jnp.bfloat16
F32 = jnp.float32

NT_DIMS = (((1,), (1,)), ((), ()))


def _dot(a, b):
    return jnp.dot(a, b, preferred_element_type=F32)


def _dot_nt(a, b):
    return lax.dot_general(a, b, NT_DIMS, preferred_element_type=F32)


def _layer_norm(z, g, b):
    mu = jnp.mean(z, axis=-1, keepdims=True)
    zc = z - mu
    var = jnp.mean(zc * zc, axis=-1, keepdims=True)
    return zc * lax.rsqrt(var + LN_EPS) * g + b


def _sigmoid(z):
    return 1.0 / (1.0 + jnp.exp(-z))


def _resident(shape):
    nd = len(shape)
    return pl.BlockSpec(shape, lambda *_: (0,) * nd, pipeline_mode=pl.Buffered(1))


def _params(sem):
    return pltpu.CompilerParams(dimension_semantics=sem, vmem_limit_bytes=VMEM_LIMIT)


def _matmul_kernel(a_ref, w_ref, o_ref):
    o_ref[...] = _dot(a_ref[...].astype(BF16), w_ref[...]).astype(o_ref.dtype)


def _matmul(a, w, tm, out_dtype):
    t, k = a.shape
    n = w.shape[1]
    return pl.pallas_call(
        _matmul_kernel,
        grid=(t // tm,),
        in_specs=[pl.BlockSpec((tm, k), lambda i: (i, 0)), _resident((k, n))],
        out_specs=pl.BlockSpec((tm, n), lambda i: (i, 0)),
        out_shape=jax.ShapeDtypeStruct((t, n), out_dtype),
        compiler_params=_params(("parallel",)),
        name="matmul",
    )(a, w)


def _proj_ln_kernel(a_ref, r_ref, w_ref, g_ref, b_ref, o_ref):
    y = _dot(a_ref[...], w_ref[...])
    o_ref[...] = _layer_norm(ALPHA * r_ref[...] + y, g_ref[...], b_ref[...])


def _proj_ln(a, resid, w, g, b, tm=512):
    t, k = a.shape
    d = w.shape[1]
    return pl.pallas_call(
        _proj_ln_kernel,
        grid=(t // tm,),
        in_specs=[pl.BlockSpec((tm, k), lambda i: (i, 0)),
                  pl.BlockSpec((tm, d), lambda i: (i, 0)),
                  _resident((k, d)), _resident((1, d)), _resident((1, d))],
        out_specs=pl.BlockSpec((tm, d), lambda i: (i, 0)),
        out_shape=jax.ShapeDtypeStruct((t, d), F32),
        compiler_params=_params(("parallel",)),
        name="proj_ln",
    )(a, resid, w, g, b)


def _rope_slab(slab, cos, sin_signed, first_half):
    rot = jnp.where(first_half, pltpu.roll(slab, LANES - 32, 1), pltpu.roll(slab, 32, 1))
    return slab * cos + rot * sin_signed


def _rope_tables(seq):
    inv = 1.0 / (ROPE_THETA ** (jnp.arange(0, A_HEAD_DIM, 2, dtype=F32) / A_HEAD_DIM))
    ang = jnp.arange(seq, dtype=F32)[:, None] * inv[None, :]
    cos, sin = jnp.cos(ang), jnp.sin(ang)
    return (jnp.concatenate([cos, cos, cos, cos], axis=-1),
            jnp.concatenate([-sin, sin, -sin, sin], axis=-1))


SWA_TM = 256
SWA_NBLK = SWA_TM // BLOCK


def _swa_kernel(sinks_ref, x_ref, cos_ref, sin_ref, wqkv_ref, wo_ref, g_ref, b_ref, o_ref,
                kext_scr, vext_scr, q_scr, o_scr):
    i = pl.program_id(1)
    tm = SWA_TM

    @pl.when(i == 0)
    def _():
        kext_scr[:, 0:BLOCK, :] = jnp.zeros((2 * A_KV_HEADS, BLOCK, LANES), BF16)
        vext_scr[:, 0:BLOCK, :] = jnp.zeros((2 * A_KV_HEADS, BLOCK, LANES), BF16)

    @pl.when(i > 0)
    def _():
        kext_scr[:, 0:BLOCK, :] = kext_scr[:, tm:tm + BLOCK, :]
        vext_scr[:, 0:BLOCK, :] = vext_scr[:, tm:tm + BLOCK, :]

    x = x_ref[0]
    qkv = _dot(x.astype(BF16), wqkv_ref[...])
    cos = cos_ref[...]
    sin = sin_ref[...]
    lane = lax.broadcasted_iota(jnp.int32, (tm, LANES), 1)
    first_half = (lane & 32) == 0
    low = lane < 64

    nq = A_HEADS * A_HEAD_DIM
    nkv = A_KV_HEADS * A_HEAD_DIM
    for j in range(nq // LANES):
        slab = _rope_slab(qkv[:, j * LANES:(j + 1) * LANES], cos, sin, first_half)
        q_scr[:, j * LANES:(j + 1) * LANES] = (slab * (A_HEAD_DIM ** -0.5)).astype(BF16)

    for m in range(nkv // LANES):
        kslab = _rope_slab(qkv[:, nq + m * LANES:nq + (m + 1) * LANES], cos, sin, first_half)
        vslab = qkv[:, nq + nkv + m * LANES:nq + nkv + (m + 1) * LANES]
        for slab, ext in ((kslab, kext_scr), (vslab, vext_scr)):
            swapped = pltpu.roll(slab, 64, 1)
            h0, h1 = 2 * m, 2 * m + 1
            ext[2 * h0, BLOCK:BLOCK + tm, :] = jnp.where(low, slab, 0.0).astype(BF16)
            ext[2 * h0 + 1, BLOCK:BLOCK + tm, :] = jnp.where(low, 0.0, swapped).astype(BF16)
            ext[2 * h1, BLOCK:BLOCK + tm, :] = jnp.where(low, swapped, 0.0).astype(BF16)
            ext[2 * h1 + 1, BLOCK:BLOCK + tm, :] = jnp.where(low, 0.0, slab).astype(BF16)

    qi = lax.broadcasted_iota(jnp.int32, (2 * BLOCK, 2 * BLOCK), 0) & (BLOCK - 1)
    kj = lax.broadcasted_iota(jnp.int32, (2 * BLOCK, 2 * BLOCK), 1)
    dist = qi + BLOCK - kj
    band = (dist >= 0) & (dist < BLOCK)
    band_first = band & ((kj >= BLOCK) | (i > 0))
    top = lax.broadcasted_iota(jnp.int32, (2 * BLOCK, 1), 0) < BLOCK

    for n in range(SWA_NBLK):
        mask = band_first if n == 0 else band
        r0 = n * BLOCK
        for h in range(A_KV_HEADS):
            qs = jnp.concatenate([q_scr[r0:r0 + BLOCK, (2 * h) * LANES:(2 * h + 1) * LANES],
                                  q_scr[r0:r0 + BLOCK, (2 * h + 1) * LANES:(2 * h + 2) * LANES]], axis=0)
            acc = None
            for half in range(2):
                kk = kext_scr[2 * h + half, r0:r0 + 2 * BLOCK, :]
                vv = vext_scr[2 * h + half, r0:r0 + 2 * BLOCK, :]
                s = jnp.where(mask, _dot_nt(qs, kk), NEG)
                sink = jnp.where(top, sinks_ref[4 * h + half], sinks_ref[4 * h + 2 + half])
                mx = jnp.maximum(jnp.max(s, axis=-1, keepdims=True), sink)
                p = jnp.exp(s - mx)
                denom = jnp.sum(p, axis=-1, keepdims=True) + jnp.exp(sink - mx)
                o = _dot(p.astype(BF16), vv) / denom
                acc = o if acc is None else acc + o
            o_scr[r0:r0 + BLOCK, (2 * h) * LANES:(2 * h + 1) * LANES] = acc[:BLOCK].astype(BF16)
            o_scr[r0:r0 + BLOCK, (2 * h + 1) * LANES:(2 * h + 2) * LANES] = acc[BLOCK:].astype(BF16)

    y = _dot(o_scr[...], wo_ref[...])
    o_ref[0] = _layer_norm(ALPHA * x + y, g_ref[...], b_ref[...])


def _swa_layer(x, sinks, cos, sin, wqkv, wo, g, b):
    bsz, s, d = x.shape
    tm = SWA_TM
    nqkv = wqkv.shape[1]
    return pl.pallas_call(
        _swa_kernel,
        grid=(bsz, s // tm),
        in_specs=[pl.BlockSpec(memory_space=pltpu.SMEM),
                  pl.BlockSpec((1, tm, d), lambda bi, i: (bi, i, 0)),
                  pl.BlockSpec((tm, LANES), lambda bi, i: (i, 0)),
                  pl.BlockSpec((tm, LANES), lambda bi, i: (i, 0)),
                  _resident((d, nqkv)), _resident((d, d)), _resident((1, d)), _resident((1, d))],
        out_specs=pl.BlockSpec((1, tm, d), lambda bi, i: (bi, i, 0)),
        out_shape=jax.ShapeDtypeStruct((bsz, s, d), F32),
        scratch_shapes=[pltpu.VMEM((2 * A_KV_HEADS, tm + BLOCK, LANES), BF16),
                        pltpu.VMEM((2 * A_KV_HEADS, tm + BLOCK, LANES), BF16),
                        pltpu.VMEM((tm, d), BF16),
                        pltpu.VMEM((tm, d), BF16)],
        compiler_params=_params(("arbitrary", "arbitrary")),
        name="swa_layer",
    )(sinks, x, cos, sin, wqkv, wo, g, b)


LRU_TM = 256


def _rglru_kernel(x_ref, win_ref, cw_ref, cb_ref, wr_ref, br_ref, wi_ref, bi_ref, lam_ref, wo_ref,
                  g_ref, b_ref, o_ref, uprev_scr, hcar_scr, a_scr, b_scr, h_scr):
    i = pl.program_id(1)
    tm = LRU_TM
    w = LRU_WIDTH

    @pl.when(i == 0)
    def _():
        uprev_scr[...] = jnp.zeros_like(uprev_scr)
        hcar_scr[...] = jnp.zeros_like(hcar_scr)

    x = x_ref[0]
    gu = _dot(x.astype(BF16), win_ref[...])
    gate = gu[:, :w]
    u = gu[:, w:]
    full = jnp.concatenate([uprev_scr[...], u], axis=0)
    uprev_scr[...] = u[tm - SUBLANES:, :]
    cw = cw_ref[...]
    uc = full[SUBLANES - 3:SUBLANES - 3 + tm] * cw[0:1]
    uc = uc + full[SUBLANES - 2:SUBLANES - 2 + tm] * cw[1:2]
    uc = uc + full[SUBLANES - 1:SUBLANES - 1 + tm] * cw[2:3]
    uc = uc + u * cw[3:4]
    uc = uc + cb_ref[...]

    ucb = uc.astype(BF16)
    rs, igs = [], []
    for h in range(LRU_BLOCKS):
        blk = ucb[:, h * LRU_BLOCK_W:(h + 1) * LRU_BLOCK_W]
        rs.append(_dot(blk, wr_ref[h]))
        igs.append(_dot(blk, wi_ref[h]))
    r = _sigmoid(jnp.concatenate(rs, axis=-1) + br_ref[...])
    ig = _sigmoid(jnp.concatenate(igs, axis=-1) + bi_ref[...])
    z = -lam_ref[...]
    softplus = jnp.maximum(z, 0.0) + jnp.log1p(jnp.exp(-jnp.abs(z)))
    log_a = (-LRU_C * r) * softplus
    a_scr[...] = jnp.exp(log_a)
    th = jnp.tanh(log_a)
    b_scr[...] = jnp.sqrt((-2.0 * th) / (1.0 - th)) * (ig * uc)

    row = lax.broadcasted_iota(jnp.int32, (SUBLANES, w), 0)

    def group(gi, hprev):
        r0 = pl.multiple_of(gi * SUBLANES, SUBLANES)
        a = a_scr[pl.ds(r0, SUBLANES), :]
        bb = b_scr[pl.ds(r0, SUBLANES), :]
        for d in (1, 2, 4):
            keep = row >= d
            a_s = jnp.where(keep, pltpu.roll(a, d, 0), 1.0)
            b_s = jnp.where(keep, pltpu.roll(bb, d, 0), 0.0)
            bb = a * b_s + bb
            a = a * a_s
        h = a * hprev + bb
        h_scr[pl.ds(r0, SUBLANES), :] = h
        return jnp.broadcast_to(h[SUBLANES - 1:SUBLANES, :], (SUBLANES, w))

    hlast = lax.fori_loop(0, tm // SUBLANES, group, hcar_scr[...])
    hcar_scr[...] = hlast

    c0 = math.sqrt(2.0 / math.pi)
    gelu = gate * (0.5 * (1.0 + jnp.tanh(c0 * (gate + 0.044715 * (gate * gate * gate)))))
    y = _dot((h_scr[...] * gelu).astype(BF16), wo_ref[...])
    o_ref[0] = _layer_norm(ALPHA * x + y, g_ref[...], b_ref[...])


def _rglru_layer(x, win, cw, cb, wr, br, wi, bi, lam, wo, g, b):
    bsz, s, d = x.shape
    tm = LRU_TM
    w = LRU_WIDTH
    return pl.pallas_call(
        _rglru_kernel,
        grid=(bsz, s // tm),
        in_specs=[pl.BlockSpec((1, tm, d), lambda bi_, i: (bi_, i, 0)),
                  _resident((d, 2 * w)), _resident((LRU_CONV, w)), _resident((1, w)),
                  _resident((LRU_BLOCKS, LRU_BLOCK_W, LRU_BLOCK_W)), _resident((1, w)),
                  _resident((LRU_BLOCKS, LRU_BLOCK_W, LRU_BLOCK_W)), _resident((1, w)),
                  _resident((1, w)), _resident((w, d)), _resident((1, d)), _resident((1, d))],
        out_specs=pl.BlockSpec((1, tm, d), lambda bi_, i: (bi_, i, 0)),
        out_shape=jax.ShapeDtypeStruct((bsz, s, d), F32),
        scratch_shapes=[pltpu.VMEM((SUBLANES, w), F32), pltpu.VMEM((SUBLANES, w), F32),
                        pltpu.VMEM((tm, w), F32), pltpu.VMEM((tm, w), F32), pltpu.VMEM((tm, w), F32)],
        compiler_params=_params(("arbitrary", "arbitrary")),
        name="rglru_layer",
    )(x, win, cw, cb, wr, br, wi, bi, lam, wo, g, b)


MLA_TM = 512
MLA_TQ = 1024
MLA_HALF = MLA_TQ // 2
MLA_TK = 512
C_DOWN_PAD = 768
MLA_Q_SCALE = (C_NOPE + C_ROPE) ** -0.5 * math.log2(math.e)


def _mla_proj_kernel(x_ref, cos_ref, sin_ref, wd_ref, qn_ref, kvn_ref, wuq_ref, wukv_ref,
                     q_ref, k_ref, v_ref):
    tm = MLA_TM
    xb = x_ref[0].astype(BF16)
    c = _dot(xb, wd_ref[...])
    cq = c[:, :C_Q_RANK]
    ckv = c[:, C_Q_RANK:C_Q_RANK + C_KV_RANK]
    kr_slab = c[:, C_Q_RANK + C_KV_RANK:]
    cq = cq * lax.rsqrt(jnp.mean(cq * cq, axis=-1, keepdims=True) + RMS_EPS) * qn_ref[...]
    ckv = ckv * lax.rsqrt(jnp.mean(ckv * ckv, axis=-1, keepdims=True) + RMS_EPS) * kvn_ref[...]
    q = _dot(cq.astype(BF16), wuq_ref[...]) * MLA_Q_SCALE
    kv = _dot(ckv.astype(BF16), wukv_ref[...])

    cos = cos_ref[...]
    sin = sin_ref[...]
    lane = lax.broadcasted_iota(jnp.int32, (tm, LANES), 1)
    first_half = (lane & 32) == 0
    low = lane < 64
    nn = C_HEADS * C_NOPE

    kr = _rope_slab(kr_slab, cos, sin, first_half)
    kr_pair = (kr.astype(BF16), pltpu.roll(kr, 64, 1).astype(BF16))
    for pair in range(C_HEADS // 2):
        qr = _rope_slab(q[:, nn + pair * LANES:nn + (pair + 1) * LANES], cos, sin, first_half)
        for half in range(2):
            h = 2 * pair + half
            qr_h = jnp.where(low, qr, 0.0) if half == 0 else jnp.where(low, 0.0, qr)
            q_ref[0, h, :, 0:C_NOPE] = q[:, h * C_NOPE:(h + 1) * C_NOPE].astype(BF16)
            q_ref[0, h, :, C_NOPE:C_QK_PAD] = qr_h.astype(BF16)
            k_ref[0, h, :, 0:C_NOPE] = kv[:, h * C_NOPE:(h + 1) * C_NOPE].astype(BF16)
            k_ref[0, h, :, C_NOPE:C_QK_PAD] = kr_pair[half]
            v_ref[0, h] = kv[:, nn + h * C_V:nn + (h + 1) * C_V].astype(BF16)


def _mla_proj(x, cos, sin, wd, qn, kvn, wuq, wukv):
    bsz, s, d = x.shape
    tm = MLA_TM
    qk_shape = jax.ShapeDtypeStruct((bsz, C_HEADS, s, C_QK_PAD), BF16)
    v_shape = jax.ShapeDtypeStruct((bsz, C_HEADS, s, C_V), BF16)
    return pl.pallas_call(
        _mla_proj_kernel,
        grid=(bsz, s // tm),
        in_specs=[pl.BlockSpec((1, tm, d), lambda bi, i: (bi, i, 0)),
                  pl.BlockSpec((tm, LANES), lambda bi, i: (i, 0)),
                  pl.BlockSpec((tm, LANES), lambda bi, i: (i, 0)),
                  _resident(wd.shape), _resident(qn.shape), _resident(kvn.shape),
                  _resident(wuq.shape), _resident(wukv.shape)],
        out_specs=[pl.BlockSpec((1, C_HEADS, tm, C_QK_PAD), lambda bi, i: (bi, 0, i, 0)),
                   pl.BlockSpec((1, C_HEADS, tm, C_QK_PAD), lambda bi, i: (bi, 0, i, 0)),
                   pl.BlockSpec((1, C_HEADS, tm, C_V), lambda bi, i: (bi, 0, i, 0))],
        out_shape=[qk_shape, qk_shape, v_shape],
        compiler_params=_params(("parallel", "parallel")),
        name="mla_proj",
    )(x, cos, sin, wd, qn, kvn, wuq, wukv)


def _mla_attn_kernel(q_ref, k_ref, v_ref, o_ref, m_scr, acc_scr, s_scr):
    i = pl.program_id(2)
    tk, hq = MLA_TK, MLA_HALF

    m_scr[...] = jnp.full_like(m_scr, -jnp.inf)
    acc_scr[...] = jnp.zeros_like(acc_scr)
    ones = jnp.ones((tk, LANES), BF16)

    def scores(slot, hf, j):
        k0 = pl.multiple_of(j * tk, tk)
        s_scr[slot, hf] = _dot_nt(q_ref[0, 0, hf * hq:(hf + 1) * hq, :], k_ref[0, 0, pl.ds(k0, tk), :])

    def consume(slot, hf, j, diagonal):
        k0 = pl.multiple_of(j * tk, tk)
        vv = jnp.concatenate([v_ref[0, 0, pl.ds(k0, tk), :], ones], axis=1)
        s = s_scr[slot, hf]
        if diagonal:
            qpos = lax.broadcasted_iota(jnp.int32, (hq, tk), 0)
            kpos = lax.broadcasted_iota(jnp.int32, (hq, tk), 1)
            s = jnp.where(kpos <= qpos, s, NEG)
        m_prev = m_scr[hf]
        m_next = jnp.maximum(m_prev, jnp.max(s, axis=-1, keepdims=True))
        p = jnp.exp2(s - jnp.tile(m_next, (1, tk // LANES)))
        corr = jnp.exp2(m_prev - m_next)
        acc_scr[hf] = jnp.tile(corr, (1, 2)) * acc_scr[hf] + _dot(p.astype(BF16), vv)
        m_scr[hf] = m_next

    scores(0, 0, 0)
    scores(0, 1, 0)

    def body(t, carry):
        j = 2 * t
        scores(1, 0, j + 1)
        scores(1, 1, j + 1)
        consume(0, 0, j, False)
        consume(0, 1, j, False)
        scores(0, 0, j + 2)
        scores(0, 1, j + 2)
        consume(1, 0, j + 1, False)
        consume(1, 1, j + 1, False)
        return carry

    lax.fori_loop(0, i, body, 0)
    scores(1, 1, 2 * i + 1)
    consume(0, 0, 2 * i, True)
    consume(0, 1, 2 * i, False)
    consume(1, 1, 2 * i + 1, True)
    for hf in range(2):
        acc = acc_scr[hf]
        o_ref[0, hf * hq:(hf + 1) * hq, :] = (acc[:, :C_V] / acc[:, C_V:]).astype(BF16)


def _mla_attn(q, k, v):
    bsz, nh, s, _ = q.shape
    tq = MLA_TQ
    return pl.pallas_call(
        _mla_attn_kernel,
        grid=(bsz, nh, s // tq),
        in_specs=[pl.BlockSpec((1, 1, tq, C_QK_PAD), lambda b, h, i: (b, h, i, 0)),
                  pl.BlockSpec((1, 1, s, C_QK_PAD), lambda b, h, i: (b, h, 0, 0)),
                  pl.BlockSpec((1, 1, s, C_V), lambda b, h, i: (b, h, 0, 0))],
        out_specs=pl.BlockSpec((1, tq, C_V), lambda b, h, i: (b, i, h)),
        out_shape=jax.ShapeDtypeStruct((bsz, s, nh * C_V), BF16),
        scratch_shapes=[pltpu.VMEM((2, MLA_HALF, LANES), F32),
                        pltpu.VMEM((2, MLA_HALF, 2 * C_V), F32),
                        pltpu.VMEM((2, 2, MLA_HALF, MLA_TK), F32)],
        compiler_params=_params(("parallel", "parallel", "arbitrary")),
        name="mla_attn",
    )(q, k, v)


X_TM = 512


def _xattn_kernel(x_ref, wq_ref, mkv_ref, wo_ref, g_ref, b_ref, o_ref):
    x = x_ref[0]
    xb = x.astype(BF16)
    outs = []
    for h in range(X_HEADS):
        lo, hi = h * X_HEAD_DIM, (h + 1) * X_HEAD_DIM
        q = _dot(xb, wq_ref[:, lo:hi]) * (X_HEAD_DIM ** -0.5)
        s = _dot_nt(q.astype(BF16), mkv_ref[0, :, lo:hi])
        p = jnp.exp(s - jnp.max(s, axis=-1, keepdims=True))
        denom = jnp.sum(p, axis=-1, keepdims=True)
        o = _dot(p.astype(BF16), mkv_ref[0, :, D_MODEL + lo:D_MODEL + hi]) / denom
        outs.append(o.astype(BF16))
    y = _dot(jnp.concatenate(outs, axis=-1), wo_ref[...])
    o_ref[0] = _layer_norm(ALPHA * x + y, g_ref[...], b_ref[...])


def _xattn_layer(x, wq, mkv, wo, g, b):
    bsz, s, d = x.shape
    tm = X_TM
    return pl.pallas_call(
        _xattn_kernel,
        grid=(bsz, s // tm),
        in_specs=[pl.BlockSpec((1, tm, d), lambda bi, i: (bi, i, 0)),
                  _resident((d, d)),
                  pl.BlockSpec((1, MEM_LEN, 2 * d), lambda bi, i: (bi, 0, 0)),
                  _resident((d, d)), _resident((1, d)), _resident((1, d))],
        out_specs=pl.BlockSpec((1, tm, d), lambda bi, i: (bi, i, 0)),
        out_shape=jax.ShapeDtypeStruct((bsz, s, d), F32),
        compiler_params=_params(("parallel", "parallel")),
        name="xattn_layer",
    )(x, wq, mkv, wo, g, b)


FFN_TM = 512


def _ffn_kernel(x_ref, wup_ref, cw_ref, cb_ref, wdn_ref, g_ref, b_ref, o_ref, hprev_scr, act_scr):
    i = pl.program_id(1)
    tm = FFN_TM
    cc = 2 * FFN_CHUNK

    @pl.when(i == 0)
    def _():
        hprev_scr[...] = jnp.zeros_like(hprev_scr)

    x = x_ref[0]
    xb = x.astype(BF16)
    for c in range(FFN_NCHUNK):
        h = _dot(xb, wup_ref[:, c * cc:(c + 1) * cc])
        full = jnp.concatenate([hprev_scr[:, c * cc:(c + 1) * cc], h], axis=0)
        hprev_scr[:, c * cc:(c + 1) * cc] = h[tm - SUBLANES:, :]
        cw = cw_ref[:, c * cc:(c + 1) * cc]
        y = full[SUBLANES - 2:SUBLANES - 2 + tm] * cw[0:1]
        y = y + full[SUBLANES - 1:SUBLANES - 1 + tm] * cw[1:2]
        y = y + h * cw[2:3]
        y = y + cb_ref[:, c * cc:(c + 1) * cc]
        gt = y[:, :FFN_CHUNK]
        act = (gt * _sigmoid(gt)) * y[:, FFN_CHUNK:]
        act_scr[:, c * FFN_CHUNK:(c + 1) * FFN_CHUNK] = act.astype(BF16)
    out = _dot(act_scr[...], wdn_ref[...])
    o_ref[0] = _layer_norm(ALPHA * x + out, g_ref[...], b_ref[...])


def _ffn_layer(x, wup, cw, cb, wdn, g, b):
    bsz, s, d = x.shape
    tm = FFN_TM
    return pl.pallas_call(
        _ffn_kernel,
        grid=(bsz, s // tm),
        in_specs=[pl.BlockSpec((1, tm, d), lambda bi, i: (bi, i, 0)),
                  _resident((d, 2 * D_FF)), _resident((FFN_CONV, 2 * D_FF)), _resident((1, 2 * D_FF)),
                  _resident((D_FF, d)), _resident((1, d)), _resident((1, d))],
        out_specs=pl.BlockSpec((1, tm, d), lambda bi, i: (bi, i, 0)),
        out_shape=jax.ShapeDtypeStruct((bsz, s, d), F32),
        scratch_shapes=[pltpu.VMEM((SUBLANES, 2 * D_FF), F32), pltpu.VMEM((tm, D_FF), BF16)],
        compiler_params=_params(("arbitrary", "arbitrary")),
        name="ffn_layer",
    )(x, wup, cw, cb, wdn, g, b)


def _chunk_interleave(w):
    lead = w.shape[:-1]
    w = w.reshape(lead + (2, FFN_NCHUNK, FFN_CHUNK))
    w = jnp.swapaxes(w, -3, -2)
    return w.reshape(lead + (2 * D_FF,))


def kernel(x, mem, a_w_qkv, a_sinks, a_w_o, b_w_in, b_conv_w, b_conv_b, b_w_rgate, b_b_rgate, b_w_igate, b_b_igate, b_lambda, b_w_o, c_w_down, c_q_norm, c_kv_norm, c_w_uq, c_w_ukv, c_w_o, mem_w_kv, x_w_q, x_w_o, f_w_up, f_conv_w, f_conv_b, f_w_down, ln_g, ln_b):
    bsz, s, d = x.shape
    cos, sin = _rope_tables(s)

    mkv = _matmul(mem.reshape(bsz * MEM_LEN, d), mem_w_kv.astype(BF16), 512, BF16)
    mkv = mkv.reshape(bsz, MEM_LEN, 2 * d)

    def row(v):
        return v.reshape(1, -1)

    for i in range(DEPTH):
        kind, j = i % N_MIXERS, i // N_MIXERS
        g0, b0 = row(ln_g[i, 0]), row(ln_b[i, 0])
        if kind == 0:
            x = _swa_layer(x, a_sinks[j], cos, sin, a_w_qkv[j].astype(BF16), a_w_o[j].astype(BF16), g0, b0)
        elif kind == 1:
            x = _rglru_layer(x, b_w_in[j].astype(BF16), b_conv_w[j], row(b_conv_b[j]),
                             b_w_rgate[j].astype(BF16), row(b_b_rgate[j]),
                             b_w_igate[j].astype(BF16), row(b_b_igate[j]),
                             row(b_lambda[j]), b_w_o[j].astype(BF16), g0, b0)
        else:
            wd = jnp.pad(c_w_down[j], ((0, 0), (0, C_DOWN_PAD - c_w_down.shape[-1]))).astype(BF16)
            wuq = c_w_uq[j].reshape(C_Q_RANK, C_HEADS, C_NOPE + C_ROPE)
            wuq = jnp.concatenate([wuq[:, :, :C_NOPE].reshape(C_Q_RANK, -1),
                                   wuq[:, :, C_NOPE:].reshape(C_Q_RANK, -1)], axis=-1).astype(BF16)
            wukv = c_w_ukv[j].reshape(C_KV_RANK, C_HEADS, C_NOPE + C_V)
            wukv = jnp.concatenate([wukv[:, :, :C_NOPE].reshape(C_KV_RANK, -1),
                                    wukv[:, :, C_NOPE:].reshape(C_KV_RANK, -1)], axis=-1).astype(BF16)
            q, k, v = _mla_proj(x, cos, sin, wd, row(c_q_norm[j]), row(c_kv_norm[j]), wuq, wukv)
            o = _mla_attn(q, k, v)
            x = _proj_ln(o.reshape(bsz * s, d), x.reshape(bsz * s, d), c_w_o[j].astype(BF16), g0, b0)
            x = x.reshape(bsz, s, d)
        x = _xattn_layer(x, x_w_q[i].astype(BF16), mkv, x_w_o[i].astype(BF16),
                         row(ln_g[i, 1]), row(ln_b[i, 1]))
        x = _ffn_layer(x, _chunk_interleave(f_w_up[i]).astype(BF16), _chunk_interleave(f_conv_w[i]),
                       row(_chunk_interleave(f_conv_b[i])), f_w_down[i].astype(BF16),
                       row(ln_g[i, 2]), row(ln_b[i, 2]))
    return x
```

```python
import functools
import math

import jax
import jax.numpy as jnp
import numpy as np
from jax import lax
from jax.experimental import pallas as pl
from jax.experimental.pallas import tpu as pltpu

D_MODEL = 1024
DEPTH = 4
N_MIXERS = 3
MEM_LEN = 256
BLOCK = 128
ROPE_THETA = 10000.0
NEG = -1e30
LN_EPS = 1e-5
RMS_EPS = 1e-6

A_HEADS = 16
A_KV_HEADS = 4
A_HEAD_DIM = 64

LRU_WIDTH = D_MODEL
LRU_BLOCKS = 4
LRU_BLOCK_W = LRU_WIDTH // LRU_BLOCKS
LRU_CONV = 4
LRU_C = 8.0

C_HEADS = 8
C_NOPE = 128
C_ROPE = 64
C_V = 128
C_Q_RANK = 384
C_KV_RANK = 256
C_QK_PAD = 256

X_HEADS = 4
X_HEAD_DIM = D_MODEL // X_HEADS

D_FF = 2816
FFN_CONV = 3
FFN_CHUNK = 256
FFN_NCHUNK = D_FF // FFN_CHUNK

ALPHA = (2.0 * DEPTH) ** 0.25

LANES = 128
SUBLANES = 8
VMEM_LIMIT = 56 * 1024 * 1024

BF16 = jnp.bfloat16
F32 = jnp.float32

NT_DIMS = (((1,), (1,)), ((), ()))


def _dot(a, b):
    return jnp.dot(a, b, preferred_element_type=F32)


def _dot_nt(a, b):
    return lax.dot_general(a, b, NT_DIMS, preferred_element_type=F32)


def _layer_norm(z, g, b):
    mu = jnp.mean(z, axis=-1, keepdims=True)
    zc = z - mu
    var = jnp.mean(zc * zc, axis=-1, keepdims=True)
    return zc * lax.rsqrt(var + LN_EPS) * g + b


def _sigmoid(z):
    return 1.0 / (1.0 + jnp.exp(-z))


def _resident(shape):
    nd = len(shape)
    return pl.BlockSpec(shape, lambda *_: (0,) * nd, pipeline_mode=pl.Buffered(1))


def _params(sem):
    return pltpu.CompilerParams(dimension_semantics=sem, vmem_limit_bytes=VMEM_LIMIT)


def _matmul_kernel(a_ref, w_ref, o_ref):
    o_ref[...] = _dot(a_ref[...].astype(BF16), w_ref[...]).astype(o_ref.dtype)


def _matmul(a, w, tm, out_dtype):
    t, k = a.shape
    n = w.shape[1]
    return pl.pallas_call(
        _matmul_kernel,
        grid=(t // tm,),
        in_specs=[pl.BlockSpec((tm, k), lambda i: (i, 0)), _resident((k, n))],
        out_specs=pl.BlockSpec((tm, n), lambda i: (i, 0)),
        out_shape=jax.ShapeDtypeStruct((t, n), out_dtype),
        compiler_params=_params(("parallel",)),
        name="matmul",
    )(a, w)


def _proj_ln_kernel(a_ref, r_ref, w_ref, g_ref, b_ref, o_ref):
    y = _dot(a_ref[...], w_ref[...])
    o_ref[...] = _layer_norm(ALPHA * r_ref[...] + y, g_ref[...], b_ref[...])


def _proj_ln(a, resid, w, g, b, tm=512):
    t, k = a.shape
    d = w.shape[1]
    return pl.pallas_call(
        _proj_ln_kernel,
        grid=(t // tm,),
        in_specs=[pl.BlockSpec((tm, k), lambda i: (i, 0)),
                  pl.BlockSpec((tm, d), lambda i: (i, 0)),
                  _resident((k, d)), _resident((1, d)), _resident((1, d))],
        out_specs=pl.BlockSpec((tm, d), lambda i: (i, 0)),
        out_shape=jax.ShapeDtypeStruct((t, d), F32),
        compiler_params=_params(("parallel",)),
        name="proj_ln",
    )(a, resid, w, g, b)


def _rope_slab(slab, cos, sin_signed, first_half):
    rot = jnp.where(first_half, pltpu.roll(slab, LANES - 32, 1), pltpu.roll(slab, 32, 1))
    return slab * cos + rot * sin_signed


def _rope_tables(seq):
    inv = 1.0 / (ROPE_THETA ** (jnp.arange(0, A_HEAD_DIM, 2, dtype=F32) / A_HEAD_DIM))
    ang = jnp.arange(seq, dtype=F32)[:, None] * inv[None, :]
    cos, sin = jnp.cos(ang), jnp.sin(ang)
    return (jnp.concatenate([cos, cos, cos, cos], axis=-1),
            jnp.concatenate([-sin, sin, -sin, sin], axis=-1))


SWA_TM = 512
SWA_NBLK = SWA_TM // BLOCK


def _swa_kernel(sinks_ref, x_ref, cos_ref, sin_ref, wqkv_ref, wo_ref, g_ref, b_ref, o_ref,
                kext_scr, vext_scr, q_scr, o_scr):
    i = pl.program_id(1)
    tm = SWA_TM

    @pl.when(i == 0)
    def _():
        kext_scr[:, 0:BLOCK, :] = jnp.zeros((2 * A_KV_HEADS, BLOCK, LANES), BF16)
        vext_scr[:, 0:BLOCK, :] = jnp.zeros((2 * A_KV_HEADS, BLOCK, LANES), BF16)

    @pl.when(i > 0)
    def _():
        kext_scr[:, 0:BLOCK, :] = kext_scr[:, tm:tm + BLOCK, :]
        vext_scr[:, 0:BLOCK, :] = vext_scr[:, tm:tm + BLOCK, :]

    x = x_ref[0]
    qkv = _dot(x.astype(BF16), wqkv_ref[...])
    cos = cos_ref[...]
    sin = sin_ref[...]
    lane = lax.broadcasted_iota(jnp.int32, (tm, LANES), 1)
    first_half = (lane & 32) == 0
    low = lane < 64

    nq = A_HEADS * A_HEAD_DIM
    nkv = A_KV_HEADS * A_HEAD_DIM
    for j in range(nq // LANES):
        slab = _rope_slab(qkv[:, j * LANES:(j + 1) * LANES], cos, sin, first_half)
        q_scr[:, j * LANES:(j + 1) * LANES] = (slab * (A_HEAD_DIM ** -0.5)).astype(BF16)

    for m in range(nkv // LANES):
        kslab = _rope_slab(qkv[:, nq + m * LANES:nq + (m + 1) * LANES], cos, sin, first_half)
        vslab = qkv[:, nq + nkv + m * LANES:nq + nkv + (m + 1) * LANES]
        for slab, ext in ((kslab, kext_scr), (vslab, vext_scr)):
            swapped = pltpu.roll(slab, 64, 1)
            h0, h1 = 2 * m, 2 * m + 1
            ext[2 * h0, BLOCK:BLOCK + tm, :] = jnp.where(low, slab, 0.0).astype(BF16)
            ext[2 * h0 + 1, BLOCK:BLOCK + tm, :] = jnp.where(low, 0.0, swapped).astype(BF16)
            ext[2 * h1, BLOCK:BLOCK + tm, :] = jnp.where(low, swapped, 0.0).astype(BF16)
            ext[2 * h1 + 1, BLOCK:BLOCK + tm, :] = jnp.where(low, 0.0, slab).astype(BF16)

    qi = lax.broadcasted_iota(jnp.int32, (2 * BLOCK, 4 * BLOCK), 0) & (BLOCK - 1)
    kj = lax.broadcasted_iota(jnp.int32, (2 * BLOCK, 4 * BLOCK), 1) & (2 * BLOCK - 1)
    dist = qi + BLOCK - kj
    band = (dist >= 0) & (dist < BLOCK)
    band_first = band & ((kj >= BLOCK) | (i > 0))
    top = lax.broadcasted_iota(jnp.int32, (2 * BLOCK, 1), 0) < BLOCK
    low2 = lax.broadcasted_iota(jnp.int32, (2 * BLOCK, LANES), 1) < 64

    def window(ext, h, r0):
        return jnp.concatenate([ext[2 * h, r0:r0 + 2 * BLOCK, :], ext[2 * h + 1, r0:r0 + 2 * BLOCK, :]],
                               axis=0)

    for n in range(SWA_NBLK):
        mask = band_first if n == 0 else band
        r0 = n * BLOCK
        scores = []
        for h in range(A_KV_HEADS):
            qs = jnp.concatenate([q_scr[r0:r0 + BLOCK, (2 * h) * LANES:(2 * h + 1) * LANES],
                                  q_scr[r0:r0 + BLOCK, (2 * h + 1) * LANES:(2 * h + 2) * LANES]], axis=0)
            scores.append(_dot_nt(qs, window(kext_scr, h, r0)))
        probs, inv_denoms = [], []
        for h in range(A_KV_HEADS):
            s = jnp.where(mask, scores[h], NEG)
            ps, rs = [], []
            for half in range(2):
                sh = s[:, half * 2 * BLOCK:(half + 1) * 2 * BLOCK]
                sink = jnp.where(top, sinks_ref[4 * h + half], sinks_ref[4 * h + 2 + half])
                mx = jnp.maximum(jnp.max(sh, axis=-1, keepdims=True), sink)
                p = jnp.exp(sh - mx)
                rs.append(1.0 / (jnp.sum(p, axis=-1, keepdims=True) + jnp.exp(sink - mx)))
                ps.append(p.astype(BF16))
            probs.append(jnp.concatenate(ps, axis=1))
            inv_denoms.append(jnp.where(low2, rs[0], rs[1]))
        for h in range(A_KV_HEADS):
            o = _dot(probs[h], window(vext_scr, h, r0)) * inv_denoms[h]
            o_scr[r0:r0 + BLOCK, (2 * h) * LANES:(2 * h + 1) * LANES] = o[:BLOCK].astype(BF16)
            o_scr[r0:r0 + BLOCK, (2 * h + 1) * LANES:(2 * h + 2) * LANES] = o[BLOCK:].astype(BF16)

    y = _dot(o_scr[...], wo_ref[...])
    o_ref[0] = _layer_norm(ALPHA * x + y, g_ref[...], b_ref[...])


def _swa_layer(x, sinks, cos, sin, wqkv, wo, g, b):
    bsz, s, d = x.shape
    tm = SWA_TM
    nqkv = wqkv.shape[1]
    return pl.pallas_call(
        _swa_kernel,
        grid=(bsz, s // tm),
        in_specs=[pl.BlockSpec(memory_space=pltpu.SMEM),
                  pl.BlockSpec((1, tm, d), lambda bi, i: (bi, i, 0)),
                  pl.BlockSpec((tm, LANES), lambda bi, i: (i, 0)),
                  pl.BlockSpec((tm, LANES), lambda bi, i: (i, 0)),
                  _resident((d, nqkv)), _resident((d, d)), _resident((1, d)), _resident((1, d))],
        out_specs=pl.BlockSpec((1, tm, d), lambda bi, i: (bi, i, 0)),
        out_shape=jax.ShapeDtypeStruct((bsz, s, d), F32),
        scratch_shapes=[pltpu.VMEM((2 * A_KV_HEADS, tm + BLOCK, LANES), BF16),
                        pltpu.VMEM((2 * A_KV_HEADS, tm + BLOCK, LANES), BF16),
                        pltpu.VMEM((tm, d), BF16),
                        pltpu.VMEM((tm, d), BF16)],
        compiler_params=_params(("arbitrary", "arbitrary")),
        name="swa_layer",
    )(sinks, x, cos, sin, wqkv, wo, g, b)


LRU_TM = 256


def _rglru_kernel(x_ref, win_ref, cw_ref, cb_ref, wr_ref, br_ref, wi_ref, bi_ref, lam_ref, wo_ref,
                  g_ref, b_ref, o_ref, uprev_scr, hcar_scr, a_scr, b_scr, h_scr):
    i = pl.program_id(1)
    tm = LRU_TM
    w = LRU_WIDTH

    @pl.when(i == 0)
    def _():
        uprev_scr[...] = jnp.zeros_like(uprev_scr)
        hcar_scr[...] = jnp.zeros_like(hcar_scr)

    x = x_ref[0]
    gu = _dot(x.astype(BF16), win_ref[...])
    gate = gu[:, :w]
    u = gu[:, w:]
    full = jnp.concatenate([uprev_scr[...], u], axis=0)
    uprev_scr[...] = u[tm - SUBLANES:, :]
    cw = cw_ref[...]
    uc = full[SUBLANES - 3:SUBLANES - 3 + tm] * cw[0:1]
    uc = uc + full[SUBLANES - 2:SUBLANES - 2 + tm] * cw[1:2]
    uc = uc + full[SUBLANES - 1:SUBLANES - 1 + tm] * cw[2:3]
    uc = uc + u * cw[3:4]
    uc = uc + cb_ref[...]

    ucb = uc.astype(BF16)
    rs, igs = [], []
    for h in range(LRU_BLOCKS):
        blk = ucb[:, h * LRU_BLOCK_W:(h + 1) * LRU_BLOCK_W]
        rs.append(_dot(blk, wr_ref[h]))
        igs.append(_dot(blk, wi_ref[h]))
    r = _sigmoid(jnp.concatenate(rs, axis=-1) + br_ref[...])
    ig = _sigmoid(jnp.concatenate(igs, axis=-1) + bi_ref[...])
    z = -lam_ref[...]
    softplus = jnp.maximum(z, 0.0) + jnp.log1p(jnp.exp(-jnp.abs(z)))
    log_a = (-LRU_C * r) * softplus
    a_scr[...] = jnp.exp(log_a)
    th = jnp.tanh(log_a)
    b_scr[...] = jnp.sqrt((-2.0 * th) / (1.0 - th)) * (ig * uc)

    row = lax.broadcasted_iota(jnp.int32, (SUBLANES, w), 0)

    def group(gi, hprev):
        r0 = pl.multiple_of(gi * SUBLANES, SUBLANES)
        a = a_scr[pl.ds(r0, SUBLANES), :]
        bb = b_scr[pl.ds(r0, SUBLANES), :]
        for d in (1, 2, 4):
            keep = row >= d
            a_s = jnp.where(keep, pltpu.roll(a, d, 0), 1.0)
            b_s = jnp.where(keep, pltpu.roll(bb, d, 0), 0.0)
            bb = a * b_s + bb
            a = a * a_s
        h = a * hprev + bb
        h_scr[pl.ds(r0, SUBLANES), :] = h
        return jnp.broadcast_to(h[SUBLANES - 1:SUBLANES, :], (SUBLANES, w))

    hlast = lax.fori_loop(0, tm // SUBLANES, group, hcar_scr[...])
    hcar_scr[...] = hlast

    c0 = math.sqrt(2.0 / math.pi)
    gelu = gate * (0.5 * (1.0 + jnp.tanh(c0 * (gate + 0.044715 * (gate * gate * gate)))))
    y = _dot((h_scr[...] * gelu).astype(BF16), wo_ref[...])
    o_ref[0] = _layer_norm(ALPHA * x + y, g_ref[...], b_ref[...])


def _rglru_layer(x, win, cw, cb, wr, br, wi, bi, lam, wo, g, b):
    bsz, s, d = x.shape
    tm = LRU_TM
    w = LRU_WIDTH
    return pl.pallas_call(
        _rglru_kernel,
        grid=(bsz, s // tm),
        in_specs=[pl.BlockSpec((1, tm, d), lambda bi_, i: (bi_, i, 0)),
                  _resident((d, 2 * w)), _resident((LRU_CONV, w)), _resident((1, w)),
                  _resident((LRU_BLOCKS, LRU_BLOCK_W, LRU_BLOCK_W)), _resident((1, w)),
                  _resident((LRU_BLOCKS, LRU_BLOCK_W, LRU_BLOCK_W)), _resident((1, w)),
                  _resident((1, w)), _resident((w, d)), _resident((1, d)), _resident((1, d))],
        out_specs=pl.BlockSpec((1, tm, d), lambda bi_, i: (bi_, i, 0)),
        out_shape=jax.ShapeDtypeStruct((bsz, s, d), F32),
        scratch_shapes=[pltpu.VMEM((SUBLANES, w), F32), pltpu.VMEM((SUBLANES, w), F32),
                        pltpu.VMEM((tm, w), F32), pltpu.VMEM((tm, w), F32), pltpu.VMEM((tm, w), F32)],
        compiler_params=_params(("arbitrary", "arbitrary")),
        name="rglru_layer",
    )(x, win, cw, cb, wr, br, wi, bi, lam, wo, g, b)


MLA_TM = 512
MLA_TQ = 1024
MLA_HALF = MLA_TQ // 2
MLA_TK = 512
C_DOWN_PAD = 768
MLA_Q_SCALE = (C_NOPE + C_ROPE) ** -0.5 * math.log2(math.e)


def _mla_proj_kernel(x_ref, cos_ref, sin_ref, wd_ref, qn_ref, kvn_ref, wuq_ref, wukv_ref,
                     q_ref, k_ref, v_ref):
    tm = MLA_TM
    xb = x_ref[0].astype(BF16)
    c = _dot(xb, wd_ref[...])
    cq = c[:, :C_Q_RANK]
    ckv = c[:, C_Q_RANK:C_Q_RANK + C_KV_RANK]
    kr_slab = c[:, C_Q_RANK + C_KV_RANK:]
    cq = cq * lax.rsqrt(jnp.mean(cq * cq, axis=-1, keepdims=True) + RMS_EPS) * qn_ref[...]
    ckv = ckv * lax.rsqrt(jnp.mean(ckv * ckv, axis=-1, keepdims=True) + RMS_EPS) * kvn_ref[...]
    q = _dot(cq.astype(BF16), wuq_ref[...]) * MLA_Q_SCALE
    kv = _dot(ckv.astype(BF16), wukv_ref[...])

    cos = cos_ref[...]
    sin = sin_ref[...]
    lane = lax.broadcasted_iota(jnp.int32, (tm, LANES), 1)
    first_half = (lane & 32) == 0
    low = lane < 64
    nn = C_HEADS * C_NOPE

    kr = _rope_slab(kr_slab, cos, sin, first_half)
    kr_pair = (kr.astype(BF16), pltpu.roll(kr, 64, 1).astype(BF16))
    for pair in range(C_HEADS // 2):
        qr = _rope_slab(q[:, nn + pair * LANES:nn + (pair + 1) * LANES], cos, sin, first_half)
        for half in range(2):
            h = 2 * pair + half
            qr_h = jnp.where(low, qr, 0.0) if half == 0 else jnp.where(low, 0.0, qr)
            q_ref[0, h, :, 0:C_NOPE] = q[:, h * C_NOPE:(h + 1) * C_NOPE].astype(BF16)
            q_ref[0, h, :, C_NOPE:C_QK_PAD] = qr_h.astype(BF16)
            k_ref[0, h, :, 0:C_NOPE] = kv[:, h * C_NOPE:(h + 1) * C_NOPE].astype(BF16)
            k_ref[0, h, :, C_NOPE:C_QK_PAD] = kr_pair[half]
            v_ref[0, h] = kv[:, nn + h * C_V:nn + (h + 1) * C_V].astype(BF16)


def _mla_proj(x, cos, sin, wd, qn, kvn, wuq, wukv):
    bsz, s, d = x.shape
    tm = MLA_TM
    qk_shape = jax.ShapeDtypeStruct((bsz, C_HEADS, s, C_QK_PAD), BF16)
    v_shape = jax.ShapeDtypeStruct((bsz, C_HEADS, s, C_V), BF16)
    return pl.pallas_call(
        _mla_proj_kernel,
        grid=(bsz, s // tm),
        in_specs=[pl.BlockSpec((1, tm, d), lambda bi, i: (bi, i, 0)),
                  pl.BlockSpec((tm, LANES), lambda bi, i: (i, 0)),
                  pl.BlockSpec((tm, LANES), lambda bi, i: (i, 0)),
                  _resident(wd.shape), _resident(qn.shape), _resident(kvn.shape),
                  _resident(wuq.shape), _resident(wukv.shape)],
        out_specs=[pl.BlockSpec((1, C_HEADS, tm, C_QK_PAD), lambda bi, i: (bi, 0, i, 0)),
                   pl.BlockSpec((1, C_HEADS, tm, C_QK_PAD), lambda bi, i: (bi, 0, i, 0)),
                   pl.BlockSpec((1, C_HEADS, tm, C_V), lambda bi, i: (bi, 0, i, 0))],
        out_shape=[qk_shape, qk_shape, v_shape],
        compiler_params=_params(("parallel", "parallel")),
        name="mla_proj",
    )(x, cos, sin, wd, qn, kvn, wuq, wukv)


def _mla_attn_kernel(q_ref, k_ref, v_ref, o_ref, m_scr, acc_scr, s_scr):
    i = pl.program_id(2)
    tk, hq = MLA_TK, MLA_HALF

    m_scr[...] = jnp.full_like(m_scr, -jnp.inf)
    acc_scr[...] = jnp.zeros_like(acc_scr)
    ones = jnp.ones((tk, LANES), BF16)

    def scores(slot, hf, j):
        k0 = pl.multiple_of(j * tk, tk)
        s_scr[slot, hf] = _dot_nt(q_ref[0, 0, hf * hq:(hf + 1) * hq, :], k_ref[0, 0, pl.ds(k0, tk), :])

    def consume(slot, hf, j, diagonal):
        k0 = pl.multiple_of(j * tk, tk)
        vv = jnp.concatenate([v_ref[0, 0, pl.ds(k0, tk), :], ones], axis=1)
        s = s_scr[slot, hf]
        if diagonal:
            qpos = lax.broadcasted_iota(jnp.int32, (hq, tk), 0)
            kpos = lax.broadcasted_iota(jnp.int32, (hq, tk), 1)
            s = jnp.where(kpos <= qpos, s, NEG)
        m_prev = m_scr[hf]
        m_next = jnp.maximum(m_prev, jnp.max(s, axis=-1, keepdims=True))
        p = jnp.exp2(s - jnp.tile(m_next, (1, tk // LANES)))
        corr = jnp.exp2(m_prev - m_next)
        acc_scr[hf] = jnp.tile(corr, (1, 2)) * acc_scr[hf] + _dot(p.astype(BF16), vv)
        m_scr[hf] = m_next

    scores(0, 0, 0)
    scores(0, 1, 0)

    def body(t, carry):
        j = 2 * t
        scores(1, 0, j + 1)
        scores(1, 1, j + 1)
        consume(0, 0, j, False)
        consume(0, 1, j, False)
        scores(0, 0, j + 2)
        scores(0, 1, j + 2)
        consume(1, 0, j + 1, False)
        consume(1, 1, j + 1, False)
        return carry

    lax.fori_loop(0, i, body, 0)
    scores(1, 1, 2 * i + 1)
    consume(0, 0, 2 * i, True)
    consume(0, 1, 2 * i, False)
    consume(1, 1, 2 * i + 1, True)
    for hf in range(2):
        acc = acc_scr[hf]
        o_ref[0, hf * hq:(hf + 1) * hq, :] = (acc[:, :C_V] / acc[:, C_V:]).astype(BF16)


def _mla_attn(q, k, v):
    bsz, nh, s, _ = q.shape
    tq = MLA_TQ
    return pl.pallas_call(
        _mla_attn_kernel,
        grid=(bsz, nh, s // tq),
        in_specs=[pl.BlockSpec((1, 1, tq, C_QK_PAD), lambda b, h, i: (b, h, i, 0)),
                  pl.BlockSpec((1, 1, s, C_QK_PAD), lambda b, h, i: (b, h, 0, 0)),
                  pl.BlockSpec((1, 1, s, C_V), lambda b, h, i: (b, h, 0, 0))],
        out_specs=pl.BlockSpec((1, tq, C_V), lambda b, h, i: (b, i, h)),
        out_shape=jax.ShapeDtypeStruct((bsz, s, nh * C_V), BF16),
        scratch_shapes=[pltpu.VMEM((2, MLA_HALF, LANES), F32),
                        pltpu.VMEM((2, MLA_HALF, 2 * C_V), F32),
                        pltpu.VMEM((2, 2, MLA_HALF, MLA_TK), F32)],
        compiler_params=_params(("parallel", "parallel", "arbitrary")),
        name="mla_attn",
    )(q, k, v)


X_TM = 512


def _xattn_kernel(x_ref, wq_ref, mkv_ref, wo_ref, g_ref, b_ref, o_ref):
    tm = X_TM
    x = x_ref[0]
    q = (_dot(x.astype(BF16), wq_ref[...]) * (X_HEAD_DIM ** -0.5)).astype(BF16)
    heads = [(h * X_HEAD_DIM, (h + 1) * X_HEAD_DIM) for h in range(X_HEADS)]
    scores = [_dot_nt(q[:, lo:hi], mkv_ref[0, :, lo:hi]) for lo, hi in heads]
    probs, inv_denoms = [], []
    for s in scores:
        p = jnp.exp(s - jnp.max(s, axis=-1, keepdims=True))
        inv_denoms.append(1.0 / jnp.sum(p, axis=-1, keepdims=True))
        probs.append(p.astype(BF16))
    outs = [(_dot(probs[h], mkv_ref[0, :, D_MODEL + lo:D_MODEL + hi]) * inv_denoms[h]).astype(BF16)
            for h, (lo, hi) in enumerate(heads)]
    o = jnp.concatenate(outs, axis=-1)
    for r0 in range(0, tm, tm // 2):
        rows = slice(r0, r0 + tm // 2)
        y = _dot(o[rows], wo_ref[...])
        o_ref[0, rows, :] = _layer_norm(ALPHA * x[rows] + y, g_ref[...], b_ref[...])


def _xattn_layer(x, wq, mkv, wo, g, b):
    bsz, s, d = x.shape
    tm = X_TM
    return pl.pallas_call(
        _xattn_kernel,
        grid=(bsz, s // tm),
        in_specs=[pl.BlockSpec((1, tm, d), lambda bi, i: (bi, i, 0)),
                  _resident((d, d)),
                  pl.BlockSpec((1, MEM_LEN, 2 * d), lambda bi, i: (bi, 0, 0)),
                  _resident((d, d)), _resident((1, d)), _resident((1, d))],
        out_specs=pl.BlockSpec((1, tm, d), lambda bi, i: (bi, i, 0)),
        out_shape=jax.ShapeDtypeStruct((bsz, s, d), F32),
        compiler_params=_params(("parallel", "parallel")),
        name="xattn_layer",
    )(x, wq, mkv, wo, g, b)


FFN_TM = 512
FFN_DOWN_GROUPS = (4, 8, FFN_NCHUNK)


def _ffn_kernel(x_ref, wup_ref, cw_ref, cb_ref, wdn_ref, g_ref, b_ref, o_ref, hprev_scr, act_scr,
                acc_scr):
    i = pl.program_id(1)
    tm = FFN_TM

    @pl.when(i == 0)
    def _():
        hprev_scr[...] = jnp.zeros_like(hprev_scr)

    x = x_ref[0]
    xb = x.astype(BF16)

    def up_conv(col0):
        cols = slice(col0, col0 + FFN_CHUNK)
        h = _dot(xb, wup_ref[:, cols])
        full = jnp.concatenate([hprev_scr[:, cols], h], axis=0)
        hprev_scr[:, cols] = h[tm - SUBLANES:, :]
        cw = cw_ref[:, cols]
        y = full[SUBLANES - 2:SUBLANES - 2 + tm] * cw[0:1]
        y = y + full[SUBLANES - 1:SUBLANES - 1 + tm] * cw[1:2]
        y = y + h * cw[2:3]
        return y + cb_ref[:, cols]

    def down(rows, c0, c1):
        ks = slice(c0 * FFN_CHUNK, c1 * FFN_CHUNK)
        return _dot(act_scr[rows, ks], wdn_ref[ks, :])

    everything = slice(0, tm)
    for c in range(FFN_NCHUNK):
        gt = up_conv(c * FFN_CHUNK)
        val = up_conv(D_FF + c * FFN_CHUNK)
        act_scr[:, c * FFN_CHUNK:(c + 1) * FFN_CHUNK] = ((gt * _sigmoid(gt)) * val).astype(BF16)
        if c + 1 in FFN_DOWN_GROUPS[:-1]:
            gi = FFN_DOWN_GROUPS.index(c + 1)
            if gi == 0:
                acc_scr[...] = down(everything, 0, c + 1)
            else:
                acc_scr[...] += down(everything, FFN_DOWN_GROUPS[gi - 1], c + 1)
    for r0 in range(0, tm, tm // 2):
        rows = slice(r0, r0 + tm // 2)
        out = acc_scr[rows, :] + down(rows, FFN_DOWN_GROUPS[-2], FFN_DOWN_GROUPS[-1])
        o_ref[0, rows, :] = _layer_norm(ALPHA * x[rows] + out, g_ref[...], b_ref[...])


def _ffn_layer(x, wup, cw, cb, wdn, g, b):
    bsz, s, d = x.shape
    tm = FFN_TM
    return pl.pallas_call(
        _ffn_kernel,
        grid=(bsz, s // tm),
        in_specs=[pl.BlockSpec((1, tm, d), lambda bi, i: (bi, i, 0)),
                  _resident((d, 2 * D_FF)), _resident((FFN_CONV, 2 * D_FF)), _resident((1, 2 * D_FF)),
                  _resident((D_FF, d)), _resident((1, d)), _resident((1, d))],
        out_specs=pl.BlockSpec((1, tm, d), lambda bi, i: (bi, i, 0)),
        out_shape=jax.ShapeDtypeStruct((bsz, s, d), F32),
        scratch_shapes=[pltpu.VMEM((SUBLANES, 2 * D_FF), F32), pltpu.VMEM((tm, D_FF), BF16),
                        pltpu.VMEM((tm, d), F32)],
        compiler_params=_params(("arbitrary", "arbitrary")),
        name="ffn_layer",
    )(x, wup, cw, cb, wdn, g, b)


def kernel(x, mem, a_w_qkv, a_sinks, a_w_o, b_w_in, b_conv_w, b_conv_b, b_w_rgate, b_b_rgate, b_w_igate, b_b_igate, b_lambda, b_w_o, c_w_down, c_q_norm, c_kv_norm, c_w_uq, c_w_ukv, c_w_o, mem_w_kv, x_w_q, x_w_o, f_w_up, f_conv_w, f_conv_b, f_w_down, ln_g, ln_b):
    bsz, s, d = x.shape
    cos, sin = _rope_tables(s)

    mkv = _matmul(mem.reshape(bsz * MEM_LEN, d), mem_w_kv.astype(BF16), 512, BF16)
    mkv = mkv.reshape(bsz, MEM_LEN, 2 * d)

    def row(v):
        return v.reshape(1, -1)

    for i in range(DEPTH):
        kind, j = i % N_MIXERS, i // N_MIXERS
        g0, b0 = row(ln_g[i, 0]), row(ln_b[i, 0])
        if kind == 0:
            x = _swa_layer(x, a_sinks[j], cos, sin, a_w_qkv[j].astype(BF16), a_w_o[j].astype(BF16), g0, b0)
        elif kind == 1:
            x = _rglru_layer(x, b_w_in[j].astype(BF16), b_conv_w[j], row(b_conv_b[j]),
                             b_w_rgate[j].astype(BF16), row(b_b_rgate[j]),
                             b_w_igate[j].astype(BF16), row(b_b_igate[j]),
                             row(b_lambda[j]), b_w_o[j].astype(BF16), g0, b0)
        else:
            wd = jnp.pad(c_w_down[j], ((0, 0), (0, C_DOWN_PAD - c_w_down.shape[-1]))).astype(BF16)
            wuq = c_w_uq[j].reshape(C_Q_RANK, C_HEADS, C_NOPE + C_ROPE)
            wuq = jnp.concatenate([wuq[:, :, :C_NOPE].reshape(C_Q_RANK, -1),
                                   wuq[:, :, C_NOPE:].reshape(C_Q_RANK, -1)], axis=-1).astype(BF16)
            wukv = c_w_ukv[j].reshape(C_KV_RANK, C_HEADS, C_NOPE + C_V)
            wukv = jnp.concatenate([wukv[:, :, :C_NOPE].reshape(C_KV_RANK, -1),
                                    wukv[:, :, C_NOPE:].reshape(C_KV_RANK, -1)], axis=-1).astype(BF16)
            q, k, v = _mla_proj(x, cos, sin, wd, row(c_q_norm[j]), row(c_kv_norm[j]), wuq, wukv)
            o = _mla_attn(q, k, v)
            x = _proj_ln(o.reshape(bsz * s, d), x.reshape(bsz * s, d), c_w_o[j].astype(BF16), g0, b0)
            x = x.reshape(bsz, s, d)
        x = _xattn_layer(x, x_w_q[i].astype(BF16), mkv, x_w_o[i].astype(BF16),
                         row(ln_g[i, 1]), row(ln_b[i, 1]))
        x = _ffn_layer(x, f_w_up[i].astype(BF16), f_conv_w[i], row(f_conv_b[i]), f_w_down[i].astype(BF16),
                       row(ln_g[i, 2]), row(ln_b[i, 2]))
    return x
```

```python
import functools
import math

import jax
import jax.numpy as jnp
import numpy as np
from jax import lax
from jax.experimental import pallas as pl
from jax.experimental.pallas import tpu as pltpu

D_MODEL = 1024
DEPTH = 4
N_MIXERS = 3
MEM_LEN = 256
BLOCK = 128
ROPE_THETA = 10000.0
NEG = -1e30
LN_EPS = 1e-5
RMS_EPS = 1e-6

A_HEADS = 16
A_KV_HEADS = 4
A_HEAD_DIM = 64

LRU_WIDTH = D_MODEL
LRU_BLOCKS = 4
LRU_BLOCK_W = LRU_WIDTH // LRU_BLOCKS
LRU_CONV = 4
LRU_C = 8.0

C_HEADS = 8
C_NOPE = 128
C_ROPE = 64
C_V = 128
C_Q_RANK = 384
C_KV_RANK = 256
C_QK_PAD = 256

X_HEADS = 4
X_HEAD_DIM = D_MODEL // X_HEADS

D_FF = 2816
FFN_CONV = 3
FFN_CHUNK = 256
FFN_NCHUNK = D_FF // FFN_CHUNK

ALPHA = (2.0 * DEPTH) ** 0.25

LANES = 128
SUBLANES = 8
VMEM_LIMIT = 56 * 1024 * 1024

BF16 = jnp.bfloat16
F32 = jnp.float32

NT_DIMS = (((1,), (1,)), ((), ()))


def _dot(a, b):
    return jnp.dot(a, b, preferred_element_type=F32)


def _dot_nt(a, b):
    return lax.dot_general(a, b, NT_DIMS, preferred_element_type=F32)


def _layer_norm(z, g, b):
    mu = jnp.mean(z, axis=-1, keepdims=True)
    zc = z - mu
    var = jnp.mean(zc * zc, axis=-1, keepdims=True)
    return zc * lax.rsqrt(var + LN_EPS) * g + b


def _sigmoid(z):
    return 0.5 * jnp.tanh(0.5 * z) + 0.5


def _resident(shape):
    nd = len(shape)
    return pl.BlockSpec(shape, lambda *_: (0,) * nd, pipeline_mode=pl.Buffered(1))


def _params(sem, flags=None):
    return pltpu.CompilerParams(dimension_semantics=sem, vmem_limit_bytes=VMEM_LIMIT, flags=flags)


def _matmul_kernel(a_ref, w_ref, o_ref):
    o_ref[...] = _dot(a_ref[...].astype(BF16), w_ref[...]).astype(o_ref.dtype)


def _matmul(a, w, tm, out_dtype):
    t, k = a.shape
    n = w.shape[1]
    return pl.pallas_call(
        _matmul_kernel,
        grid=(t // tm,),
        in_specs=[pl.BlockSpec((tm, k), lambda i: (i, 0)), _resident((k, n))],
        out_specs=pl.BlockSpec((tm, n), lambda i: (i, 0)),
        out_shape=jax.ShapeDtypeStruct((t, n), out_dtype),
        compiler_params=_params(("parallel",)),
        name="matmul",
    )(a, w)


def _proj_ln_kernel(a_ref, r_ref, w_ref, g_ref, b_ref, o_ref):
    y = _dot(a_ref[...], w_ref[...])
    o_ref[...] = _layer_norm(ALPHA * r_ref[...] + y, g_ref[...], b_ref[...])


def _proj_ln(a, resid, w, g, b, tm=512):
    t, k = a.shape
    d = w.shape[1]
    return pl.pallas_call(
        _proj_ln_kernel,
        grid=(t // tm,),
        in_specs=[pl.BlockSpec((tm, k), lambda i: (i, 0)),
                  pl.BlockSpec((tm, d), lambda i: (i, 0)),
                  _resident((k, d)), _resident((1, d)), _resident((1, d))],
        out_specs=pl.BlockSpec((tm, d), lambda i: (i, 0)),
        out_shape=jax.ShapeDtypeStruct((t, d), F32),
        compiler_params=_params(("parallel",)),
        name="proj_ln",
    )(a, resid, w, g, b)


def _rope_slab(slab, cos, sin_signed, first_half):
    rot = jnp.where(first_half, pltpu.roll(slab, LANES - 32, 1), pltpu.roll(slab, 32, 1))
    return slab * cos + rot * sin_signed


def _rope_tables(seq):
    inv = 1.0 / (ROPE_THETA ** (jnp.arange(0, A_HEAD_DIM, 2, dtype=F32) / A_HEAD_DIM))
    ang = jnp.arange(seq, dtype=F32)[:, None] * inv[None, :]
    cos, sin = jnp.cos(ang), jnp.sin(ang)
    return (jnp.concatenate([cos, cos, cos, cos], axis=-1),
            jnp.concatenate([-sin, sin, -sin, sin], axis=-1))


SWA_TM = 512
SWA_NBLK = SWA_TM // BLOCK


def _swa_kernel(sinks_ref, x_ref, cos_ref, sin_ref, wqkv_ref, wo_ref, g_ref, b_ref, o_ref,
                kext_scr, vext_scr, q_scr, o_scr):
    i = pl.program_id(1)
    tm = SWA_TM

    @pl.when(i == 0)
    def _():
        kext_scr[:, 0:BLOCK, :] = jnp.zeros((2 * A_KV_HEADS, BLOCK, LANES), BF16)
        vext_scr[:, 0:BLOCK, :] = jnp.zeros((2 * A_KV_HEADS, BLOCK, LANES), BF16)

    @pl.when(i > 0)
    def _():
        kext_scr[:, 0:BLOCK, :] = kext_scr[:, tm:tm + BLOCK, :]
        vext_scr[:, 0:BLOCK, :] = vext_scr[:, tm:tm + BLOCK, :]

    x = x_ref[0]
    qkv = _dot(x.astype(BF16), wqkv_ref[...])
    cos = cos_ref[...]
    sin = sin_ref[...]
    lane = lax.broadcasted_iota(jnp.int32, (tm, LANES), 1)
    first_half = (lane & 32) == 0
    low = lane < 64

    nq = A_HEADS * A_HEAD_DIM
    nkv = A_KV_HEADS * A_HEAD_DIM
    for j in range(nq // LANES):
        slab = _rope_slab(qkv[:, j * LANES:(j + 1) * LANES], cos, sin, first_half)
        q_scr[:, j * LANES:(j + 1) * LANES] = (slab * (A_HEAD_DIM ** -0.5)).astype(BF16)

    for m in range(nkv // LANES):
        kslab = _rope_slab(qkv[:, nq + m * LANES:nq + (m + 1) * LANES], cos, sin, first_half)
        vslab = qkv[:, nq + nkv + m * LANES:nq + nkv + (m + 1) * LANES]
        for slab, ext in ((kslab, kext_scr), (vslab, vext_scr)):
            swapped = pltpu.roll(slab, 64, 1)
            h0, h1 = 2 * m, 2 * m + 1
            ext[2 * h0, BLOCK:BLOCK + tm, :] = jnp.where(low, slab, 0.0).astype(BF16)
            ext[2 * h0 + 1, BLOCK:BLOCK + tm, :] = jnp.where(low, 0.0, swapped).astype(BF16)
            ext[2 * h1, BLOCK:BLOCK + tm, :] = jnp.where(low, swapped, 0.0).astype(BF16)
            ext[2 * h1 + 1, BLOCK:BLOCK + tm, :] = jnp.where(low, 0.0, slab).astype(BF16)

    qi = lax.broadcasted_iota(jnp.int32, (2 * BLOCK, 4 * BLOCK), 0) & (BLOCK - 1)
    kj = lax.broadcasted_iota(jnp.int32, (2 * BLOCK, 4 * BLOCK), 1) & (2 * BLOCK - 1)
    dist = qi + BLOCK - kj
    band = (dist >= 0) & (dist < BLOCK)
    band_first = band & ((kj >= BLOCK) | (i > 0))
    top = lax.broadcasted_iota(jnp.int32, (2 * BLOCK, 1), 0) < BLOCK
    low2 = lax.broadcasted_iota(jnp.int32, (2 * BLOCK, LANES), 1) < 64

    def window(ext, h, r0):
        return jnp.concatenate([ext[2 * h, r0:r0 + 2 * BLOCK, :], ext[2 * h + 1, r0:r0 + 2 * BLOCK, :]],
                               axis=0)

    for n in range(SWA_NBLK):
        mask = band_first if n == 0 else band
        r0 = n * BLOCK
        scores = []
        for h in range(A_KV_HEADS):
            qs = jnp.concatenate([q_scr[r0:r0 + BLOCK, (2 * h) * LANES:(2 * h + 1) * LANES],
                                  q_scr[r0:r0 + BLOCK, (2 * h + 1) * LANES:(2 * h + 2) * LANES]], axis=0)
            scores.append(_dot_nt(qs, window(kext_scr, h, r0)))
        probs, inv_denoms = [], []
        for h in range(A_KV_HEADS):
            s = jnp.where(mask, scores[h], NEG)
            ps, rs = [], []
            for half in range(2):
                sh = s[:, half * 2 * BLOCK:(half + 1) * 2 * BLOCK]
                sink = jnp.where(top, sinks_ref[4 * h + half], sinks_ref[4 * h + 2 + half])
                mx = jnp.maximum(jnp.max(sh, axis=-1, keepdims=True), sink)
                p = jnp.exp(sh - mx)
                rs.append(1.0 / (jnp.sum(p, axis=-1, keepdims=True) + jnp.exp(sink - mx)))
                ps.append(p.astype(BF16))
            probs.append(jnp.concatenate(ps, axis=1))
            inv_denoms.append(jnp.where(low2, rs[0], rs[1]))
        for h in range(A_KV_HEADS):
            o = _dot(probs[h], window(vext_scr, h, r0)) * inv_denoms[h]
            o_scr[r0:r0 + BLOCK, (2 * h) * LANES:(2 * h + 1) * LANES] = o[:BLOCK].astype(BF16)
            o_scr[r0:r0 + BLOCK, (2 * h + 1) * LANES:(2 * h + 2) * LANES] = o[BLOCK:].astype(BF16)

    y = _dot(o_scr[...], wo_ref[...])
    o_ref[0] = _layer_norm(ALPHA * x + y, g_ref[...], b_ref[...])


def _swa_layer(x, sinks, cos, sin, wqkv, wo, g, b):
    bsz, s, d = x.shape
    tm = SWA_TM
    nqkv = wqkv.shape[1]
    return pl.pallas_call(
        _swa_kernel,
        grid=(bsz, s // tm),
        in_specs=[pl.BlockSpec(memory_space=pltpu.SMEM),
                  pl.BlockSpec((1, tm, d), lambda bi, i: (bi, i, 0)),
                  pl.BlockSpec((tm, LANES), lambda bi, i: (i, 0)),
                  pl.BlockSpec((tm, LANES), lambda bi, i: (i, 0)),
                  _resident((d, nqkv)), _resident((d, d)), _resident((1, d)), _resident((1, d))],
        out_specs=pl.BlockSpec((1, tm, d), lambda bi, i: (bi, i, 0)),
        out_shape=jax.ShapeDtypeStruct((bsz, s, d), F32),
        scratch_shapes=[pltpu.VMEM((2 * A_KV_HEADS, tm + BLOCK, LANES), BF16),
                        pltpu.VMEM((2 * A_KV_HEADS, tm + BLOCK, LANES), BF16),
                        pltpu.VMEM((tm, d), BF16),
                        pltpu.VMEM((tm, d), BF16)],
        compiler_params=_params(("arbitrary", "arbitrary")),
        name="swa_layer",
    )(sinks, x, cos, sin, wqkv, wo, g, b)


LRU_TM = 256
LRU_NLT = LRU_WIDTH // LANES
LRU_PAD = 2 * SUBLANES


def _rglru_kernel(x_ref, win_ref, cw_ref, cb_ref, wr_ref, br_ref, wi_ref, bi_ref, lam_ref, wo_ref,
                  g_ref, b_ref, o_ref, ustage_scr, a_scr, b_scr, hcar_scr):
    i = pl.program_id(1)
    tm = LRU_TM
    w = LRU_WIDTH
    pad = LRU_PAD
    sl = SUBLANES

    @pl.when(i == 0)
    def _():
        ustage_scr[:, 0:sl, :] = jnp.zeros((LRU_NLT, sl, LANES), F32)
        a_scr[:, :, 0:pad, :] = jnp.ones((3, LRU_NLT, pad, LANES), F32)
        b_scr[:, :, 0:pad, :] = jnp.zeros((3, LRU_NLT, pad, LANES), F32)
        hcar_scr[...] = jnp.zeros_like(hcar_scr)

    @pl.when(i > 0)
    def _():
        ustage_scr[:, 0:sl, :] = ustage_scr[:, tm:tm + sl, :]
        a_scr[0, :, sl:pad, :] = a_scr[0, :, tm + sl:tm + pad, :]
        b_scr[0, :, sl:pad, :] = b_scr[0, :, tm + sl:tm + pad, :]

    x = x_ref[0]
    gu = _dot(x.astype(BF16), win_ref[...])
    gate = gu[:, :w]
    u = gu[:, w:]
    cw = cw_ref[...]
    cb = cb_ref[...]
    ucs = []
    for lt in range(LRU_NLT):
        ls = slice(lt * LANES, (lt + 1) * LANES)
        us = ustage_scr.at[lt]
        us[sl:sl + tm, :] = u[:, ls]
        uc = us[sl - 3:sl - 3 + tm, :] * cw[0:1, ls]
        uc = uc + us[sl - 2:sl - 2 + tm, :] * cw[1:2, ls]
        uc = uc + us[sl - 1:sl - 1 + tm, :] * cw[2:3, ls]
        uc = uc + u[:, ls] * cw[3:4, ls]
        ucs.append(uc + cb[:, ls])
    uc = jnp.concatenate(ucs, axis=1)

    ucb = uc.astype(BF16)
    rs, igs = [], []
    for h in range(LRU_BLOCKS):
        blk = ucb[:, h * LRU_BLOCK_W:(h + 1) * LRU_BLOCK_W]
        rs.append(_dot(blk, wr_ref[h]))
        igs.append(_dot(blk, wi_ref[h]))
    r = _sigmoid(jnp.concatenate(rs, axis=-1) + br_ref[...])
    ig = _sigmoid(jnp.concatenate(igs, axis=-1) + bi_ref[...])
    z = -lam_ref[...]
    softplus = jnp.maximum(z, 0.0) + jnp.log1p(jnp.exp(-jnp.abs(z)))
    log_a = (-LRU_C * r) * softplus
    a0 = jnp.exp(log_a)
    th = jnp.tanh(log_a)
    b0 = jnp.sqrt((-2.0 * th) / (1.0 - th)) * (ig * uc)

    a8, b8 = [], []
    for lt in range(LRU_NLT):
        ls = slice(lt * LANES, (lt + 1) * LANES)
        a_scr[0, lt, pad:pad + tm, :] = a0[:, ls]
        b_scr[0, lt, pad:pad + tm, :] = b0[:, ls]
        for k, d in enumerate((1, 2)):
            a_cur = a_scr[k, lt, sl:pad + tm, :]
            b_cur = b_scr[k, lt, sl:pad + tm, :]
            a_scr[k + 1, lt, sl:pad + tm, :] = a_cur * a_scr[k, lt, sl - d:pad + tm - d, :]
            b_scr[k + 1, lt, sl:pad + tm, :] = a_cur * b_scr[k, lt, sl - d:pad + tm - d, :] + b_cur
        a_cur = a_scr[2, lt, pad:pad + tm, :]
        a8.append(a_cur * a_scr[2, lt, pad - 4:pad - 4 + tm, :])
        b8.append(a_cur * b_scr[2, lt, pad - 4:pad - 4 + tm, :] + b_scr[2, lt, pad:pad + tm, :])
    a8 = jnp.concatenate(a8, axis=1)
    b8 = jnp.concatenate(b8, axis=1)

    hblk = hcar_scr[...]
    hs = []
    for j in range(tm // sl):
        hblk = a8[j * sl:(j + 1) * sl] * hblk + b8[j * sl:(j + 1) * sl]
        hs.append(hblk)
    hcar_scr[...] = hblk
    h = jnp.concatenate(hs, axis=0)

    c0 = math.sqrt(2.0 / math.pi)
    gelu = gate * (0.5 * (1.0 + jnp.tanh(c0 * (gate + 0.044715 * (gate * gate * gate)))))
    y = _dot((h * gelu).astype(BF16), wo_ref[...])
    o_ref[0] = _layer_norm(ALPHA * x + y, g_ref[...], b_ref[...])


def _rglru_layer(x, win, cw, cb, wr, br, wi, bi, lam, wo, g, b):
    bsz, s, d = x.shape
    tm = LRU_TM
    w = LRU_WIDTH
    return pl.pallas_call(
        _rglru_kernel,
        grid=(bsz, s // tm),
        in_specs=[pl.BlockSpec((1, tm, d), lambda bi_, i: (bi_, i, 0)),
                  _resident((d, 2 * w)), _resident((LRU_CONV, w)), _resident((1, w)),
                  _resident((LRU_BLOCKS, LRU_BLOCK_W, LRU_BLOCK_W)), _resident((1, w)),
                  _resident((LRU_BLOCKS, LRU_BLOCK_W, LRU_BLOCK_W)), _resident((1, w)),
                  _resident((1, w)), _resident((w, d)), _resident((1, d)), _resident((1, d))],
        out_specs=pl.BlockSpec((1, tm, d), lambda bi_, i: (bi_, i, 0)),
        out_shape=jax.ShapeDtypeStruct((bsz, s, d), F32),
        scratch_shapes=[pltpu.VMEM((LRU_NLT, SUBLANES + tm, LANES), F32),
                        pltpu.VMEM((3, LRU_NLT, LRU_PAD + tm, LANES), F32),
                        pltpu.VMEM((3, LRU_NLT, LRU_PAD + tm, LANES), F32),
                        pltpu.VMEM((SUBLANES, w), F32)],
        compiler_params=_params(("arbitrary", "arbitrary")),
        name="rglru_layer",
    )(x, win, cw, cb, wr, br, wi, bi, lam, wo, g, b)


MLA_TM = 512
MLA_TQ = 1024
MLA_HALF = MLA_TQ // 2
MLA_TK = 512
C_DOWN_PAD = 768
MLA_Q_SCALE = (C_NOPE + C_ROPE) ** -0.5 * math.log2(math.e)


def _mla_proj_kernel(x_ref, cos_ref, sin_ref, wd_ref, qn_ref, kvn_ref, wuq_ref, wukv_ref,
                     q_ref, k_ref, v_ref):
    tm = MLA_TM
    xb = x_ref[0].astype(BF16)
    c = _dot(xb, wd_ref[...])
    cq = c[:, :C_Q_RANK]
    ckv = c[:, C_Q_RANK:C_Q_RANK + C_KV_RANK]
    kr_slab = c[:, C_Q_RANK + C_KV_RANK:]
    cq = cq * lax.rsqrt(jnp.mean(cq * cq, axis=-1, keepdims=True) + RMS_EPS) * qn_ref[...]
    ckv = ckv * lax.rsqrt(jnp.mean(ckv * ckv, axis=-1, keepdims=True) + RMS_EPS) * kvn_ref[...]
    q = _dot(cq.astype(BF16), wuq_ref[...]) * MLA_Q_SCALE
    kv = _dot(ckv.astype(BF16), wukv_ref[...])

    cos = cos_ref[...]
    sin = sin_ref[...]
    lane = lax.broadcasted_iota(jnp.int32, (tm, LANES), 1)
    first_half = (lane & 32) == 0
    low = lane < 64
    nn = C_HEADS * C_NOPE

    kr = _rope_slab(kr_slab, cos, sin, first_half)
    kr_pair = (kr.astype(BF16), pltpu.roll(kr, 64, 1).astype(BF16))
    for pair in range(C_HEADS // 2):
        qr = _rope_slab(q[:, nn + pair * LANES:nn + (pair + 1) * LANES], cos, sin, first_half)
        for half in range(2):
            h = 2 * pair + half
            qr_h = jnp.where(low, qr, 0.0) if half == 0 else jnp.where(low, 0.0, qr)
            q_ref[0, h, :, 0:C_NOPE] = q[:, h * C_NOPE:(h + 1) * C_NOPE].astype(BF16)
            q_ref[0, h, :, C_NOPE:C_QK_PAD] = qr_h.astype(BF16)
            k_ref[0, h, :, 0:C_NOPE] = kv[:, h * C_NOPE:(h + 1) * C_NOPE].astype(BF16)
            k_ref[0, h, :, C_NOPE:C_QK_PAD] = kr_pair[half]
            v_ref[0, h] = kv[:, nn + h * C_V:nn + (h + 1) * C_V].astype(BF16)


def _mla_proj(x, cos, sin, wd, qn, kvn, wuq, wukv):
    bsz, s, d = x.shape
    tm = MLA_TM
    qk_shape = jax.ShapeDtypeStruct((bsz, C_HEADS, s, C_QK_PAD), BF16)
    v_shape = jax.ShapeDtypeStruct((bsz, C_HEADS, s, C_V), BF16)
    return pl.pallas_call(
        _mla_proj_kernel,
        grid=(bsz, s // tm),
        in_specs=[pl.BlockSpec((1, tm, d), lambda bi, i: (bi, i, 0)),
                  pl.BlockSpec((tm, LANES), lambda bi, i: (i, 0)),
                  pl.BlockSpec((tm, LANES), lambda bi, i: (i, 0)),
                  _resident(wd.shape), _resident(qn.shape), _resident(kvn.shape),
                  _resident(wuq.shape), _resident(wukv.shape)],
        out_specs=[pl.BlockSpec((1, C_HEADS, tm, C_QK_PAD), lambda bi, i: (bi, 0, i, 0)),
                   pl.BlockSpec((1, C_HEADS, tm, C_QK_PAD), lambda bi, i: (bi, 0, i, 0)),
                   pl.BlockSpec((1, C_HEADS, tm, C_V), lambda bi, i: (bi, 0, i, 0))],
        out_shape=[qk_shape, qk_shape, v_shape],
        compiler_params=_params(("parallel", "parallel")),
        name="mla_proj",
    )(x, cos, sin, wd, qn, kvn, wuq, wukv)


def _mla_attn_kernel(q_ref, k_ref, v_ref, o_ref, m_scr, acc_scr, s_scr):
    i = pl.program_id(2)
    tk, hq = MLA_TK, MLA_HALF

    m_scr[...] = jnp.full_like(m_scr, -jnp.inf)
    acc_scr[...] = jnp.zeros_like(acc_scr)
    ones = jnp.ones((tk, LANES), BF16)

    def scores(slot, hf, j):
        k0 = pl.multiple_of(j * tk, tk)
        s_scr[slot, hf] = _dot_nt(q_ref[0, 0, hf * hq:(hf + 1) * hq, :], k_ref[0, 0, pl.ds(k0, tk), :])

    def consume(slot, hf, j, diagonal):
        k0 = pl.multiple_of(j * tk, tk)
        vv = jnp.concatenate([v_ref[0, 0, pl.ds(k0, tk), :], ones], axis=1)
        s = s_scr[slot, hf]
        if diagonal:
            qpos = lax.broadcasted_iota(jnp.int32, (hq, tk), 0)
            kpos = lax.broadcasted_iota(jnp.int32, (hq, tk), 1)
            s = jnp.where(kpos <= qpos, s, NEG)
        m_prev = m_scr[hf]
        m_next = jnp.maximum(m_prev, jnp.max(s, axis=-1, keepdims=True))
        p = jnp.exp2(s - jnp.tile(m_next, (1, tk // LANES)))
        corr = jnp.exp2(m_prev - m_next)
        acc_scr[hf] = jnp.tile(corr, (1, 2)) * acc_scr[hf] + _dot(p.astype(BF16), vv)
        m_scr[hf] = m_next

    scores(0, 0, 0)
    scores(0, 1, 0)

    def body(t, carry):
        j = 2 * t
        scores(1, 0, j + 1)
        scores(1, 1, j + 1)
        consume(0, 0, j, False)
        consume(0, 1, j, False)
        scores(0, 0, j + 2)
        scores(0, 1, j + 2)
        consume(1, 0, j + 1, False)
        consume(1, 1, j + 1, False)
        return carry

    lax.fori_loop(0, i, body, 0)
    scores(1, 1, 2 * i + 1)
    consume(0, 0, 2 * i, True)
    consume(0, 1, 2 * i, False)
    consume(1, 1, 2 * i + 1, True)
    for hf in range(2):
        acc = acc_scr[hf]
        o_ref[0, hf * hq:(hf + 1) * hq, :] = (acc[:, :C_V] / acc[:, C_V:]).astype(BF16)


def _mla_attn(q, k, v):
    bsz, nh, s, _ = q.shape
    tq = MLA_TQ
    return pl.pallas_call(
        _mla_attn_kernel,
        grid=(bsz, nh, s // tq),
        in_specs=[pl.BlockSpec((1, 1, tq, C_QK_PAD), lambda b, h, i: (b, h, i, 0)),
                  pl.BlockSpec((1, 1, s, C_QK_PAD), lambda b, h, i: (b, h, 0, 0)),
                  pl.BlockSpec((1, 1, s, C_V), lambda b, h, i: (b, h, 0, 0))],
        out_specs=pl.BlockSpec((1, tq, C_V), lambda b, h, i: (b, i, h)),
        out_shape=jax.ShapeDtypeStruct((bsz, s, nh * C_V), BF16),
        scratch_shapes=[pltpu.VMEM((2, MLA_HALF, LANES), F32),
                        pltpu.VMEM((2, MLA_HALF, 2 * C_V), F32),
                        pltpu.VMEM((2, 2, MLA_HALF, MLA_TK), F32)],
        compiler_params=_params(("parallel", "parallel", "arbitrary")),
        name="mla_attn",
    )(q, k, v)


X_TM = 512


def _xattn_kernel(x_ref, wq_ref, mkv_ref, wo_ref, g_ref, b_ref, o_ref):
    tm = X_TM
    x = x_ref[0]
    q = (_dot(x.astype(BF16), wq_ref[...]) * (X_HEAD_DIM ** -0.5)).astype(BF16)
    heads = [(h * X_HEAD_DIM, (h + 1) * X_HEAD_DIM) for h in range(X_HEADS)]
    scores = [_dot_nt(q[:, lo:hi], mkv_ref[0, :, lo:hi]) for lo, hi in heads]
    probs, inv_denoms = [], []
    for s in scores:
        p = jnp.exp(s - jnp.max(s, axis=-1, keepdims=True))
        inv_denoms.append(1.0 / jnp.sum(p, axis=-1, keepdims=True))
        probs.append(p.astype(BF16))
    outs = [(_dot(probs[h], mkv_ref[0, :, D_MODEL + lo:D_MODEL + hi]) * inv_denoms[h]).astype(BF16)
            for h, (lo, hi) in enumerate(heads)]
    o = jnp.concatenate(outs, axis=-1)
    for r0 in range(0, tm, tm // 2):
        rows = slice(r0, r0 + tm // 2)
        y = _dot(o[rows], wo_ref[...])
        o_ref[0, rows, :] = _layer_norm(ALPHA * x[rows] + y, g_ref[...], b_ref[...])


def _xattn_layer(x, wq, mkv, wo, g, b):
    bsz, s, d = x.shape
    tm = X_TM
    return pl.pallas_call(
        _xattn_kernel,
        grid=(bsz, s // tm),
        in_specs=[pl.BlockSpec((1, tm, d), lambda bi, i: (bi, i, 0)),
                  _resident((d, d)),
                  pl.BlockSpec((1, MEM_LEN, 2 * d), lambda bi, i: (bi, 0, 0)),
                  _resident((d, d)), _resident((1, d)), _resident((1, d))],
        out_specs=pl.BlockSpec((1, tm, d), lambda bi, i: (bi, i, 0)),
        out_shape=jax.ShapeDtypeStruct((bsz, s, d), F32),
        compiler_params=_params(("parallel", "parallel")),
        name="xattn_layer",
    )(x, wq, mkv, wo, g, b)


FFN_TM = 512
FFN_DOWN_GROUPS = (4, 8, FFN_NCHUNK)


def _ffn_kernel(x_ref, wup_ref, cw_ref, cb_ref, wdn_ref, g_ref, b_ref, o_ref, hprev_scr, act_scr,
                acc_scr, hstage_scr):
    i = pl.program_id(1)
    tm = FFN_TM

    @pl.when(i == 0)
    def _():
        hprev_scr[...] = jnp.zeros_like(hprev_scr)

    x = x_ref[0]
    xb = x.astype(BF16)

    def up_conv(col0, slot):
        cols = slice(col0, col0 + FFN_CHUNK)
        h = _dot(xb, wup_ref[:, cols])
        prev = hprev_scr[:, cols]
        hprev_scr[:, cols] = h[tm - SUBLANES:, :]
        cw = cw_ref[:, cols]
        cb = cb_ref[:, cols]
        outs = []
        for lt in range(FFN_CHUNK // LANES):
            ls = slice(lt * LANES, (lt + 1) * LANES)
            hs = hstage_scr.at[slot, lt]
            hs[0:SUBLANES, :] = prev[:, ls]
            hs[SUBLANES:SUBLANES + tm, :] = h[:, ls]
            y = hs[SUBLANES - 2:SUBLANES - 2 + tm, :] * cw[0:1, ls]
            y = y + hs[SUBLANES - 1:SUBLANES - 1 + tm, :] * cw[1:2, ls]
            y = y + h[:, ls] * cw[2:3, ls]
            outs.append(y + cb[:, ls])
        return jnp.concatenate(outs, axis=1)

    def down(rows, c0, c1):
        ks = slice(c0 * FFN_CHUNK, c1 * FFN_CHUNK)
        return _dot(act_scr[rows, ks], wdn_ref[ks, :])

    everything = slice(0, tm)
    for c in range(FFN_NCHUNK):
        gt = up_conv(c * FFN_CHUNK, 2 * (c % 2))
        val = up_conv(D_FF + c * FFN_CHUNK, 2 * (c % 2) + 1)
        act_scr[:, c * FFN_CHUNK:(c + 1) * FFN_CHUNK] = ((gt * _sigmoid(gt)) * val).astype(BF16)
        if c + 1 in FFN_DOWN_GROUPS[:-1]:
            gi = FFN_DOWN_GROUPS.index(c + 1)
            if gi == 0:
                acc_scr[...] = down(everything, 0, c + 1)
            else:
                acc_scr[...] += down(everything, FFN_DOWN_GROUPS[gi - 1], c + 1)
    for r0 in range(0, tm, tm // 2):
        rows = slice(r0, r0 + tm // 2)
        out = acc_scr[rows, :] + down(rows, FFN_DOWN_GROUPS[-2], FFN_DOWN_GROUPS[-1])
        o_ref[0, rows, :] = _layer_norm(ALPHA * x[rows] + out, g_ref[...], b_ref[...])


def _ffn_layer(x, wup, cw, cb, wdn, g, b):
    bsz, s, d = x.shape
    tm = FFN_TM
    return pl.pallas_call(
        _ffn_kernel,
        grid=(bsz, s // tm),
        in_specs=[pl.BlockSpec((1, tm, d), lambda bi, i: (bi, i, 0)),
                  _resident((d, 2 * D_FF)), _resident((FFN_CONV, 2 * D_FF)), _resident((1, 2 * D_FF)),
                  _resident((D_FF, d)), _resident((1, d)), _resident((1, d))],
        out_specs=pl.BlockSpec((1, tm, d), lambda bi, i: (bi, i, 0)),
        out_shape=jax.ShapeDtypeStruct((bsz, s, d), F32),
        scratch_shapes=[pltpu.VMEM((SUBLANES, 2 * D_FF), F32), pltpu.VMEM((tm, D_FF), BF16),
                        pltpu.VMEM((tm, d), F32),
                        pltpu.VMEM((4, FFN_CHUNK // LANES, SUBLANES + tm, LANES), F32)],
        compiler_params=_params(("arbitrary", "arbitrary")),
        name="ffn_layer",
    )(x, wup, cw, cb, wdn, g, b)


def kernel(x, mem, a_w_qkv, a_sinks, a_w_o, b_w_in, b_conv_w, b_conv_b, b_w_rgate, b_b_rgate, b_w_igate, b_b_igate, b_lambda, b_w_o, c_w_down, c_q_norm, c_kv_norm, c_w_uq, c_w_ukv, c_w_o, mem_w_kv, x_w_q, x_w_o, f_w_up, f_conv_w, f_conv_b, f_w_down, ln_g, ln_b):
    bsz, s, d = x.shape
    cos, sin = _rope_tables(s)

    mkv = _matmul(mem.reshape(bsz * MEM_LEN, d), mem_w_kv.astype(BF16), 512, BF16)
    mkv = mkv.reshape(bsz, MEM_LEN, 2 * d)

    def row(v):
        return v.reshape(1, -1)

    for i in range(DEPTH):
        kind, j = i % N_MIXERS, i // N_MIXERS
        g0, b0 = row(ln_g[i, 0]), row(ln_b[i, 0])
        if kind == 0:
            x = _swa_layer(x, a_sinks[j], cos, sin, a_w_qkv[j].astype(BF16), a_w_o[j].astype(BF16), g0, b0)
        elif kind == 1:
            x = _rglru_layer(x, b_w_in[j].astype(BF16), b_conv_w[j], row(b_conv_b[j]),
                             b_w_rgate[j].astype(BF16), row(b_b_rgate[j]),
                             b_w_igate[j].astype(BF16), row(b_b_igate[j]),
                             row(b_lambda[j]), b_w_o[j].astype(BF16), g0, b0)
        else:
            wd = jnp.pad(c_w_down[j], ((0, 0), (0, C_DOWN_PAD - c_w_down.shape[-1]))).astype(BF16)
            wuq = c_w_uq[j].reshape(C_Q_RANK, C_HEADS, C_NOPE + C_ROPE)
            wuq = jnp.concatenate([wuq[:, :, :C_NOPE].reshape(C_Q_RANK, -1),
                                   wuq[:, :, C_NOPE:].reshape(C_Q_RANK, -1)], axis=-1).astype(BF16)
            wukv = c_w_ukv[j].reshape(C_KV_RANK, C_HEADS, C_NOPE + C_V)
            wukv = jnp.concatenate([wukv[:, :, :C_NOPE].reshape(C_KV_RANK, -1),
                                    wukv[:, :, C_NOPE:].reshape(C_KV_RANK, -1)], axis=-1).astype(BF16)
            q, k, v = _mla_proj(x, cos, sin, wd, row(c_q_norm[j]), row(c_kv_norm[j]), wuq, wukv)
            o = _mla_attn(q, k, v)
            x = _proj_ln(o.reshape(bsz * s, d), x.reshape(bsz * s, d), c_w_o[j].astype(BF16), g0, b0)
            x = x.reshape(bsz, s, d)
        x = _xattn_layer(x, x_w_q[i].astype(BF16), mkv, x_w_o[i].astype(BF16),
                         row(ln_g[i, 1]), row(ln_b[i, 1]))
        x = _ffn_layer(x, f_w_up[i].astype(BF16), f_conv_w[i], row(f_conv_b[i]), f_w_down[i].astype(BF16),
                       row(ln_g[i, 2]), row(ln_b[i, 2]))
    return x
```

```python
import functools
import math

import jax
import jax.numpy as jnp
import numpy as np
from jax import lax
from jax.experimental import pallas as pl
from jax.experimental.pallas import tpu as pltpu

D_MODEL = 1024
DEPTH = 4
N_MIXERS = 3
MEM_LEN = 256
BLOCK = 128
ROPE_THETA = 10000.0
NEG = -1e30
LN_EPS = 1e-5
RMS_EPS = 1e-6

A_HEADS = 16
A_KV_HEADS = 4
A_HEAD_DIM = 64

LRU_WIDTH = D_MODEL
LRU_BLOCKS = 4
LRU_BLOCK_W = LRU_WIDTH // LRU_BLOCKS
LRU_CONV = 4
LRU_C = 8.0

C_HEADS = 8
C_NOPE = 128
C_ROPE = 64
C_V = 128
C_Q_RANK = 384
C_KV_RANK = 256
C_QK_PAD = 256

X_HEADS = 4
X_HEAD_DIM = D_MODEL // X_HEADS

D_FF = 2816
FFN_CONV = 3
FFN_CHUNK = 256
FFN_NCHUNK = D_FF // FFN_CHUNK

ALPHA = (2.0 * DEPTH) ** 0.25

LANES = 128
SUBLANES = 8
VMEM_LIMIT = 56 * 1024 * 1024

BF16 = jnp.bfloat16
F32 = jnp.float32

NT_DIMS = (((1,), (1,)), ((), ()))


def _dot(a, b):
    return jnp.dot(a, b, preferred_element_type=F32)


def _dot_nt(a, b):
    return lax.dot_general(a, b, NT_DIMS, preferred_element_type=F32)


def _layer_norm(z, g, b):
    mu = jnp.mean(z, axis=-1, keepdims=True)
    zc = z - mu
    var = jnp.mean(zc * zc, axis=-1, keepdims=True)
    return zc * lax.rsqrt(var + LN_EPS) * g + b


def _sigmoid(z):
    return 0.5 * jnp.tanh(0.5 * z) + 0.5


def _resident(shape):
    nd = len(shape)
    return pl.BlockSpec(shape, lambda *_: (0,) * nd, pipeline_mode=pl.Buffered(1))


def _params(sem, flags=None):
    return pltpu.CompilerParams(dimension_semantics=sem, vmem_limit_bytes=VMEM_LIMIT, flags=flags)


def _matmul_kernel(a_ref, w_ref, o_ref):
    o_ref[...] = _dot(a_ref[...].astype(BF16), w_ref[...]).astype(o_ref.dtype)


def _matmul(a, w, tm, out_dtype):
    t, k = a.shape
    n = w.shape[1]
    return pl.pallas_call(
        _matmul_kernel,
        grid=(t // tm,),
        in_specs=[pl.BlockSpec((tm, k), lambda i: (i, 0)), _resident((k, n))],
        out_specs=pl.BlockSpec((tm, n), lambda i: (i, 0)),
        out_shape=jax.ShapeDtypeStruct((t, n), out_dtype),
        compiler_params=_params(("parallel",)),
        name="matmul",
    )(a, w)


def _proj_ln_kernel(a_ref, r_ref, w_ref, g_ref, b_ref, o_ref):
    y = _dot(a_ref[...], w_ref[...])
    o_ref[...] = _layer_norm(ALPHA * r_ref[...] + y, g_ref[...], b_ref[...])


def _proj_ln(a, resid, w, g, b, tm=512):
    t, k = a.shape
    d = w.shape[1]
    return pl.pallas_call(
        _proj_ln_kernel,
        grid=(t // tm,),
        in_specs=[pl.BlockSpec((tm, k), lambda i: (i, 0)),
                  pl.BlockSpec((tm, d), lambda i: (i, 0)),
                  _resident((k, d)), _resident((1, d)), _resident((1, d))],
        out_specs=pl.BlockSpec((tm, d), lambda i: (i, 0)),
        out_shape=jax.ShapeDtypeStruct((t, d), F32),
        compiler_params=_params(("parallel",)),
        name="proj_ln",
    )(a, resid, w, g, b)


def _rope_slab(slab, cos, sin_signed, first_half):
    rot = jnp.where(first_half, pltpu.roll(slab, LANES - 32, 1), pltpu.roll(slab, 32, 1))
    return slab * cos + rot * sin_signed


def _rope_tables(seq):
    inv = 1.0 / (ROPE_THETA ** (jnp.arange(0, A_HEAD_DIM, 2, dtype=F32) / A_HEAD_DIM))
    ang = jnp.arange(seq, dtype=F32)[:, None] * inv[None, :]
    cos, sin = jnp.cos(ang), jnp.sin(ang)
    return (jnp.concatenate([cos, cos, cos, cos], axis=-1),
            jnp.concatenate([-sin, sin, -sin, sin], axis=-1))


SWA_TM = 512
SWA_NBLK = SWA_TM // BLOCK


def _swa_kernel(sinks_ref, x_ref, cos_ref, sin_ref, wqkv_ref, wo_ref, g_ref, b_ref, o_ref,
                kext_scr, vext_scr, q_scr, o_scr):
    i = pl.program_id(1)
    tm = SWA_TM

    @pl.when(i == 0)
    def _():
        kext_scr[:, 0:BLOCK, :] = jnp.zeros((2 * A_KV_HEADS, BLOCK, LANES), BF16)
        vext_scr[:, 0:BLOCK, :] = jnp.zeros((2 * A_KV_HEADS, BLOCK, LANES), BF16)

    @pl.when(i > 0)
    def _():
        kext_scr[:, 0:BLOCK, :] = kext_scr[:, tm:tm + BLOCK, :]
        vext_scr[:, 0:BLOCK, :] = vext_scr[:, tm:tm + BLOCK, :]

    x = x_ref[0]
    qkv = _dot(x.astype(BF16), wqkv_ref[...])
    cos = cos_ref[...]
    sin = sin_ref[...]
    lane = lax.broadcasted_iota(jnp.int32, (tm, LANES), 1)
    first_half = (lane & 32) == 0
    low = lane < 64

    nq = A_HEADS * A_HEAD_DIM
    nkv = A_KV_HEADS * A_HEAD_DIM
    for j in range(nq // LANES):
        slab = _rope_slab(qkv[:, j * LANES:(j + 1) * LANES], cos, sin, first_half)
        q_scr[:, j * LANES:(j + 1) * LANES] = (slab * (A_HEAD_DIM ** -0.5)).astype(BF16)

    for m in range(nkv // LANES):
        kslab = _rope_slab(qkv[:, nq + m * LANES:nq + (m + 1) * LANES], cos, sin, first_half)
        vslab = qkv[:, nq + nkv + m * LANES:nq + nkv + (m + 1) * LANES]
        for slab, ext in ((kslab, kext_scr), (vslab, vext_scr)):
            swapped = pltpu.roll(slab, 64, 1)
            h0, h1 = 2 * m, 2 * m + 1
            ext[2 * h0, BLOCK:BLOCK + tm, :] = jnp.where(low, slab, 0.0).astype(BF16)
            ext[2 * h0 + 1, BLOCK:BLOCK + tm, :] = jnp.where(low, 0.0, swapped).astype(BF16)
            ext[2 * h1, BLOCK:BLOCK + tm, :] = jnp.where(low, swapped, 0.0).astype(BF16)
            ext[2 * h1 + 1, BLOCK:BLOCK + tm, :] = jnp.where(low, 0.0, slab).astype(BF16)

    qi = lax.broadcasted_iota(jnp.int32, (2 * BLOCK, 4 * BLOCK), 0) & (BLOCK - 1)
    kj = lax.broadcasted_iota(jnp.int32, (2 * BLOCK, 4 * BLOCK), 1) & (2 * BLOCK - 1)
    dist = qi + BLOCK - kj
    band = (dist >= 0) & (dist < BLOCK)
    band_first = band & ((kj >= BLOCK) | (i > 0))
    top = lax.broadcasted_iota(jnp.int32, (2 * BLOCK, 1), 0) < BLOCK
    low2 = lax.broadcasted_iota(jnp.int32, (2 * BLOCK, LANES), 1) < 64

    def window(ext, h, r0):
        return jnp.concatenate([ext[2 * h, r0:r0 + 2 * BLOCK, :], ext[2 * h + 1, r0:r0 + 2 * BLOCK, :]],
                               axis=0)

    for n in range(SWA_NBLK):
        mask = band_first if n == 0 else band
        r0 = n * BLOCK
        scores = []
        for h in range(A_KV_HEADS):
            qs = jnp.concatenate([q_scr[r0:r0 + BLOCK, (2 * h) * LANES:(2 * h + 1) * LANES],
                                  q_scr[r0:r0 + BLOCK, (2 * h + 1) * LANES:(2 * h + 2) * LANES]], axis=0)
            scores.append(_dot_nt(qs, window(kext_scr, h, r0)))
        probs, inv_denoms = [], []
        for h in range(A_KV_HEADS):
            s = jnp.where(mask, scores[h], NEG)
            ps, rs = [], []
            for half in range(2):
                sh = s[:, half * 2 * BLOCK:(half + 1) * 2 * BLOCK]
                sink = jnp.where(top, sinks_ref[4 * h + half], sinks_ref[4 * h + 2 + half])
                mx = jnp.maximum(jnp.max(sh, axis=-1, keepdims=True), sink)
                p = jnp.exp(sh - mx)
                rs.append(1.0 / (jnp.sum(p, axis=-1, keepdims=True) + jnp.exp(sink - mx)))
                ps.append(p.astype(BF16))
            probs.append(jnp.concatenate(ps, axis=1))
            inv_denoms.append(jnp.where(low2, rs[0], rs[1]))
        for h in range(A_KV_HEADS):
            o = _dot(probs[h], window(vext_scr, h, r0)) * inv_denoms[h]
            o_scr[r0:r0 + BLOCK, (2 * h) * LANES:(2 * h + 1) * LANES] = o[:BLOCK].astype(BF16)
            o_scr[r0:r0 + BLOCK, (2 * h + 1) * LANES:(2 * h + 2) * LANES] = o[BLOCK:].astype(BF16)

    y = _dot(o_scr[...], wo_ref[...])
    o_ref[0] = _layer_norm(ALPHA * x + y, g_ref[...], b_ref[...])


def _swa_layer(x, sinks, cos, sin, wqkv, wo, g, b):
    bsz, s, d = x.shape
    tm = SWA_TM
    nqkv = wqkv.shape[1]
    return pl.pallas_call(
        _swa_kernel,
        grid=(bsz, s // tm),
        in_specs=[pl.BlockSpec(memory_space=pltpu.SMEM),
                  pl.BlockSpec((1, tm, d), lambda bi, i: (bi, i, 0)),
                  pl.BlockSpec((tm, LANES), lambda bi, i: (i, 0)),
                  pl.BlockSpec((tm, LANES), lambda bi, i: (i, 0)),
                  _resident((d, nqkv)), _resident((d, d)), _resident((1, d)), _resident((1, d))],
        out_specs=pl.BlockSpec((1, tm, d), lambda bi, i: (bi, i, 0)),
        out_shape=jax.ShapeDtypeStruct((bsz, s, d), F32),
        scratch_shapes=[pltpu.VMEM((2 * A_KV_HEADS, tm + BLOCK, LANES), BF16),
                        pltpu.VMEM((2 * A_KV_HEADS, tm + BLOCK, LANES), BF16),
                        pltpu.VMEM((tm, d), BF16),
                        pltpu.VMEM((tm, d), BF16)],
        compiler_params=_params(("arbitrary", "arbitrary")),
        name="swa_layer",
    )(sinks, x, cos, sin, wqkv, wo, g, b)


LRU_TM = 256
LRU_NLT = LRU_WIDTH // LANES
LRU_PAD = 2 * SUBLANES


def _rglru_kernel(x_ref, win_ref, cw_ref, cb_ref, wr_ref, br_ref, wi_ref, bi_ref, lam_ref, wo_ref,
                  g_ref, b_ref, o_ref, ustage_scr, a_scr, b_scr, hcar_scr):
    i = pl.program_id(1)
    tm = LRU_TM
    w = LRU_WIDTH
    pad = LRU_PAD
    sl = SUBLANES

    @pl.when(i == 0)
    def _():
        ustage_scr[:, 0:sl, :] = jnp.zeros((LRU_NLT, sl, LANES), F32)
        a_scr[:, :, 0:pad, :] = jnp.ones((3, LRU_NLT, pad, LANES), F32)
        b_scr[:, :, 0:pad, :] = jnp.zeros((3, LRU_NLT, pad, LANES), F32)
        hcar_scr[...] = jnp.zeros_like(hcar_scr)

    @pl.when(i > 0)
    def _():
        ustage_scr[:, 0:sl, :] = ustage_scr[:, tm:tm + sl, :]
        a_scr[0, :, sl:pad, :] = a_scr[0, :, tm + sl:tm + pad, :]
        b_scr[0, :, sl:pad, :] = b_scr[0, :, tm + sl:tm + pad, :]

    x = x_ref[0]
    gu = _dot(x.astype(BF16), win_ref[...])
    gate = gu[:, :w]
    u = gu[:, w:]
    cw = cw_ref[...]
    cb = cb_ref[...]
    ucs = []
    for lt in range(LRU_NLT):
        ls = slice(lt * LANES, (lt + 1) * LANES)
        us = ustage_scr.at[lt]
        us[sl:sl + tm, :] = u[:, ls]
        uc = us[sl - 3:sl - 3 + tm, :] * cw[0:1, ls]
        uc = uc + us[sl - 2:sl - 2 + tm, :] * cw[1:2, ls]
        uc = uc + us[sl - 1:sl - 1 + tm, :] * cw[2:3, ls]
        uc = uc + u[:, ls] * cw[3:4, ls]
        ucs.append(uc + cb[:, ls])
    uc = jnp.concatenate(ucs, axis=1)

    ucb = uc.astype(BF16)
    rs, igs = [], []
    for h in range(LRU_BLOCKS):
        blk = ucb[:, h * LRU_BLOCK_W:(h + 1) * LRU_BLOCK_W]
        rs.append(_dot(blk, wr_ref[h]))
        igs.append(_dot(blk, wi_ref[h]))
    r = _sigmoid(jnp.concatenate(rs, axis=-1) + br_ref[...])
    ig = _sigmoid(jnp.concatenate(igs, axis=-1) + bi_ref[...])
    z = -lam_ref[...]
    softplus = jnp.maximum(z, 0.0) + jnp.log1p(jnp.exp(-jnp.abs(z)))
    log_a = (-LRU_C * r) * softplus
    a0 = jnp.exp(log_a)
    th = jnp.tanh(log_a)
    b0 = jnp.sqrt((-2.0 * th) / (1.0 - th)) * (ig * uc)

    a8, b8 = [], []
    for lt in range(LRU_NLT):
        ls = slice(lt * LANES, (lt + 1) * LANES)
        a_scr[0, lt, pad:pad + tm, :] = a0[:, ls]
        b_scr[0, lt, pad:pad + tm, :] = b0[:, ls]
        for k, d in enumerate((1, 2)):
            a_cur = a_scr[k, lt, sl:pad + tm, :]
            b_cur = b_scr[k, lt, sl:pad + tm, :]
            a_scr[k + 1, lt, sl:pad + tm, :] = a_cur * a_scr[k, lt, sl - d:pad + tm - d, :]
            b_scr[k + 1, lt, sl:pad + tm, :] = a_cur * b_scr[k, lt, sl - d:pad + tm - d, :] + b_cur
        a_cur = a_scr[2, lt, pad:pad + tm, :]
        a8.append(a_cur * a_scr[2, lt, pad - 4:pad - 4 + tm, :])
        b8.append(a_cur * b_scr[2, lt, pad - 4:pad - 4 + tm, :] + b_scr[2, lt, pad:pad + tm, :])
    a8 = jnp.concatenate(a8, axis=1)
    b8 = jnp.concatenate(b8, axis=1)

    hblk = hcar_scr[...]
    hs = []
    for j in range(tm // sl):
        hblk = a8[j * sl:(j + 1) * sl] * hblk + b8[j * sl:(j + 1) * sl]
        hs.append(hblk)
    hcar_scr[...] = hblk
    h = jnp.concatenate(hs, axis=0)

    c0 = math.sqrt(2.0 / math.pi)
    gelu = gate * (0.5 * (1.0 + jnp.tanh(c0 * (gate + 0.044715 * (gate * gate * gate)))))
    y = _dot((h * gelu).astype(BF16), wo_ref[...])
    o_ref[0] = _layer_norm(ALPHA * x + y, g_ref[...], b_ref[...])


def _rglru_layer(x, win, cw, cb, wr, br, wi, bi, lam, wo, g, b):
    bsz, s, d = x.shape
    tm = LRU_TM
    w = LRU_WIDTH
    return pl.pallas_call(
        _rglru_kernel,
        grid=(bsz, s // tm),
        in_specs=[pl.BlockSpec((1, tm, d), lambda bi_, i: (bi_, i, 0)),
                  _resident((d, 2 * w)), _resident((LRU_CONV, w)), _resident((1, w)),
                  _resident((LRU_BLOCKS, LRU_BLOCK_W, LRU_BLOCK_W)), _resident((1, w)),
                  _resident((LRU_BLOCKS, LRU_BLOCK_W, LRU_BLOCK_W)), _resident((1, w)),
                  _resident((1, w)), _resident((w, d)), _resident((1, d)), _resident((1, d))],
        out_specs=pl.BlockSpec((1, tm, d), lambda bi_, i: (bi_, i, 0)),
        out_shape=jax.ShapeDtypeStruct((bsz, s, d), F32),
        scratch_shapes=[pltpu.VMEM((LRU_NLT, SUBLANES + tm, LANES), F32),
                        pltpu.VMEM((3, LRU_NLT, LRU_PAD + tm, LANES), F32),
                        pltpu.VMEM((3, LRU_NLT, LRU_PAD + tm, LANES), F32),
                        pltpu.VMEM((SUBLANES, w), F32)],
        compiler_params=_params(("arbitrary", "arbitrary")),
        name="rglru_layer",
    )(x, win, cw, cb, wr, br, wi, bi, lam, wo, g, b)


MLA_TM = 512
MLA_TQ = 1024
MLA_HALF = MLA_TQ // 2
MLA_TK = 512
C_DOWN_PAD = 768
MLA_Q_SCALE = (C_NOPE + C_ROPE) ** -0.5 * math.log2(math.e)


def _mla_proj_kernel(x_ref, cos_ref, sin_ref, wd_ref, qn_ref, kvn_ref, wuq_ref, wuk_ref, wuvt_ref,
                     q_ref, k_ref, vt_ref):
    tm = MLA_TM
    xb = x_ref[0].astype(BF16)
    c = _dot(xb, wd_ref[...])
    cq = c[:, :C_Q_RANK]
    ckv = c[:, C_Q_RANK:C_Q_RANK + C_KV_RANK]
    kr_slab = c[:, C_Q_RANK + C_KV_RANK:]
    cq = cq * lax.rsqrt(jnp.mean(cq * cq, axis=-1, keepdims=True) + RMS_EPS) * qn_ref[...]
    ckv = ckv * lax.rsqrt(jnp.mean(ckv * ckv, axis=-1, keepdims=True) + RMS_EPS) * kvn_ref[...]
    q = _dot(cq.astype(BF16), wuq_ref[...]) * MLA_Q_SCALE
    kn = _dot(ckv.astype(BF16), wuk_ref[...])
    ckv_t = ckv.T.astype(BF16)

    cos = cos_ref[...]
    sin = sin_ref[...]
    lane = lax.broadcasted_iota(jnp.int32, (tm, LANES), 1)
    first_half = (lane & 32) == 0
    low = lane < 64
    nn = C_HEADS * C_NOPE

    kr = _rope_slab(kr_slab, cos, sin, first_half)
    kr_pair = (kr.astype(BF16), pltpu.roll(kr, 64, 1).astype(BF16))
    for pair in range(C_HEADS // 2):
        qr = _rope_slab(q[:, nn + pair * LANES:nn + (pair + 1) * LANES], cos, sin, first_half)
        for half in range(2):
            h = 2 * pair + half
            qr_h = jnp.where(low, qr, 0.0) if half == 0 else jnp.where(low, 0.0, qr)
            q_ref[0, h, :, 0:C_NOPE] = q[:, h * C_NOPE:(h + 1) * C_NOPE].astype(BF16)
            q_ref[0, h, :, C_NOPE:C_QK_PAD] = qr_h.astype(BF16)
            k_ref[0, h, :, 0:C_NOPE] = kn[:, h * C_NOPE:(h + 1) * C_NOPE].astype(BF16)
            k_ref[0, h, :, C_NOPE:C_QK_PAD] = kr_pair[half]
            vt_ref[0, h, 0] = _dot(wuvt_ref[h], ckv_t).astype(BF16)


def _mla_proj(x, cos, sin, wd, qn, kvn, wuq, wuk, wuvt):
    bsz, s, d = x.shape
    tm = MLA_TM
    assert tm == MLA_TK
    qk_shape = jax.ShapeDtypeStruct((bsz, C_HEADS, s, C_QK_PAD), BF16)
    vt_shape = jax.ShapeDtypeStruct((bsz, C_HEADS, s // tm, C_V, tm), BF16)
    return pl.pallas_call(
        _mla_proj_kernel,
        grid=(bsz, s // tm),
        in_specs=[pl.BlockSpec((1, tm, d), lambda bi, i: (bi, i, 0)),
                  pl.BlockSpec((tm, LANES), lambda bi, i: (i, 0)),
                  pl.BlockSpec((tm, LANES), lambda bi, i: (i, 0)),
                  _resident(wd.shape), _resident(qn.shape), _resident(kvn.shape),
                  _resident(wuq.shape), _resident(wuk.shape), _resident(wuvt.shape)],
        out_specs=[pl.BlockSpec((1, C_HEADS, tm, C_QK_PAD), lambda bi, i: (bi, 0, i, 0)),
                   pl.BlockSpec((1, C_HEADS, tm, C_QK_PAD), lambda bi, i: (bi, 0, i, 0)),
                   pl.BlockSpec((1, C_HEADS, 1, C_V, tm), lambda bi, i: (bi, 0, i, 0, 0))],
        out_shape=[qk_shape, qk_shape, vt_shape],
        compiler_params=_params(("parallel", "parallel")),
        name="mla_proj",
    )(x, cos, sin, wd, qn, kvn, wuq, wuk, wuvt)


def _mla_attn_kernel(q_ref, k_ref, vt_ref, o_ref, m_scr, l_scr, acc_scr, s_scr):
    i = pl.program_id(2)
    tk, hq = MLA_TK, MLA_HALF

    m_scr[...] = jnp.full_like(m_scr, -jnp.inf)
    l_scr[...] = jnp.zeros_like(l_scr)
    acc_scr[...] = jnp.zeros_like(acc_scr)

    def scores(slot, hf, j):
        k0 = pl.multiple_of(j * tk, tk)
        s_scr[slot, hf] = _dot_nt(k_ref[0, 0, pl.ds(k0, tk), :], q_ref[0, 0, hf * hq:(hf + 1) * hq, :])

    def consume(slot, hf, j, diagonal):
        s = s_scr[slot, hf]
        if diagonal:
            kpos = lax.broadcasted_iota(jnp.int32, (tk, hq), 0)
            qpos = lax.broadcasted_iota(jnp.int32, (tk, hq), 1)
            s = jnp.where(kpos <= qpos, s, NEG)
        m_prev = m_scr[hf]
        m_next = jnp.maximum(m_prev, jnp.max(s, axis=0, keepdims=True))
        p = jnp.exp2(s - m_next)
        corr = jnp.exp2(m_prev - m_next)
        l_scr[hf] = corr * l_scr[hf] + jnp.sum(p, axis=0, keepdims=True)
        acc_scr[hf] = corr * acc_scr[hf] + _dot(vt_ref[0, 0, j], p.astype(BF16))
        m_scr[hf] = m_next

    scores(0, 0, 0)
    scores(0, 1, 0)

    def body(t, carry):
        j = 2 * t
        scores(1, 0, j + 1)
        scores(1, 1, j + 1)
        consume(0, 0, j, False)
        consume(0, 1, j, False)
        scores(0, 0, j + 2)
        scores(0, 1, j + 2)
        consume(1, 0, j + 1, False)
        consume(1, 1, j + 1, False)
        return carry

    lax.fori_loop(0, i, body, 0)
    scores(1, 1, 2 * i + 1)
    consume(0, 0, 2 * i, True)
    consume(0, 1, 2 * i, False)
    consume(1, 1, 2 * i + 1, True)
    for hf in range(2):
        o_t = acc_scr[hf] / l_scr[hf]
        o_ref[0, hf * hq:(hf + 1) * hq, :] = o_t.T.astype(BF16)


def _mla_attn(q, k, vt):
    bsz, nh, s, _ = q.shape
    tq = MLA_TQ
    return pl.pallas_call(
        _mla_attn_kernel,
        grid=(bsz, nh, s // tq),
        in_specs=[pl.BlockSpec((1, 1, tq, C_QK_PAD), lambda b, h, i: (b, h, i, 0)),
                  pl.BlockSpec((1, 1, s, C_QK_PAD), lambda b, h, i: (b, h, 0, 0)),
                  pl.BlockSpec((1, 1, s // MLA_TK, C_V, MLA_TK), lambda b, h, i: (b, h, 0, 0, 0))],
        out_specs=pl.BlockSpec((1, tq, C_V), lambda b, h, i: (b, i, h)),
        out_shape=jax.ShapeDtypeStruct((bsz, s, nh * C_V), BF16),
        scratch_shapes=[pltpu.VMEM((2, 1, MLA_HALF), F32),
                        pltpu.VMEM((2, 1, MLA_HALF), F32),
                        pltpu.VMEM((2, C_V, MLA_HALF), F32),
                        pltpu.VMEM((2, 2, MLA_TK, MLA_HALF), F32)],
        compiler_params=_params(("parallel", "parallel", "arbitrary")),
        name="mla_attn",
    )(q, k, vt)


X_TM = 512


def _xattn_kernel(x_ref, wq_ref, mkv_ref, wo_ref, g_ref, b_ref, o_ref):
    tm = X_TM
    x = x_ref[0]
    q = (_dot(x.astype(BF16), wq_ref[...]) * (X_HEAD_DIM ** -0.5)).astype(BF16)
    heads = [(h * X_HEAD_DIM, (h + 1) * X_HEAD_DIM) for h in range(X_HEADS)]
    scores = [_dot_nt(q[:, lo:hi], mkv_ref[0, :, lo:hi]) for lo, hi in heads]
    probs, inv_denoms = [], []
    for s in scores:
        p = jnp.exp(s - jnp.max(s, axis=-1, keepdims=True))
        inv_denoms.append(1.0 / jnp.sum(p, axis=-1, keepdims=True))
        probs.append(p.astype(BF16))
    outs = [(_dot(probs[h], mkv_ref[0, :, D_MODEL + lo:D_MODEL + hi]) * inv_denoms[h]).astype(BF16)
            for h, (lo, hi) in enumerate(heads)]
    o = jnp.concatenate(outs, axis=-1)
    for r0 in range(0, tm, tm // 2):
        rows = slice(r0, r0 + tm // 2)
        y = _dot(o[rows], wo_ref[...])
        o_ref[0, rows, :] = _layer_norm(ALPHA * x[rows] + y, g_ref[...], b_ref[...])


def _xattn_layer(x, wq, mkv, wo, g, b):
    bsz, s, d = x.shape
    tm = X_TM
    return pl.pallas_call(
        _xattn_kernel,
        grid=(bsz, s // tm),
        in_specs=[pl.BlockSpec((1, tm, d), lambda bi, i: (bi, i, 0)),
                  _resident((d, d)),
                  pl.BlockSpec((1, MEM_LEN, 2 * d), lambda bi, i: (bi, 0, 0)),
                  _resident((d, d)), _resident((1, d)), _resident((1, d))],
        out_specs=pl.BlockSpec((1, tm, d), lambda bi, i: (bi, i, 0)),
        out_shape=jax.ShapeDtypeStruct((bsz, s, d), F32),
        compiler_params=_params(("parallel", "parallel")),
        name="xattn_layer",
    )(x, wq, mkv, wo, g, b)


FFN_TM = 512
FFN_DOWN_GROUPS = (4, 8, FFN_NCHUNK)


def _ffn_kernel(x_ref, wup_ref, cw_ref, cb_ref, wdn_ref, g_ref, b_ref, o_ref, hprev_scr, act_scr,
                acc_scr, hstage_scr):
    i = pl.program_id(1)
    tm = FFN_TM

    @pl.when(i == 0)
    def _():
        hprev_scr[...] = jnp.zeros_like(hprev_scr)

    x = x_ref[0]
    xb = x.astype(BF16)

    def up_conv(col0, slot):
        cols = slice(col0, col0 + FFN_CHUNK)
        h = _dot(xb, wup_ref[:, cols])
        prev = hprev_scr[:, cols]
        hprev_scr[:, cols] = h[tm - SUBLANES:, :]
        cw = cw_ref[:, cols]
        cb = cb_ref[:, cols]
        outs = []
        for lt in range(FFN_CHUNK // LANES):
            ls = slice(lt * LANES, (lt + 1) * LANES)
            hs = hstage_scr.at[slot, lt]
            hs[0:SUBLANES, :] = prev[:, ls]
            hs[SUBLANES:SUBLANES + tm, :] = h[:, ls]
            y = hs[SUBLANES - 2:SUBLANES - 2 + tm, :] * cw[0:1, ls]
            y = y + hs[SUBLANES - 1:SUBLANES - 1 + tm, :] * cw[1:2, ls]
            y = y + h[:, ls] * cw[2:3, ls]
            outs.append(y + cb[:, ls])
        return jnp.concatenate(outs, axis=1)

    def down(rows, c0, c1):
        ks = slice(c0 * FFN_CHUNK, c1 * FFN_CHUNK)
        return _dot(act_scr[rows, ks], wdn_ref[ks, :])

    everything = slice(0, tm)
    for c in range(FFN_NCHUNK):
        gt = up_conv(c * FFN_CHUNK, 2 * (c % 2))
        val = up_conv(D_FF + c * FFN_CHUNK, 2 * (c % 2) + 1)
        act_scr[:, c * FFN_CHUNK:(c + 1) * FFN_CHUNK] = ((gt * _sigmoid(gt)) * val).astype(BF16)
        if c + 1 in FFN_DOWN_GROUPS[:-1]:
            gi = FFN_DOWN_GROUPS.index(c + 1)
            if gi == 0:
                acc_scr[...] = down(everything, 0, c + 1)
            else:
                acc_scr[...] += down(everything, FFN_DOWN_GROUPS[gi - 1], c + 1)
    for r0 in range(0, tm, tm // 2):
        rows = slice(r0, r0 + tm // 2)
        out = acc_scr[rows, :] + down(rows, FFN_DOWN_GROUPS[-2], FFN_DOWN_GROUPS[-1])
        o_ref[0, rows, :] = _layer_norm(ALPHA * x[rows] + out, g_ref[...], b_ref[...])


def _ffn_layer(x, wup, cw, cb, wdn, g, b):
    bsz, s, d = x.shape
    tm = FFN_TM
    return pl.pallas_call(
        _ffn_kernel,
        grid=(bsz, s // tm),
        in_specs=[pl.BlockSpec((1, tm, d), lambda bi, i: (bi, i, 0)),
                  _resident((d, 2 * D_FF)), _resident((FFN_CONV, 2 * D_FF)), _resident((1, 2 * D_FF)),
                  _resident((D_FF, d)), _resident((1, d)), _resident((1, d))],
        out_specs=pl.BlockSpec((1, tm, d), lambda bi, i: (bi, i, 0)),
        out_shape=jax.ShapeDtypeStruct((bsz, s, d), F32),
        scratch_shapes=[pltpu.VMEM((SUBLANES, 2 * D_FF), F32), pltpu.VMEM((tm, D_FF), BF16),
                        pltpu.VMEM((tm, d), F32),
                        pltpu.VMEM((4, FFN_CHUNK // LANES, SUBLANES + tm, LANES), F32)],
        compiler_params=_params(("arbitrary", "arbitrary")),
        name="ffn_layer",
    )(x, wup, cw, cb, wdn, g, b)


def kernel(x, mem, a_w_qkv, a_sinks, a_w_o, b_w_in, b_conv_w, b_conv_b, b_w_rgate, b_b_rgate, b_w_igate, b_b_igate, b_lambda, b_w_o, c_w_down, c_q_norm, c_kv_norm, c_w_uq, c_w_ukv, c_w_o, mem_w_kv, x_w_q, x_w_o, f_w_up, f_conv_w, f_conv_b, f_w_down, ln_g, ln_b):
    bsz, s, d = x.shape
    cos, sin = _rope_tables(s)

    mkv = _matmul(mem.reshape(bsz * MEM_LEN, d), mem_w_kv.astype(BF16), 512, BF16)
    mkv = mkv.reshape(bsz, MEM_LEN, 2 * d)

    def row(v):
        return v.reshape(1, -1)

    for i in range(DEPTH):
        kind, j = i % N_MIXERS, i // N_MIXERS
        g0, b0 = row(ln_g[i, 0]), row(ln_b[i, 0])
        if kind == 0:
            x = _swa_layer(x, a_sinks[j], cos, sin, a_w_qkv[j].astype(BF16), a_w_o[j].astype(BF16), g0, b0)
        elif kind == 1:
            x = _rglru_layer(x, b_w_in[j].astype(BF16), b_conv_w[j], row(b_conv_b[j]),
                             b_w_rgate[j].astype(BF16), row(b_b_rgate[j]),
                             b_w_igate[j].astype(BF16), row(b_b_igate[j]),
                             row(b_lambda[j]), b_w_o[j].astype(BF16), g0, b0)
        else:
            wd = jnp.pad(c_w_down[j], ((0, 0), (0, C_DOWN_PAD - c_w_down.shape[-1]))).astype(BF16)
            wuq = c_w_uq[j].reshape(C_Q_RANK, C_HEADS, C_NOPE + C_ROPE)
            wuq = jnp.concatenate([wuq[:, :, :C_NOPE].reshape(C_Q_RANK, -1),
                                   wuq[:, :, C_NOPE:].reshape(C_Q_RANK, -1)], axis=-1).astype(BF16)
            wukv = c_w_ukv[j].reshape(C_KV_RANK, C_HEADS, C_NOPE + C_V)
            wuk = wukv[:, :, :C_NOPE].reshape(C_KV_RANK, -1).astype(BF16)
            wuvt = jnp.transpose(wukv[:, :, C_NOPE:], (1, 2, 0)).astype(BF16)
            q, k, vt = _mla_proj(x, cos, sin, wd, row(c_q_norm[j]), row(c_kv_norm[j]), wuq, wuk, wuvt)
            o = _mla_attn(q, k, vt)
            x = _proj_ln(o.reshape(bsz * s, d), x.reshape(bsz * s, d), c_w_o[j].astype(BF16), g0, b0)
            x = x.reshape(bsz, s, d)
        x = _xattn_layer(x, x_w_q[i].astype(BF16), mkv, x_w_o[i].astype(BF16),
                         row(ln_g[i, 1]), row(ln_b[i, 1]))
        x = _ffn_layer(x, f_w_up[i].astype(BF16), f_conv_w[i], row(f_conv_b[i]), f_w_down[i].astype(BF16),
                       row(ln_g[i, 2]), row(ln_b[i, 2]))
    return x
```

```python
import functools
import math

import jax
import jax.numpy as jnp
import numpy as np
from jax import lax
from jax.experimental import pallas as pl
from jax.experimental.pallas import tpu as pltpu

D_MODEL = 1024
DEPTH = 4
N_MIXERS = 3
MEM_LEN = 256
BLOCK = 128
ROPE_THETA = 10000.0
NEG = -1e30
LN_EPS = 1e-5
RMS_EPS = 1e-6

A_HEADS = 16
A_KV_HEADS = 4
A_HEAD_DIM = 64

LRU_WIDTH = D_MODEL
LRU_BLOCKS = 4
LRU_BLOCK_W = LRU_WIDTH // LRU_BLOCKS
LRU_CONV = 4
LRU_C = 8.0

C_HEADS = 8
C_NOPE = 128
C_ROPE = 64
C_V = 128
C_Q_RANK = 384
C_KV_RANK = 256
C_QK_PAD = 256

X_HEADS = 4
X_HEAD_DIM = D_MODEL // X_HEADS

D_FF = 2816
FFN_CONV = 3
FFN_CHUNK = 256
FFN_NCHUNK = D_FF // FFN_CHUNK

ALPHA = (2.0 * DEPTH) ** 0.25

LANES = 128
SUBLANES = 8
VMEM_LIMIT = 56 * 1024 * 1024

BF16 = jnp.bfloat16
F32 = jnp.float32

NT_DIMS = (((1,), (1,)), ((), ()))
LOG2E = math.log2(math.e)


def _dot(a, b):
    return jnp.dot(a, b, preferred_element_type=F32)


def _dot_nt(a, b):
    return lax.dot_general(a, b, NT_DIMS, preferred_element_type=F32)


def _layer_norm(z, g, b):
    mu = jnp.mean(z, axis=-1, keepdims=True)
    zc = z - mu
    var = jnp.mean(zc * zc, axis=-1, keepdims=True)
    return zc * lax.rsqrt(var + LN_EPS) * g + b


def _sigmoid(z):
    return 0.5 * jnp.tanh(0.5 * z) + 0.5


def _resident(shape):
    nd = len(shape)
    return pl.BlockSpec(shape, lambda *_: (0,) * nd, pipeline_mode=pl.Buffered(1))


def _params(sem, flags=None):
    return pltpu.CompilerParams(dimension_semantics=sem, vmem_limit_bytes=VMEM_LIMIT, flags=flags)


def _matmul_kernel(a_ref, w_ref, o_ref):
    o_ref[...] = _dot(a_ref[...].astype(BF16), w_ref[...]).astype(o_ref.dtype)


def _matmul(a, w, tm, out_dtype):
    t, k = a.shape
    n = w.shape[1]
    return pl.pallas_call(
        _matmul_kernel,
        grid=(t // tm,),
        in_specs=[pl.BlockSpec((tm, k), lambda i: (i, 0)), _resident((k, n))],
        out_specs=pl.BlockSpec((tm, n), lambda i: (i, 0)),
        out_shape=jax.ShapeDtypeStruct((t, n), out_dtype),
        compiler_params=_params(("parallel",)),
        name="matmul",
    )(a, w)


def _rope_slab(slab, cos, sin_signed, first_half):
    rot = jnp.where(first_half, pltpu.roll(slab, LANES - 32, 1), pltpu.roll(slab, 32, 1))
    return slab * cos + rot * sin_signed


def _rope_tables(seq):
    inv = 1.0 / (ROPE_THETA ** (jnp.arange(0, A_HEAD_DIM, 2, dtype=F32) / A_HEAD_DIM))
    ang = jnp.arange(seq, dtype=F32)[:, None] * inv[None, :]
    cos, sin = jnp.cos(ang), jnp.sin(ang)
    return (jnp.concatenate([cos, cos, cos, cos], axis=-1),
            jnp.concatenate([-sin, sin, -sin, sin], axis=-1))


SWA_TM = 512
SWA_NBLK = SWA_TM // BLOCK


def _swa_kernel(sinks_ref, x_ref, cos_ref, sin_ref, wqkv_ref, wo_ref, g_ref, b_ref, o_ref,
                kext_scr, vext_scr, q_scr, o_scr):
    i = pl.program_id(1)
    tm = SWA_TM

    @pl.when(i == 0)
    def _():
        kext_scr[:, 0:BLOCK, :] = jnp.zeros((2 * A_KV_HEADS, BLOCK, LANES), BF16)
        vext_scr[:, 0:BLOCK, :] = jnp.zeros((2 * A_KV_HEADS, BLOCK, LANES), BF16)

    @pl.when(i > 0)
    def _():
        kext_scr[:, 0:BLOCK, :] = kext_scr[:, tm:tm + BLOCK, :]
        vext_scr[:, 0:BLOCK, :] = vext_scr[:, tm:tm + BLOCK, :]

    x = x_ref[0]
    qkv = _dot(x.astype(BF16), wqkv_ref[...])
    cos = cos_ref[...]
    sin = sin_ref[...]
    lane = lax.broadcasted_iota(jnp.int32, (tm, LANES), 1)
    first_half = (lane & 32) == 0
    low = lane < 64

    nq = A_HEADS * A_HEAD_DIM
    nkv = A_KV_HEADS * A_HEAD_DIM
    for j in range(nq // LANES):
        slab = _rope_slab(qkv[:, j * LANES:(j + 1) * LANES], cos, sin, first_half)
        q_scr[:, j * LANES:(j + 1) * LANES] = (slab * (A_HEAD_DIM ** -0.5 * LOG2E)).astype(BF16)

    for m in range(nkv // LANES):
        kslab = _rope_slab(qkv[:, nq + m * LANES:nq + (m + 1) * LANES], cos, sin, first_half)
        vslab = qkv[:, nq + nkv + m * LANES:nq + nkv + (m + 1) * LANES]
        for slab, ext in ((kslab, kext_scr), (vslab, vext_scr)):
            swapped = pltpu.roll(slab, 64, 1)
            h0, h1 = 2 * m, 2 * m + 1
            ext[2 * h0, BLOCK:BLOCK + tm, :] = jnp.where(low, slab, 0.0).astype(BF16)
            ext[2 * h0 + 1, BLOCK:BLOCK + tm, :] = jnp.where(low, 0.0, swapped).astype(BF16)
            ext[2 * h1, BLOCK:BLOCK + tm, :] = jnp.where(low, swapped, 0.0).astype(BF16)
            ext[2 * h1 + 1, BLOCK:BLOCK + tm, :] = jnp.where(low, 0.0, slab).astype(BF16)

    qi = lax.broadcasted_iota(jnp.int32, (2 * BLOCK, 4 * BLOCK), 0) & (BLOCK - 1)
    kj = lax.broadcasted_iota(jnp.int32, (2 * BLOCK, 4 * BLOCK), 1) & (2 * BLOCK - 1)
    dist = qi + BLOCK - kj
    band = (dist >= 0) & (dist < BLOCK)
    band_first = band & ((kj >= BLOCK) | (i > 0))
    top = lax.broadcasted_iota(jnp.int32, (2 * BLOCK, 1), 0) < BLOCK
    low2 = lax.broadcasted_iota(jnp.int32, (2 * BLOCK, LANES), 1) < 64

    def window(ext, h, r0):
        return jnp.concatenate([ext[2 * h, r0:r0 + 2 * BLOCK, :], ext[2 * h + 1, r0:r0 + 2 * BLOCK, :]],
                               axis=0)

    for n in range(SWA_NBLK):
        mask = band_first if n == 0 else band
        r0 = n * BLOCK
        scores = []
        for h in range(A_KV_HEADS):
            qs = jnp.concatenate([q_scr[r0:r0 + BLOCK, (2 * h) * LANES:(2 * h + 1) * LANES],
                                  q_scr[r0:r0 + BLOCK, (2 * h + 1) * LANES:(2 * h + 2) * LANES]], axis=0)
            scores.append(_dot_nt(qs, window(kext_scr, h, r0)))
        probs, inv_denoms = [], []
        for h in range(A_KV_HEADS):
            s = jnp.where(mask, scores[h], NEG)
            ps, rs = [], []
            for half in range(2):
                sh = s[:, half * 2 * BLOCK:(half + 1) * 2 * BLOCK]
                sink = jnp.where(top, sinks_ref[4 * h + half] * LOG2E, sinks_ref[4 * h + 2 + half] * LOG2E)
                mx = jnp.maximum(jnp.max(sh, axis=-1, keepdims=True), sink)
                p = jnp.exp2(sh - mx)
                rs.append(1.0 / (jnp.sum(p, axis=-1, keepdims=True) + jnp.exp2(sink - mx)))
                ps.append(p.astype(BF16))
            probs.append(jnp.concatenate(ps, axis=1))
            inv_denoms.append(jnp.where(low2, rs[0], rs[1]))
        for h in range(A_KV_HEADS):
            o = _dot(probs[h], window(vext_scr, h, r0)) * inv_denoms[h]
            o_scr[r0:r0 + BLOCK, (2 * h) * LANES:(2 * h + 1) * LANES] = o[:BLOCK].astype(BF16)
            o_scr[r0:r0 + BLOCK, (2 * h + 1) * LANES:(2 * h + 2) * LANES] = o[BLOCK:].astype(BF16)

    for r0 in range(0, tm, tm // 2):
        rows = slice(r0, r0 + tm // 2)
        y = _dot(o_scr[rows, :], wo_ref[...])
        o_ref[0, rows, :] = _layer_norm(ALPHA * x[rows] + y, g_ref[...], b_ref[...])


def _swa_layer(x, sinks, cos, sin, wqkv, wo, g, b):
    bsz, s, d = x.shape
    tm = SWA_TM
    nqkv = wqkv.shape[1]
    return pl.pallas_call(
        _swa_kernel,
        grid=(bsz, s // tm),
        in_specs=[pl.BlockSpec(memory_space=pltpu.SMEM),
                  pl.BlockSpec((1, tm, d), lambda bi, i: (bi, i, 0)),
                  pl.BlockSpec((tm, LANES), lambda bi, i: (i, 0)),
                  pl.BlockSpec((tm, LANES), lambda bi, i: (i, 0)),
                  _resident((d, nqkv)), _resident((d, d)), _resident((1, d)), _resident((1, d))],
        out_specs=pl.BlockSpec((1, tm, d), lambda bi, i: (bi, i, 0)),
        out_shape=jax.ShapeDtypeStruct((bsz, s, d), F32),
        scratch_shapes=[pltpu.VMEM((2 * A_KV_HEADS, tm + BLOCK, LANES), BF16),
                        pltpu.VMEM((2 * A_KV_HEADS, tm + BLOCK, LANES), BF16),
                        pltpu.VMEM((tm, d), BF16),
                        pltpu.VMEM((tm, d), BF16)],
        compiler_params=_params(("arbitrary", "arbitrary")),
        name="swa_layer",
    )(sinks, x, cos, sin, wqkv, wo, g, b)


LRU_TM = 256
LRU_NLT = LRU_WIDTH // LANES
LRU_PAD = 2 * SUBLANES


def _rglru_kernel(x_ref, win_ref, cw_ref, cb_ref, wr_ref, br_ref, wi_ref, bi_ref, lam_ref, wo_ref,
                  g_ref, b_ref, o_ref, ustage_scr, a_scr, b_scr, hcar_scr):
    i = pl.program_id(1)
    tm = LRU_TM
    w = LRU_WIDTH
    pad = LRU_PAD
    sl = SUBLANES

    @pl.when(i == 0)
    def _():
        ustage_scr[:, 0:sl, :] = jnp.zeros((LRU_NLT, sl, LANES), F32)
        a_scr[:, :, 0:pad, :] = jnp.ones((3, LRU_NLT, pad, LANES), F32)
        b_scr[:, :, 0:pad, :] = jnp.zeros((3, LRU_NLT, pad, LANES), F32)
        hcar_scr[...] = jnp.zeros_like(hcar_scr)

    @pl.when(i > 0)
    def _():
        ustage_scr[:, 0:sl, :] = ustage_scr[:, tm:tm + sl, :]
        a_scr[0, :, sl:pad, :] = a_scr[0, :, tm + sl:tm + pad, :]
        b_scr[0, :, sl:pad, :] = b_scr[0, :, tm + sl:tm + pad, :]

    x = x_ref[0]
    gu = _dot(x.astype(BF16), win_ref[...])
    gate = gu[:, :w]
    u = gu[:, w:]
    cw = cw_ref[...]
    cb = cb_ref[...]
    ucs = []
    for lt in range(LRU_NLT):
        ls = slice(lt * LANES, (lt + 1) * LANES)
        us = ustage_scr.at[lt]
        us[sl:sl + tm, :] = u[:, ls]
        uc = us[sl - 3:sl - 3 + tm, :] * cw[0:1, ls]
        uc = uc + us[sl - 2:sl - 2 + tm, :] * cw[1:2, ls]
        uc = uc + us[sl - 1:sl - 1 + tm, :] * cw[2:3, ls]
        uc = uc + u[:, ls] * cw[3:4, ls]
        ucs.append(uc + cb[:, ls])
    uc = jnp.concatenate(ucs, axis=1)

    ucb = uc.astype(BF16)
    rs, igs = [], []
    for h in range(LRU_BLOCKS):
        blk = ucb[:, h * LRU_BLOCK_W:(h + 1) * LRU_BLOCK_W]
        rs.append(_dot(blk, wr_ref[h]))
        igs.append(_dot(blk, wi_ref[h]))
    r = _sigmoid(jnp.concatenate(rs, axis=-1) + br_ref[...])
    ig = _sigmoid(jnp.concatenate(igs, axis=-1) + bi_ref[...])
    z = -lam_ref[...]
    softplus = jnp.maximum(z, 0.0) + jnp.log1p(jnp.exp(-jnp.abs(z)))
    log_a = (-LRU_C * r) * softplus
    a0 = jnp.exp(log_a)
    th = jnp.tanh(log_a)
    b0 = jnp.sqrt((-2.0 * th) / (1.0 - th)) * (ig * uc)

    a8, b8 = [], []
    for lt in range(LRU_NLT):
        ls = slice(lt * LANES, (lt + 1) * LANES)
        a_scr[0, lt, pad:pad + tm, :] = a0[:, ls]
        b_scr[0, lt, pad:pad + tm, :] = b0[:, ls]
        for k, d in enumerate((1, 2)):
            a_cur = a_scr[k, lt, sl:pad + tm, :]
            b_cur = b_scr[k, lt, sl:pad + tm, :]
            a_scr[k + 1, lt, sl:pad + tm, :] = a_cur * a_scr[k, lt, sl - d:pad + tm - d, :]
            b_scr[k + 1, lt, sl:pad + tm, :] = a_cur * b_scr[k, lt, sl - d:pad + tm - d, :] + b_cur
        a_cur = a_scr[2, lt, pad:pad + tm, :]
        a8.append(a_cur * a_scr[2, lt, pad - 4:pad - 4 + tm, :])
        b8.append(a_cur * b_scr[2, lt, pad - 4:pad - 4 + tm, :] + b_scr[2, lt, pad:pad + tm, :])
    a8 = jnp.concatenate(a8, axis=1)
    b8 = jnp.concatenate(b8, axis=1)

    hblk = hcar_scr[...]
    hs = []
    for j in range(tm // sl):
        hblk = a8[j * sl:(j + 1) * sl] * hblk + b8[j * sl:(j + 1) * sl]
        hs.append(hblk)
    hcar_scr[...] = hblk
    h = jnp.concatenate(hs, axis=0)

    c0 = math.sqrt(2.0 / math.pi)
    gelu = gate * (0.5 * (1.0 + jnp.tanh(c0 * (gate + 0.044715 * (gate * gate * gate)))))
    y = _dot((h * gelu).astype(BF16), wo_ref[...])
    o_ref[0] = _layer_norm(ALPHA * x + y, g_ref[...], b_ref[...])


def _rglru_layer(x, win, cw, cb, wr, br, wi, bi, lam, wo, g, b):
    bsz, s, d = x.shape
    tm = LRU_TM
    w = LRU_WIDTH
    return pl.pallas_call(
        _rglru_kernel,
        grid=(bsz, s // tm),
        in_specs=[pl.BlockSpec((1, tm, d), lambda bi_, i: (bi_, i, 0)),
                  _resident((d, 2 * w)), _resident((LRU_CONV, w)), _resident((1, w)),
                  _resident((LRU_BLOCKS, LRU_BLOCK_W, LRU_BLOCK_W)), _resident((1, w)),
                  _resident((LRU_BLOCKS, LRU_BLOCK_W, LRU_BLOCK_W)), _resident((1, w)),
                  _resident((1, w)), _resident((w, d)), _resident((1, d)), _resident((1, d))],
        out_specs=pl.BlockSpec((1, tm, d), lambda bi_, i: (bi_, i, 0)),
        out_shape=jax.ShapeDtypeStruct((bsz, s, d), F32),
        scratch_shapes=[pltpu.VMEM((LRU_NLT, SUBLANES + tm, LANES), F32),
                        pltpu.VMEM((3, LRU_NLT, LRU_PAD + tm, LANES), F32),
                        pltpu.VMEM((3, LRU_NLT, LRU_PAD + tm, LANES), F32),
                        pltpu.VMEM((SUBLANES, w), F32)],
        compiler_params=_params(("arbitrary", "arbitrary")),
        name="rglru_layer",
    )(x, win, cw, cb, wr, br, wi, bi, lam, wo, g, b)


MLA_TM = 512
MLA_TQ = 1024
MLA_HALF = MLA_TQ // 2
MLA_TK = 512
C_DOWN_PAD = 768
MLA_Q_SCALE = (C_NOPE + C_ROPE) ** -0.5 * math.log2(math.e)


def _mla_proj_kernel(x_ref, cos_ref, sin_ref, wd_ref, qn_ref, kvn_ref, wuq_ref, wukv_ref,
                     q_ref, k_ref, v_ref):
    tm = MLA_TM
    xb = x_ref[0].astype(BF16)
    c = _dot(xb, wd_ref[...])
    cq = c[:, :C_Q_RANK]
    ckv = c[:, C_Q_RANK:C_Q_RANK + C_KV_RANK]
    kr_slab = c[:, C_Q_RANK + C_KV_RANK:]
    cq = cq * lax.rsqrt(jnp.mean(cq * cq, axis=-1, keepdims=True) + RMS_EPS) * qn_ref[...]
    ckv = ckv * lax.rsqrt(jnp.mean(ckv * ckv, axis=-1, keepdims=True) + RMS_EPS) * kvn_ref[...]
    q = _dot(cq.astype(BF16), wuq_ref[...]) * MLA_Q_SCALE
    kv = _dot(ckv.astype(BF16), wukv_ref[...])

    cos = cos_ref[...]
    sin = sin_ref[...]
    lane = lax.broadcasted_iota(jnp.int32, (tm, LANES), 1)
    first_half = (lane & 32) == 0
    low = lane < 64
    nn = C_HEADS * C_NOPE

    kr = _rope_slab(kr_slab, cos, sin, first_half)
    kr_pair = (kr.astype(BF16), pltpu.roll(kr, 64, 1).astype(BF16))
    for pair in range(C_HEADS // 2):
        qr = _rope_slab(q[:, nn + pair * LANES:nn + (pair + 1) * LANES], cos, sin, first_half)
        for half in range(2):
            h = 2 * pair + half
            qr_h = jnp.where(low, qr, 0.0) if half == 0 else jnp.where(low, 0.0, qr)
            q_ref[0, h, :, 0:C_NOPE] = q[:, h * C_NOPE:(h + 1) * C_NOPE].astype(BF16)
            q_ref[0, h, :, C_NOPE:C_QK_PAD] = qr_h.astype(BF16)
            k_ref[0, h, :, 0:C_NOPE] = kv[:, h * C_NOPE:(h + 1) * C_NOPE].astype(BF16)
            k_ref[0, h, :, C_NOPE:C_QK_PAD] = kr_pair[half]
            v_ref[0, h] = kv[:, nn + h * C_V:nn + (h + 1) * C_V].astype(BF16)


def _mla_proj(x, cos, sin, wd, qn, kvn, wuq, wukv):
    bsz, s, d = x.shape
    tm = MLA_TM
    qk_shape = jax.ShapeDtypeStruct((bsz, C_HEADS, s, C_QK_PAD), BF16)
    v_shape = jax.ShapeDtypeStruct((bsz, C_HEADS, s, C_V), BF16)
    return pl.pallas_call(
        _mla_proj_kernel,
        grid=(bsz, s // tm),
        in_specs=[pl.BlockSpec((1, tm, d), lambda bi, i: (bi, i, 0)),
                  pl.BlockSpec((tm, LANES), lambda bi, i: (i, 0)),
                  pl.BlockSpec((tm, LANES), lambda bi, i: (i, 0)),
                  _resident(wd.shape), _resident(qn.shape), _resident(kvn.shape),
                  _resident(wuq.shape), _resident(wukv.shape)],
        out_specs=[pl.BlockSpec((1, C_HEADS, tm, C_QK_PAD), lambda bi, i: (bi, 0, i, 0)),
                   pl.BlockSpec((1, C_HEADS, tm, C_QK_PAD), lambda bi, i: (bi, 0, i, 0)),
                   pl.BlockSpec((1, C_HEADS, tm, C_V), lambda bi, i: (bi, 0, i, 0))],
        out_shape=[qk_shape, qk_shape, v_shape],
        compiler_params=_params(("parallel", "parallel")),
        name="mla_proj",
    )(x, cos, sin, wd, qn, kvn, wuq, wukv)


def _mla_attn_kernel(q_ref, k_ref, v_ref, o_ref, m_scr, acc_scr, s_scr):
    i = pl.program_id(2)
    tk, hq = MLA_TK, MLA_HALF

    m_scr[...] = jnp.full_like(m_scr, -jnp.inf)
    acc_scr[...] = jnp.zeros_like(acc_scr)
    ones = jnp.ones((tk, LANES), BF16)

    def scores(slot, hf, j):
        k0 = pl.multiple_of(j * tk, tk)
        s_scr[slot, hf] = _dot_nt(q_ref[0, 0, hf * hq:(hf + 1) * hq, :], k_ref[0, 0, pl.ds(k0, tk), :])

    def consume(slot, hf, j, diagonal):
        k0 = pl.multiple_of(j * tk, tk)
        vv = jnp.concatenate([v_ref[0, 0, pl.ds(k0, tk), :], ones], axis=1)
        s = s_scr[slot, hf]
        if diagonal:
            qpos = lax.broadcasted_iota(jnp.int32, (hq, tk), 0)
            kpos = lax.broadcasted_iota(jnp.int32, (hq, tk), 1)
            s = jnp.where(kpos <= qpos, s, NEG)
        m_prev = m_scr[hf]
        m_next = jnp.maximum(m_prev, jnp.max(s, axis=-1, keepdims=True))
        p = jnp.exp2(s - jnp.tile(m_next, (1, tk // LANES)))
        corr = jnp.exp2(m_prev - m_next)
        acc_scr[hf] = jnp.tile(corr, (1, 2)) * acc_scr[hf] + _dot(p.astype(BF16), vv)
        m_scr[hf] = m_next

    scores(0, 0, 0)
    scores(0, 1, 0)

    def body(t, carry):
        j = 2 * t
        scores(1, 0, j + 1)
        scores(1, 1, j + 1)
        consume(0, 0, j, False)
        consume(0, 1, j, False)
        scores(0, 0, j + 2)
        scores(0, 1, j + 2)
        consume(1, 0, j + 1, False)
        consume(1, 1, j + 1, False)
        return carry

    lax.fori_loop(0, i, body, 0)
    scores(1, 1, 2 * i + 1)
    consume(0, 0, 2 * i, True)
    consume(0, 1, 2 * i, False)
    consume(1, 1, 2 * i + 1, True)
    for hf in range(2):
        acc = acc_scr[hf]
        o_ref[0, hf * hq:(hf + 1) * hq, :] = (acc[:, :C_V] / acc[:, C_V:]).astype(BF16)


def _mla_attn(q, k, v):
    bsz, nh, s, _ = q.shape
    tq = MLA_TQ
    return pl.pallas_call(
        _mla_attn_kernel,
        grid=(bsz, nh, s // tq),
        in_specs=[pl.BlockSpec((1, 1, tq, C_QK_PAD), lambda b, h, i: (b, h, i, 0)),
                  pl.BlockSpec((1, 1, s, C_QK_PAD), lambda b, h, i: (b, h, 0, 0)),
                  pl.BlockSpec((1, 1, s, C_V), lambda b, h, i: (b, h, 0, 0))],
        out_specs=pl.BlockSpec((1, tq, C_V), lambda b, h, i: (b, i, h)),
        out_shape=jax.ShapeDtypeStruct((bsz, s, nh * C_V), BF16),
        scratch_shapes=[pltpu.VMEM((2, MLA_HALF, LANES), F32),
                        pltpu.VMEM((2, MLA_HALF, 2 * C_V), F32),
                        pltpu.VMEM((2, 2, MLA_HALF, MLA_TK), F32)],
        compiler_params=_params(("parallel", "parallel", "arbitrary")),
        name="mla_attn",
    )(q, k, v)


X_TM = 1024
X_LN_ROWS = 256


def _xattn_kernel(*refs, mixer_prologue):
    tm = X_TM
    if mixer_prologue:
        a_ref, wa_ref, ga_ref, ba_ref, x_ref, wq_ref, mkv_ref, wo_ref, g_ref, b_ref, o_ref = refs
        x = _layer_norm(ALPHA * x_ref[0] + _dot(a_ref[0], wa_ref[...]), ga_ref[...], ba_ref[...])
    else:
        x_ref, wq_ref, mkv_ref, wo_ref, g_ref, b_ref, o_ref = refs
        x = x_ref[0]
    q = (_dot(x.astype(BF16), wq_ref[...]) * (X_HEAD_DIM ** -0.5 * LOG2E)).astype(BF16)
    heads = [(h * X_HEAD_DIM, (h + 1) * X_HEAD_DIM) for h in range(X_HEADS)]
    scores = [_dot_nt(q[:, lo:hi], mkv_ref[0, :, lo:hi]) for lo, hi in heads]
    probs, inv_denoms = [], []
    for s in scores:
        p = jnp.exp2(s - jnp.max(s, axis=-1, keepdims=True))
        inv_denoms.append(1.0 / jnp.sum(p, axis=-1, keepdims=True))
        probs.append(p.astype(BF16))
    outs = [(_dot(probs[h], mkv_ref[0, :, D_MODEL + lo:D_MODEL + hi]) * inv_denoms[h]).astype(BF16)
            for h, (lo, hi) in enumerate(heads)]
    o = jnp.concatenate(outs, axis=-1)
    for r0 in range(0, tm, X_LN_ROWS):
        rows = slice(r0, r0 + X_LN_ROWS)
        y = _dot(o[rows], wo_ref[...])
        o_ref[0, rows, :] = _layer_norm(ALPHA * x[rows] + y, g_ref[...], b_ref[...])


def _xattn_layer(x, wq, mkv, wo, g, b, mixer=None):
    bsz, s, d = x.shape
    tm = X_TM
    tile = pl.BlockSpec((1, tm, d), lambda bi, i: (bi, i, 0))
    in_specs = [tile, _resident((d, d)),
                pl.BlockSpec((1, MEM_LEN, 2 * d), lambda bi, i: (bi, 0, 0)),
                _resident((d, d)), _resident((1, d)), _resident((1, d))]
    args = (x, wq, mkv, wo, g, b)
    if mixer is not None:
        in_specs = [tile, _resident((d, d)), _resident((1, d)), _resident((1, d))] + in_specs
        args = tuple(mixer) + args
    return pl.pallas_call(
        functools.partial(_xattn_kernel, mixer_prologue=mixer is not None),
        grid=(bsz, s // tm),
        in_specs=in_specs,
        out_specs=tile,
        out_shape=jax.ShapeDtypeStruct((bsz, s, d), F32),
        compiler_params=_params(("parallel", "parallel")),
        name="xattn_layer",
    )(*args)


FFN_TM = 512
FFN_LN_ROWS = 256
FFN_DOWN_GROUPS = (4, 8, FFN_NCHUNK)


def _ffn_kernel(x_ref, wup_ref, cw_ref, cb_ref, wdn_ref, g_ref, b_ref, o_ref, hprev_scr, act_scr,
                acc_scr, hstage_scr):
    i = pl.program_id(1)
    tm = FFN_TM

    @pl.when(i == 0)
    def _():
        hprev_scr[...] = jnp.zeros_like(hprev_scr)

    x = x_ref[0]
    xb = x.astype(BF16)

    def up_conv(col0, slot):
        cols = slice(col0, col0 + FFN_CHUNK)
        h = _dot(xb, wup_ref[:, cols])
        prev = hprev_scr[:, cols]
        hprev_scr[:, cols] = h[tm - SUBLANES:, :]
        cw = cw_ref[:, cols]
        cb = cb_ref[:, cols]
        outs = []
        for lt in range(FFN_CHUNK // LANES):
            ls = slice(lt * LANES, (lt + 1) * LANES)
            hs = hstage_scr.at[slot, lt]
            hs[0:SUBLANES, :] = prev[:, ls]
            hs[SUBLANES:SUBLANES + tm, :] = h[:, ls]
            y = hs[SUBLANES - 2:SUBLANES - 2 + tm, :] * cw[0:1, ls]
            y = y + hs[SUBLANES - 1:SUBLANES - 1 + tm, :] * cw[1:2, ls]
            y = y + h[:, ls] * cw[2:3, ls]
            outs.append(y + cb[:, ls])
        return jnp.concatenate(outs, axis=1)

    def down(rows, c0, c1):
        ks = slice(c0 * FFN_CHUNK, c1 * FFN_CHUNK)
        return _dot(act_scr[rows, ks], wdn_ref[ks, :])

    everything = slice(0, tm)
    for c in range(FFN_NCHUNK):
        gt = up_conv(c * FFN_CHUNK, 2 * (c % 2))
        val = up_conv(D_FF + c * FFN_CHUNK, 2 * (c % 2) + 1)
        act_scr[:, c * FFN_CHUNK:(c + 1) * FFN_CHUNK] = ((gt * _sigmoid(gt)) * val).astype(BF16)
        if c + 1 in FFN_DOWN_GROUPS[:-1]:
            gi = FFN_DOWN_GROUPS.index(c + 1)
            if gi == 0:
                acc_scr[...] = down(everything, 0, c + 1)
            else:
                acc_scr[...] += down(everything, FFN_DOWN_GROUPS[gi - 1], c + 1)
    for r0 in range(0, tm, FFN_LN_ROWS):
        rows = slice(r0, r0 + FFN_LN_ROWS)
        out = acc_scr[rows, :] + down(rows, FFN_DOWN_GROUPS[-2], FFN_DOWN_GROUPS[-1])
        o_ref[0, rows, :] = _layer_norm(ALPHA * x[rows] + out, g_ref[...], b_ref[...])


def _ffn_layer(x, wup, cw, cb, wdn, g, b):
    bsz, s, d = x.shape
    tm = FFN_TM
    return pl.pallas_call(
        _ffn_kernel,
        grid=(bsz, s // tm),
        in_specs=[pl.BlockSpec((1, tm, d), lambda bi, i: (bi, i, 0)),
                  _resident((d, 2 * D_FF)), _resident((FFN_CONV, 2 * D_FF)), _resident((1, 2 * D_FF)),
                  _resident((D_FF, d)), _resident((1, d)), _resident((1, d))],
        out_specs=pl.BlockSpec((1, tm, d), lambda bi, i: (bi, i, 0)),
        out_shape=jax.ShapeDtypeStruct((bsz, s, d), F32),
        scratch_shapes=[pltpu.VMEM((SUBLANES, 2 * D_FF), F32), pltpu.VMEM((tm, D_FF), BF16),
                        pltpu.VMEM((tm, d), F32),
                        pltpu.VMEM((4, FFN_CHUNK // LANES, SUBLANES + tm, LANES), F32)],
        compiler_params=_params(("arbitrary", "arbitrary")),
        name="ffn_layer",
    )(x, wup, cw, cb, wdn, g, b)


def kernel(x, mem, a_w_qkv, a_sinks, a_w_o, b_w_in, b_conv_w, b_conv_b, b_w_rgate, b_b_rgate, b_w_igate, b_b_igate, b_lambda, b_w_o, c_w_down, c_q_norm, c_kv_norm, c_w_uq, c_w_ukv, c_w_o, mem_w_kv, x_w_q, x_w_o, f_w_up, f_conv_w, f_conv_b, f_w_down, ln_g, ln_b):
    bsz, s, d = x.shape
    cos, sin = _rope_tables(s)

    mkv = _matmul(mem.reshape(bsz * MEM_LEN, d), mem_w_kv.astype(BF16), 512, BF16)
    mkv = mkv.reshape(bsz, MEM_LEN, 2 * d)

    def row(v):
        return v.reshape(1, -1)

    for i in range(DEPTH):
        kind, j = i % N_MIXERS, i // N_MIXERS
        g0, b0 = row(ln_g[i, 0]), row(ln_b[i, 0])
        mixer = None
        if kind == 0:
            x = _swa_layer(x, a_sinks[j], cos, sin, a_w_qkv[j].astype(BF16), a_w_o[j].astype(BF16), g0, b0)
        elif kind == 1:
            x = _rglru_layer(x, b_w_in[j].astype(BF16), b_conv_w[j], row(b_conv_b[j]),
                             b_w_rgate[j].astype(BF16), row(b_b_rgate[j]),
                             b_w_igate[j].astype(BF16), row(b_b_igate[j]),
                             row(b_lambda[j]), b_w_o[j].astype(BF16), g0, b0)
        else:
            wd = jnp.pad(c_w_down[j], ((0, 0), (0, C_DOWN_PAD - c_w_down.shape[-1]))).astype(BF16)
            wuq = c_w_uq[j].reshape(C_Q_RANK, C_HEADS, C_NOPE + C_ROPE)
            wuq = jnp.concatenate([wuq[:, :, :C_NOPE].reshape(C_Q_RANK, -1),
                                   wuq[:, :, C_NOPE:].reshape(C_Q_RANK, -1)], axis=-1).astype(BF16)
            wukv = c_w_ukv[j].reshape(C_KV_RANK, C_HEADS, C_NOPE + C_V)
            wukv = jnp.concatenate([wukv[:, :, :C_NOPE].reshape(C_KV_RANK, -1),
                                    wukv[:, :, C_NOPE:].reshape(C_KV_RANK, -1)], axis=-1).astype(BF16)
            q, k, v = _mla_proj(x, cos, sin, wd, row(c_q_norm[j]), row(c_kv_norm[j]), wuq, wukv)
            mixer = (_mla_attn(q, k, v), c_w_o[j].astype(BF16), g0, b0)
        x = _xattn_layer(x, x_w_q[i].astype(BF16), mkv, x_w_o[i].astype(BF16),
                         row(ln_g[i, 1]), row(ln_b[i, 1]), mixer=mixer)
        x = _ffn_layer(x, f_w_up[i].astype(BF16), f_conv_w[i], row(f_conv_b[i]), f_w_down[i].astype(BF16),
                       row(ln_g[i, 2]), row(ln_b[i, 2]))
    return x
```

```python
import functools
import math

import jax
import jax.numpy as jnp
import numpy as np
from jax import lax
from jax.experimental import pallas as pl
from jax.experimental.pallas import tpu as pltpu

D_MODEL = 1024
DEPTH = 4
N_MIXERS = 3
MEM_LEN = 256
BLOCK = 128
ROPE_THETA = 10000.0
NEG = -1e30
LN_EPS = 1e-5
RMS_EPS = 1e-6

A_HEADS = 16
A_KV_HEADS = 4
A_HEAD_DIM = 64

LRU_WIDTH = D_MODEL
LRU_BLOCKS = 4
LRU_BLOCK_W = LRU_WIDTH // LRU_BLOCKS
LRU_CONV = 4
LRU_C = 8.0

C_HEADS = 8
C_NOPE = 128
C_ROPE = 64
C_V = 128
C_Q_RANK = 384
C_KV_RANK = 256
C_QK_PAD = 256

X_HEADS = 4
X_HEAD_DIM = D_MODEL // X_HEADS

D_FF = 2816
FFN_CONV = 3
FFN_CHUNK = 256
FFN_NCHUNK = D_FF // FFN_CHUNK

ALPHA = (2.0 * DEPTH) ** 0.25

LANES = 128
SUBLANES = 8
VMEM_LIMIT = 56 * 1024 * 1024

BF16 = jnp.bfloat16
F32 = jnp.float32

NT_DIMS = (((1,), (1,)), ((), ()))
LOG2E = math.log2(math.e)


def _dot(a, b):
    return jnp.dot(a, b, preferred_element_type=F32)


def _dot_nt(a, b):
    return lax.dot_general(a, b, NT_DIMS, preferred_element_type=F32)


def _layer_norm(z, g, b):
    mu = jnp.mean(z, axis=-1, keepdims=True)
    zc = z - mu
    var = jnp.mean(zc * zc, axis=-1, keepdims=True)
    return zc * lax.rsqrt(var + LN_EPS) * g + b


def _sigmoid(z):
    return 0.5 * jnp.tanh(0.5 * z) + 0.5


def _resident(shape):
    nd = len(shape)
    return pl.BlockSpec(shape, lambda *_: (0,) * nd, pipeline_mode=pl.Buffered(1))


def _params(sem, flags=None):
    return pltpu.CompilerParams(dimension_semantics=sem, vmem_limit_bytes=VMEM_LIMIT, flags=flags)


def _matmul_kernel(a_ref, w_ref, o_ref):
    o_ref[...] = _dot(a_ref[...].astype(BF16), w_ref[...]).astype(o_ref.dtype)


def _matmul(a, w, tm, out_dtype):
    t, k = a.shape
    n = w.shape[1]
    return pl.pallas_call(
        _matmul_kernel,
        grid=(t // tm,),
        in_specs=[pl.BlockSpec((tm, k), lambda i: (i, 0)), _resident((k, n))],
        out_specs=pl.BlockSpec((tm, n), lambda i: (i, 0)),
        out_shape=jax.ShapeDtypeStruct((t, n), out_dtype),
        compiler_params=_params(("parallel",)),
        name="matmul",
    )(a, w)


def _rope_slab(slab, cos, sin_signed, first_half):
    rot = jnp.where(first_half, pltpu.roll(slab, LANES - 32, 1), pltpu.roll(slab, 32, 1))
    return slab * cos + rot * sin_signed


def _rope_tables(seq):
    inv = 1.0 / (ROPE_THETA ** (jnp.arange(0, A_HEAD_DIM, 2, dtype=F32) / A_HEAD_DIM))
    ang = jnp.arange(seq, dtype=F32)[:, None] * inv[None, :]
    cos, sin = jnp.cos(ang), jnp.sin(ang)
    return (jnp.concatenate([cos, cos, cos, cos], axis=-1),
            jnp.concatenate([-sin, sin, -sin, sin], axis=-1))


SWA_TM = 512
SWA_NBLK = SWA_TM // BLOCK


def _swa_kernel(sinks_ref, x_ref, cos_ref, sin_ref, wqkv_ref, wo_ref, g_ref, b_ref, o_ref,
                kext_scr, vext_scr, q_scr, o_scr):
    i = pl.program_id(1)
    tm = SWA_TM

    @pl.when(i == 0)
    def _():
        kext_scr[:, 0:BLOCK, :] = jnp.zeros((2 * A_KV_HEADS, BLOCK, LANES), BF16)
        vext_scr[:, 0:BLOCK, :] = jnp.zeros((2 * A_KV_HEADS, BLOCK, LANES), BF16)

    @pl.when(i > 0)
    def _():
        kext_scr[:, 0:BLOCK, :] = kext_scr[:, tm:tm + BLOCK, :]
        vext_scr[:, 0:BLOCK, :] = vext_scr[:, tm:tm + BLOCK, :]

    x = x_ref[0]
    qkv = _dot(x.astype(BF16), wqkv_ref[...])
    cos = cos_ref[...]
    sin = sin_ref[...]
    lane = lax.broadcasted_iota(jnp.int32, (tm, LANES), 1)
    first_half = (lane & 32) == 0
    low = lane < 64

    nq = A_HEADS * A_HEAD_DIM
    nkv = A_KV_HEADS * A_HEAD_DIM
    for j in range(nq // LANES):
        slab = _rope_slab(qkv[:, j * LANES:(j + 1) * LANES], cos, sin, first_half)
        q_scr[:, j * LANES:(j + 1) * LANES] = (slab * (A_HEAD_DIM ** -0.5 * LOG2E)).astype(BF16)

    for m in range(nkv // LANES):
        kslab = _rope_slab(qkv[:, nq + m * LANES:nq + (m + 1) * LANES], cos, sin, first_half)
        vslab = qkv[:, nq + nkv + m * LANES:nq + nkv + (m + 1) * LANES]
        for slab, ext in ((kslab, kext_scr), (vslab, vext_scr)):
            swapped = pltpu.roll(slab, 64, 1)
            h0, h1 = 2 * m, 2 * m + 1
            ext[2 * h0, BLOCK:BLOCK + tm, :] = jnp.where(low, slab, 0.0).astype(BF16)
            ext[2 * h0 + 1, BLOCK:BLOCK + tm, :] = jnp.where(low, 0.0, swapped).astype(BF16)
            ext[2 * h1, BLOCK:BLOCK + tm, :] = jnp.where(low, swapped, 0.0).astype(BF16)
            ext[2 * h1 + 1, BLOCK:BLOCK + tm, :] = jnp.where(low, 0.0, slab).astype(BF16)

    qi = lax.broadcasted_iota(jnp.int32, (2 * BLOCK, 4 * BLOCK), 0) & (BLOCK - 1)
    kj = lax.broadcasted_iota(jnp.int32, (2 * BLOCK, 4 * BLOCK), 1) & (2 * BLOCK - 1)
    dist = qi + BLOCK - kj
    band = (dist >= 0) & (dist < BLOCK)
    band_first = band & ((kj >= BLOCK) | (i > 0))
    top = lax.broadcasted_iota(jnp.int32, (2 * BLOCK, 1), 0) < BLOCK
    low2 = lax.broadcasted_iota(jnp.int32, (2 * BLOCK, LANES), 1) < 64

    def window(ext, h, r0):
        return jnp.concatenate([ext[2 * h, r0:r0 + 2 * BLOCK, :], ext[2 * h + 1, r0:r0 + 2 * BLOCK, :]],
                               axis=0)

    for n in range(SWA_NBLK):
        mask = band_first if n == 0 else band
        r0 = n * BLOCK
        scores = []
        for h in range(A_KV_HEADS):
            qs = jnp.concatenate([q_scr[r0:r0 + BLOCK, (2 * h) * LANES:(2 * h + 1) * LANES],
                                  q_scr[r0:r0 + BLOCK, (2 * h + 1) * LANES:(2 * h + 2) * LANES]], axis=0)
            scores.append(_dot_nt(qs, window(kext_scr, h, r0)))
        probs, inv_denoms = [], []
        for h in range(A_KV_HEADS):
            s = jnp.where(mask, scores[h], NEG)
            ps, rs = [], []
            for half in range(2):
                sh = s[:, half * 2 * BLOCK:(half + 1) * 2 * BLOCK]
                sink = jnp.where(top, sinks_ref[4 * h + half] * LOG2E, sinks_ref[4 * h + 2 + half] * LOG2E)
                mx = jnp.maximum(jnp.max(sh, axis=-1, keepdims=True), sink)
                p = jnp.exp2(sh - mx)
                rs.append(1.0 / (jnp.sum(p, axis=-1, keepdims=True) + jnp.exp2(sink - mx)))
                ps.append(p.astype(BF16))
            probs.append(jnp.concatenate(ps, axis=1))
            inv_denoms.append(jnp.where(low2, rs[0], rs[1]))
        for h in range(A_KV_HEADS):
            o = _dot(probs[h], window(vext_scr, h, r0)) * inv_denoms[h]
            o_scr[r0:r0 + BLOCK, (2 * h) * LANES:(2 * h + 1) * LANES] = o[:BLOCK].astype(BF16)
            o_scr[r0:r0 + BLOCK, (2 * h + 1) * LANES:(2 * h + 2) * LANES] = o[BLOCK:].astype(BF16)

    for r0 in range(0, tm, tm // 2):
        rows = slice(r0, r0 + tm // 2)
        y = _dot(o_scr[rows, :], wo_ref[...])
        o_ref[0, rows, :] = _layer_norm(ALPHA * x[rows] + y, g_ref[...], b_ref[...])


def _swa_layer(x, sinks, cos, sin, wqkv, wo, g, b):
    bsz, s, d = x.shape
    tm = SWA_TM
    nqkv = wqkv.shape[1]
    return pl.pallas_call(
        _swa_kernel,
        grid=(bsz, s // tm),
        in_specs=[pl.BlockSpec(memory_space=pltpu.SMEM),
                  pl.BlockSpec((1, tm, d), lambda bi, i: (bi, i, 0)),
                  pl.BlockSpec((tm, LANES), lambda bi, i: (i, 0)),
                  pl.BlockSpec((tm, LANES), lambda bi, i: (i, 0)),
                  _resident((d, nqkv)), _resident((d, d)), _resident((1, d)), _resident((1, d))],
        out_specs=pl.BlockSpec((1, tm, d), lambda bi, i: (bi, i, 0)),
        out_shape=jax.ShapeDtypeStruct((bsz, s, d), F32),
        scratch_shapes=[pltpu.VMEM((2 * A_KV_HEADS, tm + BLOCK, LANES), BF16),
                        pltpu.VMEM((2 * A_KV_HEADS, tm + BLOCK, LANES), BF16),
                        pltpu.VMEM((tm, d), BF16),
                        pltpu.VMEM((tm, d), BF16)],
        compiler_params=_params(("arbitrary", "arbitrary")),
        name="swa_layer",
    )(sinks, x, cos, sin, wqkv, wo, g, b)


LRU_TM = 256
LRU_NLT = LRU_WIDTH // LANES
LRU_PAD = 2 * SUBLANES


def _rglru_kernel(x_ref, win_ref, cw_ref, cb_ref, wr_ref, br_ref, wi_ref, bi_ref, lam_ref, wo_ref,
                  g_ref, b_ref, o_ref, ustage_scr, a_scr, b_scr, hcar_scr):
    i = pl.program_id(1)
    tm = LRU_TM
    w = LRU_WIDTH
    pad = LRU_PAD
    sl = SUBLANES

    @pl.when(i == 0)
    def _():
        ustage_scr[:, 0:sl, :] = jnp.zeros((LRU_NLT, sl, LANES), F32)
        a_scr[:, :, 0:pad, :] = jnp.ones((3, LRU_NLT, pad, LANES), F32)
        b_scr[:, :, 0:pad, :] = jnp.zeros((3, LRU_NLT, pad, LANES), F32)
        hcar_scr[...] = jnp.zeros_like(hcar_scr)

    @pl.when(i > 0)
    def _():
        ustage_scr[:, 0:sl, :] = ustage_scr[:, tm:tm + sl, :]
        a_scr[0, :, sl:pad, :] = a_scr[0, :, tm + sl:tm + pad, :]
        b_scr[0, :, sl:pad, :] = b_scr[0, :, tm + sl:tm + pad, :]

    x = x_ref[0]
    gu = _dot(x.astype(BF16), win_ref[...])
    gate = gu[:, :w]
    u = gu[:, w:]
    cw = cw_ref[...]
    cb = cb_ref[...]
    ucs = []
    for lt in range(LRU_NLT):
        ls = slice(lt * LANES, (lt + 1) * LANES)
        us = ustage_scr.at[lt]
        us[sl:sl + tm, :] = u[:, ls]
        uc = us[sl - 3:sl - 3 + tm, :] * cw[0:1, ls]
        uc = uc + us[sl - 2:sl - 2 + tm, :] * cw[1:2, ls]
        uc = uc + us[sl - 1:sl - 1 + tm, :] * cw[2:3, ls]
        uc = uc + u[:, ls] * cw[3:4, ls]
        ucs.append(uc + cb[:, ls])
    uc = jnp.concatenate(ucs, axis=1)

    ucb = uc.astype(BF16)
    rs, igs = [], []
    for h in range(LRU_BLOCKS):
        blk = ucb[:, h * LRU_BLOCK_W:(h + 1) * LRU_BLOCK_W]
        rs.append(_dot(blk, wr_ref[h]))
        igs.append(_dot(blk, wi_ref[h]))
    r = _sigmoid(jnp.concatenate(rs, axis=-1) + br_ref[...])
    ig = _sigmoid(jnp.concatenate(igs, axis=-1) + bi_ref[...])
    z = -lam_ref[...]
    softplus = jnp.maximum(z, 0.0) + jnp.log1p(jnp.exp(-jnp.abs(z)))
    log_a = (-LRU_C * r) * softplus
    a0 = jnp.exp(log_a)
    th = jnp.tanh(log_a)
    b0 = (jnp.sqrt(-2.0 * th) * lax.rsqrt(1.0 - th)) * (ig * uc)

    a8, b8 = [], []
    for lt in range(LRU_NLT):
        ls = slice(lt * LANES, (lt + 1) * LANES)
        a_scr[0, lt, pad:pad + tm, :] = a0[:, ls]
        b_scr[0, lt, pad:pad + tm, :] = b0[:, ls]
        for k, d in enumerate((1, 2)):
            a_cur = a_scr[k, lt, sl:pad + tm, :]
            b_cur = b_scr[k, lt, sl:pad + tm, :]
            a_scr[k + 1, lt, sl:pad + tm, :] = a_cur * a_scr[k, lt, sl - d:pad + tm - d, :]
            b_scr[k + 1, lt, sl:pad + tm, :] = a_cur * b_scr[k, lt, sl - d:pad + tm - d, :] + b_cur
        a_cur = a_scr[2, lt, pad:pad + tm, :]
        a8.append(a_cur * a_scr[2, lt, pad - 4:pad - 4 + tm, :])
        b8.append(a_cur * b_scr[2, lt, pad - 4:pad - 4 + tm, :] + b_scr[2, lt, pad:pad + tm, :])
    a8 = jnp.concatenate(a8, axis=1)
    b8 = jnp.concatenate(b8, axis=1)

    hblk = hcar_scr[...]
    hs = []
    for j in range(tm // sl):
        hblk = a8[j * sl:(j + 1) * sl] * hblk + b8[j * sl:(j + 1) * sl]
        hs.append(hblk)
    hcar_scr[...] = hblk
    h = jnp.concatenate(hs, axis=0)

    c0 = math.sqrt(2.0 / math.pi)
    gelu = gate * (0.5 * (1.0 + jnp.tanh(c0 * (gate + 0.044715 * (gate * gate * gate)))))
    y = _dot((h * gelu).astype(BF16), wo_ref[...])
    o_ref[0] = _layer_norm(ALPHA * x + y, g_ref[...], b_ref[...])


def _rglru_layer(x, win, cw, cb, wr, br, wi, bi, lam, wo, g, b):
    bsz, s, d = x.shape
    tm = LRU_TM
    w = LRU_WIDTH
    return pl.pallas_call(
        _rglru_kernel,
        grid=(bsz, s // tm),
        in_specs=[pl.BlockSpec((1, tm, d), lambda bi_, i: (bi_, i, 0)),
                  _resident((d, 2 * w)), _resident((LRU_CONV, w)), _resident((1, w)),
                  _resident((LRU_BLOCKS, LRU_BLOCK_W, LRU_BLOCK_W)), _resident((1, w)),
                  _resident((LRU_BLOCKS, LRU_BLOCK_W, LRU_BLOCK_W)), _resident((1, w)),
                  _resident((1, w)), _resident((w, d)), _resident((1, d)), _resident((1, d))],
        out_specs=pl.BlockSpec((1, tm, d), lambda bi_, i: (bi_, i, 0)),
        out_shape=jax.ShapeDtypeStruct((bsz, s, d), F32),
        scratch_shapes=[pltpu.VMEM((LRU_NLT, SUBLANES + tm, LANES), F32),
                        pltpu.VMEM((3, LRU_NLT, LRU_PAD + tm, LANES), F32),
                        pltpu.VMEM((3, LRU_NLT, LRU_PAD + tm, LANES), F32),
                        pltpu.VMEM((SUBLANES, w), F32)],
        compiler_params=_params(("arbitrary", "arbitrary")),
        name="rglru_layer",
    )(x, win, cw, cb, wr, br, wi, bi, lam, wo, g, b)


MLA_TM = 512
MLA_TQ = 1024
MLA_HALF = MLA_TQ // 2
MLA_TK = 512
C_DOWN_PAD = 768
MLA_Q_SCALE = (C_NOPE + C_ROPE) ** -0.5 * math.log2(math.e)


def _mla_proj_kernel(x_ref, cos_ref, sin_ref, wd_ref, qn_ref, kvn_ref, wuq_ref, wukv_ref,
                     q_ref, k_ref, v_ref):
    tm = MLA_TM
    xb = x_ref[0].astype(BF16)
    c = _dot(xb, wd_ref[...])
    cq = c[:, :C_Q_RANK]
    ckv = c[:, C_Q_RANK:C_Q_RANK + C_KV_RANK]
    kr_slab = c[:, C_Q_RANK + C_KV_RANK:]
    cq = cq * lax.rsqrt(jnp.mean(cq * cq, axis=-1, keepdims=True) + RMS_EPS) * qn_ref[...]
    ckv = ckv * lax.rsqrt(jnp.mean(ckv * ckv, axis=-1, keepdims=True) + RMS_EPS) * kvn_ref[...]
    q = _dot(cq.astype(BF16), wuq_ref[...]) * MLA_Q_SCALE
    kv = _dot(ckv.astype(BF16), wukv_ref[...])

    cos = cos_ref[...]
    sin = sin_ref[...]
    lane = lax.broadcasted_iota(jnp.int32, (tm, LANES), 1)
    first_half = (lane & 32) == 0
    low = lane < 64
    nn = C_HEADS * C_NOPE

    kr = _rope_slab(kr_slab, cos, sin, first_half)
    kr_pair = (kr.astype(BF16), pltpu.roll(kr, 64, 1).astype(BF16))
    for pair in range(C_HEADS // 2):
        qr = _rope_slab(q[:, nn + pair * LANES:nn + (pair + 1) * LANES], cos, sin, first_half)
        for half in range(2):
            h = 2 * pair + half
            qr_h = jnp.where(low, qr, 0.0) if half == 0 else jnp.where(low, 0.0, qr)
            q_ref[0, h, :, 0:C_NOPE] = q[:, h * C_NOPE:(h + 1) * C_NOPE].astype(BF16)
            q_ref[0, h, :, C_NOPE:C_QK_PAD] = qr_h.astype(BF16)
            k_ref[0, h, :, 0:C_NOPE] = kv[:, h * C_NOPE:(h + 1) * C_NOPE].astype(BF16)
            k_ref[0, h, :, C_NOPE:C_QK_PAD] = kr_pair[half]
            v_ref[0, h] = kv[:, nn + h * C_V:nn + (h + 1) * C_V].astype(BF16)


def _mla_proj(x, cos, sin, wd, qn, kvn, wuq, wukv):
    bsz, s, d = x.shape
    tm = MLA_TM
    qk_shape = jax.ShapeDtypeStruct((bsz, C_HEADS, s, C_QK_PAD), BF16)
    v_shape = jax.ShapeDtypeStruct((bsz, C_HEADS, s, C_V), BF16)
    return pl.pallas_call(
        _mla_proj_kernel,
        grid=(bsz, s // tm),
        in_specs=[pl.BlockSpec((1, tm, d), lambda bi, i: (bi, i, 0)),
                  pl.BlockSpec((tm, LANES), lambda bi, i: (i, 0)),
                  pl.BlockSpec((tm, LANES), lambda bi, i: (i, 0)),
                  _resident(wd.shape), _resident(qn.shape), _resident(kvn.shape),
                  _resident(wuq.shape), _resident(wukv.shape)],
        out_specs=[pl.BlockSpec((1, C_HEADS, tm, C_QK_PAD), lambda bi, i: (bi, 0, i, 0)),
                   pl.BlockSpec((1, C_HEADS, tm, C_QK_PAD), lambda bi, i: (bi, 0, i, 0)),
                   pl.BlockSpec((1, C_HEADS, tm, C_V), lambda bi, i: (bi, 0, i, 0))],
        out_shape=[qk_shape, qk_shape, v_shape],
        compiler_params=_params(("parallel", "parallel")),
        name="mla_proj",
    )(x, cos, sin, wd, qn, kvn, wuq, wukv)


def _mla_attn_kernel(q_ref, k_ref, v_ref, o_ref, m_scr, acc_scr, s_scr):
    i = pl.program_id(2)
    tk, hq = MLA_TK, MLA_HALF

    m_scr[...] = jnp.full_like(m_scr, -jnp.inf)
    acc_scr[...] = jnp.zeros_like(acc_scr)
    ones = jnp.ones((tk, LANES), BF16)

    def scores(slot, hf, j):
        k0 = pl.multiple_of(j * tk, tk)
        s_scr[slot, hf] = _dot_nt(q_ref[0, 0, hf * hq:(hf + 1) * hq, :], k_ref[0, 0, pl.ds(k0, tk), :])

    def consume(slot, hf, j, diagonal):
        k0 = pl.multiple_of(j * tk, tk)
        vv = jnp.concatenate([v_ref[0, 0, pl.ds(k0, tk), :], ones], axis=1)
        s = s_scr[slot, hf]
        if diagonal:
            qpos = lax.broadcasted_iota(jnp.int32, (hq, tk), 0)
            kpos = lax.broadcasted_iota(jnp.int32, (hq, tk), 1)
            s = jnp.where(kpos <= qpos, s, NEG)
        m_prev = m_scr[hf]
        m_next = jnp.maximum(m_prev, jnp.max(s, axis=-1, keepdims=True))
        p = jnp.exp2(s - jnp.tile(m_next, (1, tk // LANES)))
        corr = jnp.exp2(m_prev - m_next)
        acc_scr[hf] = jnp.tile(corr, (1, 2)) * acc_scr[hf] + _dot(p.astype(BF16), vv)
        m_scr[hf] = m_next

    scores(0, 0, 0)
    scores(0, 1, 0)

    def chunk_pair(j):
        scores(1, 0, j + 1)
        scores(1, 1, j + 1)
        consume(0, 0, j, False)
        consume(0, 1, j, False)
        scores(0, 0, j + 2)
        scores(0, 1, j + 2)
        consume(1, 0, j + 1, False)
        consume(1, 1, j + 1, False)

    def body(t, carry):
        chunk_pair(4 * t)
        chunk_pair(4 * t + 2)
        return carry

    lax.fori_loop(0, i // 2, body, 0)

    @pl.when(i % 2 == 1)
    def _():
        chunk_pair(2 * i - 2)

    scores(1, 1, 2 * i + 1)
    consume(0, 0, 2 * i, True)
    consume(0, 1, 2 * i, False)
    consume(1, 1, 2 * i + 1, True)
    for hf in range(2):
        acc = acc_scr[hf]
        o_ref[0, hf * hq:(hf + 1) * hq, :] = (acc[:, :C_V] / acc[:, C_V:]).astype(BF16)


def _mla_attn(q, k, v):
    bsz, nh, s, _ = q.shape
    tq = MLA_TQ
    return pl.pallas_call(
        _mla_attn_kernel,
        grid=(bsz, nh, s // tq),
        in_specs=[pl.BlockSpec((1, 1, tq, C_QK_PAD), lambda b, h, i: (b, h, i, 0)),
                  pl.BlockSpec((1, 1, s, C_QK_PAD), lambda b, h, i: (b, h, 0, 0)),
                  pl.BlockSpec((1, 1, s, C_V), lambda b, h, i: (b, h, 0, 0))],
        out_specs=pl.BlockSpec((1, tq, C_V), lambda b, h, i: (b, i, h)),
        out_shape=jax.ShapeDtypeStruct((bsz, s, nh * C_V), BF16),
        scratch_shapes=[pltpu.VMEM((2, MLA_HALF, LANES), F32),
                        pltpu.VMEM((2, MLA_HALF, 2 * C_V), F32),
                        pltpu.VMEM((2, 2, MLA_HALF, MLA_TK), F32)],
        compiler_params=_params(("parallel", "parallel", "arbitrary")),
        name="mla_attn",
    )(q, k, v)


X_TM = 1024
X_LN_ROWS = 256


def _xattn_kernel(*refs, mixer_prologue):
    tm = X_TM
    if mixer_prologue:
        a_ref, wa_ref, ga_ref, ba_ref, x_ref, wq_ref, mkv_ref, wo_ref, g_ref, b_ref, o_ref = refs
        x = _layer_norm(ALPHA * x_ref[0] + _dot(a_ref[0], wa_ref[...]), ga_ref[...], ba_ref[...])
    else:
        x_ref, wq_ref, mkv_ref, wo_ref, g_ref, b_ref, o_ref = refs
        x = x_ref[0]
    q = (_dot(x.astype(BF16), wq_ref[...]) * (X_HEAD_DIM ** -0.5 * LOG2E)).astype(BF16)
    heads = [(h * X_HEAD_DIM, (h + 1) * X_HEAD_DIM) for h in range(X_HEADS)]
    scores = [_dot_nt(q[:, lo:hi], mkv_ref[0, :, lo:hi]) for lo, hi in heads]
    probs, inv_denoms = [], []
    for s in scores:
        p = jnp.exp2(s - jnp.max(s, axis=-1, keepdims=True))
        inv_denoms.append(1.0 / jnp.sum(p, axis=-1, keepdims=True))
        probs.append(p.astype(BF16))
    outs = [(_dot(probs[h], mkv_ref[0, :, D_MODEL + lo:D_MODEL + hi]) * inv_denoms[h]).astype(BF16)
            for h, (lo, hi) in enumerate(heads)]
    o = jnp.concatenate(outs, axis=-1)
    for r0 in range(0, tm, X_LN_ROWS):
        rows = slice(r0, r0 + X_LN_ROWS)
        y = _dot(o[rows], wo_ref[...])
        o_ref[0, rows, :] = _layer_norm(ALPHA * x[rows] + y, g_ref[...], b_ref[...])


def _xattn_layer(x, wq, mkv, wo, g, b, mixer=None):
    bsz, s, d = x.shape
    tm = X_TM
    tile = pl.BlockSpec((1, tm, d), lambda bi, i: (bi, i, 0))
    in_specs = [tile, _resident((d, d)),
                pl.BlockSpec((1, MEM_LEN, 2 * d), lambda bi, i: (bi, 0, 0)),
                _resident((d, d)), _resident((1, d)), _resident((1, d))]
    args = (x, wq, mkv, wo, g, b)
    if mixer is not None:
        in_specs = [tile, _resident((d, d)), _resident((1, d)), _resident((1, d))] + in_specs
        args = tuple(mixer) + args
    return pl.pallas_call(
        functools.partial(_xattn_kernel, mixer_prologue=mixer is not None),
        grid=(bsz, s // tm),
        in_specs=in_specs,
        out_specs=tile,
        out_shape=jax.ShapeDtypeStruct((bsz, s, d), F32),
        compiler_params=_params(("parallel", "parallel")),
        name="xattn_layer",
    )(*args)


FFN_TM = 512
FFN_LN_ROWS = 256
FFN_DOWN_GROUPS = (4, 8, FFN_NCHUNK)


def _ffn_kernel(x_ref, wup_ref, cw_ref, cb_ref, wdn_ref, g_ref, b_ref, o_ref, hprev_scr, act_scr,
                acc_scr, hstage_scr):
    i = pl.program_id(1)
    tm = FFN_TM

    @pl.when(i == 0)
    def _():
        hprev_scr[...] = jnp.zeros_like(hprev_scr)

    x = x_ref[0]
    xb = x.astype(BF16)

    def up_conv(col0, slot):
        cols = slice(col0, col0 + FFN_CHUNK)
        h = _dot(xb, wup_ref[:, cols])
        prev = hprev_scr[:, cols]
        hprev_scr[:, cols] = h[tm - SUBLANES:, :]
        cw = cw_ref[:, cols]
        cb = cb_ref[:, cols]
        outs = []
        for lt in range(FFN_CHUNK // LANES):
            ls = slice(lt * LANES, (lt + 1) * LANES)
            hs = hstage_scr.at[slot, lt]
            hs[0:SUBLANES, :] = prev[:, ls]
            hs[SUBLANES:SUBLANES + tm, :] = h[:, ls]
            y = hs[SUBLANES - 2:SUBLANES - 2 + tm, :] * cw[0:1, ls]
            y = y + hs[SUBLANES - 1:SUBLANES - 1 + tm, :] * cw[1:2, ls]
            y = y + h[:, ls] * cw[2:3, ls]
            outs.append(y + cb[:, ls])
        return jnp.concatenate(outs, axis=1)

    def down(rows, c0, c1):
        ks = slice(c0 * FFN_CHUNK, c1 * FFN_CHUNK)
        return _dot(act_scr[rows, ks], wdn_ref[ks, :])

    everything = slice(0, tm)
    for c in range(FFN_NCHUNK):
        gt = up_conv(c * FFN_CHUNK, 2 * (c % 2))
        val = up_conv(D_FF + c * FFN_CHUNK, 2 * (c % 2) + 1)
        act_scr[:, c * FFN_CHUNK:(c + 1) * FFN_CHUNK] = ((gt * _sigmoid(gt)) * val).astype(BF16)
        if c + 1 in FFN_DOWN_GROUPS[:-1]:
            gi = FFN_DOWN_GROUPS.index(c + 1)
            if gi == 0:
                acc_scr[...] = down(everything, 0, c + 1)
            else:
                acc_scr[...] += down(everything, FFN_DOWN_GROUPS[gi - 1], c + 1)
    for r0 in range(0, tm, FFN_LN_ROWS):
        rows = slice(r0, r0 + FFN_LN_ROWS)
        out = acc_scr[rows, :] + down(rows, FFN_DOWN_GROUPS[-2], FFN_DOWN_GROUPS[-1])
        o_ref[0, rows, :] = _layer_norm(ALPHA * x[rows] + out, g_ref[...], b_ref[...])


def _ffn_layer(x, wup, cw, cb, wdn, g, b):
    bsz, s, d = x.shape
    tm = FFN_TM
    return pl.pallas_call(
        _ffn_kernel,
        grid=(bsz, s // tm),
        in_specs=[pl.BlockSpec((1, tm, d), lambda bi, i: (bi, i, 0)),
                  _resident((d, 2 * D_FF)), _resident((FFN_CONV, 2 * D_FF)), _resident((1, 2 * D_FF)),
                  _resident((D_FF, d)), _resident((1, d)), _resident((1, d))],
        out_specs=pl.BlockSpec((1, tm, d), lambda bi, i: (bi, i, 0)),
        out_shape=jax.ShapeDtypeStruct((bsz, s, d), F32),
        scratch_shapes=[pltpu.VMEM((SUBLANES, 2 * D_FF), F32), pltpu.VMEM((tm, D_FF), BF16),
                        pltpu.VMEM((tm, d), F32),
                        pltpu.VMEM((4, FFN_CHUNK // LANES, SUBLANES + tm, LANES), F32)],
        compiler_params=_params(("arbitrary", "arbitrary")),
        name="ffn_layer",
    )(x, wup, cw, cb, wdn, g, b)


def kernel(x, mem, a_w_qkv, a_sinks, a_w_o, b_w_in, b_conv_w, b_conv_b, b_w_rgate, b_b_rgate, b_w_igate, b_b_igate, b_lambda, b_w_o, c_w_down, c_q_norm, c_kv_norm, c_w_uq, c_w_ukv, c_w_o, mem_w_kv, x_w_q, x_w_o, f_w_up, f_conv_w, f_conv_b, f_w_down, ln_g, ln_b):
    bsz, s, d = x.shape
    cos, sin = _rope_tables(s)

    mkv = _matmul(mem.reshape(bsz * MEM_LEN, d), mem_w_kv.astype(BF16), 512, BF16)
    mkv = mkv.reshape(bsz, MEM_LEN, 2 * d)

    def row(v):
        return v.reshape(1, -1)

    for i in range(DEPTH):
        kind, j = i % N_MIXERS, i // N_MIXERS
        g0, b0 = row(ln_g[i, 0]), row(ln_b[i, 0])
        mixer = None
        if kind == 0:
            x = _swa_layer(x, a_sinks[j], cos, sin, a_w_qkv[j].astype(BF16), a_w_o[j].astype(BF16), g0, b0)
        elif kind == 1:
            x = _rglru_layer(x, b_w_in[j].astype(BF16), b_conv_w[j], row(b_conv_b[j]),
                             b_w_rgate[j].astype(BF16), row(b_b_rgate[j]),
                             b_w_igate[j].astype(BF16), row(b_b_igate[j]),
                             row(b_lambda[j]), b_w_o[j].astype(BF16), g0, b0)
        else:
            wd = jnp.pad(c_w_down[j], ((0, 0), (0, C_DOWN_PAD - c_w_down.shape[-1]))).astype(BF16)
            wuq = c_w_uq[j].reshape(C_Q_RANK, C_HEADS, C_NOPE + C_ROPE)
            wuq = jnp.concatenate([wuq[:, :, :C_NOPE].reshape(C_Q_RANK, -1),
                                   wuq[:, :, C_NOPE:].reshape(C_Q_RANK, -1)], axis=-1).astype(BF16)
            wukv = c_w_ukv[j].reshape(C_KV_RANK, C_HEADS, C_NOPE + C_V)
            wukv = jnp.concatenate([wukv[:, :, :C_NOPE].reshape(C_KV_RANK, -1),
                                    wukv[:, :, C_NOPE:].reshape(C_KV_RANK, -1)], axis=-1).astype(BF16)
            q, k, v = _mla_proj(x, cos, sin, wd, row(c_q_norm[j]), row(c_kv_norm[j]), wuq, wukv)
            mixer = (_mla_attn(q, k, v), c_w_o[j].astype(BF16), g0, b0)
        x = _xattn_layer(x, x_w_q[i].astype(BF16), mkv, x_w_o[i].astype(BF16),
                         row(ln_g[i, 1]), row(ln_b[i, 1]), mixer=mixer)
        x = _ffn_layer(x, f_w_up[i].astype(BF16), f_conv_w[i], row(f_conv_b[i]), f_w_down[i].astype(BF16),
                       row(ln_g[i, 2]), row(ln_b[i, 2]))
    return x
```

```python
import functools
import math

import jax
import jax.numpy as jnp
import numpy as np
from jax import lax
from jax.experimental import pallas as pl
from jax.experimental.pallas import tpu as pltpu

D_MODEL = 1024
DEPTH = 4
N_MIXERS = 3
MEM_LEN = 256
BLOCK = 128
ROPE_THETA = 10000.0
NEG = -1e30
LN_EPS = 1e-5
RMS_EPS = 1e-6

A_HEADS = 16
A_KV_HEADS = 4
A_HEAD_DIM = 64

LRU_WIDTH = D_MODEL
LRU_BLOCKS = 4
LRU_BLOCK_W = LRU_WIDTH // LRU_BLOCKS
LRU_CONV = 4
LRU_C = 8.0

C_HEADS = 8
C_NOPE = 128
C_ROPE = 64
C_V = 128
C_Q_RANK = 384
C_KV_RANK = 256
C_QK_PAD = 256

X_HEADS = 4
X_HEAD_DIM = D_MODEL // X_HEADS

D_FF = 2816
FFN_CONV = 3
FFN_CHUNK = 256
FFN_NCHUNK = D_FF // FFN_CHUNK

ALPHA = (2.0 * DEPTH) ** 0.25

LANES = 128
SUBLANES = 8
VMEM_LIMIT = 56 * 1024 * 1024

BF16 = jnp.bfloat16
F32 = jnp.float32

NT_DIMS = (((1,), (1,)), ((), ()))
LOG2E = math.log2(math.e)


def _dot(a, b):
    return jnp.dot(a, b, preferred_element_type=F32)


def _dot_nt(a, b):
    return lax.dot_general(a, b, NT_DIMS, preferred_element_type=F32)


def _layer_norm(z, g, b):
    mu = jnp.mean(z, axis=-1, keepdims=True)
    zc = z - mu
    var = jnp.mean(zc * zc, axis=-1, keepdims=True)
    return zc * lax.rsqrt(var + LN_EPS) * g + b


def _sigmoid(z):
    return 0.5 * jnp.tanh(0.5 * z) + 0.5


def _resident(shape):
    nd = len(shape)
    return pl.BlockSpec(shape, lambda *_: (0,) * nd, pipeline_mode=pl.Buffered(1))


def _params(sem, flags=None):
    return pltpu.CompilerParams(dimension_semantics=sem, vmem_limit_bytes=VMEM_LIMIT, flags=flags)


def _matmul_kernel(a_ref, w_ref, o_ref):
    o_ref[...] = _dot(a_ref[...].astype(BF16), w_ref[...]).astype(o_ref.dtype)


def _matmul(a, w, tm, out_dtype):
    t, k = a.shape
    n = w.shape[1]
    return pl.pallas_call(
        _matmul_kernel,
        grid=(t // tm,),
        in_specs=[pl.BlockSpec((tm, k), lambda i: (i, 0)), _resident((k, n))],
        out_specs=pl.BlockSpec((tm, n), lambda i: (i, 0)),
        out_shape=jax.ShapeDtypeStruct((t, n), out_dtype),
        compiler_params=_params(("parallel",)),
        name="matmul",
    )(a, w)


def _rope_slab(slab, cos, sin_signed, first_half):
    rot = jnp.where(first_half, pltpu.roll(slab, LANES - 32, 1), pltpu.roll(slab, 32, 1))
    return slab * cos + rot * sin_signed


def _rope_tables(seq):
    inv = 1.0 / (ROPE_THETA ** (jnp.arange(0, A_HEAD_DIM, 2, dtype=F32) / A_HEAD_DIM))
    ang = jnp.arange(seq, dtype=F32)[:, None] * inv[None, :]
    cos, sin = jnp.cos(ang), jnp.sin(ang)
    return (jnp.concatenate([cos, cos, cos, cos], axis=-1),
            jnp.concatenate([-sin, sin, -sin, sin], axis=-1))


SWA_TM = 512
SWA_NBLK = SWA_TM // BLOCK


def _swa_kernel(sinks_ref, x_ref, cos_ref, sin_ref, wqkv_ref, wo_ref, g_ref, b_ref, o_ref,
                kext_scr, vext_scr, q_scr, o_scr):
    i = pl.program_id(1)
    tm = SWA_TM

    @pl.when(i == 0)
    def _():
        kext_scr[:, 0:BLOCK, :] = jnp.zeros((2 * A_KV_HEADS, BLOCK, LANES), BF16)
        vext_scr[:, 0:BLOCK, :] = jnp.zeros((2 * A_KV_HEADS, BLOCK, LANES), BF16)

    @pl.when(i > 0)
    def _():
        kext_scr[:, 0:BLOCK, :] = kext_scr[:, tm:tm + BLOCK, :]
        vext_scr[:, 0:BLOCK, :] = vext_scr[:, tm:tm + BLOCK, :]

    x = x_ref[0]
    qkv = _dot(x.astype(BF16), wqkv_ref[...])
    cos = cos_ref[...]
    sin = sin_ref[...]
    lane = lax.broadcasted_iota(jnp.int32, (tm, LANES), 1)
    first_half = (lane & 32) == 0
    low = lane < 64

    nq = A_HEADS * A_HEAD_DIM
    nkv = A_KV_HEADS * A_HEAD_DIM
    for j in range(nq // LANES):
        slab = _rope_slab(qkv[:, j * LANES:(j + 1) * LANES], cos, sin, first_half)
        q_scr[:, j * LANES:(j + 1) * LANES] = (slab * (A_HEAD_DIM ** -0.5 * LOG2E)).astype(BF16)

    for m in range(nkv // LANES):
        kslab = _rope_slab(qkv[:, nq + m * LANES:nq + (m + 1) * LANES], cos, sin, first_half)
        vslab = qkv[:, nq + nkv + m * LANES:nq + nkv + (m + 1) * LANES]
        for slab, ext in ((kslab, kext_scr), (vslab, vext_scr)):
            swapped = pltpu.roll(slab, 64, 1)
            h0, h1 = 2 * m, 2 * m + 1
            ext[2 * h0, BLOCK:BLOCK + tm, :] = jnp.where(low, slab, 0.0).astype(BF16)
            ext[2 * h0 + 1, BLOCK:BLOCK + tm, :] = jnp.where(low, 0.0, swapped).astype(BF16)
            ext[2 * h1, BLOCK:BLOCK + tm, :] = jnp.where(low, swapped, 0.0).astype(BF16)
            ext[2 * h1 + 1, BLOCK:BLOCK + tm, :] = jnp.where(low, 0.0, slab).astype(BF16)

    qi = lax.broadcasted_iota(jnp.int32, (2 * BLOCK, 4 * BLOCK), 0) & (BLOCK - 1)
    kj = lax.broadcasted_iota(jnp.int32, (2 * BLOCK, 4 * BLOCK), 1) & (2 * BLOCK - 1)
    dist = qi + BLOCK - kj
    band = (dist >= 0) & (dist < BLOCK)
    band_first = band & ((kj >= BLOCK) | (i > 0))
    top = lax.broadcasted_iota(jnp.int32, (2 * BLOCK, 1), 0) < BLOCK
    low2 = lax.broadcasted_iota(jnp.int32, (2 * BLOCK, LANES), 1) < 64

    def window(ext, h, r0):
        return jnp.concatenate([ext[2 * h, r0:r0 + 2 * BLOCK, :], ext[2 * h + 1, r0:r0 + 2 * BLOCK, :]],
                               axis=0)

    for n in range(SWA_NBLK):
        mask = band_first if n == 0 else band
        r0 = n * BLOCK
        scores = []
        for h in range(A_KV_HEADS):
            qs = jnp.concatenate([q_scr[r0:r0 + BLOCK, (2 * h) * LANES:(2 * h + 1) * LANES],
                                  q_scr[r0:r0 + BLOCK, (2 * h + 1) * LANES:(2 * h + 2) * LANES]], axis=0)
            scores.append(_dot_nt(qs, window(kext_scr, h, r0)))
        probs, inv_denoms = [], []
        for h in range(A_KV_HEADS):
            s = jnp.where(mask, scores[h], NEG)
            ps, rs = [], []
            for half in range(2):
                sh = s[:, half * 2 * BLOCK:(half + 1) * 2 * BLOCK]
                sink = jnp.where(top, sinks_ref[4 * h + half] * LOG2E, sinks_ref[4 * h + 2 + half] * LOG2E)
                mx = jnp.maximum(jnp.max(sh, axis=-1, keepdims=True), sink)
                p = jnp.exp2(sh - mx)
                rs.append(1.0 / (jnp.sum(p, axis=-1, keepdims=True) + jnp.exp2(sink - mx)))
                ps.append(p.astype(BF16))
            probs.append(jnp.concatenate(ps, axis=1))
            inv_denoms.append(jnp.where(low2, rs[0], rs[1]))
        for h in range(A_KV_HEADS):
            o = _dot(probs[h], window(vext_scr, h, r0)) * inv_denoms[h]
            o_scr[r0:r0 + BLOCK, (2 * h) * LANES:(2 * h + 1) * LANES] = o[:BLOCK].astype(BF16)
            o_scr[r0:r0 + BLOCK, (2 * h + 1) * LANES:(2 * h + 2) * LANES] = o[BLOCK:].astype(BF16)

    for r0 in range(0, tm, tm // 2):
        rows = slice(r0, r0 + tm // 2)
        y = _dot(o_scr[rows, :], wo_ref[...])
        o_ref[0, rows, :] = _layer_norm(ALPHA * x[rows] + y, g_ref[...], b_ref[...])


def _swa_layer(x, sinks, cos, sin, wqkv, wo, g, b):
    bsz, s, d = x.shape
    tm = SWA_TM
    nqkv = wqkv.shape[1]
    return pl.pallas_call(
        _swa_kernel,
        grid=(bsz, s // tm),
        in_specs=[pl.BlockSpec(memory_space=pltpu.SMEM),
                  pl.BlockSpec((1, tm, d), lambda bi, i: (bi, i, 0)),
                  pl.BlockSpec((tm, LANES), lambda bi, i: (i, 0)),
                  pl.BlockSpec((tm, LANES), lambda bi, i: (i, 0)),
                  _resident((d, nqkv)), _resident((d, d)), _resident((1, d)), _resident((1, d))],
        out_specs=pl.BlockSpec((1, tm, d), lambda bi, i: (bi, i, 0)),
        out_shape=jax.ShapeDtypeStruct((bsz, s, d), F32),
        scratch_shapes=[pltpu.VMEM((2 * A_KV_HEADS, tm + BLOCK, LANES), BF16),
                        pltpu.VMEM((2 * A_KV_HEADS, tm + BLOCK, LANES), BF16),
                        pltpu.VMEM((tm, d), BF16),
                        pltpu.VMEM((tm, d), BF16)],
        compiler_params=_params(("arbitrary", "arbitrary")),
        name="swa_layer",
    )(sinks, x, cos, sin, wqkv, wo, g, b)


LRU_TM = 256
LRU_NLT = LRU_WIDTH // LANES
LRU_PAD = 2 * SUBLANES


def _rglru_kernel(x_ref, win_ref, cw_ref, cb_ref, wr_ref, br_ref, wi_ref, bi_ref, lam_ref, wo_ref,
                  g_ref, b_ref, o_ref, ustage_scr, a_scr, b_scr, hcar_scr):
    i = pl.program_id(1)
    tm = LRU_TM
    w = LRU_WIDTH
    pad = LRU_PAD
    sl = SUBLANES

    @pl.when(i == 0)
    def _():
        ustage_scr[:, 0:sl, :] = jnp.zeros((LRU_NLT, sl, LANES), F32)
        a_scr[:, :, 0:pad, :] = jnp.ones((3, LRU_NLT, pad, LANES), F32)
        b_scr[:, :, 0:pad, :] = jnp.zeros((3, LRU_NLT, pad, LANES), F32)
        hcar_scr[...] = jnp.zeros_like(hcar_scr)

    @pl.when(i > 0)
    def _():
        ustage_scr[:, 0:sl, :] = ustage_scr[:, tm:tm + sl, :]
        a_scr[0, :, sl:pad, :] = a_scr[0, :, tm + sl:tm + pad, :]
        b_scr[0, :, sl:pad, :] = b_scr[0, :, tm + sl:tm + pad, :]

    x = x_ref[0]
    xb = x.astype(BF16)
    gate = _dot(xb, win_ref[:, :w])
    u = _dot(xb, win_ref[:, w:])
    cw = cw_ref[...]
    cb = cb_ref[...]
    ucs = []
    for lt in range(LRU_NLT):
        ls = slice(lt * LANES, (lt + 1) * LANES)
        us = ustage_scr.at[lt]
        us[sl:sl + tm, :] = u[:, ls]
        uc = us[sl - 3:sl - 3 + tm, :] * cw[0:1, ls]
        uc = uc + us[sl - 2:sl - 2 + tm, :] * cw[1:2, ls]
        uc = uc + us[sl - 1:sl - 1 + tm, :] * cw[2:3, ls]
        uc = uc + u[:, ls] * cw[3:4, ls]
        ucs.append(uc + cb[:, ls])
    uc = jnp.concatenate(ucs, axis=1)

    ucb = uc.astype(BF16)
    rs, igs = [], []
    for h in range(LRU_BLOCKS):
        blk = ucb[:, h * LRU_BLOCK_W:(h + 1) * LRU_BLOCK_W]
        rs.append(_dot(blk, wr_ref[h]))
        igs.append(_dot(blk, wi_ref[h]))
    r = _sigmoid(jnp.concatenate(rs, axis=-1) + br_ref[...])
    ig = _sigmoid(jnp.concatenate(igs, axis=-1) + bi_ref[...])
    z = -lam_ref[...]
    softplus = jnp.maximum(z, 0.0) + jnp.log1p(jnp.exp(-jnp.abs(z)))
    log_a = (-LRU_C * r) * softplus
    a0 = jnp.exp(log_a)
    th = jnp.tanh(log_a)
    b0 = (jnp.sqrt(-2.0 * th) * lax.rsqrt(1.0 - th)) * (ig * uc)

    a8, b8 = [], []
    for lt in range(LRU_NLT):
        ls = slice(lt * LANES, (lt + 1) * LANES)
        a_scr[0, lt, pad:pad + tm, :] = a0[:, ls]
        b_scr[0, lt, pad:pad + tm, :] = b0[:, ls]
        for k, d in enumerate((1, 2)):
            a_cur = a_scr[k, lt, sl:pad + tm, :]
            b_cur = b_scr[k, lt, sl:pad + tm, :]
            a_scr[k + 1, lt, sl:pad + tm, :] = a_cur * a_scr[k, lt, sl - d:pad + tm - d, :]
            b_scr[k + 1, lt, sl:pad + tm, :] = a_cur * b_scr[k, lt, sl - d:pad + tm - d, :] + b_cur
        a_cur = a_scr[2, lt, pad:pad + tm, :]
        a8.append(a_cur * a_scr[2, lt, pad - 4:pad - 4 + tm, :])
        b8.append(a_cur * b_scr[2, lt, pad - 4:pad - 4 + tm, :] + b_scr[2, lt, pad:pad + tm, :])
    a8 = jnp.concatenate(a8, axis=1)
    b8 = jnp.concatenate(b8, axis=1)

    hblk = hcar_scr[...]
    hs = []
    for j in range(tm // sl):
        hblk = a8[j * sl:(j + 1) * sl] * hblk + b8[j * sl:(j + 1) * sl]
        hs.append(hblk)
    hcar_scr[...] = hblk
    h = jnp.concatenate(hs, axis=0)

    c0 = math.sqrt(2.0 / math.pi)
    gelu = gate * (0.5 * (1.0 + jnp.tanh(c0 * (gate + 0.044715 * (gate * gate * gate)))))
    y = _dot((h * gelu).astype(BF16), wo_ref[...])
    o_ref[0] = _layer_norm(ALPHA * x + y, g_ref[...], b_ref[...])


def _rglru_layer(x, win, cw, cb, wr, br, wi, bi, lam, wo, g, b):
    bsz, s, d = x.shape
    tm = LRU_TM
    w = LRU_WIDTH
    return pl.pallas_call(
        _rglru_kernel,
        grid=(bsz, s // tm),
        in_specs=[pl.BlockSpec((1, tm, d), lambda bi_, i: (bi_, i, 0)),
                  _resident((d, 2 * w)), _resident((LRU_CONV, w)), _resident((1, w)),
                  _resident((LRU_BLOCKS, LRU_BLOCK_W, LRU_BLOCK_W)), _resident((1, w)),
                  _resident((LRU_BLOCKS, LRU_BLOCK_W, LRU_BLOCK_W)), _resident((1, w)),
                  _resident((1, w)), _resident((w, d)), _resident((1, d)), _resident((1, d))],
        out_specs=pl.BlockSpec((1, tm, d), lambda bi_, i: (bi_, i, 0)),
        out_shape=jax.ShapeDtypeStruct((bsz, s, d), F32),
        scratch_shapes=[pltpu.VMEM((LRU_NLT, SUBLANES + tm, LANES), F32),
                        pltpu.VMEM((3, LRU_NLT, LRU_PAD + tm, LANES), F32),
                        pltpu.VMEM((3, LRU_NLT, LRU_PAD + tm, LANES), F32),
                        pltpu.VMEM((SUBLANES, w), F32)],
        compiler_params=_params(("arbitrary", "arbitrary")),
        name="rglru_layer",
    )(x, win, cw, cb, wr, br, wi, bi, lam, wo, g, b)


MLA_TM = 512
MLA_TQ = 1024
MLA_HALF = MLA_TQ // 2
MLA_TK = 512
C_DOWN_PAD = 768
MLA_Q_SCALE = (C_NOPE + C_ROPE) ** -0.5 * math.log2(math.e)


def _mla_proj_kernel(x_ref, cos_ref, sin_ref, wd_ref, qn_ref, kvn_ref, wuq_ref, wukv_ref,
                     q_ref, k_ref, v_ref):
    tm = MLA_TM
    xb = x_ref[0].astype(BF16)
    c = _dot(xb, wd_ref[...])
    cq = c[:, :C_Q_RANK]
    ckv = c[:, C_Q_RANK:C_Q_RANK + C_KV_RANK]
    kr_slab = c[:, C_Q_RANK + C_KV_RANK:]
    cq = cq * lax.rsqrt(jnp.mean(cq * cq, axis=-1, keepdims=True) + RMS_EPS) * qn_ref[...]
    ckv = ckv * lax.rsqrt(jnp.mean(ckv * ckv, axis=-1, keepdims=True) + RMS_EPS) * kvn_ref[...]
    q = _dot(cq.astype(BF16), wuq_ref[...]) * MLA_Q_SCALE
    kv = _dot(ckv.astype(BF16), wukv_ref[...])

    cos = cos_ref[...]
    sin = sin_ref[...]
    lane = lax.broadcasted_iota(jnp.int32, (tm, LANES), 1)
    first_half = (lane & 32) == 0
    low = lane < 64
    nn = C_HEADS * C_NOPE

    kr = _rope_slab(kr_slab, cos, sin, first_half)
    kr_pair = (kr.astype(BF16), pltpu.roll(kr, 64, 1).astype(BF16))
    for pair in range(C_HEADS // 2):
        qr = _rope_slab(q[:, nn + pair * LANES:nn + (pair + 1) * LANES], cos, sin, first_half)
        for half in range(2):
            h = 2 * pair + half
            qr_h = jnp.where(low, qr, 0.0) if half == 0 else jnp.where(low, 0.0, qr)
            q_ref[0, h, :, 0:C_NOPE] = q[:, h * C_NOPE:(h + 1) * C_NOPE].astype(BF16)
            q_ref[0, h, :, C_NOPE:C_QK_PAD] = qr_h.astype(BF16)
            k_ref[0, h, :, 0:C_NOPE] = kv[:, h * C_NOPE:(h + 1) * C_NOPE].astype(BF16)
            k_ref[0, h, :, C_NOPE:C_QK_PAD] = kr_pair[half]
            v_ref[0, h] = kv[:, nn + h * C_V:nn + (h + 1) * C_V].astype(BF16)


def _mla_proj(x, cos, sin, wd, qn, kvn, wuq, wukv):
    bsz, s, d = x.shape
    tm = MLA_TM
    qk_shape = jax.ShapeDtypeStruct((bsz, C_HEADS, s, C_QK_PAD), BF16)
    v_shape = jax.ShapeDtypeStruct((bsz, C_HEADS, s, C_V), BF16)
    return pl.pallas_call(
        _mla_proj_kernel,
        grid=(bsz, s // tm),
        in_specs=[pl.BlockSpec((1, tm, d), lambda bi, i: (bi, i, 0)),
                  pl.BlockSpec((tm, LANES), lambda bi, i: (i, 0)),
                  pl.BlockSpec((tm, LANES), lambda bi, i: (i, 0)),
                  _resident(wd.shape), _resident(qn.shape), _resident(kvn.shape),
                  _resident(wuq.shape), _resident(wukv.shape)],
        out_specs=[pl.BlockSpec((1, C_HEADS, tm, C_QK_PAD), lambda bi, i: (bi, 0, i, 0)),
                   pl.BlockSpec((1, C_HEADS, tm, C_QK_PAD), lambda bi, i: (bi, 0, i, 0)),
                   pl.BlockSpec((1, C_HEADS, tm, C_V), lambda bi, i: (bi, 0, i, 0))],
        out_shape=[qk_shape, qk_shape, v_shape],
        compiler_params=_params(("parallel", "parallel")),
        name="mla_proj",
    )(x, cos, sin, wd, qn, kvn, wuq, wukv)


def _mla_attn_kernel(q_ref, k_ref, v_ref, o_ref, m_scr, acc_scr, s_scr):
    i = pl.program_id(2)
    tk, hq = MLA_TK, MLA_HALF

    m_scr[...] = jnp.full_like(m_scr, -jnp.inf)
    acc_scr[...] = jnp.zeros_like(acc_scr)
    ones = jnp.ones((tk, LANES), BF16)

    def scores(slot, hf, j):
        k0 = pl.multiple_of(j * tk, tk)
        s_scr[slot, hf] = _dot_nt(q_ref[0, 0, hf * hq:(hf + 1) * hq, :], k_ref[0, 0, pl.ds(k0, tk), :])

    def consume(slot, hf, j, diagonal):
        k0 = pl.multiple_of(j * tk, tk)
        vv = jnp.concatenate([v_ref[0, 0, pl.ds(k0, tk), :], ones], axis=1)
        s = s_scr[slot, hf]
        if diagonal:
            qpos = lax.broadcasted_iota(jnp.int32, (hq, tk), 0)
            kpos = lax.broadcasted_iota(jnp.int32, (hq, tk), 1)
            s = jnp.where(kpos <= qpos, s, NEG)
        m_prev = m_scr[hf]
        m_next = jnp.maximum(m_prev, jnp.max(s, axis=-1, keepdims=True))
        p = jnp.exp2(s - jnp.tile(m_next, (1, tk // LANES)))
        corr = jnp.exp2(m_prev - m_next)
        acc_scr[hf] = jnp.tile(corr, (1, 2)) * acc_scr[hf] + _dot(p.astype(BF16), vv)
        m_scr[hf] = m_next

    def chunk_pair(j, prefetch=True):
        scores(1, 0, j + 1)
        scores(1, 1, j + 1)
        consume(0, 0, j, False)
        consume(0, 1, j, False)
        if prefetch:
            scores(0, 0, j + 2)
            scores(0, 1, j + 2)
        consume(1, 0, j + 1, False)
        consume(1, 1, j + 1, False)

    def diagonal_scores():
        scores(2, 1, 2 * i + 1)
        scores(1, 0, 2 * i)
        scores(1, 1, 2 * i)

    @pl.when(i == 0)
    def _():
        diagonal_scores()
        consume(2, 1, 2 * i + 1, True)
        consume(1, 0, 2 * i, True)
        consume(1, 1, 2 * i, False)

    @pl.when(i > 0)
    def _():
        diagonal_scores()
        consume(2, 1, 2 * i + 1, True)
        scores(0, 0, 0)
        scores(0, 1, 0)
        consume(1, 0, 2 * i, True)
        consume(1, 1, 2 * i, False)

        def body(t, carry):
            chunk_pair(4 * t)
            chunk_pair(4 * t + 2)
            return carry

        lax.fori_loop(0, (i - 1) // 2, body, 0)

        @pl.when((i - 1) % 2 == 1)
        def _():
            chunk_pair(2 * i - 4)

        chunk_pair(2 * i - 2, prefetch=False)

    for hf in range(2):
        acc = acc_scr[hf]
        o_ref[0, hf * hq:(hf + 1) * hq, :] = (acc[:, :C_V] / acc[:, C_V:]).astype(BF16)


def _mla_attn(q, k, v):
    bsz, nh, s, _ = q.shape
    tq = MLA_TQ
    return pl.pallas_call(
        _mla_attn_kernel,
        grid=(bsz, nh, s // tq),
        in_specs=[pl.BlockSpec((1, 1, tq, C_QK_PAD), lambda b, h, i: (b, h, i, 0)),
                  pl.BlockSpec((1, 1, s, C_QK_PAD), lambda b, h, i: (b, h, 0, 0)),
                  pl.BlockSpec((1, 1, s, C_V), lambda b, h, i: (b, h, 0, 0))],
        out_specs=pl.BlockSpec((1, tq, C_V), lambda b, h, i: (b, i, h)),
        out_shape=jax.ShapeDtypeStruct((bsz, s, nh * C_V), BF16),
        scratch_shapes=[pltpu.VMEM((2, MLA_HALF, LANES), F32),
                        pltpu.VMEM((2, MLA_HALF, 2 * C_V), F32),
                        pltpu.VMEM((3, 2, MLA_HALF, MLA_TK), F32)],
        compiler_params=_params(("parallel", "parallel", "arbitrary")),
        name="mla_attn",
    )(q, k, v)


X_TM = 1024
X_LN_ROWS = 256


def _xattn_kernel(*refs, mixer_prologue):
    tm = X_TM
    if mixer_prologue:
        a_ref, wa_ref, ga_ref, ba_ref, x_ref, wq_ref, mkv_ref, wo_ref, g_ref, b_ref, o_ref = refs
        x = _layer_norm(ALPHA * x_ref[0] + _dot(a_ref[0], wa_ref[...]), ga_ref[...], ba_ref[...])
    else:
        x_ref, wq_ref, mkv_ref, wo_ref, g_ref, b_ref, o_ref = refs
        x = x_ref[0]
    q = (_dot(x.astype(BF16), wq_ref[...]) * (X_HEAD_DIM ** -0.5 * LOG2E)).astype(BF16)
    heads = [(h * X_HEAD_DIM, (h + 1) * X_HEAD_DIM) for h in range(X_HEADS)]
    scores = [_dot_nt(q[:, lo:hi], mkv_ref[0, :, lo:hi]) for lo, hi in heads]
    probs, inv_denoms = [], []
    for s in scores:
        p = jnp.exp2(s - jnp.max(s, axis=-1, keepdims=True))
        inv_denoms.append(1.0 / jnp.sum(p, axis=-1, keepdims=True))
        probs.append(p.astype(BF16))
    outs = [(_dot(probs[h], mkv_ref[0, :, D_MODEL + lo:D_MODEL + hi]) * inv_denoms[h]).astype(BF16)
            for h, (lo, hi) in enumerate(heads)]
    o = jnp.concatenate(outs, axis=-1)
    for r0 in range(0, tm, X_LN_ROWS):
        rows = slice(r0, r0 + X_LN_ROWS)
        y = _dot(o[rows], wo_ref[...])
        o_ref[0, rows, :] = _layer_norm(ALPHA * x[rows] + y, g_ref[...], b_ref[...])


def _xattn_layer(x, wq, mkv, wo, g, b, mixer=None):
    bsz, s, d = x.shape
    tm = X_TM
    tile = pl.BlockSpec((1, tm, d), lambda bi, i: (bi, i, 0))
    in_specs = [tile, _resident((d, d)),
                pl.BlockSpec((1, MEM_LEN, 2 * d), lambda bi, i: (bi, 0, 0)),
                _resident((d, d)), _resident((1, d)), _resident((1, d))]
    args = (x, wq, mkv, wo, g, b)
    if mixer is not None:
        in_specs = [tile, _resident((d, d)), _resident((1, d)), _resident((1, d))] + in_specs
        args = tuple(mixer) + args
    return pl.pallas_call(
        functools.partial(_xattn_kernel, mixer_prologue=mixer is not None),
        grid=(bsz, s // tm),
        in_specs=in_specs,
        out_specs=tile,
        out_shape=jax.ShapeDtypeStruct((bsz, s, d), F32),
        compiler_params=_params(("parallel", "parallel")),
        name="xattn_layer",
    )(*args)


FFN_TM = 512
FFN_LN_ROWS = 256
FFN_DOWN_GROUPS = (4, 8, FFN_NCHUNK)


def _ffn_kernel(x_ref, wup_ref, cw_ref, cb_ref, wdn_ref, g_ref, b_ref, o_ref, hprev_scr, act_scr,
                acc_scr, hstage_scr):
    i = pl.program_id(1)
    tm = FFN_TM

    @pl.when(i == 0)
    def _():
        hprev_scr[...] = jnp.zeros_like(hprev_scr)

    x = x_ref[0]
    xb = x.astype(BF16)

    def up_conv(col0, slot):
        cols = slice(col0, col0 + FFN_CHUNK)
        h = _dot(xb, wup_ref[:, cols])
        prev = hprev_scr[:, cols]
        hprev_scr[:, cols] = h[tm - SUBLANES:, :]
        cw = cw_ref[:, cols]
        cb = cb_ref[:, cols]
        outs = []
        for lt in range(FFN_CHUNK // LANES):
            ls = slice(lt * LANES, (lt + 1) * LANES)
            hs = hstage_scr.at[slot, lt]
            hs[0:SUBLANES, :] = prev[:, ls]
            hs[SUBLANES:SUBLANES + tm, :] = h[:, ls]
            y = hs[SUBLANES - 2:SUBLANES - 2 + tm, :] * cw[0:1, ls]
            y = y + hs[SUBLANES - 1:SUBLANES - 1 + tm, :] * cw[1:2, ls]
            y = y + h[:, ls] * cw[2:3, ls]
            outs.append(y + cb[:, ls])
        return jnp.concatenate(outs, axis=1)

    def down(rows, c0, c1):
        ks = slice(c0 * FFN_CHUNK, c1 * FFN_CHUNK)
        return _dot(act_scr[rows, ks], wdn_ref[ks, :])

    everything = slice(0, tm)
    for c in range(FFN_NCHUNK):
        gt = up_conv(c * FFN_CHUNK, 2 * (c % 2))
        val = up_conv(D_FF + c * FFN_CHUNK, 2 * (c % 2) + 1)
        act_scr[:, c * FFN_CHUNK:(c + 1) * FFN_CHUNK] = ((gt * _sigmoid(gt)) * val).astype(BF16)
        if c + 1 in FFN_DOWN_GROUPS[:-1]:
            gi = FFN_DOWN_GROUPS.index(c + 1)
            if gi == 0:
                acc_scr[...] = down(everything, 0, c + 1)
            else:
                acc_scr[...] += down(everything, FFN_DOWN_GROUPS[gi - 1], c + 1)
    for r0 in range(0, tm, FFN_LN_ROWS):
        rows = slice(r0, r0 + FFN_LN_ROWS)
        out = acc_scr[rows, :] + down(rows, FFN_DOWN_GROUPS[-2], FFN_DOWN_GROUPS[-1])
        o_ref[0, rows, :] = _layer_norm(ALPHA * x[rows] + out, g_ref[...], b_ref[...])


def _ffn_layer(x, wup, cw, cb, wdn, g, b):
    bsz, s, d = x.shape
    tm = FFN_TM
    return pl.pallas_call(
        _ffn_kernel,
        grid=(bsz, s // tm),
        in_specs=[pl.BlockSpec((1, tm, d), lambda bi, i: (bi, i, 0)),
                  _resident((d, 2 * D_FF)), _resident((FFN_CONV, 2 * D_FF)), _resident((1, 2 * D_FF)),
                  _resident((D_FF, d)), _resident((1, d)), _resident((1, d))],
        out_specs=pl.BlockSpec((1, tm, d), lambda bi, i: (bi, i, 0)),
        out_shape=jax.ShapeDtypeStruct((bsz, s, d), F32),
        scratch_shapes=[pltpu.VMEM((SUBLANES, 2 * D_FF), F32), pltpu.VMEM((tm, D_FF), BF16),
                        pltpu.VMEM((tm, d), F32),
                        pltpu.VMEM((4, FFN_CHUNK // LANES, SUBLANES + tm, LANES), F32)],
        compiler_params=_params(("arbitrary", "arbitrary")),
        name="ffn_layer",
    )(x, wup, cw, cb, wdn, g, b)


def kernel(x, mem, a_w_qkv, a_sinks, a_w_o, b_w_in, b_conv_w, b_conv_b, b_w_rgate, b_b_rgate, b_w_igate, b_b_igate, b_lambda, b_w_o, c_w_down, c_q_norm, c_kv_norm, c_w_uq, c_w_ukv, c_w_o, mem_w_kv, x_w_q, x_w_o, f_w_up, f_conv_w, f_conv_b, f_w_down, ln_g, ln_b):
    bsz, s, d = x.shape
    cos, sin = _rope_tables(s)

    mkv = _matmul(mem.reshape(bsz * MEM_LEN, d), mem_w_kv.astype(BF16), 512, BF16)
    mkv = mkv.reshape(bsz, MEM_LEN, 2 * d)

    def row(v):
        return v.reshape(1, -1)

    for i in range(DEPTH):
        kind, j = i % N_MIXERS, i // N_MIXERS
        g0, b0 = row(ln_g[i, 0]), row(ln_b[i, 0])
        mixer = None
        if kind == 0:
            x = _swa_layer(x, a_sinks[j], cos, sin, a_w_qkv[j].astype(BF16), a_w_o[j].astype(BF16), g0, b0)
        elif kind == 1:
            x = _rglru_layer(x, b_w_in[j].astype(BF16), b_conv_w[j], row(b_conv_b[j]),
                             b_w_rgate[j].astype(BF16), row(b_b_rgate[j]),
                             b_w_igate[j].astype(BF16), row(b_b_igate[j]),
                             row(b_lambda[j]), b_w_o[j].astype(BF16), g0, b0)
        else:
            wd = jnp.pad(c_w_down[j], ((0, 0), (0, C_DOWN_PAD - c_w_down.shape[-1]))).astype(BF16)
            wuq = c_w_uq[j].reshape(C_Q_RANK, C_HEADS, C_NOPE + C_ROPE)
            wuq = jnp.concatenate([wuq[:, :, :C_NOPE].reshape(C_Q_RANK, -1),
                                   wuq[:, :, C_NOPE:].reshape(C_Q_RANK, -1)], axis=-1).astype(BF16)
            wukv = c_w_ukv[j].reshape(C_KV_RANK, C_HEADS, C_NOPE + C_V)
            wukv = jnp.concatenate([wukv[:, :, :C_NOPE].reshape(C_KV_RANK, -1),
                                    wukv[:, :, C_NOPE:].reshape(C_KV_RANK, -1)], axis=-1).astype(BF16)
            q, k, v = _mla_proj(x, cos, sin, wd, row(c_q_norm[j]), row(c_kv_norm[j]), wuq, wukv)
            mixer = (_mla_attn(q, k, v), c_w_o[j].astype(BF16), g0, b0)
        x = _xattn_layer(x, x_w_q[i].astype(BF16), mkv, x_w_o[i].astype(BF16),
                         row(ln_g[i, 1]), row(ln_b[i, 1]), mixer=mixer)
        x = _ffn_layer(x, f_w_up[i].astype(BF16), f_conv_w[i], row(f_conv_b[i]), f_w_down[i].astype(BF16),
                       row(ln_g[i, 2]), row(ln_b[i, 2]))
    return x
```

```python
import functools
import math

import jax
import jax.numpy as jnp
import numpy as np
from jax import lax
from jax.experimental import pallas as pl
from jax.experimental.pallas import tpu as pltpu

D_MODEL = 1024
DEPTH = 4
N_MIXERS = 3
MEM_LEN = 256
BLOCK = 128
ROPE_THETA = 10000.0
NEG = -1e30
LN_EPS = 1e-5
RMS_EPS = 1e-6

A_HEADS = 16
A_KV_HEADS = 4
A_HEAD_DIM = 64

LRU_WIDTH = D_MODEL
LRU_BLOCKS = 4
LRU_BLOCK_W = LRU_WIDTH // LRU_BLOCKS
LRU_CONV = 4
LRU_C = 8.0

C_HEADS = 8
C_NOPE = 128
C_ROPE = 64
C_V = 128
C_Q_RANK = 384
C_KV_RANK = 256
C_QK_PAD = 256

X_HEADS = 4
X_HEAD_DIM = D_MODEL // X_HEADS

D_FF = 2816
FFN_CONV = 3
FFN_CHUNK = 256
FFN_NCHUNK = D_FF // FFN_CHUNK

ALPHA = (2.0 * DEPTH) ** 0.25

LANES = 128
SUBLANES = 8
VMEM_LIMIT = 56 * 1024 * 1024

BF16 = jnp.bfloat16
F32 = jnp.float32

NT_DIMS = (((1,), (1,)), ((), ()))
LOG2E = math.log2(math.e)


def _dot(a, b):
    return jnp.dot(a, b, preferred_element_type=F32)


def _dot_nt(a, b):
    return lax.dot_general(a, b, NT_DIMS, preferred_element_type=F32)


def _layer_norm(z, g, b):
    mu = jnp.mean(z, axis=-1, keepdims=True)
    zc = z - mu
    var = jnp.mean(zc * zc, axis=-1, keepdims=True)
    return zc * lax.rsqrt(var + LN_EPS) * g + b


def _sigmoid(z):
    return 0.5 * jnp.tanh(0.5 * z) + 0.5


def _resident(shape):
    nd = len(shape)
    return pl.BlockSpec(shape, lambda *_: (0,) * nd, pipeline_mode=pl.Buffered(1))


def _params(sem):
    return pltpu.CompilerParams(dimension_semantics=sem, vmem_limit_bytes=VMEM_LIMIT)


def _matmul_kernel(a_ref, w_ref, o_ref):
    o_ref[...] = _dot(a_ref[...].astype(BF16), w_ref[...]).astype(o_ref.dtype)


def _matmul(a, w, tm, out_dtype):
    t, k = a.shape
    n = w.shape[1]
    return pl.pallas_call(
        _matmul_kernel,
        grid=(t // tm,),
        in_specs=[pl.BlockSpec((tm, k), lambda i: (i, 0)), _resident((k, n))],
        out_specs=pl.BlockSpec((tm, n), lambda i: (i, 0)),
        out_shape=jax.ShapeDtypeStruct((t, n), out_dtype),
        compiler_params=_params(("parallel",)),
        name="matmul",
    )(a, w)


def _rope_slab(slab, cos, sin_signed, first_half):
    rot = jnp.where(first_half, pltpu.roll(slab, LANES - 32, 1), pltpu.roll(slab, 32, 1))
    return slab * cos + rot * sin_signed


def _rope_tables(seq):
    inv = 1.0 / (ROPE_THETA ** (jnp.arange(0, A_HEAD_DIM, 2, dtype=F32) / A_HEAD_DIM))
    ang = jnp.arange(seq, dtype=F32)[:, None] * inv[None, :]
    cos, sin = jnp.cos(ang), jnp.sin(ang)
    return (jnp.concatenate([cos, cos, cos, cos], axis=-1),
            jnp.concatenate([-sin, sin, -sin, sin], axis=-1))


SWA_TM = 512
SWA_NBLK = SWA_TM // BLOCK


def _swa_kernel(sinks_ref, x_ref, cos_ref, sin_ref, wqkv_ref, wo_ref, g_ref, b_ref, o_ref,
                kext_scr, vext_scr, q_scr, o_scr):
    i = pl.program_id(1)
    tm = SWA_TM

    @pl.when(i == 0)
    def _():
        kext_scr[:, 0:BLOCK, :] = jnp.zeros((2 * A_KV_HEADS, BLOCK, LANES), BF16)
        vext_scr[:, 0:BLOCK, :] = jnp.zeros((2 * A_KV_HEADS, BLOCK, LANES), BF16)

    @pl.when(i > 0)
    def _():
        kext_scr[:, 0:BLOCK, :] = kext_scr[:, tm:tm + BLOCK, :]
        vext_scr[:, 0:BLOCK, :] = vext_scr[:, tm:tm + BLOCK, :]

    x = x_ref[0]
    qkv = _dot(x.astype(BF16), wqkv_ref[...])
    cos = cos_ref[...]
    sin = sin_ref[...]
    lane = lax.broadcasted_iota(jnp.int32, (tm, LANES), 1)
    first_half = (lane & 32) == 0
    low = lane < 64

    nq = A_HEADS * A_HEAD_DIM
    nkv = A_KV_HEADS * A_HEAD_DIM
    for j in range(nq // LANES):
        slab = _rope_slab(qkv[:, j * LANES:(j + 1) * LANES], cos, sin, first_half)
        q_scr[:, j * LANES:(j + 1) * LANES] = (slab * (A_HEAD_DIM ** -0.5 * LOG2E)).astype(BF16)

    for m in range(nkv // LANES):
        kslab = _rope_slab(qkv[:, nq + m * LANES:nq + (m + 1) * LANES], cos, sin, first_half)
        vslab = qkv[:, nq + nkv + m * LANES:nq + nkv + (m + 1) * LANES]
        for slab, ext in ((kslab, kext_scr), (vslab, vext_scr)):
            swapped = pltpu.roll(slab, 64, 1)
            h0, h1 = 2 * m, 2 * m + 1
            ext[2 * h0, BLOCK:BLOCK + tm, :] = jnp.where(low, slab, 0.0).astype(BF16)
            ext[2 * h0 + 1, BLOCK:BLOCK + tm, :] = jnp.where(low, 0.0, swapped).astype(BF16)
            ext[2 * h1, BLOCK:BLOCK + tm, :] = jnp.where(low, swapped, 0.0).astype(BF16)
            ext[2 * h1 + 1, BLOCK:BLOCK + tm, :] = jnp.where(low, 0.0, slab).astype(BF16)

    qi = lax.broadcasted_iota(jnp.int32, (2 * BLOCK, 4 * BLOCK), 0) & (BLOCK - 1)
    kj = lax.broadcasted_iota(jnp.int32, (2 * BLOCK, 4 * BLOCK), 1) & (2 * BLOCK - 1)
    dist = qi + BLOCK - kj
    band = (dist >= 0) & (dist < BLOCK)
    band_first = band & ((kj >= BLOCK) | (i > 0))
    top = lax.broadcasted_iota(jnp.int32, (2 * BLOCK, 1), 0) < BLOCK
    low2 = lax.broadcasted_iota(jnp.int32, (2 * BLOCK, LANES), 1) < 64

    def window(ext, h, r0):
        return jnp.concatenate([ext[2 * h, r0:r0 + 2 * BLOCK, :], ext[2 * h + 1, r0:r0 + 2 * BLOCK, :]],
                               axis=0)

    for n in range(SWA_NBLK):
        mask = band_first if n == 0 else band
        r0 = n * BLOCK
        scores = []
        for h in range(A_KV_HEADS):
            qs = jnp.concatenate([q_scr[r0:r0 + BLOCK, (2 * h) * LANES:(2 * h + 1) * LANES],
                                  q_scr[r0:r0 + BLOCK, (2 * h + 1) * LANES:(2 * h + 2) * LANES]], axis=0)
            scores.append(_dot_nt(qs, window(kext_scr, h, r0)))
        probs, inv_denoms = [], []
        for h in range(A_KV_HEADS):
            s = jnp.where(mask, scores[h], NEG)
            ps, rs = [], []
            for half in range(2):
                sh = s[:, half * 2 * BLOCK:(half + 1) * 2 * BLOCK]
                sink = jnp.where(top, sinks_ref[4 * h + half] * LOG2E, sinks_ref[4 * h + 2 + half] * LOG2E)
                mx = jnp.maximum(jnp.max(sh, axis=-1, keepdims=True), sink)
                p = jnp.exp2(sh - mx)
                rs.append(1.0 / (jnp.sum(p, axis=-1, keepdims=True) + jnp.exp2(sink - mx)))
                ps.append(p.astype(BF16))
            probs.append(jnp.concatenate(ps, axis=1))
            inv_denoms.append(jnp.where(low2, rs[0], rs[1]))
        for h in range(A_KV_HEADS):
            o = _dot(probs[h], window(vext_scr, h, r0)) * inv_denoms[h]
            o_scr[r0:r0 + BLOCK, (2 * h) * LANES:(2 * h + 1) * LANES] = o[:BLOCK].astype(BF16)
            o_scr[r0:r0 + BLOCK, (2 * h + 1) * LANES:(2 * h + 2) * LANES] = o[BLOCK:].astype(BF16)

    for r0 in range(0, tm, tm // 2):
        rows = slice(r0, r0 + tm // 2)
        y = _dot(o_scr[rows, :], wo_ref[...])
        o_ref[0, rows, :] = _layer_norm(ALPHA * x[rows] + y, g_ref[...], b_ref[...])


def _swa_layer(x, sinks, cos, sin, wqkv, wo, g, b):
    bsz, s, d = x.shape
    tm = SWA_TM
    nqkv = wqkv.shape[1]
    return pl.pallas_call(
        _swa_kernel,
        grid=(bsz, s // tm),
        in_specs=[pl.BlockSpec(memory_space=pltpu.SMEM),
                  pl.BlockSpec((1, tm, d), lambda bi, i: (bi, i, 0)),
                  pl.BlockSpec((tm, LANES), lambda bi, i: (i, 0)),
                  pl.BlockSpec((tm, LANES), lambda bi, i: (i, 0)),
                  _resident((d, nqkv)), _resident((d, d)), _resident((1, d)), _resident((1, d))],
        out_specs=pl.BlockSpec((1, tm, d), lambda bi, i: (bi, i, 0)),
        out_shape=jax.ShapeDtypeStruct((bsz, s, d), F32),
        scratch_shapes=[pltpu.VMEM((2 * A_KV_HEADS, tm + BLOCK, LANES), BF16),
                        pltpu.VMEM((2 * A_KV_HEADS, tm + BLOCK, LANES), BF16),
                        pltpu.VMEM((tm, d), BF16),
                        pltpu.VMEM((tm, d), BF16)],
        compiler_params=_params(("arbitrary", "arbitrary")),
        name="swa_layer",
    )(sinks, x, cos, sin, wqkv, wo, g, b)


LRU_SUB = 256
LRU_NSUB = 2
LRU_NLT = LRU_WIDTH // LANES
LRU_PAD = 2 * SUBLANES


def _rglru_kernel(x_ref, xn_ref, win_ref, cw_ref, cb_ref, wr_ref, br_ref, wi_ref, bi_ref, lam_ref, wo_ref,
                  g_ref, b_ref, o_ref, ustage_scr, a_scr, b_scr, hcar_scr, gu_scr):
    i = pl.program_id(1)
    ts = LRU_SUB
    w = LRU_WIDTH
    pad = LRU_PAD
    sl = SUBLANES

    @pl.when(i == 0)
    def _():
        ustage_scr[:, :, 0:sl, :] = jnp.zeros((LRU_NSUB, LRU_NLT, sl, LANES), F32)
        a_scr[:, :, :, 0:pad, :] = jnp.ones((LRU_NSUB, 3, LRU_NLT, pad, LANES), F32)
        b_scr[:, :, :, 0:pad, :] = jnp.zeros((LRU_NSUB, 3, LRU_NLT, pad, LANES), F32)
        hcar_scr[...] = jnp.zeros_like(hcar_scr)

    @pl.when(i > 0)
    def _():
        ustage_scr[:, :, 0:sl, :] = ustage_scr[:, :, ts:ts + sl, :]
        a_scr[:, 0, :, sl:pad, :] = a_scr[:, 0, :, ts + sl:ts + pad, :]
        b_scr[:, 0, :, sl:pad, :] = b_scr[:, 0, :, ts + sl:ts + pad, :]

    cw = cw_ref[...]
    cb = cb_ref[...]
    z = -lam_ref[...]
    softplus = jnp.maximum(z, 0.0) + jnp.log1p(jnp.exp(-jnp.abs(z)))
    c0 = math.sqrt(2.0 / math.pi)

    @pl.when(i == 0)
    def _():
        for sub in range(LRU_NSUB):
            gu_scr[0, sub] = _dot(x_ref[sub].astype(BF16), win_ref[...])

    def project_next(slot, sub):
        gu_scr[1 - slot, sub] = _dot(xn_ref[sub].astype(BF16), win_ref[...])

    def conv_gates(slot, sub):
        u = gu_scr[slot, sub, :, w:2 * w]
        ucs = []
        for lt in range(LRU_NLT):
            ls = slice(lt * LANES, (lt + 1) * LANES)
            us = ustage_scr.at[sub, lt]
            us[sl:sl + ts, :] = u[:, ls]
            uc = us[sl - 3:sl - 3 + ts, :] * cw[0:1, ls]
            uc = uc + us[sl - 2:sl - 2 + ts, :] * cw[1:2, ls]
            uc = uc + us[sl - 1:sl - 1 + ts, :] * cw[2:3, ls]
            uc = uc + u[:, ls] * cw[3:4, ls]
            ucs.append(uc + cb[:, ls])
        uc = jnp.concatenate(ucs, axis=1)

        ucb = uc.astype(BF16)
        rs, igs = [], []
        for h in range(LRU_BLOCKS):
            blk = ucb[:, h * LRU_BLOCK_W:(h + 1) * LRU_BLOCK_W]
            rs.append(_dot(blk, wr_ref[h]))
            igs.append(_dot(blk, wi_ref[h]))
        return uc, jnp.concatenate(rs, axis=-1), jnp.concatenate(igs, axis=-1)

    def recurrence(slot, sub, uc, r_pre, ig_pre):
        hblk = hcar_scr[sub]
        gate = gu_scr[slot, sub, :, 0:w]
        r = _sigmoid(r_pre + br_ref[...])
        ig = _sigmoid(ig_pre + bi_ref[...])
        log_a = (-LRU_C * r) * softplus
        a0 = jnp.exp(log_a)
        th = jnp.tanh(log_a)
        b0 = (jnp.sqrt(-2.0 * th) * lax.rsqrt(1.0 - th)) * (ig * uc)

        a8, b8 = [], []
        for lt in range(LRU_NLT):
            ls = slice(lt * LANES, (lt + 1) * LANES)
            a_scr[sub, 0, lt, pad:pad + ts, :] = a0[:, ls]
            b_scr[sub, 0, lt, pad:pad + ts, :] = b0[:, ls]
            for k, d in enumerate((1, 2)):
                a_cur = a_scr[sub, k, lt, sl:pad + ts, :]
                b_cur = b_scr[sub, k, lt, sl:pad + ts, :]
                a_scr[sub, k + 1, lt, sl:pad + ts, :] = a_cur * a_scr[sub, k, lt, sl - d:pad + ts - d, :]
                b_scr[sub, k + 1, lt, sl:pad + ts, :] = (a_cur * b_scr[sub, k, lt, sl - d:pad + ts - d, :]
                                                        + b_cur)
            a_cur = a_scr[sub, 2, lt, pad:pad + ts, :]
            a8.append(a_cur * a_scr[sub, 2, lt, pad - 4:pad - 4 + ts, :])
            b8.append(a_cur * b_scr[sub, 2, lt, pad - 4:pad - 4 + ts, :] + b_scr[sub, 2, lt, pad:pad + ts, :])
        a8 = jnp.concatenate(a8, axis=1)
        b8 = jnp.concatenate(b8, axis=1)

        hs = []
        for j in range(ts // sl):
            hblk = a8[j * sl:(j + 1) * sl] * hblk + b8[j * sl:(j + 1) * sl]
            hs.append(hblk)
        h = jnp.concatenate(hs, axis=0)

        gelu = gate * (0.5 * (1.0 + jnp.tanh(c0 * (gate + 0.044715 * (gate * gate * gate)))))
        hcar_scr[sub] = hblk
        return (h * gelu).astype(BF16)

    def out_proj(sub, yb):
        return ALPHA * x_ref[sub] + _dot(yb, wo_ref[...])

    def step(slot):
        zs = []
        for sub in range(LRU_NSUB):
            uc, r_pre, ig_pre = conv_gates(slot, sub)
            project_next(slot, sub)
            zs.append(out_proj(sub, recurrence(slot, sub, uc, r_pre, ig_pre)))
            if sub > 0:
                o_ref[sub - 1] = _layer_norm(zs[sub - 1], g_ref[...], b_ref[...])
        o_ref[LRU_NSUB - 1] = _layer_norm(zs[-1], g_ref[...], b_ref[...])

    for parity in range(2):
        pl.when(lax.rem(i, 2) == parity)(functools.partial(step, parity))


def _rglru_layer(x, win, cw, cb, wr, br, wi, bi, lam, wo, g, b):
    bsz, s, d = x.shape
    tm = LRU_SUB
    w = LRU_WIDTH
    last = s // tm - 1
    return pl.pallas_call(
        _rglru_kernel,
        grid=(bsz // LRU_NSUB, s // tm),
        in_specs=[pl.BlockSpec((LRU_NSUB, tm, d), lambda bi_, i: (bi_, i, 0)),
                  pl.BlockSpec((LRU_NSUB, tm, d), lambda bi_, i: (bi_, jnp.minimum(i + 1, last), 0)),
                  _resident((d, 2 * w)), _resident((LRU_CONV, w)), _resident((1, w)),
                  _resident((LRU_BLOCKS, LRU_BLOCK_W, LRU_BLOCK_W)), _resident((1, w)),
                  _resident((LRU_BLOCKS, LRU_BLOCK_W, LRU_BLOCK_W)), _resident((1, w)),
                  _resident((1, w)), _resident((w, d)), _resident((1, d)), _resident((1, d))],
        out_specs=pl.BlockSpec((LRU_NSUB, tm, d), lambda bi_, i: (bi_, i, 0)),
        out_shape=jax.ShapeDtypeStruct((bsz, s, d), F32),
        scratch_shapes=[pltpu.VMEM((LRU_NSUB, LRU_NLT, SUBLANES + LRU_SUB, LANES), F32),
                        pltpu.VMEM((LRU_NSUB, 3, LRU_NLT, LRU_PAD + LRU_SUB, LANES), F32),
                        pltpu.VMEM((LRU_NSUB, 3, LRU_NLT, LRU_PAD + LRU_SUB, LANES), F32),
                        pltpu.VMEM((LRU_NSUB, SUBLANES, w), F32),
                        pltpu.VMEM((2, LRU_NSUB, tm, 2 * w), F32)],
        compiler_params=_params(("arbitrary", "arbitrary")),
        name="rglru_layer",
    )(x, x, win, cw, cb, wr, br, wi, bi, lam, wo, g, b)


MLA_TM = 512
MLA_TQ = 1024
MLA_HALF = MLA_TQ // 2
MLA_TK = 512
C_DOWN_PAD = 768
MLA_Q_SCALE = (C_NOPE + C_ROPE) ** -0.5 * math.log2(math.e)


def _mla_proj_kernel(x_ref, cos_ref, sin_ref, wd_ref, qn_ref, kvn_ref, wuq_ref, wukv_ref,
                     q_ref, k_ref, v_ref):
    tm = MLA_TM
    xb = x_ref[0].astype(BF16)
    c = _dot(xb, wd_ref[...])
    cq = c[:, :C_Q_RANK]
    ckv = c[:, C_Q_RANK:C_Q_RANK + C_KV_RANK]
    kr_slab = c[:, C_Q_RANK + C_KV_RANK:]
    cq = cq * lax.rsqrt(jnp.mean(cq * cq, axis=-1, keepdims=True) + RMS_EPS) * qn_ref[...]
    ckv = ckv * lax.rsqrt(jnp.mean(ckv * ckv, axis=-1, keepdims=True) + RMS_EPS) * kvn_ref[...]
    q = _dot(cq.astype(BF16), wuq_ref[...]) * MLA_Q_SCALE
    kv = _dot(ckv.astype(BF16), wukv_ref[...])

    cos = cos_ref[...]
    sin = sin_ref[...]
    lane = lax.broadcasted_iota(jnp.int32, (tm, LANES), 1)
    first_half = (lane & 32) == 0
    low = lane < 64
    nn = C_HEADS * C_NOPE

    kr = _rope_slab(kr_slab, cos, sin, first_half)
    kr_pair = (kr.astype(BF16), pltpu.roll(kr, 64, 1).astype(BF16))
    for pair in range(C_HEADS // 2):
        qr = _rope_slab(q[:, nn + pair * LANES:nn + (pair + 1) * LANES], cos, sin, first_half)
        for half in range(2):
            h = 2 * pair + half
            qr_h = jnp.where(low, qr, 0.0) if half == 0 else jnp.where(low, 0.0, qr)
            q_ref[0, h, :, 0:C_NOPE] = q[:, h * C_NOPE:(h + 1) * C_NOPE].astype(BF16)
            q_ref[0, h, :, C_NOPE:C_QK_PAD] = qr_h.astype(BF16)
            k_ref[0, h, :, 0:C_NOPE] = kv[:, h * C_NOPE:(h + 1) * C_NOPE].astype(BF16)
            k_ref[0, h, :, C_NOPE:C_QK_PAD] = kr_pair[half]
            v_ref[0, h] = kv[:, nn + h * C_V:nn + (h + 1) * C_V].astype(BF16)


def _mla_proj(x, cos, sin, wd, qn, kvn, wuq, wukv):
    bsz, s, d = x.shape
    tm = MLA_TM
    qk_shape = jax.ShapeDtypeStruct((bsz, C_HEADS, s, C_QK_PAD), BF16)
    v_shape = jax.ShapeDtypeStruct((bsz, C_HEADS, s, C_V), BF16)
    return pl.pallas_call(
        _mla_proj_kernel,
        grid=(bsz, s // tm),
        in_specs=[pl.BlockSpec((1, tm, d), lambda bi, i: (bi, i, 0)),
                  pl.BlockSpec((tm, LANES), lambda bi, i: (i, 0)),
                  pl.BlockSpec((tm, LANES), lambda bi, i: (i, 0)),
                  _resident(wd.shape), _resident(qn.shape), _resident(kvn.shape),
                  _resident(wuq.shape), _resident(wukv.shape)],
        out_specs=[pl.BlockSpec((1, C_HEADS, tm, C_QK_PAD), lambda bi, i: (bi, 0, i, 0)),
                   pl.BlockSpec((1, C_HEADS, tm, C_QK_PAD), lambda bi, i: (bi, 0, i, 0)),
                   pl.BlockSpec((1, C_HEADS, tm, C_V), lambda bi, i: (bi, 0, i, 0))],
        out_shape=[qk_shape, qk_shape, v_shape],
        compiler_params=_params(("parallel", "parallel")),
        name="mla_proj",
    )(x, cos, sin, wd, qn, kvn, wuq, wukv)


def _mla_attn_kernel(q_ref, k_ref, v_ref, o_ref, m_scr, acc_scr, s_scr):
    i = pl.program_id(2)
    tk, hq = MLA_TK, MLA_HALF

    m_scr[...] = jnp.full_like(m_scr, -jnp.inf)
    acc_scr[...] = jnp.zeros_like(acc_scr)
    ones = jnp.ones((tk, LANES), BF16)

    def scores(slot, hf, j):
        k0 = pl.multiple_of(j * tk, tk)
        s_scr[slot, hf] = _dot_nt(q_ref[0, 0, hf * hq:(hf + 1) * hq, :], k_ref[0, 0, pl.ds(k0, tk), :])

    def consume(slot, hf, j, diagonal):
        k0 = pl.multiple_of(j * tk, tk)
        vv = jnp.concatenate([v_ref[0, 0, pl.ds(k0, tk), :], ones], axis=1)
        s = s_scr[slot, hf]
        if diagonal:
            qpos = lax.broadcasted_iota(jnp.int32, (hq, tk), 0)
            kpos = lax.broadcasted_iota(jnp.int32, (hq, tk), 1)
            s = jnp.where(kpos <= qpos, s, NEG)
        m_prev = m_scr[hf]
        m_next = jnp.maximum(m_prev, jnp.max(s, axis=-1, keepdims=True))
        p = jnp.exp2(s - jnp.tile(m_next, (1, tk // LANES)))
        corr = jnp.exp2(m_prev - m_next)
        acc_scr[hf] = jnp.tile(corr, (1, 2)) * acc_scr[hf] + _dot(p.astype(BF16), vv)
        m_scr[hf] = m_next

    scores(0, 0, 0)
    scores(0, 1, 0)

    def chunk_pair(j):
        scores(1, 0, j + 1)
        scores(1, 1, j + 1)
        consume(0, 0, j, False)
        consume(0, 1, j, False)
        scores(0, 0, j + 2)
        scores(0, 1, j + 2)
        consume(1, 0, j + 1, False)
        consume(1, 1, j + 1, False)

    def body(t, carry):
        chunk_pair(4 * t)
        chunk_pair(4 * t + 2)
        return carry

    lax.fori_loop(0, i // 2, body, 0)

    @pl.when(i % 2 == 1)
    def _():
        chunk_pair(2 * i - 2)

    scores(1, 1, 2 * i + 1)
    consume(0, 0, 2 * i, True)
    consume(0, 1, 2 * i, False)
    consume(1, 1, 2 * i + 1, True)
    for hf in range(2):
        acc = acc_scr[hf]
        o_ref[0, hf * hq:(hf + 1) * hq, :] = (acc[:, :C_V] / acc[:, C_V:]).astype(BF16)


def _mla_attn(q, k, v):
    bsz, nh, s, _ = q.shape
    tq = MLA_TQ
    return pl.pallas_call(
        _mla_attn_kernel,
        grid=(bsz, nh, s // tq),
        in_specs=[pl.BlockSpec((1, 1, tq, C_QK_PAD), lambda b, h, i: (b, h, i, 0)),
                  pl.BlockSpec((1, 1, s, C_QK_PAD), lambda b, h, i: (b, h, 0, 0)),
                  pl.BlockSpec((1, 1, s, C_V), lambda b, h, i: (b, h, 0, 0))],
        out_specs=pl.BlockSpec((1, tq, C_V), lambda b, h, i: (b, i, h)),
        out_shape=jax.ShapeDtypeStruct((bsz, s, nh * C_V), BF16),
        scratch_shapes=[pltpu.VMEM((2, MLA_HALF, LANES), F32),
                        pltpu.VMEM((2, MLA_HALF, 2 * C_V), F32),
                        pltpu.VMEM((2, 2, MLA_HALF, MLA_TK), F32)],
        compiler_params=_params(("parallel", "parallel", "arbitrary")),
        name="mla_attn",
    )(q, k, v)


X_TM = 1024
X_LN_ROWS = 256


def _xattn_kernel(*refs, mixer_prologue):
    tm = X_TM
    if mixer_prologue:
        a_ref, wa_ref, ga_ref, ba_ref, x_ref, wq_ref, mkv_ref, wo_ref, g_ref, b_ref, o_ref = refs
        x = _layer_norm(ALPHA * x_ref[0] + _dot(a_ref[0], wa_ref[...]), ga_ref[...], ba_ref[...])
    else:
        x_ref, wq_ref, mkv_ref, wo_ref, g_ref, b_ref, o_ref = refs
        x = x_ref[0]
    q = (_dot(x.astype(BF16), wq_ref[...]) * (X_HEAD_DIM ** -0.5 * LOG2E)).astype(BF16)
    heads = [(h * X_HEAD_DIM, (h + 1) * X_HEAD_DIM) for h in range(X_HEADS)]
    scores = [_dot_nt(q[:, lo:hi], mkv_ref[0, :, lo:hi]) for lo, hi in heads]
    probs, inv_denoms = [], []
    for s in scores:
        p = jnp.exp2(s - jnp.max(s, axis=-1, keepdims=True))
        inv_denoms.append(1.0 / jnp.sum(p, axis=-1, keepdims=True))
        probs.append(p.astype(BF16))
    outs = [(_dot(probs[h], mkv_ref[0, :, D_MODEL + lo:D_MODEL + hi]) * inv_denoms[h]).astype(BF16)
            for h, (lo, hi) in enumerate(heads)]
    o = jnp.concatenate(outs, axis=-1)
    for r0 in range(0, tm, X_LN_ROWS):
        rows = slice(r0, r0 + X_LN_ROWS)
        y = _dot(o[rows], wo_ref[...])
        o_ref[0, rows, :] = _layer_norm(ALPHA * x[rows] + y, g_ref[...], b_ref[...])


def _xattn_layer(x, wq, mkv, wo, g, b, mixer=None):
    bsz, s, d = x.shape
    tm = X_TM
    tile = pl.BlockSpec((1, tm, d), lambda bi, i: (bi, i, 0))
    in_specs = [tile, _resident((d, d)),
                pl.BlockSpec((1, MEM_LEN, 2 * d), lambda bi, i: (bi, 0, 0)),
                _resident((d, d)), _resident((1, d)), _resident((1, d))]
    args = (x, wq, mkv, wo, g, b)
    if mixer is not None:
        in_specs = [tile, _resident((d, d)), _resident((1, d)), _resident((1, d))] + in_specs
        args = tuple(mixer) + args
    return pl.pallas_call(
        functools.partial(_xattn_kernel, mixer_prologue=mixer is not None),
        grid=(bsz, s // tm),
        in_specs=in_specs,
        out_specs=tile,
        out_shape=jax.ShapeDtypeStruct((bsz, s, d), F32),
        compiler_params=_params(("parallel", "parallel")),
        name="xattn_layer",
    )(*args)


FFN_TM = 512
FFN_LN_ROWS = 256
FFN_DOWN_GROUPS = (4, 8, FFN_NCHUNK)


def _ffn_kernel(x_ref, wup_ref, cw_ref, cb_ref, wdn_ref, g_ref, b_ref, o_ref, hprev_scr, act_scr,
                acc_scr, hstage_scr):
    i = pl.program_id(1)
    tm = FFN_TM

    @pl.when(i == 0)
    def _():
        hprev_scr[...] = jnp.zeros_like(hprev_scr)

    x = x_ref[0]
    xb = x.astype(BF16)

    def up_conv(col0, slot):
        cols = slice(col0, col0 + FFN_CHUNK)
        h = _dot(xb, wup_ref[:, cols])
        prev = hprev_scr[:, cols]
        hprev_scr[:, cols] = h[tm - SUBLANES:, :]
        cw = cw_ref[:, cols]
        cb = cb_ref[:, cols]
        outs = []
        for lt in range(FFN_CHUNK // LANES):
            ls = slice(lt * LANES, (lt + 1) * LANES)
            hs = hstage_scr.at[slot, lt]
            hs[0:SUBLANES, :] = prev[:, ls]
            hs[SUBLANES:SUBLANES + tm, :] = h[:, ls]
            y = hs[SUBLANES - 2:SUBLANES - 2 + tm, :] * cw[0:1, ls]
            y = y + hs[SUBLANES - 1:SUBLANES - 1 + tm, :] * cw[1:2, ls]
            y = y + h[:, ls] * cw[2:3, ls]
            outs.append(y + cb[:, ls])
        return jnp.concatenate(outs, axis=1)

    def down(rows, c0, c1):
        ks = slice(c0 * FFN_CHUNK, c1 * FFN_CHUNK)
        return _dot(act_scr[rows, ks], wdn_ref[ks, :])

    everything = slice(0, tm)
    for c in range(FFN_NCHUNK):
        gt = up_conv(c * FFN_CHUNK, 2 * (c % 2))
        val = up_conv(D_FF + c * FFN_CHUNK, 2 * (c % 2) + 1)
        act_scr[:, c * FFN_CHUNK:(c + 1) * FFN_CHUNK] = ((gt * _sigmoid(gt)) * val).astype(BF16)
        if c + 1 in FFN_DOWN_GROUPS[:-1]:
            gi = FFN_DOWN_GROUPS.index(c + 1)
            if gi == 0:
                acc_scr[...] = down(everything, 0, c + 1)
            else:
                acc_scr[...] += down(everything, FFN_DOWN_GROUPS[gi - 1], c + 1)
    for r0 in range(0, tm, FFN_LN_ROWS):
        rows = slice(r0, r0 + FFN_LN_ROWS)
        out = acc_scr[rows, :] + down(rows, FFN_DOWN_GROUPS[-2], FFN_DOWN_GROUPS[-1])
        o_ref[0, rows, :] = _layer_norm(ALPHA * x[rows] + out, g_ref[...], b_ref[...])


def _ffn_layer(x, wup, cw, cb, wdn, g, b):
    bsz, s, d = x.shape
    tm = FFN_TM
    return pl.pallas_call(
        _ffn_kernel,
        grid=(bsz, s // tm),
        in_specs=[pl.BlockSpec((1, tm, d), lambda bi, i: (bi, i, 0)),
                  _resident((d, 2 * D_FF)), _resident((FFN_CONV, 2 * D_FF)), _resident((1, 2 * D_FF)),
                  _resident((D_FF, d)), _resident((1, d)), _resident((1, d))],
        out_specs=pl.BlockSpec((1, tm, d), lambda bi, i: (bi, i, 0)),
        out_shape=jax.ShapeDtypeStruct((bsz, s, d), F32),
        scratch_shapes=[pltpu.VMEM((SUBLANES, 2 * D_FF), F32), pltpu.VMEM((tm, D_FF), BF16),
                        pltpu.VMEM((tm, d), F32),
                        pltpu.VMEM((4, FFN_CHUNK // LANES, SUBLANES + tm, LANES), F32)],
        compiler_params=_params(("arbitrary", "arbitrary")),
        name="ffn_layer",
    )(x, wup, cw, cb, wdn, g, b)


def kernel(x, mem, a_w_qkv, a_sinks, a_w_o, b_w_in, b_conv_w, b_conv_b, b_w_rgate, b_b_rgate, b_w_igate, b_b_igate, b_lambda, b_w_o, c_w_down, c_q_norm, c_kv_norm, c_w_uq, c_w_ukv, c_w_o, mem_w_kv, x_w_q, x_w_o, f_w_up, f_conv_w, f_conv_b, f_w_down, ln_g, ln_b):
    bsz, s, d = x.shape
    cos, sin = _rope_tables(s)

    mkv = _matmul(mem.reshape(bsz * MEM_LEN, d), mem_w_kv.astype(BF16), 512, BF16)
    mkv = mkv.reshape(bsz, MEM_LEN, 2 * d)

    def row(v):
        return v.reshape(1, -1)

    for i in range(DEPTH):
        kind, j = i % N_MIXERS, i // N_MIXERS
        g0, b0 = row(ln_g[i, 0]), row(ln_b[i, 0])
        mixer = None
        if kind == 0:
            x = _swa_layer(x, a_sinks[j], cos, sin, a_w_qkv[j].astype(BF16), a_w_o[j].astype(BF16), g0, b0)
        elif kind == 1:
            x = _rglru_layer(x, b_w_in[j].astype(BF16), b_conv_w[j], row(b_conv_b[j]),
                             b_w_rgate[j].astype(BF16), row(b_b_rgate[j]),
                             b_w_igate[j].astype(BF16), row(b_b_igate[j]),
                             row(b_lambda[j]), b_w_o[j].astype(BF16), g0, b0)
        else:
            wd = jnp.pad(c_w_down[j], ((0, 0), (0, C_DOWN_PAD - c_w_down.shape[-1]))).astype(BF16)
            wuq = c_w_uq[j].reshape(C_Q_RANK, C_HEADS, C_NOPE + C_ROPE)
            wuq = jnp.concatenate([wuq[:, :, :C_NOPE].reshape(C_Q_RANK, -1),
                                   wuq[:, :, C_NOPE:].reshape(C_Q_RANK, -1)], axis=-1).astype(BF16)
            wukv = c_w_ukv[j].reshape(C_KV_RANK, C_HEADS, C_NOPE + C_V)
            wukv = jnp.concatenate([wukv[:, :, :C_NOPE].reshape(C_KV_RANK, -1),
                                    wukv[:, :, C_NOPE:].reshape(C_KV_RANK, -1)], axis=-1).astype(BF16)
            q, k, v = _mla_proj(x, cos, sin, wd, row(c_q_norm[j]), row(c_kv_norm[j]), wuq, wukv)
            mixer = (_mla_attn(q, k, v), c_w_o[j].astype(BF16), g0, b0)
        x = _xattn_layer(x, x_w_q[i].astype(BF16), mkv, x_w_o[i].astype(BF16),
                         row(ln_g[i, 1]), row(ln_b[i, 1]), mixer=mixer)
        x = _ffn_layer(x, f_w_up[i].astype(BF16), f_conv_w[i], row(f_conv_b[i]), f_w_down[i].astype(BF16),
                       row(ln_g[i, 2]), row(ln_b[i, 2]))
    return x
```

```python
import functools
import math

import jax
import jax.numpy as jnp
import numpy as np
from jax import lax
from jax.experimental import pallas as pl
from jax.experimental.pallas import tpu as pltpu

D_MODEL = 1024
DEPTH = 4
N_MIXERS = 3
MEM_LEN = 256
BLOCK = 128
ROPE_THETA = 10000.0
NEG = -1e30
LN_EPS = 1e-5
RMS_EPS = 1e-6

A_HEADS = 16
A_KV_HEADS = 4
A_HEAD_DIM = 64

LRU_WIDTH = D_MODEL
LRU_BLOCKS = 4
LRU_BLOCK_W = LRU_WIDTH // LRU_BLOCKS
LRU_CONV = 4
LRU_C = 8.0

C_HEADS = 8
C_NOPE = 128
C_ROPE = 64
C_V = 128
C_Q_RANK = 384
C_KV_RANK = 256
C_QK_PAD = 256

X_HEADS = 4
X_HEAD_DIM = D_MODEL // X_HEADS

D_FF = 2816
FFN_CONV = 3
FFN_CHUNK = 256
FFN_NCHUNK = D_FF // FFN_CHUNK

ALPHA = (2.0 * DEPTH) ** 0.25

LANES = 128
SUBLANES = 8
VMEM_LIMIT = 56 * 1024 * 1024

BF16 = jnp.bfloat16
F32 = jnp.float32

NT_DIMS = (((1,), (1,)), ((), ()))
LOG2E = math.log2(math.e)


def _dot(a, b):
    return jnp.dot(a, b, preferred_element_type=F32)


def _dot_nt(a, b):
    return lax.dot_general(a, b, NT_DIMS, preferred_element_type=F32)


def _layer_norm(z, g, b):
    mu = jnp.mean(z, axis=-1, keepdims=True)
    zc = z - mu
    var = jnp.mean(zc * zc, axis=-1, keepdims=True)
    return zc * lax.rsqrt(var + LN_EPS) * g + b


def _sigmoid(z):
    return 0.5 * jnp.tanh(0.5 * z) + 0.5


def _resident(shape):
    nd = len(shape)
    return pl.BlockSpec(shape, lambda *_: (0,) * nd, pipeline_mode=pl.Buffered(1))


def _params(sem):
    return pltpu.CompilerParams(dimension_semantics=sem, vmem_limit_bytes=VMEM_LIMIT)


def _matmul_kernel(a_ref, w_ref, o_ref):
    o_ref[...] = _dot(a_ref[...].astype(BF16), w_ref[...]).astype(o_ref.dtype)


def _matmul(a, w, tm, out_dtype):
    t, k = a.shape
    n = w.shape[1]
    return pl.pallas_call(
        _matmul_kernel,
        grid=(t // tm,),
        in_specs=[pl.BlockSpec((tm, k), lambda i: (i, 0)), _resident((k, n))],
        out_specs=pl.BlockSpec((tm, n), lambda i: (i, 0)),
        out_shape=jax.ShapeDtypeStruct((t, n), out_dtype),
        compiler_params=_params(("parallel",)),
        name="matmul",
    )(a, w)


def _rope_slab(slab, cos, sin_signed, first_half):
    rot = jnp.where(first_half, pltpu.roll(slab, LANES - 32, 1), pltpu.roll(slab, 32, 1))
    return slab * cos + rot * sin_signed


def _rope_tables(seq):
    inv = 1.0 / (ROPE_THETA ** (jnp.arange(0, A_HEAD_DIM, 2, dtype=F32) / A_HEAD_DIM))
    ang = jnp.arange(seq, dtype=F32)[:, None] * inv[None, :]
    cos, sin = jnp.cos(ang), jnp.sin(ang)
    return (jnp.concatenate([cos, cos, cos, cos], axis=-1),
            jnp.concatenate([-sin, sin, -sin, sin], axis=-1))


SWA_TM = 512
SWA_NBLK = SWA_TM // BLOCK


def _swa_kernel(sinks_ref, x_ref, xn_ref, cos_ref, sin_ref, wqkv_ref, wo_ref, g_ref, b_ref, o_ref,
                kext_scr, vext_scr, q_scr, o_scr, qkv_scr):
    i = pl.program_id(1)
    tm = SWA_TM

    @pl.when(i == 0)
    def _():
        kext_scr[:, 0:BLOCK, :] = jnp.zeros((2 * A_KV_HEADS, BLOCK, LANES), BF16)
        vext_scr[:, 0:BLOCK, :] = jnp.zeros((2 * A_KV_HEADS, BLOCK, LANES), BF16)

    @pl.when(i > 0)
    def _():
        kext_scr[:, 0:BLOCK, :] = kext_scr[:, tm:tm + BLOCK, :]
        vext_scr[:, 0:BLOCK, :] = vext_scr[:, tm:tm + BLOCK, :]

    nq = A_HEADS * A_HEAD_DIM
    nkv = A_KV_HEADS * A_HEAD_DIM
    nqkv = nq + 2 * nkv
    chunk = nqkv // SWA_NBLK

    @pl.when(i == 0)
    def _():
        qkv_scr[0] = _dot(x_ref[0].astype(BF16), wqkv_ref[...])

    for parity in range(2):
        pl.when(lax.rem(i, 2) == parity)(functools.partial(
            _swa_step, parity, i, chunk, sinks_ref, x_ref, xn_ref, cos_ref, sin_ref, wqkv_ref, wo_ref,
            g_ref, b_ref, o_ref, kext_scr, vext_scr, q_scr, o_scr, qkv_scr))


def _swa_step(slot, i, chunk, sinks_ref, x_ref, xn_ref, cos_ref, sin_ref, wqkv_ref, wo_ref, g_ref, b_ref,
              o_ref, kext_scr, vext_scr, q_scr, o_scr, qkv_scr):
    tm = SWA_TM
    nq = A_HEADS * A_HEAD_DIM
    nkv = A_KV_HEADS * A_HEAD_DIM
    x = x_ref[0]
    xnb = xn_ref[0].astype(BF16)
    cos = cos_ref[...]
    sin = sin_ref[...]
    lane = lax.broadcasted_iota(jnp.int32, (tm, LANES), 1)
    first_half = (lane & 32) == 0
    low = lane < 64

    def qkv_cols(c0):
        return qkv_scr[slot, :, c0:c0 + LANES]

    for j in range(nq // LANES):
        slab = _rope_slab(qkv_cols(j * LANES), cos, sin, first_half)
        q_scr[:, j * LANES:(j + 1) * LANES] = (slab * (A_HEAD_DIM ** -0.5 * LOG2E)).astype(BF16)

    for m in range(nkv // LANES):
        kslab = _rope_slab(qkv_cols(nq + m * LANES), cos, sin, first_half)
        vslab = qkv_cols(nq + nkv + m * LANES)
        for slab, ext in ((kslab, kext_scr), (vslab, vext_scr)):
            swapped = pltpu.roll(slab, 64, 1)
            h0, h1 = 2 * m, 2 * m + 1
            ext[2 * h0, BLOCK:BLOCK + tm, :] = jnp.where(low, slab, 0.0).astype(BF16)
            ext[2 * h0 + 1, BLOCK:BLOCK + tm, :] = jnp.where(low, 0.0, swapped).astype(BF16)
            ext[2 * h1, BLOCK:BLOCK + tm, :] = jnp.where(low, swapped, 0.0).astype(BF16)
            ext[2 * h1 + 1, BLOCK:BLOCK + tm, :] = jnp.where(low, 0.0, slab).astype(BF16)

    qi = lax.broadcasted_iota(jnp.int32, (2 * BLOCK, 4 * BLOCK), 0) & (BLOCK - 1)
    kj = lax.broadcasted_iota(jnp.int32, (2 * BLOCK, 4 * BLOCK), 1) & (2 * BLOCK - 1)
    dist = qi + BLOCK - kj
    band = (dist >= 0) & (dist < BLOCK)
    band_first = band & ((kj >= BLOCK) | (i > 0))
    top = lax.broadcasted_iota(jnp.int32, (2 * BLOCK, 1), 0) < BLOCK
    low2 = lax.broadcasted_iota(jnp.int32, (2 * BLOCK, LANES), 1) < 64

    def window(ext, h, r0):
        return jnp.concatenate([ext[2 * h, r0:r0 + 2 * BLOCK, :], ext[2 * h + 1, r0:r0 + 2 * BLOCK, :]],
                               axis=0)

    for n in range(SWA_NBLK):
        mask = band_first if n == 0 else band
        r0 = n * BLOCK
        scores = []
        for h in range(A_KV_HEADS):
            qs = jnp.concatenate([q_scr[r0:r0 + BLOCK, (2 * h) * LANES:(2 * h + 1) * LANES],
                                  q_scr[r0:r0 + BLOCK, (2 * h + 1) * LANES:(2 * h + 2) * LANES]], axis=0)
            scores.append(_dot_nt(qs, window(kext_scr, h, r0)))
        qkv_scr[1 - slot, :, n * chunk:(n + 1) * chunk] = _dot(xnb, wqkv_ref[:, n * chunk:(n + 1) * chunk])
        probs, inv_denoms = [], []
        for h in range(A_KV_HEADS):
            s = jnp.where(mask, scores[h], NEG)
            ps, rs = [], []
            for half in range(2):
                sh = s[:, half * 2 * BLOCK:(half + 1) * 2 * BLOCK]
                sink = jnp.where(top, sinks_ref[4 * h + half] * LOG2E, sinks_ref[4 * h + 2 + half] * LOG2E)
                mx = jnp.maximum(jnp.max(sh, axis=-1, keepdims=True), sink)
                p = jnp.exp2(sh - mx)
                rs.append(1.0 / (jnp.sum(p, axis=-1, keepdims=True) + jnp.exp2(sink - mx)))
                ps.append(p.astype(BF16))
            probs.append(jnp.concatenate(ps, axis=1))
            inv_denoms.append(jnp.where(low2, rs[0], rs[1]))
        for h in range(A_KV_HEADS):
            o = _dot(probs[h], window(vext_scr, h, r0)) * inv_denoms[h]
            o_scr[r0:r0 + BLOCK, (2 * h) * LANES:(2 * h + 1) * LANES] = o[:BLOCK].astype(BF16)
            o_scr[r0:r0 + BLOCK, (2 * h + 1) * LANES:(2 * h + 2) * LANES] = o[BLOCK:].astype(BF16)

    for r0 in range(0, tm, tm // 2):
        rows = slice(r0, r0 + tm // 2)
        y = _dot(o_scr[rows, :], wo_ref[...])
        o_ref[0, rows, :] = _layer_norm(ALPHA * x[rows] + y, g_ref[...], b_ref[...])


def _swa_layer(x, sinks, cos, sin, wqkv, wo, g, b):
    bsz, s, d = x.shape
    tm = SWA_TM
    nqkv = wqkv.shape[1]
    last = s // tm - 1
    return pl.pallas_call(
        _swa_kernel,
        grid=(bsz, s // tm),
        in_specs=[pl.BlockSpec(memory_space=pltpu.SMEM),
                  pl.BlockSpec((1, tm, d), lambda bi, i: (bi, i, 0)),
                  pl.BlockSpec((1, tm, d), lambda bi, i: (bi, jnp.minimum(i + 1, last), 0)),
                  pl.BlockSpec((tm, LANES), lambda bi, i: (i, 0)),
                  pl.BlockSpec((tm, LANES), lambda bi, i: (i, 0)),
                  _resident((d, nqkv)), _resident((d, d)), _resident((1, d)), _resident((1, d))],
        out_specs=pl.BlockSpec((1, tm, d), lambda bi, i: (bi, i, 0)),
        out_shape=jax.ShapeDtypeStruct((bsz, s, d), F32),
        scratch_shapes=[pltpu.VMEM((2 * A_KV_HEADS, tm + BLOCK, LANES), BF16),
                        pltpu.VMEM((2 * A_KV_HEADS, tm + BLOCK, LANES), BF16),
                        pltpu.VMEM((tm, d), BF16),
                        pltpu.VMEM((tm, d), BF16),
                        pltpu.VMEM((2, tm, nqkv), F32)],
        compiler_params=_params(("arbitrary", "arbitrary")),
        name="swa_layer",
    )(sinks, x, x, cos, sin, wqkv, wo, g, b)


LRU_SUB = 256
LRU_NSUB = 2
LRU_NLT = LRU_WIDTH // LANES
LRU_PAD = 2 * SUBLANES


def _rglru_kernel(x_ref, xn_ref, win_ref, cw_ref, cb_ref, wr_ref, br_ref, wi_ref, bi_ref, lam_ref, wo_ref,
                  g_ref, b_ref, o_ref, ustage_scr, a_scr, b_scr, hcar_scr, gu_scr):
    i = pl.program_id(1)
    ts = LRU_SUB
    w = LRU_WIDTH
    pad = LRU_PAD
    sl = SUBLANES

    @pl.when(i == 0)
    def _():
        ustage_scr[:, :, 0:sl, :] = jnp.zeros((LRU_NSUB, LRU_NLT, sl, LANES), F32)
        a_scr[:, :, :, 0:pad, :] = jnp.ones((LRU_NSUB, 3, LRU_NLT, pad, LANES), F32)
        b_scr[:, :, :, 0:pad, :] = jnp.zeros((LRU_NSUB, 3, LRU_NLT, pad, LANES), F32)
        hcar_scr[...] = jnp.zeros_like(hcar_scr)

    @pl.when(i > 0)
    def _():
        ustage_scr[:, :, 0:sl, :] = ustage_scr[:, :, ts:ts + sl, :]
        a_scr[:, 0, :, sl:pad, :] = a_scr[:, 0, :, ts + sl:ts + pad, :]
        b_scr[:, 0, :, sl:pad, :] = b_scr[:, 0, :, ts + sl:ts + pad, :]

    cw = cw_ref[...]
    cb = cb_ref[...]
    z = -lam_ref[...]
    softplus = jnp.maximum(z, 0.0) + jnp.log1p(jnp.exp(-jnp.abs(z)))
    c0 = math.sqrt(2.0 / math.pi)

    @pl.when(i == 0)
    def _():
        for sub in range(LRU_NSUB):
            gu_scr[0, sub] = _dot(x_ref[sub].astype(BF16), win_ref[...])

    def project_next(slot, sub):
        gu_scr[1 - slot, sub] = _dot(xn_ref[sub].astype(BF16), win_ref[...])

    def conv_gates(slot, sub):
        u = gu_scr[slot, sub, :, w:2 * w]
        ucs = []
        for lt in range(LRU_NLT):
            ls = slice(lt * LANES, (lt + 1) * LANES)
            us = ustage_scr.at[sub, lt]
            us[sl:sl + ts, :] = u[:, ls]
            uc = us[sl - 3:sl - 3 + ts, :] * cw[0:1, ls]
            uc = uc + us[sl - 2:sl - 2 + ts, :] * cw[1:2, ls]
            uc = uc + us[sl - 1:sl - 1 + ts, :] * cw[2:3, ls]
            uc = uc + u[:, ls] * cw[3:4, ls]
            ucs.append(uc + cb[:, ls])
        uc = jnp.concatenate(ucs, axis=1)

        ucb = uc.astype(BF16)
        rs, igs = [], []
        for h in range(LRU_BLOCKS):
            blk = ucb[:, h * LRU_BLOCK_W:(h + 1) * LRU_BLOCK_W]
            rs.append(_dot(blk, wr_ref[h]))
            igs.append(_dot(blk, wi_ref[h]))
        return uc, jnp.concatenate(rs, axis=-1), jnp.concatenate(igs, axis=-1)

    def recurrence(slot, sub, uc, r_pre, ig_pre):
        hblk = hcar_scr[sub]
        gate = gu_scr[slot, sub, :, 0:w]
        r = _sigmoid(r_pre + br_ref[...])
        ig = _sigmoid(ig_pre + bi_ref[...])
        log_a = (-LRU_C * r) * softplus
        a0 = jnp.exp(log_a)
        th = jnp.tanh(log_a)
        b0 = (jnp.sqrt(-2.0 * th) * lax.rsqrt(1.0 - th)) * (ig * uc)

        a8, b8 = [], []
        for lt in range(LRU_NLT):
            ls = slice(lt * LANES, (lt + 1) * LANES)
            a_scr[sub, 0, lt, pad:pad + ts, :] = a0[:, ls]
            b_scr[sub, 0, lt, pad:pad + ts, :] = b0[:, ls]
            for k, d in enumerate((1, 2)):
                a_cur = a_scr[sub, k, lt, sl:pad + ts, :]
                b_cur = b_scr[sub, k, lt, sl:pad + ts, :]
                a_scr[sub, k + 1, lt, sl:pad + ts, :] = a_cur * a_scr[sub, k, lt, sl - d:pad + ts - d, :]
                b_scr[sub, k + 1, lt, sl:pad + ts, :] = (a_cur * b_scr[sub, k, lt, sl - d:pad + ts - d, :]
                                                        + b_cur)
            a_cur = a_scr[sub, 2, lt, pad:pad + ts, :]
            a8.append(a_cur * a_scr[sub, 2, lt, pad - 4:pad - 4 + ts, :])
            b8.append(a_cur * b_scr[sub, 2, lt, pad - 4:pad - 4 + ts, :] + b_scr[sub, 2, lt, pad:pad + ts, :])
        a8 = jnp.concatenate(a8, axis=1)
        b8 = jnp.concatenate(b8, axis=1)

        hs = []
        for j in range(ts // sl):
            hblk = a8[j * sl:(j + 1) * sl] * hblk + b8[j * sl:(j + 1) * sl]
            hs.append(hblk)
        h = jnp.concatenate(hs, axis=0)

        gelu = gate * (0.5 * (1.0 + jnp.tanh(c0 * (gate + 0.044715 * (gate * gate * gate)))))
        hcar_scr[sub] = hblk
        return (h * gelu).astype(BF16)

    def out_proj(sub, yb):
        return ALPHA * x_ref[sub] + _dot(yb, wo_ref[...])

    def step(slot):
        zs = []
        for sub in range(LRU_NSUB):
            uc, r_pre, ig_pre = conv_gates(slot, sub)
            project_next(slot, sub)
            zs.append(out_proj(sub, recurrence(slot, sub, uc, r_pre, ig_pre)))
            if sub > 0:
                o_ref[sub - 1] = _layer_norm(zs[sub - 1], g_ref[...], b_ref[...])
        o_ref[LRU_NSUB - 1] = _layer_norm(zs[-1], g_ref[...], b_ref[...])

    for parity in range(2):
        pl.when(lax.rem(i, 2) == parity)(functools.partial(step, parity))


def _rglru_layer(x, win, cw, cb, wr, br, wi, bi, lam, wo, g, b):
    bsz, s, d = x.shape
    tm = LRU_SUB
    w = LRU_WIDTH
    last = s // tm - 1
    return pl.pallas_call(
        _rglru_kernel,
        grid=(bsz // LRU_NSUB, s // tm),
        in_specs=[pl.BlockSpec((LRU_NSUB, tm, d), lambda bi_, i: (bi_, i, 0)),
                  pl.BlockSpec((LRU_NSUB, tm, d), lambda bi_, i: (bi_, jnp.minimum(i + 1, last), 0)),
                  _resident((d, 2 * w)), _resident((LRU_CONV, w)), _resident((1, w)),
                  _resident((LRU_BLOCKS, LRU_BLOCK_W, LRU_BLOCK_W)), _resident((1, w)),
                  _resident((LRU_BLOCKS, LRU_BLOCK_W, LRU_BLOCK_W)), _resident((1, w)),
                  _resident((1, w)), _resident((w, d)), _resident((1, d)), _resident((1, d))],
        out_specs=pl.BlockSpec((LRU_NSUB, tm, d), lambda bi_, i: (bi_, i, 0)),
        out_shape=jax.ShapeDtypeStruct((bsz, s, d), F32),
        scratch_shapes=[pltpu.VMEM((LRU_NSUB, LRU_NLT, SUBLANES + LRU_SUB, LANES), F32),
                        pltpu.VMEM((LRU_NSUB, 3, LRU_NLT, LRU_PAD + LRU_SUB, LANES), F32),
                        pltpu.VMEM((LRU_NSUB, 3, LRU_NLT, LRU_PAD + LRU_SUB, LANES), F32),
                        pltpu.VMEM((LRU_NSUB, SUBLANES, w), F32),
                        pltpu.VMEM((2, LRU_NSUB, tm, 2 * w), F32)],
        compiler_params=_params(("arbitrary", "arbitrary")),
        name="rglru_layer",
    )(x, x, win, cw, cb, wr, br, wi, bi, lam, wo, g, b)


MLA_TM = 512
MLA_TQ = 1024
MLA_HALF = MLA_TQ // 2
MLA_TK = 512
C_DOWN_PAD = 768
MLA_Q_SCALE = (C_NOPE + C_ROPE) ** -0.5 * math.log2(math.e)


def _mla_proj_kernel(x_ref, cos_ref, sin_ref, wd_ref, qn_ref, kvn_ref, wuq_ref, wukv_ref,
                     q_ref, k_ref, v_ref):
    tm = MLA_TM
    xb = x_ref[0].astype(BF16)
    c = _dot(xb, wd_ref[...])
    cq = c[:, :C_Q_RANK]
    ckv = c[:, C_Q_RANK:C_Q_RANK + C_KV_RANK]
    kr_slab = c[:, C_Q_RANK + C_KV_RANK:]
    cq = cq * lax.rsqrt(jnp.mean(cq * cq, axis=-1, keepdims=True) + RMS_EPS) * qn_ref[...]
    ckv = ckv * lax.rsqrt(jnp.mean(ckv * ckv, axis=-1, keepdims=True) + RMS_EPS) * kvn_ref[...]
    q = _dot(cq.astype(BF16), wuq_ref[...]) * MLA_Q_SCALE
    kv = _dot(ckv.astype(BF16), wukv_ref[...])

    cos = cos_ref[...]
    sin = sin_ref[...]
    lane = lax.broadcasted_iota(jnp.int32, (tm, LANES), 1)
    first_half = (lane & 32) == 0
    low = lane < 64
    nn = C_HEADS * C_NOPE

    kr = _rope_slab(kr_slab, cos, sin, first_half)
    kr_pair = (kr.astype(BF16), pltpu.roll(kr, 64, 1).astype(BF16))
    for pair in range(C_HEADS // 2):
        qr = _rope_slab(q[:, nn + pair * LANES:nn + (pair + 1) * LANES], cos, sin, first_half)
        for half in range(2):
            h = 2 * pair + half
            qr_h = jnp.where(low, qr, 0.0) if half == 0 else jnp.where(low, 0.0, qr)
            q_ref[0, h, :, 0:C_NOPE] = q[:, h * C_NOPE:(h + 1) * C_NOPE].astype(BF16)
            q_ref[0, h, :, C_NOPE:C_QK_PAD] = qr_h.astype(BF16)
            k_ref[0, h, :, 0:C_NOPE] = kv[:, h * C_NOPE:(h + 1) * C_NOPE].astype(BF16)
            k_ref[0, h, :, C_NOPE:C_QK_PAD] = kr_pair[half]
            v_ref[0, h] = kv[:, nn + h * C_V:nn + (h + 1) * C_V].astype(BF16)


def _mla_proj(x, cos, sin, wd, qn, kvn, wuq, wukv):
    bsz, s, d = x.shape
    tm = MLA_TM
    qk_shape = jax.ShapeDtypeStruct((bsz, C_HEADS, s, C_QK_PAD), BF16)
    v_shape = jax.ShapeDtypeStruct((bsz, C_HEADS, s, C_V), BF16)
    return pl.pallas_call(
        _mla_proj_kernel,
        grid=(bsz, s // tm),
        in_specs=[pl.BlockSpec((1, tm, d), lambda bi, i: (bi, i, 0)),
                  pl.BlockSpec((tm, LANES), lambda bi, i: (i, 0)),
                  pl.BlockSpec((tm, LANES), lambda bi, i: (i, 0)),
                  _resident(wd.shape), _resident(qn.shape), _resident(kvn.shape),
                  _resident(wuq.shape), _resident(wukv.shape)],
        out_specs=[pl.BlockSpec((1, C_HEADS, tm, C_QK_PAD), lambda bi, i: (bi, 0, i, 0)),
                   pl.BlockSpec((1, C_HEADS, tm, C_QK_PAD), lambda bi, i: (bi, 0, i, 0)),
                   pl.BlockSpec((1, C_HEADS, tm, C_V), lambda bi, i: (bi, 0, i, 0))],
        out_shape=[qk_shape, qk_shape, v_shape],
        compiler_params=_params(("parallel", "parallel")),
        name="mla_proj",
    )(x, cos, sin, wd, qn, kvn, wuq, wukv)


def _mla_attn_kernel(q_ref, k_ref, v_ref, o_ref, m_scr, acc_scr, s_scr):
    i = pl.program_id(2)
    tk, hq = MLA_TK, MLA_HALF

    m_scr[...] = jnp.full_like(m_scr, -jnp.inf)
    acc_scr[...] = jnp.zeros_like(acc_scr)
    ones = jnp.ones((tk, LANES), BF16)

    def scores(slot, hf, j):
        k0 = pl.multiple_of(j * tk, tk)
        s_scr[slot, hf] = _dot_nt(q_ref[0, 0, hf * hq:(hf + 1) * hq, :], k_ref[0, 0, pl.ds(k0, tk), :])

    def consume(slot, hf, j, diagonal):
        k0 = pl.multiple_of(j * tk, tk)
        vv = jnp.concatenate([v_ref[0, 0, pl.ds(k0, tk), :], ones], axis=1)
        s = s_scr[slot, hf]
        if diagonal:
            qpos = lax.broadcasted_iota(jnp.int32, (hq, tk), 0)
            kpos = lax.broadcasted_iota(jnp.int32, (hq, tk), 1)
            s = jnp.where(kpos <= qpos, s, NEG)
        m_prev = m_scr[hf]
        m_next = jnp.maximum(m_prev, jnp.max(s, axis=-1, keepdims=True))
        p = jnp.exp2(s - jnp.tile(m_next, (1, tk // LANES)))
        corr = jnp.exp2(m_prev - m_next)
        acc_scr[hf] = jnp.tile(corr, (1, 2)) * acc_scr[hf] + _dot(p.astype(BF16), vv)
        m_scr[hf] = m_next

    scores(0, 0, 0)
    scores(0, 1, 0)

    def chunk_pair(j):
        scores(1, 0, j + 1)
        scores(1, 1, j + 1)
        consume(0, 0, j, False)
        consume(0, 1, j, False)
        scores(0, 0, j + 2)
        scores(0, 1, j + 2)
        consume(1, 0, j + 1, False)
        consume(1, 1, j + 1, False)

    def body(t, carry):
        chunk_pair(4 * t)
        chunk_pair(4 * t + 2)
        return carry

    lax.fori_loop(0, i // 2, body, 0)

    @pl.when(i % 2 == 1)
    def _():
        chunk_pair(2 * i - 2)

    scores(1, 1, 2 * i + 1)
    consume(0, 0, 2 * i, True)
    consume(0, 1, 2 * i, False)
    consume(1, 1, 2 * i + 1, True)
    for hf in range(2):
        acc = acc_scr[hf]
        o_ref[0, hf * hq:(hf + 1) * hq, :] = (acc[:, :C_V] / acc[:, C_V:]).astype(BF16)


def _mla_attn(q, k, v):
    bsz, nh, s, _ = q.shape
    tq = MLA_TQ
    return pl.pallas_call(
        _mla_attn_kernel,
        grid=(bsz, nh, s // tq),
        in_specs=[pl.BlockSpec((1, 1, tq, C_QK_PAD), lambda b, h, i: (b, h, i, 0)),
                  pl.BlockSpec((1, 1, s, C_QK_PAD), lambda b, h, i: (b, h, 0, 0)),
                  pl.BlockSpec((1, 1, s, C_V), lambda b, h, i: (b, h, 0, 0))],
        out_specs=pl.BlockSpec((1, tq, C_V), lambda b, h, i: (b, i, h)),
        out_shape=jax.ShapeDtypeStruct((bsz, s, nh * C_V), BF16),
        scratch_shapes=[pltpu.VMEM((2, MLA_HALF, LANES), F32),
                        pltpu.VMEM((2, MLA_HALF, 2 * C_V), F32),
                        pltpu.VMEM((2, 2, MLA_HALF, MLA_TK), F32)],
        compiler_params=_params(("parallel", "parallel", "arbitrary")),
        name="mla_attn",
    )(q, k, v)


X_TM = 1024
X_LN_ROWS = 256


def _xattn_kernel(*refs, mixer_prologue):
    tm = X_TM
    if mixer_prologue:
        a_ref, wa_ref, ga_ref, ba_ref, x_ref, wq_ref, mkv_ref, wo_ref, g_ref, b_ref, o_ref = refs
        x = _layer_norm(ALPHA * x_ref[0] + _dot(a_ref[0], wa_ref[...]), ga_ref[...], ba_ref[...])
    else:
        x_ref, wq_ref, mkv_ref, wo_ref, g_ref, b_ref, o_ref = refs
        x = x_ref[0]
    q = (_dot(x.astype(BF16), wq_ref[...]) * (X_HEAD_DIM ** -0.5 * LOG2E)).astype(BF16)
    heads = [(h * X_HEAD_DIM, (h + 1) * X_HEAD_DIM) for h in range(X_HEADS)]
    scores = [_dot_nt(q[:, lo:hi], mkv_ref[0, :, lo:hi]) for lo, hi in heads]
    probs, inv_denoms = [], []
    for s in scores:
        p = jnp.exp2(s - jnp.max(s, axis=-1, keepdims=True))
        inv_denoms.append(1.0 / jnp.sum(p, axis=-1, keepdims=True))
        probs.append(p.astype(BF16))
    outs = [(_dot(probs[h], mkv_ref[0, :, D_MODEL + lo:D_MODEL + hi]) * inv_denoms[h]).astype(BF16)
            for h, (lo, hi) in enumerate(heads)]
    o = jnp.concatenate(outs, axis=-1)
    for r0 in range(0, tm, X_LN_ROWS):
        rows = slice(r0, r0 + X_LN_ROWS)
        y = _dot(o[rows], wo_ref[...])
        o_ref[0, rows, :] = _layer_norm(ALPHA * x[rows] + y, g_ref[...], b_ref[...])


def _xattn_layer(x, wq, mkv, wo, g, b, mixer=None):
    bsz, s, d = x.shape
    tm = X_TM
    tile = pl.BlockSpec((1, tm, d), lambda bi, i: (bi, i, 0))
    in_specs = [tile, _resident((d, d)),
                pl.BlockSpec((1, MEM_LEN, 2 * d), lambda bi, i: (bi, 0, 0)),
                _resident((d, d)), _resident((1, d)), _resident((1, d))]
    args = (x, wq, mkv, wo, g, b)
    if mixer is not None:
        in_specs = [tile, _resident((d, d)), _resident((1, d)), _resident((1, d))] + in_specs
        args = tuple(mixer) + args
    return pl.pallas_call(
        functools.partial(_xattn_kernel, mixer_prologue=mixer is not None),
        grid=(bsz, s // tm),
        in_specs=in_specs,
        out_specs=tile,
        out_shape=jax.ShapeDtypeStruct((bsz, s, d), F32),
        compiler_params=_params(("parallel", "parallel")),
        name="xattn_layer",
    )(*args)


FFN_TM = 512
FFN_LN_ROWS = 256
FFN_DOWN_GROUPS = (4, 8, FFN_NCHUNK)


def _ffn_kernel(x_ref, wup_ref, cw_ref, cb_ref, wdn_ref, g_ref, b_ref, o_ref, hprev_scr, act_scr,
                acc_scr, hstage_scr):
    i = pl.program_id(1)
    tm = FFN_TM

    @pl.when(i == 0)
    def _():
        hprev_scr[...] = jnp.zeros_like(hprev_scr)

    x = x_ref[0]
    xb = x.astype(BF16)

    def up_conv(col0, slot):
        cols = slice(col0, col0 + FFN_CHUNK)
        h = _dot(xb, wup_ref[:, cols])
        prev = hprev_scr[:, cols]
        hprev_scr[:, cols] = h[tm - SUBLANES:, :]
        cw = cw_ref[:, cols]
        cb = cb_ref[:, cols]
        outs = []
        for lt in range(FFN_CHUNK // LANES):
            ls = slice(lt * LANES, (lt + 1) * LANES)
            hs = hstage_scr.at[slot, lt]
            hs[0:SUBLANES, :] = prev[:, ls]
            hs[SUBLANES:SUBLANES + tm, :] = h[:, ls]
            y = hs[SUBLANES - 2:SUBLANES - 2 + tm, :] * cw[0:1, ls]
            y = y + hs[SUBLANES - 1:SUBLANES - 1 + tm, :] * cw[1:2, ls]
            y = y + h[:, ls] * cw[2:3, ls]
            outs.append(y + cb[:, ls])
        return jnp.concatenate(outs, axis=1)

    def down(rows, c0, c1):
        ks = slice(c0 * FFN_CHUNK, c1 * FFN_CHUNK)
        return _dot(act_scr[rows, ks], wdn_ref[ks, :])

    everything = slice(0, tm)
    for c in range(FFN_NCHUNK):
        gt = up_conv(c * FFN_CHUNK, 2 * (c % 2))
        val = up_conv(D_FF + c * FFN_CHUNK, 2 * (c % 2) + 1)
        act_scr[:, c * FFN_CHUNK:(c + 1) * FFN_CHUNK] = ((gt * _sigmoid(gt)) * val).astype(BF16)
        if c + 1 in FFN_DOWN_GROUPS[:-1]:
            gi = FFN_DOWN_GROUPS.index(c + 1)
            if gi == 0:
                acc_scr[...] = down(everything, 0, c + 1)
            else:
                acc_scr[...] += down(everything, FFN_DOWN_GROUPS[gi - 1], c + 1)
    for r0 in range(0, tm, FFN_LN_ROWS):
        rows = slice(r0, r0 + FFN_LN_ROWS)
        out = acc_scr[rows, :] + down(rows, FFN_DOWN_GROUPS[-2], FFN_DOWN_GROUPS[-1])
        o_ref[0, rows, :] = _layer_norm(ALPHA * x[rows] + out, g_ref[...], b_ref[...])


def _ffn_layer(x, wup, cw, cb, wdn, g, b):
    bsz, s, d = x.shape
    tm = FFN_TM
    return pl.pallas_call(
        _ffn_kernel,
        grid=(bsz, s // tm),
        in_specs=[pl.BlockSpec((1, tm, d), lambda bi, i: (bi, i, 0)),
                  _resident((d, 2 * D_FF)), _resident((FFN_CONV, 2 * D_FF)), _resident((1, 2 * D_FF)),
                  _resident((D_FF, d)), _resident((1, d)), _resident((1, d))],
        out_specs=pl.BlockSpec((1, tm, d), lambda bi, i: (bi, i, 0)),
        out_shape=jax.ShapeDtypeStruct((bsz, s, d), F32),
        scratch_shapes=[pltpu.VMEM((SUBLANES, 2 * D_FF), F32), pltpu.VMEM((tm, D_FF), BF16),
                        pltpu.VMEM((tm, d), F32),
                        pltpu.VMEM((4, FFN_CHUNK // LANES, SUBLANES + tm, LANES), F32)],
        compiler_params=_params(("arbitrary", "arbitrary")),
        name="ffn_layer",
    )(x, wup, cw, cb, wdn, g, b)


def kernel(x, mem, a_w_qkv, a_sinks, a_w_o, b_w_in, b_conv_w, b_conv_b, b_w_rgate, b_b_rgate, b_w_igate, b_b_igate, b_lambda, b_w_o, c_w_down, c_q_norm, c_kv_norm, c_w_uq, c_w_ukv, c_w_o, mem_w_kv, x_w_q, x_w_o, f_w_up, f_conv_w, f_conv_b, f_w_down, ln_g, ln_b):
    bsz, s, d = x.shape
    cos, sin = _rope_tables(s)

    mkv = _matmul(mem.reshape(bsz * MEM_LEN, d), mem_w_kv.astype(BF16), 512, BF16)
    mkv = mkv.reshape(bsz, MEM_LEN, 2 * d)

    def row(v):
        return v.reshape(1, -1)

    for i in range(DEPTH):
        kind, j = i % N_MIXERS, i // N_MIXERS
        g0, b0 = row(ln_g[i, 0]), row(ln_b[i, 0])
        mixer = None
        if kind == 0:
            x = _swa_layer(x, a_sinks[j], cos, sin, a_w_qkv[j].astype(BF16), a_w_o[j].astype(BF16), g0, b0)
        elif kind == 1:
            x = _rglru_layer(x, b_w_in[j].astype(BF16), b_conv_w[j], row(b_conv_b[j]),
                             b_w_rgate[j].astype(BF16), row(b_b_rgate[j]),
                             b_w_igate[j].astype(BF16), row(b_b_igate[j]),
                             row(b_lambda[j]), b_w_o[j].astype(BF16), g0, b0)
        else:
            wd = jnp.pad(c_w_down[j], ((0, 0), (0, C_DOWN_PAD - c_w_down.shape[-1]))).astype(BF16)
            wuq = c_w_uq[j].reshape(C_Q_RANK, C_HEADS, C_NOPE + C_ROPE)
            wuq = jnp.concatenate([wuq[:, :, :C_NOPE].reshape(C_Q_RANK, -1),
                                   wuq[:, :, C_NOPE:].reshape(C_Q_RANK, -1)], axis=-1).astype(BF16)
            wukv = c_w_ukv[j].reshape(C_KV_RANK, C_HEADS, C_NOPE + C_V)
            wukv = jnp.concatenate([wukv[:, :, :C_NOPE].reshape(C_KV_RANK, -1),
                                    wukv[:, :, C_NOPE:].reshape(C_KV_RANK, -1)], axis=-1).astype(BF16)
            q, k, v = _mla_proj(x, cos, sin, wd, row(c_q_norm[j]), row(c_kv_norm[j]), wuq, wukv)
            mixer = (_mla_attn(q, k, v), c_w_o[j].astype(BF16), g0, b0)
        x = _xattn_layer(x, x_w_q[i].astype(BF16), mkv, x_w_o[i].astype(BF16),
                         row(ln_g[i, 1]), row(ln_b[i, 1]), mixer=mixer)
        x = _ffn_layer(x, f_w_up[i].astype(BF16), f_conv_w[i], row(f_conv_b[i]), f_w_down[i].astype(BF16),
                       row(ln_g[i, 2]), row(ln_b[i, 2]))
    return x
```

```python
import functools
import math

import jax
import jax.numpy as jnp
import numpy as np
from jax import lax
from jax.experimental import pallas as pl
from jax.experimental.pallas import tpu as pltpu

D_MODEL = 1024
DEPTH = 4
N_MIXERS = 3
MEM_LEN = 256
BLOCK = 128
ROPE_THETA = 10000.0
NEG = -1e30
LN_EPS = 1e-5
RMS_EPS = 1e-6

A_HEADS = 16
A_KV_HEADS = 4
A_HEAD_DIM = 64

LRU_WIDTH = D_MODEL
LRU_BLOCKS = 4
LRU_BLOCK_W = LRU_WIDTH // LRU_BLOCKS
LRU_CONV = 4
LRU_C = 8.0

C_HEADS = 8
C_NOPE = 128
C_ROPE = 64
C_V = 128
C_Q_RANK = 384
C_KV_RANK = 256
C_QK_PAD = 256

X_HEADS = 4
X_HEAD_DIM = D_MODEL // X_HEADS

D_FF = 2816
FFN_CONV = 3
FFN_CHUNK = 256
FFN_NCHUNK = D_FF // FFN_CHUNK

ALPHA = (2.0 * DEPTH) ** 0.25

LANES = 128
SUBLANES = 8
VMEM_LIMIT = 56 * 1024 * 1024

BF16 = jnp.bfloat16
F32 = jnp.float32

NT_DIMS = (((1,), (1,)), ((), ()))
LOG2E = math.log2(math.e)


def _dot(a, b):
    return jnp.dot(a, b, preferred_element_type=F32)


def _dot_nt(a, b):
    return lax.dot_general(a, b, NT_DIMS, preferred_element_type=F32)


def _layer_norm(z, g, b):
    mu = jnp.mean(z, axis=-1, keepdims=True)
    zc = z - mu
    var = jnp.mean(zc * zc, axis=-1, keepdims=True)
    return zc * lax.rsqrt(var + LN_EPS) * g + b


def _sigmoid(z):
    return 0.5 * jnp.tanh(0.5 * z) + 0.5


def _resident(shape):
    nd = len(shape)
    return pl.BlockSpec(shape, lambda *_: (0,) * nd, pipeline_mode=pl.Buffered(1))


def _params(sem):
    return pltpu.CompilerParams(dimension_semantics=sem, vmem_limit_bytes=VMEM_LIMIT)


def _matmul_kernel(a_ref, w_ref, o_ref):
    o_ref[...] = _dot(a_ref[...].astype(BF16), w_ref[...]).astype(o_ref.dtype)


def _matmul(a, w, tm, out_dtype):
    t, k = a.shape
    n = w.shape[1]
    return pl.pallas_call(
        _matmul_kernel,
        grid=(t // tm,),
        in_specs=[pl.BlockSpec((tm, k), lambda i: (i, 0)), _resident((k, n))],
        out_specs=pl.BlockSpec((tm, n), lambda i: (i, 0)),
        out_shape=jax.ShapeDtypeStruct((t, n), out_dtype),
        compiler_params=_params(("parallel",)),
        name="matmul",
    )(a, w)


def _rope_slab(slab, cos, sin_signed, first_half):
    rot = jnp.where(first_half, pltpu.roll(slab, LANES - 32, 1), pltpu.roll(slab, 32, 1))
    return slab * cos + rot * sin_signed


def _rope_tables(seq):
    inv = 1.0 / (ROPE_THETA ** (jnp.arange(0, A_HEAD_DIM, 2, dtype=F32) / A_HEAD_DIM))
    ang = jnp.arange(seq, dtype=F32)[:, None] * inv[None, :]
    cos, sin = jnp.cos(ang), jnp.sin(ang)
    return (jnp.concatenate([cos, cos, cos, cos], axis=-1),
            jnp.concatenate([-sin, sin, -sin, sin], axis=-1))


SWA_TM = 512
SWA_NBLK = SWA_TM // BLOCK


def _swa_kernel(sinks_ref, x_ref, cos_ref, sin_ref, wqkv_ref, wo_ref, g_ref, b_ref, o_ref,
                kext_scr, vext_scr, q_scr, o_scr):
    i = pl.program_id(1)
    tm = SWA_TM

    @pl.when(i == 0)
    def _():
        kext_scr[:, 0:BLOCK, :] = jnp.zeros((2 * A_KV_HEADS, BLOCK, LANES), BF16)
        vext_scr[:, 0:BLOCK, :] = jnp.zeros((2 * A_KV_HEADS, BLOCK, LANES), BF16)

    @pl.when(i > 0)
    def _():
        kext_scr[:, 0:BLOCK, :] = kext_scr[:, tm:tm + BLOCK, :]
        vext_scr[:, 0:BLOCK, :] = vext_scr[:, tm:tm + BLOCK, :]

    x = x_ref[0]
    qkv = _dot(x.astype(BF16), wqkv_ref[...])
    cos = cos_ref[...]
    sin = sin_ref[...]
    lane = lax.broadcasted_iota(jnp.int32, (tm, LANES), 1)
    first_half = (lane & 32) == 0
    low = lane < 64

    nq = A_HEADS * A_HEAD_DIM
    nkv = A_KV_HEADS * A_HEAD_DIM
    for j in range(nq // LANES):
        slab = _rope_slab(qkv[:, j * LANES:(j + 1) * LANES], cos, sin, first_half)
        q_scr[:, j * LANES:(j + 1) * LANES] = (slab * (A_HEAD_DIM ** -0.5 * LOG2E)).astype(BF16)

    for m in range(nkv // LANES):
        kslab = _rope_slab(qkv[:, nq + m * LANES:nq + (m + 1) * LANES], cos, sin, first_half)
        vslab = qkv[:, nq + nkv + m * LANES:nq + nkv + (m + 1) * LANES]
        for slab, ext in ((kslab, kext_scr), (vslab, vext_scr)):
            swapped = pltpu.roll(slab, 64, 1)
            h0, h1 = 2 * m, 2 * m + 1
            ext[2 * h0, BLOCK:BLOCK + tm, :] = jnp.where(low, slab, 0.0).astype(BF16)
            ext[2 * h0 + 1, BLOCK:BLOCK + tm, :] = jnp.where(low, 0.0, swapped).astype(BF16)
            ext[2 * h1, BLOCK:BLOCK + tm, :] = jnp.where(low, swapped, 0.0).astype(BF16)
            ext[2 * h1 + 1, BLOCK:BLOCK + tm, :] = jnp.where(low, 0.0, slab).astype(BF16)

    qi = lax.broadcasted_iota(jnp.int32, (2 * BLOCK, 4 * BLOCK), 0) & (BLOCK - 1)
    kj = lax.broadcasted_iota(jnp.int32, (2 * BLOCK, 4 * BLOCK), 1) & (2 * BLOCK - 1)
    dist = qi + BLOCK - kj
    band = (dist >= 0) & (dist < BLOCK)
    band_first = band & ((kj >= BLOCK) | (i > 0))
    top = lax.broadcasted_iota(jnp.int32, (2 * BLOCK, 1), 0) < BLOCK
    low2 = lax.broadcasted_iota(jnp.int32, (2 * BLOCK, LANES), 1) < 64

    def window(ext, h, r0):
        return jnp.concatenate([ext[2 * h, r0:r0 + 2 * BLOCK, :], ext[2 * h + 1, r0:r0 + 2 * BLOCK, :]],
                               axis=0)

    for n in range(SWA_NBLK):
        mask = band_first if n == 0 else band
        r0 = n * BLOCK
        scores = []
        for h in range(A_KV_HEADS):
            qs = jnp.concatenate([q_scr[r0:r0 + BLOCK, (2 * h) * LANES:(2 * h + 1) * LANES],
                                  q_scr[r0:r0 + BLOCK, (2 * h + 1) * LANES:(2 * h + 2) * LANES]], axis=0)
            scores.append(_dot_nt(qs, window(kext_scr, h, r0)))
        probs, inv_denoms = [], []
        for h in range(A_KV_HEADS):
            s = jnp.where(mask, scores[h], NEG)
            ps, rs = [], []
            for half in range(2):
                sh = s[:, half * 2 * BLOCK:(half + 1) * 2 * BLOCK]
                sink = jnp.where(top, sinks_ref[4 * h + half] * LOG2E, sinks_ref[4 * h + 2 + half] * LOG2E)
                mx = jnp.maximum(jnp.max(sh, axis=-1, keepdims=True), sink)
                p = jnp.exp2(sh - mx)
                rs.append(1.0 / (jnp.sum(p, axis=-1, keepdims=True) + jnp.exp2(sink - mx)))
                ps.append(p.astype(BF16))
            probs.append(jnp.concatenate(ps, axis=1))
            inv_denoms.append(jnp.where(low2, rs[0], rs[1]))
        for h in range(A_KV_HEADS):
            o = _dot(probs[h], window(vext_scr, h, r0)) * inv_denoms[h]
            o_scr[r0:r0 + BLOCK, (2 * h) * LANES:(2 * h + 1) * LANES] = o[:BLOCK].astype(BF16)
            o_scr[r0:r0 + BLOCK, (2 * h + 1) * LANES:(2 * h + 2) * LANES] = o[BLOCK:].astype(BF16)

    for r0 in range(0, tm, tm // 2):
        rows = slice(r0, r0 + tm // 2)
        y = _dot(o_scr[rows, :], wo_ref[...])
        o_ref[0, rows, :] = _layer_norm(ALPHA * x[rows] + y, g_ref[...], b_ref[...])


def _swa_layer(x, sinks, cos, sin, wqkv, wo, g, b):
    bsz, s, d = x.shape
    tm = SWA_TM
    nqkv = wqkv.shape[1]
    return pl.pallas_call(
        _swa_kernel,
        grid=(bsz, s // tm),
        in_specs=[pl.BlockSpec(memory_space=pltpu.SMEM),
                  pl.BlockSpec((1, tm, d), lambda bi, i: (bi, i, 0)),
                  pl.BlockSpec((tm, LANES), lambda bi, i: (i, 0)),
                  pl.BlockSpec((tm, LANES), lambda bi, i: (i, 0)),
                  _resident((d, nqkv)), _resident((d, d)), _resident((1, d)), _resident((1, d))],
        out_specs=pl.BlockSpec((1, tm, d), lambda bi, i: (bi, i, 0)),
        out_shape=jax.ShapeDtypeStruct((bsz, s, d), F32),
        scratch_shapes=[pltpu.VMEM((2 * A_KV_HEADS, tm + BLOCK, LANES), BF16),
                        pltpu.VMEM((2 * A_KV_HEADS, tm + BLOCK, LANES), BF16),
                        pltpu.VMEM((tm, d), BF16),
                        pltpu.VMEM((tm, d), BF16)],
        compiler_params=_params(("arbitrary", "arbitrary")),
        name="swa_layer",
    )(sinks, x, cos, sin, wqkv, wo, g, b)


LRU_SUB = 256
LRU_NSUB = 2
LRU_NLT = LRU_WIDTH // LANES
LRU_PAD = 2 * SUBLANES


def _rglru_kernel(x_ref, xn_ref, win_ref, cw_ref, cb_ref, wr_ref, br_ref, wi_ref, bi_ref, lam_ref, wo_ref,
                  g_ref, b_ref, o_ref, ustage_scr, a_scr, b_scr, hcar_scr, gu_scr):
    i = pl.program_id(1)
    ts = LRU_SUB
    w = LRU_WIDTH
    pad = LRU_PAD
    sl = SUBLANES

    @pl.when(i == 0)
    def _():
        ustage_scr[:, :, 0:sl, :] = jnp.zeros((LRU_NSUB, LRU_NLT, sl, LANES), F32)
        a_scr[:, :, :, 0:pad, :] = jnp.ones((LRU_NSUB, 3, LRU_NLT, pad, LANES), F32)
        b_scr[:, :, :, 0:pad, :] = jnp.zeros((LRU_NSUB, 3, LRU_NLT, pad, LANES), F32)
        hcar_scr[...] = jnp.zeros_like(hcar_scr)

    @pl.when(i > 0)
    def _():
        ustage_scr[:, :, 0:sl, :] = ustage_scr[:, :, ts:ts + sl, :]
        a_scr[:, 0, :, sl:pad, :] = a_scr[:, 0, :, ts + sl:ts + pad, :]
        b_scr[:, 0, :, sl:pad, :] = b_scr[:, 0, :, ts + sl:ts + pad, :]

    cw = cw_ref[...]
    cb = cb_ref[...]
    z = -lam_ref[...]
    softplus = jnp.maximum(z, 0.0) + jnp.log1p(jnp.exp(-jnp.abs(z)))
    c0 = math.sqrt(2.0 / math.pi)

    @pl.when(i == 0)
    def _():
        for sub in range(LRU_NSUB):
            gu_scr[0, sub] = _dot(x_ref[sub].astype(BF16), win_ref[...])

    def project_next(slot, sub):
        gu_scr[1 - slot, sub] = _dot(xn_ref[sub].astype(BF16), win_ref[...])

    def conv_gates(slot, sub):
        u = gu_scr[slot, sub, :, w:2 * w]
        ucs = []
        for lt in range(LRU_NLT):
            ls = slice(lt * LANES, (lt + 1) * LANES)
            us = ustage_scr.at[sub, lt]
            us[sl:sl + ts, :] = u[:, ls]
            uc = us[sl - 3:sl - 3 + ts, :] * cw[0:1, ls]
            uc = uc + us[sl - 2:sl - 2 + ts, :] * cw[1:2, ls]
            uc = uc + us[sl - 1:sl - 1 + ts, :] * cw[2:3, ls]
            uc = uc + u[:, ls] * cw[3:4, ls]
            ucs.append(uc + cb[:, ls])
        uc = jnp.concatenate(ucs, axis=1)

        ucb = uc.astype(BF16)
        rs, igs = [], []
        for h in range(LRU_BLOCKS):
            blk = ucb[:, h * LRU_BLOCK_W:(h + 1) * LRU_BLOCK_W]
            rs.append(_dot(blk, wr_ref[h]))
            igs.append(_dot(blk, wi_ref[h]))
        return uc, jnp.concatenate(rs, axis=-1), jnp.concatenate(igs, axis=-1)

    def recurrence(slot, sub, uc, r_pre, ig_pre):
        hblk = hcar_scr[sub]
        gate = gu_scr[slot, sub, :, 0:w]
        r = _sigmoid(r_pre + br_ref[...])
        ig = _sigmoid(ig_pre + bi_ref[...])
        log_a = (-LRU_C * r) * softplus
        a0 = jnp.exp(log_a)
        th = jnp.tanh(log_a)
        b0 = (jnp.sqrt(-2.0 * th) * lax.rsqrt(1.0 - th)) * (ig * uc)

        a8, b8 = [], []
        for lt in range(LRU_NLT):
            ls = slice(lt * LANES, (lt + 1) * LANES)
            a_scr[sub, 0, lt, pad:pad + ts, :] = a0[:, ls]
            b_scr[sub, 0, lt, pad:pad + ts, :] = b0[:, ls]
            for k, d in enumerate((1, 2)):
                a_cur = a_scr[sub, k, lt, sl:pad + ts, :]
                b_cur = b_scr[sub, k, lt, sl:pad + ts, :]
                a_scr[sub, k + 1, lt, sl:pad + ts, :] = a_cur * a_scr[sub, k, lt, sl - d:pad + ts - d, :]
                b_scr[sub, k + 1, lt, sl:pad + ts, :] = (a_cur * b_scr[sub, k, lt, sl - d:pad + ts - d, :]
                                                        + b_cur)
            a_cur = a_scr[sub, 2, lt, pad:pad + ts, :]
            a8.append(a_cur * a_scr[sub, 2, lt, pad - 4:pad - 4 + ts, :])
            b8.append(a_cur * b_scr[sub, 2, lt, pad - 4:pad - 4 + ts, :] + b_scr[sub, 2, lt, pad:pad + ts, :])
        a8 = jnp.concatenate(a8, axis=1)
        b8 = jnp.concatenate(b8, axis=1)

        hs = []
        for j in range(ts // sl):
            hblk = a8[j * sl:(j + 1) * sl] * hblk + b8[j * sl:(j + 1) * sl]
            hs.append(hblk)
        h = jnp.concatenate(hs, axis=0)

        gelu = gate * (0.5 * (1.0 + jnp.tanh(c0 * (gate + 0.044715 * (gate * gate * gate)))))
        hcar_scr[sub] = hblk
        return (h * gelu).astype(BF16)

    def out_proj(sub, yb):
        return ALPHA * x_ref[sub] + _dot(yb, wo_ref[...])

    def step(slot):
        zs = []
        for sub in range(LRU_NSUB):
            uc, r_pre, ig_pre = conv_gates(slot, sub)
            project_next(slot, sub)
            zs.append(out_proj(sub, recurrence(slot, sub, uc, r_pre, ig_pre)))
            if sub > 0:
                o_ref[sub - 1] = _layer_norm(zs[sub - 1], g_ref[...], b_ref[...])
        o_ref[LRU_NSUB - 1] = _layer_norm(zs[-1], g_ref[...], b_ref[...])

    for parity in range(2):
        pl.when(lax.rem(i, 2) == parity)(functools.partial(step, parity))


def _rglru_layer(x, win, cw, cb, wr, br, wi, bi, lam, wo, g, b):
    bsz, s, d = x.shape
    tm = LRU_SUB
    w = LRU_WIDTH
    last = s // tm - 1
    return pl.pallas_call(
        _rglru_kernel,
        grid=(bsz // LRU_NSUB, s // tm),
        in_specs=[pl.BlockSpec((LRU_NSUB, tm, d), lambda bi_, i: (bi_, i, 0)),
                  pl.BlockSpec((LRU_NSUB, tm, d), lambda bi_, i: (bi_, jnp.minimum(i + 1, last), 0)),
                  _resident((d, 2 * w)), _resident((LRU_CONV, w)), _resident((1, w)),
                  _resident((LRU_BLOCKS, LRU_BLOCK_W, LRU_BLOCK_W)), _resident((1, w)),
                  _resident((LRU_BLOCKS, LRU_BLOCK_W, LRU_BLOCK_W)), _resident((1, w)),
                  _resident((1, w)), _resident((w, d)), _resident((1, d)), _resident((1, d))],
        out_specs=pl.BlockSpec((LRU_NSUB, tm, d), lambda bi_, i: (bi_, i, 0)),
        out_shape=jax.ShapeDtypeStruct((bsz, s, d), F32),
        scratch_shapes=[pltpu.VMEM((LRU_NSUB, LRU_NLT, SUBLANES + LRU_SUB, LANES), F32),
                        pltpu.VMEM((LRU_NSUB, 3, LRU_NLT, LRU_PAD + LRU_SUB, LANES), F32),
                        pltpu.VMEM((LRU_NSUB, 3, LRU_NLT, LRU_PAD + LRU_SUB, LANES), F32),
                        pltpu.VMEM((LRU_NSUB, SUBLANES, w), F32),
                        pltpu.VMEM((2, LRU_NSUB, tm, 2 * w), F32)],
        compiler_params=_params(("arbitrary", "arbitrary")),
        name="rglru_layer",
    )(x, x, win, cw, cb, wr, br, wi, bi, lam, wo, g, b)


MLA_TM = 512
MLA_TQ = 1024
MLA_HALF = MLA_TQ // 2
MLA_TK = 512
C_DOWN_PAD = 768
MLA_Q_SCALE = (C_NOPE + C_ROPE) ** -0.5 * math.log2(math.e)


def _mla_proj_kernel(x_ref, cos_ref, sin_ref, wd_ref, qn_ref, kvn_ref, wuq_ref, wukv_ref,
                     q_ref, k_ref, v_ref):
    tm = MLA_TM
    xb = x_ref[0].astype(BF16)
    c = _dot(xb, wd_ref[...])
    cq = c[:, :C_Q_RANK]
    ckv = c[:, C_Q_RANK:C_Q_RANK + C_KV_RANK]
    kr_slab = c[:, C_Q_RANK + C_KV_RANK:]
    cq = cq * lax.rsqrt(jnp.mean(cq * cq, axis=-1, keepdims=True) + RMS_EPS) * qn_ref[...]
    ckv = ckv * lax.rsqrt(jnp.mean(ckv * ckv, axis=-1, keepdims=True) + RMS_EPS) * kvn_ref[...]
    q = _dot(cq.astype(BF16), wuq_ref[...]) * MLA_Q_SCALE
    kv = _dot(ckv.astype(BF16), wukv_ref[...])

    cos = cos_ref[...]
    sin = sin_ref[...]
    lane = lax.broadcasted_iota(jnp.int32, (tm, LANES), 1)
    first_half = (lane & 32) == 0
    low = lane < 64
    nn = C_HEADS * C_NOPE

    kr = _rope_slab(kr_slab, cos, sin, first_half)
    kr_pair = (kr.astype(BF16), pltpu.roll(kr, 64, 1).astype(BF16))
    for pair in range(C_HEADS // 2):
        qr = _rope_slab(q[:, nn + pair * LANES:nn + (pair + 1) * LANES], cos, sin, first_half)
        for half in range(2):
            h = 2 * pair + half
            qr_h = jnp.where(low, qr, 0.0) if half == 0 else jnp.where(low, 0.0, qr)
            q_ref[0, h, :, 0:C_NOPE] = q[:, h * C_NOPE:(h + 1) * C_NOPE].astype(BF16)
            q_ref[0, h, :, C_NOPE:C_QK_PAD] = qr_h.astype(BF16)
            k_ref[0, h, :, 0:C_NOPE] = kv[:, h * C_NOPE:(h + 1) * C_NOPE].astype(BF16)
            k_ref[0, h, :, C_NOPE:C_QK_PAD] = kr_pair[half]
            v_ref[0, h] = kv[:, nn + h * C_V:nn + (h + 1) * C_V].astype(BF16)


def _mla_proj(x, cos, sin, wd, qn, kvn, wuq, wukv):
    bsz, s, d = x.shape
    tm = MLA_TM
    qk_shape = jax.ShapeDtypeStruct((bsz, C_HEADS, s, C_QK_PAD), BF16)
    v_shape = jax.ShapeDtypeStruct((bsz, C_HEADS, s, C_V), BF16)
    return pl.pallas_call(
        _mla_proj_kernel,
        grid=(bsz, s // tm),
        in_specs=[pl.BlockSpec((1, tm, d), lambda bi, i: (bi, i, 0)),
                  pl.BlockSpec((tm, LANES), lambda bi, i: (i, 0)),
                  pl.BlockSpec((tm, LANES), lambda bi, i: (i, 0)),
                  _resident(wd.shape), _resident(qn.shape), _resident(kvn.shape),
                  _resident(wuq.shape), _resident(wukv.shape)],
        out_specs=[pl.BlockSpec((1, C_HEADS, tm, C_QK_PAD), lambda bi, i: (bi, 0, i, 0)),
                   pl.BlockSpec((1, C_HEADS, tm, C_QK_PAD), lambda bi, i: (bi, 0, i, 0)),
                   pl.BlockSpec((1, C_HEADS, tm, C_V), lambda bi, i: (bi, 0, i, 0))],
        out_shape=[qk_shape, qk_shape, v_shape],
        compiler_params=_params(("parallel", "parallel")),
        name="mla_proj",
    )(x, cos, sin, wd, qn, kvn, wuq, wukv)


def _mla_attn_kernel(q_ref, qn_ref, k_ref, v_ref, o_ref, m_scr, acc_scr, s_scr):
    i = pl.program_id(2)
    tk, hq = MLA_TK, MLA_HALF

    m_scr[...] = jnp.full_like(m_scr, -jnp.inf)
    acc_scr[...] = jnp.zeros_like(acc_scr)
    ones = jnp.ones((tk, LANES), BF16)

    def scores(slot, hf, j, queries=q_ref):
        k0 = pl.multiple_of(j * tk, tk)
        s_scr[slot, hf] = _dot_nt(queries[0, 0, hf * hq:(hf + 1) * hq, :], k_ref[0, 0, pl.ds(k0, tk), :])

    def consume(slot, hf, j, diagonal):
        k0 = pl.multiple_of(j * tk, tk)
        vv = jnp.concatenate([v_ref[0, 0, pl.ds(k0, tk), :], ones], axis=1)
        s = s_scr[slot, hf]
        if diagonal:
            qpos = lax.broadcasted_iota(jnp.int32, (hq, tk), 0)
            kpos = lax.broadcasted_iota(jnp.int32, (hq, tk), 1)
            s = jnp.where(kpos <= qpos, s, NEG)
        m_prev = m_scr[hf]
        m_next = jnp.maximum(m_prev, jnp.max(s, axis=-1, keepdims=True))
        p = jnp.exp2(s - jnp.tile(m_next, (1, tk // LANES)))
        corr = jnp.exp2(m_prev - m_next)
        acc_scr[hf] = jnp.tile(corr, (1, 2)) * acc_scr[hf] + _dot(p.astype(BF16), vv)
        m_scr[hf] = m_next

    @pl.when(i == 0)
    def _():
        scores(0, 0, 0)
        scores(0, 1, 0)

    def chunk_pair(j):
        scores(1, 0, j + 1)
        scores(1, 1, j + 1)
        consume(0, 0, j, False)
        consume(0, 1, j, False)
        scores(0, 0, j + 2)
        scores(0, 1, j + 2)
        consume(1, 0, j + 1, False)
        consume(1, 1, j + 1, False)

    def body(t, carry):
        chunk_pair(4 * t)
        chunk_pair(4 * t + 2)
        return carry

    lax.fori_loop(0, i // 2, body, 0)

    @pl.when(i % 2 == 1)
    def _():
        chunk_pair(2 * i - 2)

    scores(1, 1, 2 * i + 1)
    consume(0, 0, 2 * i, True)
    consume(0, 1, 2 * i, False)
    scores(0, 0, 0, queries=qn_ref)
    scores(0, 1, 0, queries=qn_ref)
    consume(1, 1, 2 * i + 1, True)
    for hf in range(2):
        acc = acc_scr[hf]
        o_ref[0, hf * hq:(hf + 1) * hq, :] = (acc[:, :C_V] / acc[:, C_V:]).astype(BF16)


def _mla_attn(q, k, v):
    bsz, nh, s, _ = q.shape
    tq = MLA_TQ
    last = s // tq - 1
    return pl.pallas_call(
        _mla_attn_kernel,
        grid=(bsz, nh, s // tq),
        in_specs=[pl.BlockSpec((1, 1, tq, C_QK_PAD), lambda b, h, i: (b, h, i, 0)),
                  pl.BlockSpec((1, 1, tq, C_QK_PAD), lambda b, h, i: (b, h, jnp.minimum(i + 1, last), 0)),
                  pl.BlockSpec((1, 1, s, C_QK_PAD), lambda b, h, i: (b, h, 0, 0)),
                  pl.BlockSpec((1, 1, s, C_V), lambda b, h, i: (b, h, 0, 0))],
        out_specs=pl.BlockSpec((1, tq, C_V), lambda b, h, i: (b, i, h)),
        out_shape=jax.ShapeDtypeStruct((bsz, s, nh * C_V), BF16),
        scratch_shapes=[pltpu.VMEM((2, MLA_HALF, LANES), F32),
                        pltpu.VMEM((2, MLA_HALF, 2 * C_V), F32),
                        pltpu.VMEM((2, 2, MLA_HALF, MLA_TK), F32)],
        compiler_params=_params(("parallel", "parallel", "arbitrary")),
        name="mla_attn",
    )(q, q, k, v)


X_TM = 1024
X_LN_ROWS = 256


def _xattn_kernel(*refs, mixer_prologue):
    tm = X_TM
    if mixer_prologue:
        a_ref, wa_ref, ga_ref, ba_ref, x_ref, wq_ref, mkv_ref, wo_ref, g_ref, b_ref, o_ref = refs
        x = _layer_norm(ALPHA * x_ref[0] + _dot(a_ref[0], wa_ref[...]), ga_ref[...], ba_ref[...])
    else:
        x_ref, wq_ref, mkv_ref, wo_ref, g_ref, b_ref, o_ref = refs
        x = x_ref[0]
    q = (_dot(x.astype(BF16), wq_ref[...]) * (X_HEAD_DIM ** -0.5 * LOG2E)).astype(BF16)
    heads = [(h * X_HEAD_DIM, (h + 1) * X_HEAD_DIM) for h in range(X_HEADS)]
    scores = [_dot_nt(q[:, lo:hi], mkv_ref[0, :, lo:hi]) for lo, hi in heads]
    probs, inv_denoms = [], []
    for s in scores:
        p = jnp.exp2(s - jnp.max(s, axis=-1, keepdims=True))
        inv_denoms.append(1.0 / jnp.sum(p, axis=-1, keepdims=True))
        probs.append(p.astype(BF16))
    outs = [(_dot(probs[h], mkv_ref[0, :, D_MODEL + lo:D_MODEL + hi]) * inv_denoms[h]).astype(BF16)
            for h, (lo, hi) in enumerate(heads)]
    o = jnp.concatenate(outs, axis=-1)
    for r0 in range(0, tm, X_LN_ROWS):
        rows = slice(r0, r0 + X_LN_ROWS)
        y = _dot(o[rows], wo_ref[...])
        o_ref[0, rows, :] = _layer_norm(ALPHA * x[rows] + y, g_ref[...], b_ref[...])


def _xattn_layer(x, wq, mkv, wo, g, b, mixer=None):
    bsz, s, d = x.shape
    tm = X_TM
    tile = pl.BlockSpec((1, tm, d), lambda bi, i: (bi, i, 0))
    in_specs = [tile, _resident((d, d)),
                pl.BlockSpec((1, MEM_LEN, 2 * d), lambda bi, i: (bi, 0, 0)),
                _resident((d, d)), _resident((1, d)), _resident((1, d))]
    args = (x, wq, mkv, wo, g, b)
    if mixer is not None:
        in_specs = [tile, _resident((d, d)), _resident((1, d)), _resident((1, d))] + in_specs
        args = tuple(mixer) + args
    return pl.pallas_call(
        functools.partial(_xattn_kernel, mixer_prologue=mixer is not None),
        grid=(bsz, s // tm),
        in_specs=in_specs,
        out_specs=tile,
        out_shape=jax.ShapeDtypeStruct((bsz, s, d), F32),
        compiler_params=_params(("parallel", "parallel")),
        name="xattn_layer",
    )(*args)


FFN_TM = 512
FFN_LN_ROWS = 256
FFN_DOWN_GROUPS = (6, FFN_NCHUNK)


def _ffn_kernel(x_ref, wup_ref, cw_ref, cb_ref, wdn_ref, g_ref, b_ref, o_ref, hprev_scr, act_scr,
                acc_scr, hstage_scr):
    i = pl.program_id(1)
    tm = FFN_TM

    @pl.when(i == 0)
    def _():
        hprev_scr[...] = jnp.zeros_like(hprev_scr)

    x = x_ref[0]
    xb = x.astype(BF16)

    def up_conv(col0, slot):
        cols = slice(col0, col0 + FFN_CHUNK)
        h = _dot(xb, wup_ref[:, cols])
        prev = hprev_scr[:, cols]
        hprev_scr[:, cols] = h[tm - SUBLANES:, :]
        cw = cw_ref[:, cols]
        cb = cb_ref[:, cols]
        outs = []
        for lt in range(FFN_CHUNK // LANES):
            ls = slice(lt * LANES, (lt + 1) * LANES)
            hs = hstage_scr.at[slot, lt]
            hs[0:SUBLANES, :] = prev[:, ls]
            hs[SUBLANES:SUBLANES + tm, :] = h[:, ls]
            y = hs[SUBLANES - 2:SUBLANES - 2 + tm, :] * cw[0:1, ls]
            y = y + hs[SUBLANES - 1:SUBLANES - 1 + tm, :] * cw[1:2, ls]
            y = y + h[:, ls] * cw[2:3, ls]
            outs.append(y + cb[:, ls])
        return jnp.concatenate(outs, axis=1)

    def down(rows, c0, c1):
        ks = slice(c0 * FFN_CHUNK, c1 * FFN_CHUNK)
        return _dot(act_scr[rows, ks], wdn_ref[ks, :])

    everything = slice(0, tm)
    for c in range(FFN_NCHUNK):
        gt = up_conv(c * FFN_CHUNK, 2 * (c % 2))
        val = up_conv(D_FF + c * FFN_CHUNK, 2 * (c % 2) + 1)
        act_scr[:, c * FFN_CHUNK:(c + 1) * FFN_CHUNK] = ((gt * _sigmoid(gt)) * val).astype(BF16)
        if c + 1 in FFN_DOWN_GROUPS[:-1]:
            gi = FFN_DOWN_GROUPS.index(c + 1)
            if gi == 0:
                acc_scr[...] = down(everything, 0, c + 1)
            else:
                acc_scr[...] += down(everything, FFN_DOWN_GROUPS[gi - 1], c + 1)
    for r0 in range(0, tm, FFN_LN_ROWS):
        rows = slice(r0, r0 + FFN_LN_ROWS)
        out = acc_scr[rows, :] + down(rows, FFN_DOWN_GROUPS[-2], FFN_DOWN_GROUPS[-1])
        o_ref[0, rows, :] = _layer_norm(ALPHA * x[rows] + out, g_ref[...], b_ref[...])


def _ffn_layer(x, wup, cw, cb, wdn, g, b):
    bsz, s, d = x.shape
    tm = FFN_TM
    return pl.pallas_call(
        _ffn_kernel,
        grid=(bsz, s // tm),
        in_specs=[pl.BlockSpec((1, tm, d), lambda bi, i: (bi, i, 0)),
                  _resident((d, 2 * D_FF)), _resident((FFN_CONV, 2 * D_FF)), _resident((1, 2 * D_FF)),
                  _resident((D_FF, d)), _resident((1, d)), _resident((1, d))],
        out_specs=pl.BlockSpec((1, tm, d), lambda bi, i: (bi, i, 0)),
        out_shape=jax.ShapeDtypeStruct((bsz, s, d), F32),
        scratch_shapes=[pltpu.VMEM((SUBLANES, 2 * D_FF), F32), pltpu.VMEM((tm, D_FF), BF16),
                        pltpu.VMEM((tm, d), F32),
                        pltpu.VMEM((4, FFN_CHUNK // LANES, SUBLANES + tm, LANES), F32)],
        compiler_params=_params(("arbitrary", "arbitrary")),
        name="ffn_layer",
    )(x, wup, cw, cb, wdn, g, b)


def kernel(x, mem, a_w_qkv, a_sinks, a_w_o, b_w_in, b_conv_w, b_conv_b, b_w_rgate, b_b_rgate, b_w_igate, b_b_igate, b_lambda, b_w_o, c_w_down, c_q_norm, c_kv_norm, c_w_uq, c_w_ukv, c_w_o, mem_w_kv, x_w_q, x_w_o, f_w_up, f_conv_w, f_conv_b, f_w_down, ln_g, ln_b):
    bsz, s, d = x.shape
    cos, sin = _rope_tables(s)

    mkv = _matmul(mem.reshape(bsz * MEM_LEN, d), mem_w_kv.astype(BF16), 512, BF16)
    mkv = mkv.reshape(bsz, MEM_LEN, 2 * d)

    def row(v):
        return v.reshape(1, -1)

    for i in range(DEPTH):
        kind, j = i % N_MIXERS, i // N_MIXERS
        g0, b0 = row(ln_g[i, 0]), row(ln_b[i, 0])
        mixer = None
        if kind == 0:
            x = _swa_layer(x, a_sinks[j], cos, sin, a_w_qkv[j].astype(BF16), a_w_o[j].astype(BF16), g0, b0)
        elif kind == 1:
            x = _rglru_layer(x, b_w_in[j].astype(BF16), b_conv_w[j], row(b_conv_b[j]),
                             b_w_rgate[j].astype(BF16), row(b_b_rgate[j]),
                             b_w_igate[j].astype(BF16), row(b_b_igate[j]),
                             row(b_lambda[j]), b_w_o[j].astype(BF16), g0, b0)
        else:
            wd = jnp.pad(c_w_down[j], ((0, 0), (0, C_DOWN_PAD - c_w_down.shape[-1]))).astype(BF16)
            wuq = c_w_uq[j].reshape(C_Q_RANK, C_HEADS, C_NOPE + C_ROPE)
            wuq = jnp.concatenate([wuq[:, :, :C_NOPE].reshape(C_Q_RANK, -1),
                                   wuq[:, :, C_NOPE:].reshape(C_Q_RANK, -1)], axis=-1).astype(BF16)
            wukv = c_w_ukv[j].reshape(C_KV_RANK, C_HEADS, C_NOPE + C_V)
            wukv = jnp.concatenate([wukv[:, :, :C_NOPE].reshape(C_KV_RANK, -1),
                                    wukv[:, :, C_NOPE:].reshape(C_KV_RANK, -1)], axis=-1).astype(BF16)
            q, k, v = _mla_proj(x, cos, sin, wd, row(c_q_norm[j]), row(c_kv_norm[j]), wuq, wukv)
            mixer = (_mla_attn(q, k, v), c_w_o[j].astype(BF16), g0, b0)
        x = _xattn_layer(x, x_w_q[i].astype(BF16), mkv, x_w_o[i].astype(BF16),
                         row(ln_g[i, 1]), row(ln_b[i, 1]), mixer=mixer)
        x = _ffn_layer(x, f_w_up[i].astype(BF16), f_conv_w[i], row(f_conv_b[i]), f_w_down[i].astype(BF16),
                       row(ln_g[i, 2]), row(ln_b[i, 2]))
    return x
```

```python
import functools
import math

import jax
import jax.numpy as jnp
import numpy as np
from jax import lax
from jax.experimental import pallas as pl
from jax.experimental.pallas import tpu as pltpu

D_MODEL = 1024
DEPTH = 4
N_MIXERS = 3
MEM_LEN = 256
BLOCK = 128
ROPE_THETA = 10000.0
NEG = -1e30
LN_EPS = 1e-5
RMS_EPS = 1e-6

A_HEADS = 16
A_KV_HEADS = 4
A_HEAD_DIM = 64

LRU_WIDTH = D_MODEL
LRU_BLOCKS = 4
LRU_BLOCK_W = LRU_WIDTH // LRU_BLOCKS
LRU_CONV = 4
LRU_C = 8.0

C_HEADS = 8
C_NOPE = 128
C_ROPE = 64
C_V = 128
C_Q_RANK = 384
C_KV_RANK = 256
C_QK_PAD = 256

X_HEADS = 4
X_HEAD_DIM = D_MODEL // X_HEADS

D_FF = 2816
FFN_CONV = 3
FFN_CHUNK = 256
FFN_NCHUNK = D_FF // FFN_CHUNK

ALPHA = (2.0 * DEPTH) ** 0.25

LANES = 128
SUBLANES = 8
VMEM_LIMIT = 56 * 1024 * 1024

BF16 = jnp.bfloat16
F32 = jnp.float32

NT_DIMS = (((1,), (1,)), ((), ()))
LOG2E = math.log2(math.e)


def _dot(a, b):
    return jnp.dot(a, b, preferred_element_type=F32)


def _dot_nt(a, b):
    return lax.dot_general(a, b, NT_DIMS, preferred_element_type=F32)


def _layer_norm(z, g, b):
    mu = jnp.mean(z, axis=-1, keepdims=True)
    zc = z - mu
    var = jnp.mean(zc * zc, axis=-1, keepdims=True)
    return zc * lax.rsqrt(var + LN_EPS) * g + b


def _sigmoid(z):
    return 0.5 * jnp.tanh(0.5 * z) + 0.5


def _resident(shape):
    nd = len(shape)
    return pl.BlockSpec(shape, lambda *_: (0,) * nd, pipeline_mode=pl.Buffered(1))


def _params(sem):
    return pltpu.CompilerParams(dimension_semantics=sem, vmem_limit_bytes=VMEM_LIMIT)


def _matmul_kernel(a_ref, w_ref, o_ref):
    o_ref[...] = _dot(a_ref[...].astype(BF16), w_ref[...]).astype(o_ref.dtype)


def _matmul(a, w, tm, out_dtype):
    t, k = a.shape
    n = w.shape[1]
    return pl.pallas_call(
        _matmul_kernel,
        grid=(t // tm,),
        in_specs=[pl.BlockSpec((tm, k), lambda i: (i, 0)), _resident((k, n))],
        out_specs=pl.BlockSpec((tm, n), lambda i: (i, 0)),
        out_shape=jax.ShapeDtypeStruct((t, n), out_dtype),
        compiler_params=_params(("parallel",)),
        name="matmul",
    )(a, w)


def _rope_slab(slab, cos, sin_signed, first_half):
    rot = jnp.where(first_half, pltpu.roll(slab, LANES - 32, 1), pltpu.roll(slab, 32, 1))
    return slab * cos + rot * sin_signed


def _rope_tables(seq):
    inv = 1.0 / (ROPE_THETA ** (jnp.arange(0, A_HEAD_DIM, 2, dtype=F32) / A_HEAD_DIM))
    ang = jnp.arange(seq, dtype=F32)[:, None] * inv[None, :]
    cos, sin = jnp.cos(ang), jnp.sin(ang)
    return (jnp.concatenate([cos, cos, cos, cos], axis=-1),
            jnp.concatenate([-sin, sin, -sin, sin], axis=-1))


SWA_TM = 512
SWA_NBLK = SWA_TM // BLOCK


def _swa_kernel(sinks_ref, x_ref, cos_ref, sin_ref, wqkv_ref, wo_ref, g_ref, b_ref, o_ref,
                kext_scr, vext_scr, q_scr, o_scr):
    i = pl.program_id(1)
    tm = SWA_TM

    @pl.when(i == 0)
    def _():
        kext_scr[:, 0:BLOCK, :] = jnp.zeros((2 * A_KV_HEADS, BLOCK, LANES), BF16)
        vext_scr[:, 0:BLOCK, :] = jnp.zeros((2 * A_KV_HEADS, BLOCK, LANES), BF16)

    @pl.when(i > 0)
    def _():
        kext_scr[:, 0:BLOCK, :] = kext_scr[:, tm:tm + BLOCK, :]
        vext_scr[:, 0:BLOCK, :] = vext_scr[:, tm:tm + BLOCK, :]

    x = x_ref[0]
    qkv = _dot(x.astype(BF16), wqkv_ref[...])
    cos = cos_ref[...]
    sin = sin_ref[...]
    lane = lax.broadcasted_iota(jnp.int32, (tm, LANES), 1)
    first_half = (lane & 32) == 0
    low = lane < 64

    nq = A_HEADS * A_HEAD_DIM
    nkv = A_KV_HEADS * A_HEAD_DIM
    for j in range(nq // LANES):
        slab = _rope_slab(qkv[:, j * LANES:(j + 1) * LANES], cos, sin, first_half)
        q_scr[:, j * LANES:(j + 1) * LANES] = (slab * (A_HEAD_DIM ** -0.5 * LOG2E)).astype(BF16)

    for m in range(nkv // LANES):
        kslab = _rope_slab(qkv[:, nq + m * LANES:nq + (m + 1) * LANES], cos, sin, first_half)
        vslab = qkv[:, nq + nkv + m * LANES:nq + nkv + (m + 1) * LANES]
        for slab, ext in ((kslab, kext_scr), (vslab, vext_scr)):
            swapped = pltpu.roll(slab, 64, 1)
            h0, h1 = 2 * m, 2 * m + 1
            ext[2 * h0, BLOCK:BLOCK + tm, :] = jnp.where(low, slab, 0.0).astype(BF16)
            ext[2 * h0 + 1, BLOCK:BLOCK + tm, :] = jnp.where(low, 0.0, swapped).astype(BF16)
            ext[2 * h1, BLOCK:BLOCK + tm, :] = jnp.where(low, swapped, 0.0).astype(BF16)
            ext[2 * h1 + 1, BLOCK:BLOCK + tm, :] = jnp.where(low, 0.0, slab).astype(BF16)

    qi = lax.broadcasted_iota(jnp.int32, (2 * BLOCK, 4 * BLOCK), 0) & (BLOCK - 1)
    kj = lax.broadcasted_iota(jnp.int32, (2 * BLOCK, 4 * BLOCK), 1) & (2 * BLOCK - 1)
    dist = qi + BLOCK - kj
    band = (dist >= 0) & (dist < BLOCK)
    band_first = band & ((kj >= BLOCK) | (i > 0))
    top = lax.broadcasted_iota(jnp.int32, (2 * BLOCK, 1), 0) < BLOCK
    low2 = lax.broadcasted_iota(jnp.int32, (2 * BLOCK, LANES), 1) < 64

    def window(ext, h, r0):
        return jnp.concatenate([ext[2 * h, r0:r0 + 2 * BLOCK, :], ext[2 * h + 1, r0:r0 + 2 * BLOCK, :]],
                               axis=0)

    for n in range(SWA_NBLK):
        mask = band_first if n == 0 else band
        r0 = n * BLOCK
        scores = []
        for h in range(A_KV_HEADS):
            qs = jnp.concatenate([q_scr[r0:r0 + BLOCK, (2 * h) * LANES:(2 * h + 1) * LANES],
                                  q_scr[r0:r0 + BLOCK, (2 * h + 1) * LANES:(2 * h + 2) * LANES]], axis=0)
            scores.append(_dot_nt(qs, window(kext_scr, h, r0)))
        probs, inv_denoms = [], []
        for h in range(A_KV_HEADS):
            s = jnp.where(mask, scores[h], NEG)
            ps, rs = [], []
            for half in range(2):
                sh = s[:, half * 2 * BLOCK:(half + 1) * 2 * BLOCK]
                sink = jnp.where(top, sinks_ref[4 * h + half] * LOG2E, sinks_ref[4 * h + 2 + half] * LOG2E)
                mx = jnp.maximum(jnp.max(sh, axis=-1, keepdims=True), sink)
                p = jnp.exp2(sh - mx)
                rs.append(1.0 / (jnp.sum(p, axis=-1, keepdims=True) + jnp.exp2(sink - mx)))
                ps.append(p.astype(BF16))
            probs.append(jnp.concatenate(ps, axis=1))
            inv_denoms.append(jnp.where(low2, rs[0], rs[1]))
        for h in range(A_KV_HEADS):
            o = _dot(probs[h], window(vext_scr, h, r0)) * inv_denoms[h]
            o_scr[r0:r0 + BLOCK, (2 * h) * LANES:(2 * h + 1) * LANES] = o[:BLOCK].astype(BF16)
            o_scr[r0:r0 + BLOCK, (2 * h + 1) * LANES:(2 * h + 2) * LANES] = o[BLOCK:].astype(BF16)

    for r0 in range(0, tm, tm // 2):
        rows = slice(r0, r0 + tm // 2)
        y = _dot(o_scr[rows, :], wo_ref[...])
        o_ref[0, rows, :] = _layer_norm(ALPHA * x[rows] + y, g_ref[...], b_ref[...])


def _swa_layer(x, sinks, cos, sin, wqkv, wo, g, b):
    bsz, s, d = x.shape
    tm = SWA_TM
    nqkv = wqkv.shape[1]
    return pl.pallas_call(
        _swa_kernel,
        grid=(bsz, s // tm),
        in_specs=[pl.BlockSpec(memory_space=pltpu.SMEM),
                  pl.BlockSpec((1, tm, d), lambda bi, i: (bi, i, 0)),
                  pl.BlockSpec((tm, LANES), lambda bi, i: (i, 0)),
                  pl.BlockSpec((tm, LANES), lambda bi, i: (i, 0)),
                  _resident((d, nqkv)), _resident((d, d)), _resident((1, d)), _resident((1, d))],
        out_specs=pl.BlockSpec((1, tm, d), lambda bi, i: (bi, i, 0)),
        out_shape=jax.ShapeDtypeStruct((bsz, s, d), F32),
        scratch_shapes=[pltpu.VMEM((2 * A_KV_HEADS, tm + BLOCK, LANES), BF16),
                        pltpu.VMEM((2 * A_KV_HEADS, tm + BLOCK, LANES), BF16),
                        pltpu.VMEM((tm, d), BF16),
                        pltpu.VMEM((tm, d), BF16)],
        compiler_params=_params(("arbitrary", "arbitrary")),
        name="swa_layer",
    )(sinks, x, cos, sin, wqkv, wo, g, b)


LRU_SUB = 256
LRU_NSUB = 2
LRU_NLT = LRU_WIDTH // LANES
LRU_PAD = 2 * SUBLANES


def _rglru_kernel(x_ref, xn_ref, win_ref, cw_ref, cb_ref, wr_ref, br_ref, wi_ref, bi_ref, lam_ref, wo_ref,
                  g_ref, b_ref, o_ref, ustage_scr, a_scr, b_scr, hcar_scr, gu_scr):
    i = pl.program_id(1)
    ts = LRU_SUB
    w = LRU_WIDTH
    pad = LRU_PAD
    sl = SUBLANES

    @pl.when(i == 0)
    def _():
        ustage_scr[:, :, 0:sl, :] = jnp.zeros((LRU_NSUB, LRU_NLT, sl, LANES), F32)
        a_scr[:, :, :, 0:pad, :] = jnp.ones((LRU_NSUB, 3, LRU_NLT, pad, LANES), F32)
        b_scr[:, :, :, 0:pad, :] = jnp.zeros((LRU_NSUB, 3, LRU_NLT, pad, LANES), F32)
        hcar_scr[...] = jnp.zeros_like(hcar_scr)

    @pl.when(i > 0)
    def _():
        ustage_scr[:, :, 0:sl, :] = ustage_scr[:, :, ts:ts + sl, :]
        a_scr[:, 0, :, sl:pad, :] = a_scr[:, 0, :, ts + sl:ts + pad, :]
        b_scr[:, 0, :, sl:pad, :] = b_scr[:, 0, :, ts + sl:ts + pad, :]

    cw = cw_ref[...]
    cb = cb_ref[...]
    z = -lam_ref[...]
    softplus = jnp.maximum(z, 0.0) + jnp.log1p(jnp.exp(-jnp.abs(z)))
    c0 = math.sqrt(2.0 / math.pi)

    @pl.when(i == 0)
    def _():
        for sub in range(LRU_NSUB):
            gu_scr[0, sub] = _dot(x_ref[sub].astype(BF16), win_ref[...])

    def project_next(slot, sub):
        gu_scr[1 - slot, sub] = _dot(xn_ref[sub].astype(BF16), win_ref[...])

    def conv_gates(slot, sub):
        u = gu_scr[slot, sub, :, w:2 * w]
        ucs = []
        for lt in range(LRU_NLT):
            ls = slice(lt * LANES, (lt + 1) * LANES)
            us = ustage_scr.at[sub, lt]
            us[sl:sl + ts, :] = u[:, ls]
            uc = us[sl - 3:sl - 3 + ts, :] * cw[0:1, ls]
            uc = uc + us[sl - 2:sl - 2 + ts, :] * cw[1:2, ls]
            uc = uc + us[sl - 1:sl - 1 + ts, :] * cw[2:3, ls]
            uc = uc + u[:, ls] * cw[3:4, ls]
            ucs.append(uc + cb[:, ls])
        uc = jnp.concatenate(ucs, axis=1)

        ucb = uc.astype(BF16)
        rs, igs = [], []
        for h in range(LRU_BLOCKS):
            blk = ucb[:, h * LRU_BLOCK_W:(h + 1) * LRU_BLOCK_W]
            rs.append(_dot(blk, wr_ref[h]))
            igs.append(_dot(blk, wi_ref[h]))
        return uc, jnp.concatenate(rs, axis=-1), jnp.concatenate(igs, axis=-1)

    def recurrence(slot, sub, uc, r_pre, ig_pre):
        hblk = hcar_scr[sub]
        gate = gu_scr[slot, sub, :, 0:w]
        r = _sigmoid(r_pre + br_ref[...])
        ig = _sigmoid(ig_pre + bi_ref[...])
        log_a = (-LRU_C * r) * softplus
        a0 = jnp.exp(log_a)
        th = jnp.tanh(log_a)
        b0 = (jnp.sqrt(-2.0 * th) * lax.rsqrt(1.0 - th)) * (ig * uc)

        a8, b8 = [], []
        for lt in range(LRU_NLT):
            ls = slice(lt * LANES, (lt + 1) * LANES)
            a_scr[sub, 0, lt, pad:pad + ts, :] = a0[:, ls]
            b_scr[sub, 0, lt, pad:pad + ts, :] = b0[:, ls]
            for k, d in enumerate((1, 2)):
                a_cur = a_scr[sub, k, lt, sl:pad + ts, :]
                b_cur = b_scr[sub, k, lt, sl:pad + ts, :]
                a_scr[sub, k + 1, lt, sl:pad + ts, :] = a_cur * a_scr[sub, k, lt, sl - d:pad + ts - d, :]
                b_scr[sub, k + 1, lt, sl:pad + ts, :] = (a_cur * b_scr[sub, k, lt, sl - d:pad + ts - d, :]
                                                        + b_cur)
            a_cur = a_scr[sub, 2, lt, pad:pad + ts, :]
            a8.append(a_cur * a_scr[sub, 2, lt, pad - 4:pad - 4 + ts, :])
            b8.append(a_cur * b_scr[sub, 2, lt, pad - 4:pad - 4 + ts, :] + b_scr[sub, 2, lt, pad:pad + ts, :])
        a8 = jnp.concatenate(a8, axis=1)
        b8 = jnp.concatenate(b8, axis=1)

        hs = []
        for j in range(ts // sl):
            hblk = a8[j * sl:(j + 1) * sl] * hblk + b8[j * sl:(j + 1) * sl]
            hs.append(hblk)
        h = jnp.concatenate(hs, axis=0)

        gelu = gate * (0.5 * (1.0 + jnp.tanh(c0 * (gate + 0.044715 * (gate * gate * gate)))))
        hcar_scr[sub] = hblk
        return (h * gelu).astype(BF16)

    def out_proj(sub, yb):
        return ALPHA * x_ref[sub] + _dot(yb, wo_ref[...])

    def step(slot):
        zs = []
        for sub in range(LRU_NSUB):
            uc, r_pre, ig_pre = conv_gates(slot, sub)
            project_next(slot, sub)
            zs.append(out_proj(sub, recurrence(slot, sub, uc, r_pre, ig_pre)))
            if sub > 0:
                o_ref[sub - 1] = _layer_norm(zs[sub - 1], g_ref[...], b_ref[...])
        o_ref[LRU_NSUB - 1] = _layer_norm(zs[-1], g_ref[...], b_ref[...])

    for parity in range(2):
        pl.when(lax.rem(i, 2) == parity)(functools.partial(step, parity))


def _rglru_layer(x, win, cw, cb, wr, br, wi, bi, lam, wo, g, b):
    bsz, s, d = x.shape
    tm = LRU_SUB
    w = LRU_WIDTH
    last = s // tm - 1
    return pl.pallas_call(
        _rglru_kernel,
        grid=(bsz // LRU_NSUB, s // tm),
        in_specs=[pl.BlockSpec((LRU_NSUB, tm, d), lambda bi_, i: (bi_, i, 0)),
                  pl.BlockSpec((LRU_NSUB, tm, d), lambda bi_, i: (bi_, jnp.minimum(i + 1, last), 0)),
                  _resident((d, 2 * w)), _resident((LRU_CONV, w)), _resident((1, w)),
                  _resident((LRU_BLOCKS, LRU_BLOCK_W, LRU_BLOCK_W)), _resident((1, w)),
                  _resident((LRU_BLOCKS, LRU_BLOCK_W, LRU_BLOCK_W)), _resident((1, w)),
                  _resident((1, w)), _resident((w, d)), _resident((1, d)), _resident((1, d))],
        out_specs=pl.BlockSpec((LRU_NSUB, tm, d), lambda bi_, i: (bi_, i, 0)),
        out_shape=jax.ShapeDtypeStruct((bsz, s, d), F32),
        scratch_shapes=[pltpu.VMEM((LRU_NSUB, LRU_NLT, SUBLANES + LRU_SUB, LANES), F32),
                        pltpu.VMEM((LRU_NSUB, 3, LRU_NLT, LRU_PAD + LRU_SUB, LANES), F32),
                        pltpu.VMEM((LRU_NSUB, 3, LRU_NLT, LRU_PAD + LRU_SUB, LANES), F32),
                        pltpu.VMEM((LRU_NSUB, SUBLANES, w), F32),
                        pltpu.VMEM((2, LRU_NSUB, tm, 2 * w), F32)],
        compiler_params=_params(("arbitrary", "arbitrary")),
        name="rglru_layer",
    )(x, x, win, cw, cb, wr, br, wi, bi, lam, wo, g, b)


MLA_TM = 1024
MLA_TQ = 1024
MLA_HALF = MLA_TQ // 2
MLA_TK = 512
C_DOWN_PAD = 768
MLA_Q_SCALE = (C_NOPE + C_ROPE) ** -0.5 * math.log2(math.e)


def _mla_proj_kernel(x_ref, cos_ref, sin_ref, wd_ref, qn_ref, kvn_ref, wuq_ref, wukv_ref,
                     q_ref, k_ref, v_ref):
    tm = MLA_TM
    xb = x_ref[0].astype(BF16)
    c = _dot(xb, wd_ref[...])
    cq = c[:, :C_Q_RANK]
    ckv = c[:, C_Q_RANK:C_Q_RANK + C_KV_RANK]
    kr_slab = c[:, C_Q_RANK + C_KV_RANK:]
    cq = cq * lax.rsqrt(jnp.mean(cq * cq, axis=-1, keepdims=True) + RMS_EPS) * qn_ref[...]
    ckv = ckv * lax.rsqrt(jnp.mean(ckv * ckv, axis=-1, keepdims=True) + RMS_EPS) * kvn_ref[...]
    q = _dot(cq.astype(BF16), wuq_ref[...]) * MLA_Q_SCALE
    kv = _dot(ckv.astype(BF16), wukv_ref[...])

    cos = cos_ref[...]
    sin = sin_ref[...]
    lane = lax.broadcasted_iota(jnp.int32, (tm, LANES), 1)
    first_half = (lane & 32) == 0
    low = lane < 64
    nn = C_HEADS * C_NOPE

    kr = _rope_slab(kr_slab, cos, sin, first_half)
    kr_pair = (kr.astype(BF16), pltpu.roll(kr, 64, 1).astype(BF16))
    for pair in range(C_HEADS // 2):
        qr = _rope_slab(q[:, nn + pair * LANES:nn + (pair + 1) * LANES], cos, sin, first_half)
        for half in range(2):
            h = 2 * pair + half
            qr_h = jnp.where(low, qr, 0.0) if half == 0 else jnp.where(low, 0.0, qr)
            q_ref[0, h, :, 0:C_NOPE] = q[:, h * C_NOPE:(h + 1) * C_NOPE].astype(BF16)
            q_ref[0, h, :, C_NOPE:C_QK_PAD] = qr_h.astype(BF16)
            k_ref[0, h, :, 0:C_NOPE] = kv[:, h * C_NOPE:(h + 1) * C_NOPE].astype(BF16)
            k_ref[0, h, :, C_NOPE:C_QK_PAD] = kr_pair[half]
            v_ref[0, h] = kv[:, nn + h * C_V:nn + (h + 1) * C_V].astype(BF16)


def _mla_proj(x, cos, sin, wd, qn, kvn, wuq, wukv):
    bsz, s, d = x.shape
    tm = MLA_TM
    qk_shape = jax.ShapeDtypeStruct((bsz, C_HEADS, s, C_QK_PAD), BF16)
    v_shape = jax.ShapeDtypeStruct((bsz, C_HEADS, s, C_V), BF16)
    return pl.pallas_call(
        _mla_proj_kernel,
        grid=(bsz, s // tm),
        in_specs=[pl.BlockSpec((1, tm, d), lambda bi, i: (bi, i, 0)),
                  pl.BlockSpec((tm, LANES), lambda bi, i: (i, 0)),
                  pl.BlockSpec((tm, LANES), lambda bi, i: (i, 0)),
                  _resident(wd.shape), _resident(qn.shape), _resident(kvn.shape),
                  _resident(wuq.shape), _resident(wukv.shape)],
        out_specs=[pl.BlockSpec((1, C_HEADS, tm, C_QK_PAD), lambda bi, i: (bi, 0, i, 0)),
                   pl.BlockSpec((1, C_HEADS, tm, C_QK_PAD), lambda bi, i: (bi, 0, i, 0)),
                   pl.BlockSpec((1, C_HEADS, tm, C_V), lambda bi, i: (bi, 0, i, 0))],
        out_shape=[qk_shape, qk_shape, v_shape],
        compiler_params=_params(("parallel", "parallel")),
        name="mla_proj",
    )(x, cos, sin, wd, qn, kvn, wuq, wukv)


def _mla_attn_kernel(q_ref, qn_ref, k_ref, v_ref, o_ref, m_scr, acc_scr, s_scr):
    i = pl.program_id(2)
    tk, hq = MLA_TK, MLA_HALF

    m_scr[...] = jnp.full_like(m_scr, -jnp.inf)
    acc_scr[...] = jnp.zeros_like(acc_scr)
    ones = jnp.ones((tk, LANES), BF16)

    def scores(slot, hf, j, queries=q_ref):
        k0 = pl.multiple_of(j * tk, tk)
        s_scr[slot, hf] = _dot_nt(queries[0, 0, hf * hq:(hf + 1) * hq, :], k_ref[0, 0, pl.ds(k0, tk), :])

    def consume(slot, hf, j, diagonal):
        k0 = pl.multiple_of(j * tk, tk)
        vv = jnp.concatenate([v_ref[0, 0, pl.ds(k0, tk), :], ones], axis=1)
        s = s_scr[slot, hf]
        if diagonal:
            qpos = lax.broadcasted_iota(jnp.int32, (hq, tk), 0)
            kpos = lax.broadcasted_iota(jnp.int32, (hq, tk), 1)
            s = jnp.where(kpos <= qpos, s, NEG)
        m_prev = m_scr[hf]
        m_next = jnp.maximum(m_prev, jnp.max(s, axis=-1, keepdims=True))
        p = jnp.exp2(s - jnp.tile(m_next, (1, tk // LANES)))
        corr = jnp.exp2(m_prev - m_next)
        acc_scr[hf] = jnp.tile(corr, (1, 2)) * acc_scr[hf] + _dot(p.astype(BF16), vv)
        m_scr[hf] = m_next

    @pl.when(i == 0)
    def _():
        scores(0, 0, 0)
        scores(0, 1, 0)

    def chunk_pair(j):
        scores(1, 0, j + 1)
        scores(1, 1, j + 1)
        consume(0, 0, j, False)
        consume(0, 1, j, False)
        scores(0, 0, j + 2)
        scores(0, 1, j + 2)
        consume(1, 0, j + 1, False)
        consume(1, 1, j + 1, False)

    def body(t, carry):
        chunk_pair(4 * t)
        chunk_pair(4 * t + 2)
        return carry

    lax.fori_loop(0, i // 2, body, 0)

    @pl.when(i % 2 == 1)
    def _():
        chunk_pair(2 * i - 2)

    scores(1, 1, 2 * i + 1)
    consume(0, 0, 2 * i, True)
    consume(0, 1, 2 * i, False)
    scores(0, 0, 0, queries=qn_ref)
    scores(0, 1, 0, queries=qn_ref)
    consume(1, 1, 2 * i + 1, True)
    for hf in range(2):
        acc = acc_scr[hf]
        o_ref[0, hf * hq:(hf + 1) * hq, :] = (acc[:, :C_V] / acc[:, C_V:]).astype(BF16)


def _mla_attn(q, k, v):
    bsz, nh, s, _ = q.shape
    tq = MLA_TQ
    last = s // tq - 1
    return pl.pallas_call(
        _mla_attn_kernel,
        grid=(bsz, nh, s // tq),
        in_specs=[pl.BlockSpec((1, 1, tq, C_QK_PAD), lambda b, h, i: (b, h, i, 0)),
                  pl.BlockSpec((1, 1, tq, C_QK_PAD), lambda b, h, i: (b, h, jnp.minimum(i + 1, last), 0)),
                  pl.BlockSpec((1, 1, s, C_QK_PAD), lambda b, h, i: (b, h, 0, 0)),
                  pl.BlockSpec((1, 1, s, C_V), lambda b, h, i: (b, h, 0, 0))],
        out_specs=pl.BlockSpec((1, tq, C_V), lambda b, h, i: (b, i, h)),
        out_shape=jax.ShapeDtypeStruct((bsz, s, nh * C_V), BF16),
        scratch_shapes=[pltpu.VMEM((2, MLA_HALF, LANES), F32),
                        pltpu.VMEM((2, MLA_HALF, 2 * C_V), F32),
                        pltpu.VMEM((2, 2, MLA_HALF, MLA_TK), F32)],
        compiler_params=_params(("parallel", "parallel", "arbitrary")),
        name="mla_attn",
    )(q, q, k, v)


X_TM = 1024
X_LN_ROWS = 256


def _xattn_kernel(*refs, mixer_prologue):
    tm = X_TM
    if mixer_prologue:
        a_ref, wa_ref, ga_ref, ba_ref, x_ref, wq_ref, mkv_ref, wo_ref, g_ref, b_ref, o_ref = refs
        x = _layer_norm(ALPHA * x_ref[0] + _dot(a_ref[0], wa_ref[...]), ga_ref[...], ba_ref[...])
    else:
        x_ref, wq_ref, mkv_ref, wo_ref, g_ref, b_ref, o_ref = refs
        x = x_ref[0]
    q = (_dot(x.astype(BF16), wq_ref[...]) * (X_HEAD_DIM ** -0.5 * LOG2E)).astype(BF16)
    heads = [(h * X_HEAD_DIM, (h + 1) * X_HEAD_DIM) for h in range(X_HEADS)]
    scores = [_dot_nt(q[:, lo:hi], mkv_ref[0, :, lo:hi]) for lo, hi in heads]
    probs, inv_denoms = [], []
    for s in scores:
        p = jnp.exp2(s - jnp.max(s, axis=-1, keepdims=True))
        inv_denoms.append(1.0 / jnp.sum(p, axis=-1, keepdims=True))
        probs.append(p.astype(BF16))
    outs = [(_dot(probs[h], mkv_ref[0, :, D_MODEL + lo:D_MODEL + hi]) * inv_denoms[h]).astype(BF16)
            for h, (lo, hi) in enumerate(heads)]
    o = jnp.concatenate(outs, axis=-1)
    for r0 in range(0, tm, X_LN_ROWS):
        rows = slice(r0, r0 + X_LN_ROWS)
        y = _dot(o[rows], wo_ref[...])
        o_ref[0, rows, :] = _layer_norm(ALPHA * x[rows] + y, g_ref[...], b_ref[...])


def _xattn_layer(x, wq, mkv, wo, g, b, mixer=None):
    bsz, s, d = x.shape
    tm = X_TM
    tile = pl.BlockSpec((1, tm, d), lambda bi, i: (bi, i, 0))
    in_specs = [tile, _resident((d, d)),
                pl.BlockSpec((1, MEM_LEN, 2 * d), lambda bi, i: (bi, 0, 0)),
                _resident((d, d)), _resident((1, d)), _resident((1, d))]
    args = (x, wq, mkv, wo, g, b)
    if mixer is not None:
        in_specs = [tile, _resident((d, d)), _resident((1, d)), _resident((1, d))] + in_specs
        args = tuple(mixer) + args
    return pl.pallas_call(
        functools.partial(_xattn_kernel, mixer_prologue=mixer is not None),
        grid=(bsz, s // tm),
        in_specs=in_specs,
        out_specs=tile,
        out_shape=jax.ShapeDtypeStruct((bsz, s, d), F32),
        compiler_params=_params(("parallel", "parallel")),
        name="xattn_layer",
    )(*args)


FFN_TM = 512
FFN_LN_ROWS = 256
FFN_DOWN_GROUPS = (6, FFN_NCHUNK)


def _ffn_kernel(x_ref, wup_ref, cw_ref, cb_ref, wdn_ref, g_ref, b_ref, o_ref, hprev_scr, act_scr,
                acc_scr, hstage_scr):
    i = pl.program_id(1)
    tm = FFN_TM

    @pl.when(i == 0)
    def _():
        hprev_scr[...] = jnp.zeros_like(hprev_scr)

    x = x_ref[0]
    xb = x.astype(BF16)

    def up_conv(col0, slot):
        cols = slice(col0, col0 + FFN_CHUNK)
        h = _dot(xb, wup_ref[:, cols])
        prev = hprev_scr[:, cols]
        hprev_scr[:, cols] = h[tm - SUBLANES:, :]
        cw = cw_ref[:, cols]
        cb = cb_ref[:, cols]
        outs = []
        for lt in range(FFN_CHUNK // LANES):
            ls = slice(lt * LANES, (lt + 1) * LANES)
            hs = hstage_scr.at[slot, lt]
            hs[0:SUBLANES, :] = prev[:, ls]
            hs[SUBLANES:SUBLANES + tm, :] = h[:, ls]
            y = hs[SUBLANES - 2:SUBLANES - 2 + tm, :] * cw[0:1, ls]
            y = y + hs[SUBLANES - 1:SUBLANES - 1 + tm, :] * cw[1:2, ls]
            y = y + h[:, ls] * cw[2:3, ls]
            outs.append(y + cb[:, ls])
        return jnp.concatenate(outs, axis=1)

    def down(rows, c0, c1):
        ks = slice(c0 * FFN_CHUNK, c1 * FFN_CHUNK)
        return _dot(act_scr[rows, ks], wdn_ref[ks, :])

    everything = slice(0, tm)
    for c in range(FFN_NCHUNK):
        gt = up_conv(c * FFN_CHUNK, 2 * (c % 2))
        val = up_conv(D_FF + c * FFN_CHUNK, 2 * (c % 2) + 1)
        act_scr[:, c * FFN_CHUNK:(c + 1) * FFN_CHUNK] = ((gt * _sigmoid(gt)) * val).astype(BF16)
        if c + 1 in FFN_DOWN_GROUPS[:-1]:
            gi = FFN_DOWN_GROUPS.index(c + 1)
            if gi == 0:
                acc_scr[...] = down(everything, 0, c + 1)
            else:
                acc_scr[...] += down(everything, FFN_DOWN_GROUPS[gi - 1], c + 1)
    for r0 in range(0, tm, FFN_LN_ROWS):
        rows = slice(r0, r0 + FFN_LN_ROWS)
        out = acc_scr[rows, :] + down(rows, FFN_DOWN_GROUPS[-2], FFN_DOWN_GROUPS[-1])
        o_ref[0, rows, :] = _layer_norm(ALPHA * x[rows] + out, g_ref[...], b_ref[...])


def _ffn_layer(x, wup, cw, cb, wdn, g, b):
    bsz, s, d = x.shape
    tm = FFN_TM
    return pl.pallas_call(
        _ffn_kernel,
        grid=(bsz, s // tm),
        in_specs=[pl.BlockSpec((1, tm, d), lambda bi, i: (bi, i, 0)),
                  _resident((d, 2 * D_FF)), _resident((FFN_CONV, 2 * D_FF)), _resident((1, 2 * D_FF)),
                  _resident((D_FF, d)), _resident((1, d)), _resident((1, d))],
        out_specs=pl.BlockSpec((1, tm, d), lambda bi, i: (bi, i, 0)),
        out_shape=jax.ShapeDtypeStruct((bsz, s, d), F32),
        scratch_shapes=[pltpu.VMEM((SUBLANES, 2 * D_FF), F32), pltpu.VMEM((tm, D_FF), BF16),
                        pltpu.VMEM((tm, d), F32),
                        pltpu.VMEM((4, FFN_CHUNK // LANES, SUBLANES + tm, LANES), F32)],
        compiler_params=_params(("arbitrary", "arbitrary")),
        name="ffn_layer",
    )(x, wup, cw, cb, wdn, g, b)


def kernel(x, mem, a_w_qkv, a_sinks, a_w_o, b_w_in, b_conv_w, b_conv_b, b_w_rgate, b_b_rgate, b_w_igate, b_b_igate, b_lambda, b_w_o, c_w_down, c_q_norm, c_kv_norm, c_w_uq, c_w_ukv, c_w_o, mem_w_kv, x_w_q, x_w_o, f_w_up, f_conv_w, f_conv_b, f_w_down, ln_g, ln_b):
    bsz, s, d = x.shape
    cos, sin = _rope_tables(s)

    mkv = _matmul(mem.reshape(bsz * MEM_LEN, d), mem_w_kv.astype(BF16), 512, BF16)
    mkv = mkv.reshape(bsz, MEM_LEN, 2 * d)

    def row(v):
        return v.reshape(1, -1)

    for i in range(DEPTH):
        kind, j = i % N_MIXERS, i // N_MIXERS
        g0, b0 = row(ln_g[i, 0]), row(ln_b[i, 0])
        mixer = None
        if kind == 0:
            x = _swa_layer(x, a_sinks[j], cos, sin, a_w_qkv[j].astype(BF16), a_w_o[j].astype(BF16), g0, b0)
        elif kind == 1:
            x = _rglru_layer(x, b_w_in[j].astype(BF16), b_conv_w[j], row(b_conv_b[j]),
                             b_w_rgate[j].astype(BF16), row(b_b_rgate[j]),
                             b_w_igate[j].astype(BF16), row(b_b_igate[j]),
                             row(b_lambda[j]), b_w_o[j].astype(BF16), g0, b0)
        else:
            wd = jnp.pad(c_w_down[j], ((0, 0), (0, C_DOWN_PAD - c_w_down.shape[-1]))).astype(BF16)
            wuq = c_w_uq[j].reshape(C_Q_RANK, C_HEADS, C_NOPE + C_ROPE)
            wuq = jnp.concatenate([wuq[:, :, :C_NOPE].reshape(C_Q_RANK, -1),
                                   wuq[:, :, C_NOPE:].reshape(C_Q_RANK, -1)], axis=-1).astype(BF16)
            wukv = c_w_ukv[j].reshape(C_KV_RANK, C_HEADS, C_NOPE + C_V)
            wukv = jnp.concatenate([wukv[:, :, :C_NOPE].reshape(C_KV_RANK, -1),
                                    wukv[:, :, C_NOPE:].reshape(C_KV_RANK, -1)], axis=-1).astype(BF16)
            q, k, v = _mla_proj(x, cos, sin, wd, row(c_q_norm[j]), row(c_kv_norm[j]), wuq, wukv)
            mixer = (_mla_attn(q, k, v), c_w_o[j].astype(BF16), g0, b0)
        x = _xattn_layer(x, x_w_q[i].astype(BF16), mkv, x_w_o[i].astype(BF16),
                         row(ln_g[i, 1]), row(ln_b[i, 1]), mixer=mixer)
        x = _ffn_layer(x, f_w_up[i].astype(BF16), f_conv_w[i], row(f_conv_b[i]), f_w_down[i].astype(BF16),
                       row(ln_g[i, 2]), row(ln_b[i, 2]))
    return x
```

```python
import functools
import math

import jax
import jax.numpy as jnp
import numpy as np
from jax import lax
from jax.experimental import pallas as pl
from jax.experimental.pallas import tpu as pltpu

D_MODEL = 1024
DEPTH = 4
N_MIXERS = 3
MEM_LEN = 256
BLOCK = 128
ROPE_THETA = 10000.0
NEG = -1e30
LN_EPS = 1e-5
RMS_EPS = 1e-6

A_HEADS = 16
A_KV_HEADS = 4
A_HEAD_DIM = 64

LRU_WIDTH = D_MODEL
LRU_BLOCKS = 4
LRU_BLOCK_W = LRU_WIDTH // LRU_BLOCKS
LRU_CONV = 4
LRU_C = 8.0

C_HEADS = 8
C_NOPE = 128
C_ROPE = 64
C_V = 128
C_Q_RANK = 384
C_KV_RANK = 256
C_QK_PAD = 256

X_HEADS = 4
X_HEAD_DIM = D_MODEL // X_HEADS

D_FF = 2816
FFN_CONV = 3
FFN_CHUNK = 256
FFN_NCHUNK = D_FF // FFN_CHUNK

ALPHA = (2.0 * DEPTH) ** 0.25

LANES = 128
SUBLANES = 8
VMEM_LIMIT = 56 * 1024 * 1024

BF16 = jnp.bfloat16
F32 = jnp.float32

NT_DIMS = (((1,), (1,)), ((), ()))
LOG2E = math.log2(math.e)


def _dot(a, b):
    return jnp.dot(a, b, preferred_element_type=F32)


def _dot_nt(a, b):
    return lax.dot_general(a, b, NT_DIMS, preferred_element_type=F32)


def _layer_norm(z, g, b):
    mu = jnp.mean(z, axis=-1, keepdims=True)
    zc = z - mu
    var = jnp.mean(zc * zc, axis=-1, keepdims=True)
    return zc * lax.rsqrt(var + LN_EPS) * g + b


def _sigmoid(z):
    return 0.5 * jnp.tanh(0.5 * z) + 0.5


def _resident(shape, layer=None):
    nd = len(shape)
    if layer is None:
        return pl.BlockSpec(shape, lambda *_: (0,) * nd, pipeline_mode=pl.Buffered(1))
    return pl.BlockSpec((None,) + tuple(shape), lambda *_: (layer,) + (0,) * nd, pipeline_mode=pl.Buffered(1))


def _params(sem):
    return pltpu.CompilerParams(dimension_semantics=sem, vmem_limit_bytes=VMEM_LIMIT)


def _matmul_kernel(a_ref, w_ref, o_ref):
    o_ref[...] = _dot(a_ref[...].astype(BF16), w_ref[...]).astype(o_ref.dtype)


def _matmul(a, w, tm, out_dtype):
    t, k = a.shape
    n = w.shape[1]
    return pl.pallas_call(
        _matmul_kernel,
        grid=(t // tm,),
        in_specs=[pl.BlockSpec((tm, k), lambda i: (i, 0)), _resident((k, n))],
        out_specs=pl.BlockSpec((tm, n), lambda i: (i, 0)),
        out_shape=jax.ShapeDtypeStruct((t, n), out_dtype),
        compiler_params=_params(("parallel",)),
        name="matmul",
    )(a, w)


def _rope_slab(slab, cos, sin_signed, first_half):
    rot = jnp.where(first_half, pltpu.roll(slab, LANES - 32, 1), pltpu.roll(slab, 32, 1))
    return slab * cos + rot * sin_signed


def _rope_tables(seq):
    inv = 1.0 / (ROPE_THETA ** (jnp.arange(0, A_HEAD_DIM, 2, dtype=F32) / A_HEAD_DIM))
    ang = jnp.arange(seq, dtype=F32)[:, None] * inv[None, :]
    cos, sin = jnp.cos(ang), jnp.sin(ang)
    return (jnp.concatenate([cos, cos, cos, cos], axis=-1),
            jnp.concatenate([-sin, sin, -sin, sin], axis=-1))


SWA_TM = 512
SWA_NBLK = SWA_TM // BLOCK


def _swa_kernel(sinks_ref, x_ref, cos_ref, sin_ref, wqkv_ref, wo_ref, g_ref, b_ref, o_ref,
                kext_scr, vext_scr, q_scr, o_scr):
    i = pl.program_id(1)
    tm = SWA_TM

    @pl.when(i == 0)
    def _():
        kext_scr[:, 0:BLOCK, :] = jnp.zeros((2 * A_KV_HEADS, BLOCK, LANES), BF16)
        vext_scr[:, 0:BLOCK, :] = jnp.zeros((2 * A_KV_HEADS, BLOCK, LANES), BF16)

    @pl.when(i > 0)
    def _():
        kext_scr[:, 0:BLOCK, :] = kext_scr[:, tm:tm + BLOCK, :]
        vext_scr[:, 0:BLOCK, :] = vext_scr[:, tm:tm + BLOCK, :]

    x = x_ref[0]
    qkv = _dot(x.astype(BF16), wqkv_ref[...])
    cos = cos_ref[...]
    sin = sin_ref[...]
    lane = lax.broadcasted_iota(jnp.int32, (tm, LANES), 1)
    first_half = (lane & 32) == 0
    low = lane < 64

    nq = A_HEADS * A_HEAD_DIM
    nkv = A_KV_HEADS * A_HEAD_DIM
    for j in range(nq // LANES):
        slab = _rope_slab(qkv[:, j * LANES:(j + 1) * LANES], cos, sin, first_half)
        q_scr[:, j * LANES:(j + 1) * LANES] = (slab * (A_HEAD_DIM ** -0.5 * LOG2E)).astype(BF16)

    for m in range(nkv // LANES):
        kslab = _rope_slab(qkv[:, nq + m * LANES:nq + (m + 1) * LANES], cos, sin, first_half)
        vslab = qkv[:, nq + nkv + m * LANES:nq + nkv + (m + 1) * LANES]
        for slab, ext in ((kslab, kext_scr), (vslab, vext_scr)):
            swapped = pltpu.roll(slab, 64, 1)
            h0, h1 = 2 * m, 2 * m + 1
            ext[2 * h0, BLOCK:BLOCK + tm, :] = jnp.where(low, slab, 0.0).astype(BF16)
            ext[2 * h0 + 1, BLOCK:BLOCK + tm, :] = jnp.where(low, 0.0, swapped).astype(BF16)
            ext[2 * h1, BLOCK:BLOCK + tm, :] = jnp.where(low, swapped, 0.0).astype(BF16)
            ext[2 * h1 + 1, BLOCK:BLOCK + tm, :] = jnp.where(low, 0.0, slab).astype(BF16)

    qi = lax.broadcasted_iota(jnp.int32, (2 * BLOCK, 4 * BLOCK), 0) & (BLOCK - 1)
    kj = lax.broadcasted_iota(jnp.int32, (2 * BLOCK, 4 * BLOCK), 1) & (2 * BLOCK - 1)
    dist = qi + BLOCK - kj
    band = (dist >= 0) & (dist < BLOCK)
    band_first = band & ((kj >= BLOCK) | (i > 0))
    top = lax.broadcasted_iota(jnp.int32, (2 * BLOCK, 1), 0) < BLOCK
    low2 = lax.broadcasted_iota(jnp.int32, (2 * BLOCK, LANES), 1) < 64

    def window(ext, h, r0):
        return jnp.concatenate([ext[2 * h, r0:r0 + 2 * BLOCK, :], ext[2 * h + 1, r0:r0 + 2 * BLOCK, :]],
                               axis=0)

    for n in range(SWA_NBLK):
        mask = band_first if n == 0 else band
        r0 = n * BLOCK
        scores = []
        for h in range(A_KV_HEADS):
            qs = jnp.concatenate([q_scr[r0:r0 + BLOCK, (2 * h) * LANES:(2 * h + 1) * LANES],
                                  q_scr[r0:r0 + BLOCK, (2 * h + 1) * LANES:(2 * h + 2) * LANES]], axis=0)
            scores.append(_dot_nt(qs, window(kext_scr, h, r0)))
        probs, inv_denoms = [], []
        for h in range(A_KV_HEADS):
            s = jnp.where(mask, scores[h], NEG)
            ps, rs = [], []
            for half in range(2):
                sh = s[:, half * 2 * BLOCK:(half + 1) * 2 * BLOCK]
                sink = jnp.where(top, sinks_ref[4 * h + half] * LOG2E, sinks_ref[4 * h + 2 + half] * LOG2E)
                mx = jnp.maximum(jnp.max(sh, axis=-1, keepdims=True), sink)
                p = jnp.exp2(sh - mx)
                rs.append(1.0 / (jnp.sum(p, axis=-1, keepdims=True) + jnp.exp2(sink - mx)))
                ps.append(p.astype(BF16))
            probs.append(jnp.concatenate(ps, axis=1))
            inv_denoms.append(jnp.where(low2, rs[0], rs[1]))
        for h in range(A_KV_HEADS):
            o = _dot(probs[h], window(vext_scr, h, r0)) * inv_denoms[h]
            o_scr[r0:r0 + BLOCK, (2 * h) * LANES:(2 * h + 1) * LANES] = o[:BLOCK].astype(BF16)
            o_scr[r0:r0 + BLOCK, (2 * h + 1) * LANES:(2 * h + 2) * LANES] = o[BLOCK:].astype(BF16)

    for r0 in range(0, tm, tm // 2):
        rows = slice(r0, r0 + tm // 2)
        y = _dot(o_scr[rows, :], wo_ref[...])
        o_ref[0, rows, :] = _layer_norm(ALPHA * x[rows] + y, g_ref[...], b_ref[...])


def _swa_layer(x, layer, sinks, cos, sin, wqkv, wo, g, b):
    bsz, s, d = x.shape
    tm = SWA_TM
    nqkv = wqkv.shape[-1]
    return pl.pallas_call(
        _swa_kernel,
        grid=(bsz, s // tm),
        in_specs=[pl.BlockSpec(memory_space=pltpu.SMEM),
                  pl.BlockSpec((1, tm, d), lambda bi, i: (bi, i, 0)),
                  pl.BlockSpec((tm, LANES), lambda bi, i: (i, 0)),
                  pl.BlockSpec((tm, LANES), lambda bi, i: (i, 0)),
                  _resident((d, nqkv), layer), _resident((d, d), layer), _resident((1, d)), _resident((1, d))],
        out_specs=pl.BlockSpec((1, tm, d), lambda bi, i: (bi, i, 0)),
        out_shape=jax.ShapeDtypeStruct((bsz, s, d), F32),
        scratch_shapes=[pltpu.VMEM((2 * A_KV_HEADS, tm + BLOCK, LANES), BF16),
                        pltpu.VMEM((2 * A_KV_HEADS, tm + BLOCK, LANES), BF16),
                        pltpu.VMEM((tm, d), BF16),
                        pltpu.VMEM((tm, d), BF16)],
        compiler_params=_params(("arbitrary", "arbitrary")),
        name="swa_layer",
    )(sinks, x, cos, sin, wqkv, wo, g, b)


LRU_SUB = 256
LRU_NSUB = 2
LRU_NLT = LRU_WIDTH // LANES
LRU_PAD = 2 * SUBLANES


def _rglru_kernel(x_ref, xn_ref, win_ref, cw_ref, cb_ref, wr_ref, br_ref, wi_ref, bi_ref, lam_ref, wo_ref,
                  g_ref, b_ref, o_ref, ustage_scr, a_scr, b_scr, hcar_scr, gu_scr):
    i = pl.program_id(1)
    ts = LRU_SUB
    w = LRU_WIDTH
    pad = LRU_PAD
    sl = SUBLANES

    @pl.when(i == 0)
    def _():
        ustage_scr[:, :, 0:sl, :] = jnp.zeros((LRU_NSUB, LRU_NLT, sl, LANES), F32)
        a_scr[:, :, :, 0:pad, :] = jnp.ones((LRU_NSUB, 3, LRU_NLT, pad, LANES), F32)
        b_scr[:, :, :, 0:pad, :] = jnp.zeros((LRU_NSUB, 3, LRU_NLT, pad, LANES), F32)
        hcar_scr[...] = jnp.zeros_like(hcar_scr)

    @pl.when(i > 0)
    def _():
        ustage_scr[:, :, 0:sl, :] = ustage_scr[:, :, ts:ts + sl, :]
        a_scr[:, 0, :, sl:pad, :] = a_scr[:, 0, :, ts + sl:ts + pad, :]
        b_scr[:, 0, :, sl:pad, :] = b_scr[:, 0, :, ts + sl:ts + pad, :]

    cw = cw_ref[...]
    cb = cb_ref[...]
    z = -lam_ref[...]
    softplus = jnp.maximum(z, 0.0) + jnp.log1p(jnp.exp(-jnp.abs(z)))
    c0 = math.sqrt(2.0 / math.pi)

    @pl.when(i == 0)
    def _():
        for sub in range(LRU_NSUB):
            gu_scr[0, sub] = _dot(x_ref[sub].astype(BF16), win_ref[...])

    def project_next(slot, sub):
        gu_scr[1 - slot, sub] = _dot(xn_ref[sub].astype(BF16), win_ref[...])

    def conv_gates(slot, sub):
        u = gu_scr[slot, sub, :, w:2 * w]
        ucs = []
        for lt in range(LRU_NLT):
            ls = slice(lt * LANES, (lt + 1) * LANES)
            us = ustage_scr.at[sub, lt]
            us[sl:sl + ts, :] = u[:, ls]
            uc = us[sl - 3:sl - 3 + ts, :] * cw[0:1, ls]
            uc = uc + us[sl - 2:sl - 2 + ts, :] * cw[1:2, ls]
            uc = uc + us[sl - 1:sl - 1 + ts, :] * cw[2:3, ls]
            uc = uc + u[:, ls] * cw[3:4, ls]
            ucs.append(uc + cb[:, ls])
        uc = jnp.concatenate(ucs, axis=1)

        ucb = uc.astype(BF16)
        rs, igs = [], []
        for h in range(LRU_BLOCKS):
            blk = ucb[:, h * LRU_BLOCK_W:(h + 1) * LRU_BLOCK_W]
            rs.append(_dot(blk, wr_ref[h]))
            igs.append(_dot(blk, wi_ref[h]))
        return uc, jnp.concatenate(rs, axis=-1), jnp.concatenate(igs, axis=-1)

    def recurrence(slot, sub, uc, r_pre, ig_pre):
        hblk = hcar_scr[sub]
        gate = gu_scr[slot, sub, :, 0:w]
        r = _sigmoid(r_pre + br_ref[...])
        ig = _sigmoid(ig_pre + bi_ref[...])
        log_a = (-LRU_C * r) * softplus
        a0 = jnp.exp(log_a)
        th = jnp.tanh(log_a)
        b0 = (jnp.sqrt(-2.0 * th) * lax.rsqrt(1.0 - th)) * (ig * uc)

        a8, b8 = [], []
        for lt in range(LRU_NLT):
            ls = slice(lt * LANES, (lt + 1) * LANES)
            a_scr[sub, 0, lt, pad:pad + ts, :] = a0[:, ls]
            b_scr[sub, 0, lt, pad:pad + ts, :] = b0[:, ls]
            for k, d in enumerate((1, 2)):
                a_cur = a_scr[sub, k, lt, sl:pad + ts, :]
                b_cur = b_scr[sub, k, lt, sl:pad + ts, :]
                a_scr[sub, k + 1, lt, sl:pad + ts, :] = a_cur * a_scr[sub, k, lt, sl - d:pad + ts - d, :]
                b_scr[sub, k + 1, lt, sl:pad + ts, :] = (a_cur * b_scr[sub, k, lt, sl - d:pad + ts - d, :]
                                                        + b_cur)
            a_cur = a_scr[sub, 2, lt, pad:pad + ts, :]
            a8.append(a_cur * a_scr[sub, 2, lt, pad - 4:pad - 4 + ts, :])
            b8.append(a_cur * b_scr[sub, 2, lt, pad - 4:pad - 4 + ts, :] + b_scr[sub, 2, lt, pad:pad + ts, :])
        a8 = jnp.concatenate(a8, axis=1)
        b8 = jnp.concatenate(b8, axis=1)

        hs = []
        for j in range(ts // sl):
            hblk = a8[j * sl:(j + 1) * sl] * hblk + b8[j * sl:(j + 1) * sl]
            hs.append(hblk)
        h = jnp.concatenate(hs, axis=0)

        gelu = gate * (0.5 * (1.0 + jnp.tanh(c0 * (gate + 0.044715 * (gate * gate * gate)))))
        hcar_scr[sub] = hblk
        return (h * gelu).astype(BF16)

    def out_proj(sub, yb):
        return ALPHA * x_ref[sub] + _dot(yb, wo_ref[...])

    def step(slot):
        zs = []
        for sub in range(LRU_NSUB):
            uc, r_pre, ig_pre = conv_gates(slot, sub)
            project_next(slot, sub)
            zs.append(out_proj(sub, recurrence(slot, sub, uc, r_pre, ig_pre)))
            if sub > 0:
                o_ref[sub - 1] = _layer_norm(zs[sub - 1], g_ref[...], b_ref[...])
        o_ref[LRU_NSUB - 1] = _layer_norm(zs[-1], g_ref[...], b_ref[...])

    for parity in range(2):
        pl.when(lax.rem(i, 2) == parity)(functools.partial(step, parity))


def _rglru_layer(x, win, cw, cb, wr, br, wi, bi, lam, wo, g, b):
    bsz, s, d = x.shape
    tm = LRU_SUB
    w = LRU_WIDTH
    last = s // tm - 1
    return pl.pallas_call(
        _rglru_kernel,
        grid=(bsz // LRU_NSUB, s // tm),
        in_specs=[pl.BlockSpec((LRU_NSUB, tm, d), lambda bi_, i: (bi_, i, 0)),
                  pl.BlockSpec((LRU_NSUB, tm, d), lambda bi_, i: (bi_, jnp.minimum(i + 1, last), 0)),
                  _resident((d, 2 * w)), _resident((LRU_CONV, w)), _resident((1, w)),
                  _resident((LRU_BLOCKS, LRU_BLOCK_W, LRU_BLOCK_W)), _resident((1, w)),
                  _resident((LRU_BLOCKS, LRU_BLOCK_W, LRU_BLOCK_W)), _resident((1, w)),
                  _resident((1, w)), _resident((w, d)), _resident((1, d)), _resident((1, d))],
        out_specs=pl.BlockSpec((LRU_NSUB, tm, d), lambda bi_, i: (bi_, i, 0)),
        out_shape=jax.ShapeDtypeStruct((bsz, s, d), F32),
        scratch_shapes=[pltpu.VMEM((LRU_NSUB, LRU_NLT, SUBLANES + LRU_SUB, LANES), F32),
                        pltpu.VMEM((LRU_NSUB, 3, LRU_NLT, LRU_PAD + LRU_SUB, LANES), F32),
                        pltpu.VMEM((LRU_NSUB, 3, LRU_NLT, LRU_PAD + LRU_SUB, LANES), F32),
                        pltpu.VMEM((LRU_NSUB, SUBLANES, w), F32),
                        pltpu.VMEM((2, LRU_NSUB, tm, 2 * w), F32)],
        compiler_params=_params(("arbitrary", "arbitrary")),
        name="rglru_layer",
    )(x, x, win, cw, cb, wr, br, wi, bi, lam, wo, g, b)


MLA_TM = 1024
MLA_TQ = 1024
MLA_HALF = MLA_TQ // 2
MLA_TK = 512
C_DOWN_PAD = 768
MLA_Q_SCALE = (C_NOPE + C_ROPE) ** -0.5 * math.log2(math.e)


def _mla_proj_kernel(x_ref, cos_ref, sin_ref, wd_ref, qn_ref, kvn_ref, wuq_ref, wukv_ref,
                     q_ref, k_ref, v_ref):
    tm = MLA_TM
    xb = x_ref[0].astype(BF16)
    c = _dot(xb, wd_ref[...])
    cq = c[:, :C_Q_RANK]
    ckv = c[:, C_Q_RANK:C_Q_RANK + C_KV_RANK]
    kr_slab = c[:, C_Q_RANK + C_KV_RANK:]
    cq = cq * lax.rsqrt(jnp.mean(cq * cq, axis=-1, keepdims=True) + RMS_EPS) * qn_ref[...]
    ckv = ckv * lax.rsqrt(jnp.mean(ckv * ckv, axis=-1, keepdims=True) + RMS_EPS) * kvn_ref[...]
    q = _dot(cq.astype(BF16), wuq_ref[...]) * MLA_Q_SCALE
    kv = _dot(ckv.astype(BF16), wukv_ref[...])

    cos = cos_ref[...]
    sin = sin_ref[...]
    lane = lax.broadcasted_iota(jnp.int32, (tm, LANES), 1)
    first_half = (lane & 32) == 0
    low = lane < 64
    nn = C_HEADS * C_NOPE

    kr = _rope_slab(kr_slab, cos, sin, first_half)
    kr_pair = (kr.astype(BF16), pltpu.roll(kr, 64, 1).astype(BF16))
    for pair in range(C_HEADS // 2):
        qr = _rope_slab(q[:, nn + pair * LANES:nn + (pair + 1) * LANES], cos, sin, first_half)
        for half in range(2):
            h = 2 * pair + half
            qr_h = jnp.where(low, qr, 0.0) if half == 0 else jnp.where(low, 0.0, qr)
            q_ref[0, h, :, 0:C_NOPE] = q[:, h * C_NOPE:(h + 1) * C_NOPE].astype(BF16)
            q_ref[0, h, :, C_NOPE:C_QK_PAD] = qr_h.astype(BF16)
            k_ref[0, h, :, 0:C_NOPE] = kv[:, h * C_NOPE:(h + 1) * C_NOPE].astype(BF16)
            k_ref[0, h, :, C_NOPE:C_QK_PAD] = kr_pair[half]
            v_ref[0, h] = kv[:, nn + h * C_V:nn + (h + 1) * C_V].astype(BF16)


def _mla_proj(x, cos, sin, wd, qn, kvn, wuq, wukv):
    bsz, s, d = x.shape
    tm = MLA_TM
    qk_shape = jax.ShapeDtypeStruct((bsz, C_HEADS, s, C_QK_PAD), BF16)
    v_shape = jax.ShapeDtypeStruct((bsz, C_HEADS, s, C_V), BF16)
    return pl.pallas_call(
        _mla_proj_kernel,
        grid=(bsz, s // tm),
        in_specs=[pl.BlockSpec((1, tm, d), lambda bi, i: (bi, i, 0)),
                  pl.BlockSpec((tm, LANES), lambda bi, i: (i, 0)),
                  pl.BlockSpec((tm, LANES), lambda bi, i: (i, 0)),
                  _resident(wd.shape), _resident(qn.shape), _resident(kvn.shape),
                  _resident(wuq.shape), _resident(wukv.shape)],
        out_specs=[pl.BlockSpec((1, C_HEADS, tm, C_QK_PAD), lambda bi, i: (bi, 0, i, 0)),
                   pl.BlockSpec((1, C_HEADS, tm, C_QK_PAD), lambda bi, i: (bi, 0, i, 0)),
                   pl.BlockSpec((1, C_HEADS, tm, C_V), lambda bi, i: (bi, 0, i, 0))],
        out_shape=[qk_shape, qk_shape, v_shape],
        compiler_params=_params(("parallel", "parallel")),
        name="mla_proj",
    )(x, cos, sin, wd, qn, kvn, wuq, wukv)


def _mla_attn_kernel(q_ref, qn_ref, k_ref, v_ref, o_ref, m_scr, acc_scr, s_scr):
    i = pl.program_id(2)
    tk, hq = MLA_TK, MLA_HALF

    m_scr[...] = jnp.full_like(m_scr, -jnp.inf)
    acc_scr[...] = jnp.zeros_like(acc_scr)
    ones = jnp.ones((tk, LANES), BF16)

    def scores(slot, hf, j, queries=q_ref):
        k0 = pl.multiple_of(j * tk, tk)
        s_scr[slot, hf] = _dot_nt(queries[0, 0, hf * hq:(hf + 1) * hq, :], k_ref[0, 0, pl.ds(k0, tk), :])

    def consume(slot, hf, j, diagonal):
        k0 = pl.multiple_of(j * tk, tk)
        vv = jnp.concatenate([v_ref[0, 0, pl.ds(k0, tk), :], ones], axis=1)
        s = s_scr[slot, hf]
        if diagonal:
            qpos = lax.broadcasted_iota(jnp.int32, (hq, tk), 0)
            kpos = lax.broadcasted_iota(jnp.int32, (hq, tk), 1)
            s = jnp.where(kpos <= qpos, s, NEG)
        m_prev = m_scr[hf]
        m_next = jnp.maximum(m_prev, jnp.max(s, axis=-1, keepdims=True))
        p = jnp.exp2(s - jnp.tile(m_next, (1, tk // LANES)))
        corr = jnp.exp2(m_prev - m_next)
        acc_scr[hf] = jnp.tile(corr, (1, 2)) * acc_scr[hf] + _dot(p.astype(BF16), vv)
        m_scr[hf] = m_next

    @pl.when(i == 0)
    def _():
        scores(0, 0, 0)
        scores(0, 1, 0)

    def chunk_pair(j):
        scores(1, 0, j + 1)
        scores(1, 1, j + 1)
        consume(0, 0, j, False)
        consume(0, 1, j, False)
        scores(0, 0, j + 2)
        scores(0, 1, j + 2)
        consume(1, 0, j + 1, False)
        consume(1, 1, j + 1, False)

    def body(t, carry):
        chunk_pair(4 * t)
        chunk_pair(4 * t + 2)
        return carry

    lax.fori_loop(0, i // 2, body, 0)

    @pl.when(i % 2 == 1)
    def _():
        chunk_pair(2 * i - 2)

    scores(1, 1, 2 * i + 1)
    consume(0, 0, 2 * i, True)
    consume(0, 1, 2 * i, False)
    scores(0, 0, 0, queries=qn_ref)
    scores(0, 1, 0, queries=qn_ref)
    consume(1, 1, 2 * i + 1, True)
    for hf in range(2):
        acc = acc_scr[hf]
        o_ref[0, hf * hq:(hf + 1) * hq, :] = (acc[:, :C_V] / acc[:, C_V:]).astype(BF16)


def _mla_attn(q, k, v):
    bsz, nh, s, _ = q.shape
    tq = MLA_TQ
    last = s // tq - 1
    return pl.pallas_call(
        _mla_attn_kernel,
        grid=(bsz, nh, s // tq),
        in_specs=[pl.BlockSpec((1, 1, tq, C_QK_PAD), lambda b, h, i: (b, h, i, 0)),
                  pl.BlockSpec((1, 1, tq, C_QK_PAD), lambda b, h, i: (b, h, jnp.minimum(i + 1, last), 0)),
                  pl.BlockSpec((1, 1, s, C_QK_PAD), lambda b, h, i: (b, h, 0, 0)),
                  pl.BlockSpec((1, 1, s, C_V), lambda b, h, i: (b, h, 0, 0))],
        out_specs=pl.BlockSpec((1, tq, C_V), lambda b, h, i: (b, i, h)),
        out_shape=jax.ShapeDtypeStruct((bsz, s, nh * C_V), BF16),
        scratch_shapes=[pltpu.VMEM((2, MLA_HALF, LANES), F32),
                        pltpu.VMEM((2, MLA_HALF, 2 * C_V), F32),
                        pltpu.VMEM((2, 2, MLA_HALF, MLA_TK), F32)],
        compiler_params=_params(("parallel", "parallel", "arbitrary")),
        name="mla_attn",
    )(q, q, k, v)


X_TM = 1024
X_LN_ROWS = 256


def _xattn_kernel(*refs, mixer_prologue):
    tm = X_TM
    if mixer_prologue:
        a_ref, wa_ref, ga_ref, ba_ref, x_ref, wq_ref, mkv_ref, wo_ref, g_ref, b_ref, o_ref = refs
        x = _layer_norm(ALPHA * x_ref[0] + _dot(a_ref[0], wa_ref[...]), ga_ref[...], ba_ref[...])
    else:
        x_ref, wq_ref, mkv_ref, wo_ref, g_ref, b_ref, o_ref = refs
        x = x_ref[0]
    q = (_dot(x.astype(BF16), wq_ref[...]) * (X_HEAD_DIM ** -0.5 * LOG2E)).astype(BF16)
    heads = [(h * X_HEAD_DIM, (h + 1) * X_HEAD_DIM) for h in range(X_HEADS)]
    scores = [_dot_nt(q[:, lo:hi], mkv_ref[0, :, lo:hi]) for lo, hi in heads]
    probs, inv_denoms = [], []
    for s in scores:
        p = jnp.exp2(s - jnp.max(s, axis=-1, keepdims=True))
        inv_denoms.append(1.0 / jnp.sum(p, axis=-1, keepdims=True))
        probs.append(p.astype(BF16))
    outs = [(_dot(probs[h], mkv_ref[0, :, D_MODEL + lo:D_MODEL + hi]) * inv_denoms[h]).astype(BF16)
            for h, (lo, hi) in enumerate(heads)]
    o = jnp.concatenate(outs, axis=-1)
    for r0 in range(0, tm, X_LN_ROWS):
        rows = slice(r0, r0 + X_LN_ROWS)
        y = _dot(o[rows], wo_ref[...])
        o_ref[0, rows, :] = _layer_norm(ALPHA * x[rows] + y, g_ref[...], b_ref[...])


def _xattn_layer(x, layer, wq, mkv, wo, g, b, mixer=None):
    bsz, s, d = x.shape
    tm = X_TM
    tile = pl.BlockSpec((1, tm, d), lambda bi, i: (bi, i, 0))
    in_specs = [tile, _resident((d, d), layer),
                pl.BlockSpec((1, MEM_LEN, 2 * d), lambda bi, i: (bi, 0, 0)),
                _resident((d, d), layer), _resident((1, d)), _resident((1, d))]
    args = (x, wq, mkv, wo, g, b)
    if mixer is not None:
        in_specs = [tile, _resident((d, d)), _resident((1, d)), _resident((1, d))] + in_specs
        args = tuple(mixer) + args
    return pl.pallas_call(
        functools.partial(_xattn_kernel, mixer_prologue=mixer is not None),
        grid=(bsz, s // tm),
        in_specs=in_specs,
        out_specs=tile,
        out_shape=jax.ShapeDtypeStruct((bsz, s, d), F32),
        compiler_params=_params(("parallel", "parallel")),
        name="xattn_layer",
    )(*args)


FFN_TM = 512
FFN_LN_ROWS = 256
FFN_DOWN_GROUPS = (6, FFN_NCHUNK)


def _ffn_kernel(x_ref, wup_ref, cw_ref, cb_ref, wdn_ref, g_ref, b_ref, o_ref, hprev_scr, act_scr,
                acc_scr, hstage_scr):
    i = pl.program_id(1)
    tm = FFN_TM

    @pl.when(i == 0)
    def _():
        hprev_scr[...] = jnp.zeros_like(hprev_scr)

    x = x_ref[0]
    xb = x.astype(BF16)

    def up_conv(col0, slot):
        cols = slice(col0, col0 + FFN_CHUNK)
        h = _dot(xb, wup_ref[:, cols])
        prev = hprev_scr[:, cols]
        hprev_scr[:, cols] = h[tm - SUBLANES:, :]
        cw = cw_ref[:, cols]
        cb = cb_ref[:, cols]
        outs = []
        for lt in range(FFN_CHUNK // LANES):
            ls = slice(lt * LANES, (lt + 1) * LANES)
            hs = hstage_scr.at[slot, lt]
            hs[0:SUBLANES, :] = prev[:, ls]
            hs[SUBLANES:SUBLANES + tm, :] = h[:, ls]
            y = hs[SUBLANES - 2:SUBLANES - 2 + tm, :] * cw[0:1, ls]
            y = y + hs[SUBLANES - 1:SUBLANES - 1 + tm, :] * cw[1:2, ls]
            y = y + h[:, ls] * cw[2:3, ls]
            outs.append(y + cb[:, ls])
        return jnp.concatenate(outs, axis=1)

    def down(rows, c0, c1):
        ks = slice(c0 * FFN_CHUNK, c1 * FFN_CHUNK)
        return _dot(act_scr[rows, ks], wdn_ref[ks, :])

    everything = slice(0, tm)
    for c in range(FFN_NCHUNK):
        gt = up_conv(c * FFN_CHUNK, 2 * (c % 2))
        val = up_conv(D_FF + c * FFN_CHUNK, 2 * (c % 2) + 1)
        act_scr[:, c * FFN_CHUNK:(c + 1) * FFN_CHUNK] = ((gt * _sigmoid(gt)) * val).astype(BF16)
        if c + 1 in FFN_DOWN_GROUPS[:-1]:
            gi = FFN_DOWN_GROUPS.index(c + 1)
            if gi == 0:
                acc_scr[...] = down(everything, 0, c + 1)
            else:
                acc_scr[...] += down(everything, FFN_DOWN_GROUPS[gi - 1], c + 1)
    for r0 in range(0, tm, FFN_LN_ROWS):
        rows = slice(r0, r0 + FFN_LN_ROWS)
        out = acc_scr[rows, :] + down(rows, FFN_DOWN_GROUPS[-2], FFN_DOWN_GROUPS[-1])
        o_ref[0, rows, :] = _layer_norm(ALPHA * x[rows] + out, g_ref[...], b_ref[...])


def _ffn_layer(x, layer, wup, cw, cb, wdn, g, b):
    bsz, s, d = x.shape
    tm = FFN_TM
    return pl.pallas_call(
        _ffn_kernel,
        grid=(bsz, s // tm),
        in_specs=[pl.BlockSpec((1, tm, d), lambda bi, i: (bi, i, 0)),
                  _resident((d, 2 * D_FF), layer), _resident((FFN_CONV, 2 * D_FF)), _resident((1, 2 * D_FF)),
                  _resident((D_FF, d), layer), _resident((1, d)), _resident((1, d))],
        out_specs=pl.BlockSpec((1, tm, d), lambda bi, i: (bi, i, 0)),
        out_shape=jax.ShapeDtypeStruct((bsz, s, d), F32),
        scratch_shapes=[pltpu.VMEM((SUBLANES, 2 * D_FF), F32), pltpu.VMEM((tm, D_FF), BF16),
                        pltpu.VMEM((tm, d), F32),
                        pltpu.VMEM((4, FFN_CHUNK // LANES, SUBLANES + tm, LANES), F32)],
        compiler_params=_params(("arbitrary", "arbitrary")),
        name="ffn_layer",
    )(x, wup, cw, cb, wdn, g, b)


def kernel(x, mem, a_w_qkv, a_sinks, a_w_o, b_w_in, b_conv_w, b_conv_b, b_w_rgate, b_b_rgate, b_w_igate, b_b_igate, b_lambda, b_w_o, c_w_down, c_q_norm, c_kv_norm, c_w_uq, c_w_ukv, c_w_o, mem_w_kv, x_w_q, x_w_o, f_w_up, f_conv_w, f_conv_b, f_w_down, ln_g, ln_b):
    bsz, s, d = x.shape
    cos, sin = _rope_tables(s)

    mkv = _matmul(mem.reshape(bsz * MEM_LEN, d), mem_w_kv.astype(BF16), 512, BF16)
    mkv = mkv.reshape(bsz, MEM_LEN, 2 * d)

    def row(v):
        return v.reshape(1, -1)

    xq_all, xo_all = x_w_q.astype(BF16), x_w_o.astype(BF16)
    fup_all, fdn_all = f_w_up.astype(BF16), f_w_down.astype(BF16)
    aqkv_all, ao_all = a_w_qkv.astype(BF16), a_w_o.astype(BF16)

    for i in range(DEPTH):
        kind, j = i % N_MIXERS, i // N_MIXERS
        g0, b0 = row(ln_g[i, 0]), row(ln_b[i, 0])
        mixer = None
        if kind == 0:
            x = _swa_layer(x, j, a_sinks[j], cos, sin, aqkv_all, ao_all, g0, b0)
        elif kind == 1:
            x = _rglru_layer(x, b_w_in[j].astype(BF16), b_conv_w[j], row(b_conv_b[j]),
                             b_w_rgate[j].astype(BF16), row(b_b_rgate[j]),
                             b_w_igate[j].astype(BF16), row(b_b_igate[j]),
                             row(b_lambda[j]), b_w_o[j].astype(BF16), g0, b0)
        else:
            wd = jnp.pad(c_w_down[j], ((0, 0), (0, C_DOWN_PAD - c_w_down.shape[-1]))).astype(BF16)
            wuq = c_w_uq[j].reshape(C_Q_RANK, C_HEADS, C_NOPE + C_ROPE)
            wuq = jnp.concatenate([wuq[:, :, :C_NOPE].reshape(C_Q_RANK, -1),
                                   wuq[:, :, C_NOPE:].reshape(C_Q_RANK, -1)], axis=-1).astype(BF16)
            wukv = c_w_ukv[j].reshape(C_KV_RANK, C_HEADS, C_NOPE + C_V)
            wukv = jnp.concatenate([wukv[:, :, :C_NOPE].reshape(C_KV_RANK, -1),
                                    wukv[:, :, C_NOPE:].reshape(C_KV_RANK, -1)], axis=-1).astype(BF16)
            q, k, v = _mla_proj(x, cos, sin, wd, row(c_q_norm[j]), row(c_kv_norm[j]), wuq, wukv)
            mixer = (_mla_attn(q, k, v), c_w_o[j].astype(BF16), g0, b0)
        x = _xattn_layer(x, i, xq_all, mkv, xo_all, row(ln_g[i, 1]), row(ln_b[i, 1]), mixer=mixer)
        x = _ffn_layer(x, i, fup_all, f_conv_w[i], row(f_conv_b[i]), fdn_all, row(ln_g[i, 2]), row(ln_b[i, 2]))
    return x
```

```python
import functools
import math

import jax
import jax.numpy as jnp
import numpy as np
from jax import lax
from jax.experimental import pallas as pl
from jax.experimental.pallas import tpu as pltpu

D_MODEL = 1024
DEPTH = 4
N_MIXERS = 3
MEM_LEN = 256
BLOCK = 128
ROPE_THETA = 10000.0
NEG = -1e30
LN_EPS = 1e-5
RMS_EPS = 1e-6

A_HEADS = 16
A_KV_HEADS = 4
A_HEAD_DIM = 64

LRU_WIDTH = D_MODEL
LRU_BLOCKS = 4
LRU_BLOCK_W = LRU_WIDTH // LRU_BLOCKS
LRU_CONV = 4
LRU_C = 8.0

C_HEADS = 8
C_NOPE = 128
C_ROPE = 64
C_V = 128
C_Q_RANK = 384
C_KV_RANK = 256
C_QK_PAD = 256

X_HEADS = 4
X_HEAD_DIM = D_MODEL // X_HEADS

D_FF = 2816
FFN_CONV = 3
FFN_CHUNK = 256
FFN_NCHUNK = D_FF // FFN_CHUNK

ALPHA = (2.0 * DEPTH) ** 0.25

LANES = 128
SUBLANES = 8
VMEM_LIMIT = 56 * 1024 * 1024

BF16 = jnp.bfloat16
F32 = jnp.float32

NT_DIMS = (((1,), (1,)), ((), ()))
LOG2E = math.log2(math.e)


def _dot(a, b):
    return jnp.dot(a, b, preferred_element_type=F32)


def _dot_nt(a, b):
    return lax.dot_general(a, b, NT_DIMS, preferred_element_type=F32)


def _layer_norm(z, g, b):
    mu = jnp.mean(z, axis=-1, keepdims=True)
    zc = z - mu
    var = jnp.mean(zc * zc, axis=-1, keepdims=True)
    return zc * lax.rsqrt(var + LN_EPS) * g + b


def _sigmoid(z):
    return 0.5 * jnp.tanh(0.5 * z) + 0.5


def _resident(shape, layer=None):
    nd = len(shape)
    if layer is None:
        return pl.BlockSpec(shape, lambda *_: (0,) * nd, pipeline_mode=pl.Buffered(1))
    return pl.BlockSpec((None,) + tuple(shape), lambda *_: (layer,) + (0,) * nd, pipeline_mode=pl.Buffered(1))


def _params(sem):
    return pltpu.CompilerParams(dimension_semantics=sem, vmem_limit_bytes=VMEM_LIMIT)


def _matmul_kernel(a_ref, w_ref, o_ref):
    o_ref[...] = _dot(a_ref[...].astype(BF16), w_ref[...]).astype(o_ref.dtype)


def _matmul(a, w, tm, out_dtype):
    t, k = a.shape
    n = w.shape[1]
    return pl.pallas_call(
        _matmul_kernel,
        grid=(t // tm,),
        in_specs=[pl.BlockSpec((tm, k), lambda i: (i, 0)), _resident((k, n))],
        out_specs=pl.BlockSpec((tm, n), lambda i: (i, 0)),
        out_shape=jax.ShapeDtypeStruct((t, n), out_dtype),
        compiler_params=_params(("parallel",)),
        name="matmul",
    )(a, w)


def _rope_slab(slab, cos, sin_signed, first_half):
    rot = jnp.where(first_half, pltpu.roll(slab, LANES - 32, 1), pltpu.roll(slab, 32, 1))
    return slab * cos + rot * sin_signed


def _rope_tables(seq):
    inv = 1.0 / (ROPE_THETA ** (jnp.arange(0, A_HEAD_DIM, 2, dtype=F32) / A_HEAD_DIM))
    ang = jnp.arange(seq, dtype=F32)[:, None] * inv[None, :]
    cos, sin = jnp.cos(ang), jnp.sin(ang)
    return (jnp.concatenate([cos, cos, cos, cos], axis=-1),
            jnp.concatenate([-sin, sin, -sin, sin], axis=-1))


SWA_TM = 512
SWA_NBLK = SWA_TM // BLOCK


def _swa_kernel(sinks_ref, x_ref, cos_ref, sin_ref, wqkv_ref, wo_ref, g_ref, b_ref, o_ref,
                kext_scr, vext_scr, q_scr, o_scr):
    i = pl.program_id(1)
    tm = SWA_TM

    @pl.when(i == 0)
    def _():
        kext_scr[:, 0:BLOCK, :] = jnp.zeros((2 * A_KV_HEADS, BLOCK, LANES), BF16)
        vext_scr[:, 0:BLOCK, :] = jnp.zeros((2 * A_KV_HEADS, BLOCK, LANES), BF16)

    @pl.when(i > 0)
    def _():
        kext_scr[:, 0:BLOCK, :] = kext_scr[:, tm:tm + BLOCK, :]
        vext_scr[:, 0:BLOCK, :] = vext_scr[:, tm:tm + BLOCK, :]

    x = x_ref[0]
    qkv = _dot(x.astype(BF16), wqkv_ref[...])
    cos = cos_ref[...]
    sin = sin_ref[...]
    lane = lax.broadcasted_iota(jnp.int32, (tm, LANES), 1)
    first_half = (lane & 32) == 0
    low = lane < 64

    nq = A_HEADS * A_HEAD_DIM
    nkv = A_KV_HEADS * A_HEAD_DIM
    for j in range(nq // LANES):
        slab = _rope_slab(qkv[:, j * LANES:(j + 1) * LANES], cos, sin, first_half)
        q_scr[:, j * LANES:(j + 1) * LANES] = (slab * (A_HEAD_DIM ** -0.5 * LOG2E)).astype(BF16)

    for m in range(nkv // LANES):
        kslab = _rope_slab(qkv[:, nq + m * LANES:nq + (m + 1) * LANES], cos, sin, first_half)
        vslab = qkv[:, nq + nkv + m * LANES:nq + nkv + (m + 1) * LANES]
        for slab, ext in ((kslab, kext_scr), (vslab, vext_scr)):
            swapped = pltpu.roll(slab, 64, 1)
            h0, h1 = 2 * m, 2 * m + 1
            ext[2 * h0, BLOCK:BLOCK + tm, :] = jnp.where(low, slab, 0.0).astype(BF16)
            ext[2 * h0 + 1, BLOCK:BLOCK + tm, :] = jnp.where(low, 0.0, swapped).astype(BF16)
            ext[2 * h1, BLOCK:BLOCK + tm, :] = jnp.where(low, swapped, 0.0).astype(BF16)
            ext[2 * h1 + 1, BLOCK:BLOCK + tm, :] = jnp.where(low, 0.0, slab).astype(BF16)

    qi = lax.broadcasted_iota(jnp.int32, (2 * BLOCK, 4 * BLOCK), 0) & (BLOCK - 1)
    kj = lax.broadcasted_iota(jnp.int32, (2 * BLOCK, 4 * BLOCK), 1) & (2 * BLOCK - 1)
    dist = qi + BLOCK - kj
    band = (dist >= 0) & (dist < BLOCK)
    band_first = band & ((kj >= BLOCK) | (i > 0))
    top = lax.broadcasted_iota(jnp.int32, (2 * BLOCK, 1), 0) < BLOCK
    low2 = lax.broadcasted_iota(jnp.int32, (2 * BLOCK, LANES), 1) < 64

    def window(ext, h, r0):
        return jnp.concatenate([ext[2 * h, r0:r0 + 2 * BLOCK, :], ext[2 * h + 1, r0:r0 + 2 * BLOCK, :]],
                               axis=0)

    for n in range(SWA_NBLK):
        mask = band_first if n == 0 else band
        r0 = n * BLOCK
        scores = []
        for h in range(A_KV_HEADS):
            qs = jnp.concatenate([q_scr[r0:r0 + BLOCK, (2 * h) * LANES:(2 * h + 1) * LANES],
                                  q_scr[r0:r0 + BLOCK, (2 * h + 1) * LANES:(2 * h + 2) * LANES]], axis=0)
            scores.append(_dot_nt(qs, window(kext_scr, h, r0)))
        probs, inv_denoms = [], []
        for h in range(A_KV_HEADS):
            s = jnp.where(mask, scores[h], NEG)
            ps, rs = [], []
            for half in range(2):
                sh = s[:, half * 2 * BLOCK:(half + 1) * 2 * BLOCK]
                sink = jnp.where(top, sinks_ref[4 * h + half] * LOG2E, sinks_ref[4 * h + 2 + half] * LOG2E)
                mx = jnp.maximum(jnp.max(sh, axis=-1, keepdims=True), sink)
                p = jnp.exp2(sh - mx)
                rs.append(1.0 / (jnp.sum(p, axis=-1, keepdims=True) + jnp.exp2(sink - mx)))
                ps.append(p.astype(BF16))
            probs.append(jnp.concatenate(ps, axis=1))
            inv_denoms.append(jnp.where(low2, rs[0], rs[1]))
        for h in range(A_KV_HEADS):
            o = _dot(probs[h], window(vext_scr, h, r0)) * inv_denoms[h]
            o_scr[r0:r0 + BLOCK, (2 * h) * LANES:(2 * h + 1) * LANES] = o[:BLOCK].astype(BF16)
            o_scr[r0:r0 + BLOCK, (2 * h + 1) * LANES:(2 * h + 2) * LANES] = o[BLOCK:].astype(BF16)

    for r0 in range(0, tm, tm // 2):
        rows = slice(r0, r0 + tm // 2)
        y = _dot(o_scr[rows, :], wo_ref[...])
        o_ref[0, rows, :] = _layer_norm(ALPHA * x[rows] + y, g_ref[...], b_ref[...])


def _swa_layer(x, layer, sinks, cos, sin, wqkv, wo, g, b):
    bsz, s, d = x.shape
    tm = SWA_TM
    nqkv = wqkv.shape[-1]
    return pl.pallas_call(
        _swa_kernel,
        grid=(bsz, s // tm),
        in_specs=[pl.BlockSpec(memory_space=pltpu.SMEM),
                  pl.BlockSpec((1, tm, d), lambda bi, i: (bi, i, 0)),
                  pl.BlockSpec((tm, LANES), lambda bi, i: (i, 0)),
                  pl.BlockSpec((tm, LANES), lambda bi, i: (i, 0)),
                  _resident((d, nqkv), layer), _resident((d, d), layer), _resident((1, d)), _resident((1, d))],
        out_specs=pl.BlockSpec((1, tm, d), lambda bi, i: (bi, i, 0)),
        out_shape=jax.ShapeDtypeStruct((bsz, s, d), F32),
        scratch_shapes=[pltpu.VMEM((2 * A_KV_HEADS, tm + BLOCK, LANES), BF16),
                        pltpu.VMEM((2 * A_KV_HEADS, tm + BLOCK, LANES), BF16),
                        pltpu.VMEM((tm, d), BF16),
                        pltpu.VMEM((tm, d), BF16)],
        compiler_params=_params(("arbitrary", "arbitrary")),
        name="swa_layer",
    )(sinks, x, cos, sin, wqkv, wo, g, b)


LRU_SUB = 256
LRU_NSUB = 2
LRU_NLT = LRU_WIDTH // LANES
LRU_PAD = 2 * SUBLANES


def _rglru_kernel(x_ref, xn_ref, win_ref, cw_ref, cb_ref, wr_ref, br_ref, wi_ref, bi_ref, lam_ref, wo_ref,
                  g_ref, b_ref, o_ref, ustage_scr, a_scr, b_scr, hcar_scr, gu_scr):
    i = pl.program_id(1)
    ts = LRU_SUB
    w = LRU_WIDTH
    pad = LRU_PAD
    sl = SUBLANES

    @pl.when(i == 0)
    def _():
        ustage_scr[:, :, 0:sl, :] = jnp.zeros((LRU_NSUB, LRU_NLT, sl, LANES), F32)
        a_scr[:, :, :, 0:pad, :] = jnp.ones((LRU_NSUB, 3, LRU_NLT, pad, LANES), F32)
        b_scr[:, :, :, 0:pad, :] = jnp.zeros((LRU_NSUB, 3, LRU_NLT, pad, LANES), F32)
        hcar_scr[...] = jnp.zeros_like(hcar_scr)

    @pl.when(i > 0)
    def _():
        ustage_scr[:, :, 0:sl, :] = ustage_scr[:, :, ts:ts + sl, :]
        a_scr[:, 0, :, sl:pad, :] = a_scr[:, 0, :, ts + sl:ts + pad, :]
        b_scr[:, 0, :, sl:pad, :] = b_scr[:, 0, :, ts + sl:ts + pad, :]

    cw = cw_ref[...]
    cb = cb_ref[...]
    z = -lam_ref[...]
    softplus = jnp.maximum(z, 0.0) + jnp.log1p(jnp.exp(-jnp.abs(z)))
    c0 = math.sqrt(2.0 / math.pi)

    @pl.when(i == 0)
    def _():
        for sub in range(LRU_NSUB):
            gu_scr[0, sub] = _dot(x_ref[sub].astype(BF16), win_ref[...])

    def project_next(slot, sub):
        gu_scr[1 - slot, sub] = _dot(xn_ref[sub].astype(BF16), win_ref[...])

    def conv_gates(slot, sub):
        u = gu_scr[slot, sub, :, w:2 * w]
        ucs = []
        for lt in range(LRU_NLT):
            ls = slice(lt * LANES, (lt + 1) * LANES)
            us = ustage_scr.at[sub, lt]
            us[sl:sl + ts, :] = u[:, ls]
            uc = us[sl - 3:sl - 3 + ts, :] * cw[0:1, ls]
            uc = uc + us[sl - 2:sl - 2 + ts, :] * cw[1:2, ls]
            uc = uc + us[sl - 1:sl - 1 + ts, :] * cw[2:3, ls]
            uc = uc + u[:, ls] * cw[3:4, ls]
            ucs.append(uc + cb[:, ls])
        uc = jnp.concatenate(ucs, axis=1)

        ucb = uc.astype(BF16)
        rs, igs = [], []
        for h in range(LRU_BLOCKS):
            blk = ucb[:, h * LRU_BLOCK_W:(h + 1) * LRU_BLOCK_W]
            rs.append(_dot(blk, wr_ref[h]))
            igs.append(_dot(blk, wi_ref[h]))
        return uc, jnp.concatenate(rs, axis=-1), jnp.concatenate(igs, axis=-1)

    def recurrence(slot, sub, uc, r_pre, ig_pre):
        hblk = hcar_scr[sub]
        gate = gu_scr[slot, sub, :, 0:w]
        r = _sigmoid(r_pre + br_ref[...])
        ig = _sigmoid(ig_pre + bi_ref[...])
        log_a = (-LRU_C * r) * softplus
        a0 = jnp.exp(log_a)
        th = jnp.tanh(log_a)
        b0 = (jnp.sqrt(-2.0 * th) * lax.rsqrt(1.0 - th)) * (ig * uc)

        a8, b8 = [], []
        for lt in range(LRU_NLT):
            ls = slice(lt * LANES, (lt + 1) * LANES)
            a_scr[sub, 0, lt, pad:pad + ts, :] = a0[:, ls]
            b_scr[sub, 0, lt, pad:pad + ts, :] = b0[:, ls]
            for k, d in enumerate((1, 2)):
                a_cur = a_scr[sub, k, lt, sl:pad + ts, :]
                b_cur = b_scr[sub, k, lt, sl:pad + ts, :]
                a_scr[sub, k + 1, lt, sl:pad + ts, :] = a_cur * a_scr[sub, k, lt, sl - d:pad + ts - d, :]
                b_scr[sub, k + 1, lt, sl:pad + ts, :] = (a_cur * b_scr[sub, k, lt, sl - d:pad + ts - d, :]
                                                        + b_cur)
            a_cur = a_scr[sub, 2, lt, pad:pad + ts, :]
            a8.append(a_cur * a_scr[sub, 2, lt, pad - 4:pad - 4 + ts, :])
            b8.append(a_cur * b_scr[sub, 2, lt, pad - 4:pad - 4 + ts, :] + b_scr[sub, 2, lt, pad:pad + ts, :])
        a8 = jnp.concatenate(a8, axis=1)
        b8 = jnp.concatenate(b8, axis=1)

        hs = []
        for j in range(ts // sl):
            hblk = a8[j * sl:(j + 1) * sl] * hblk + b8[j * sl:(j + 1) * sl]
            hs.append(hblk)
        h = jnp.concatenate(hs, axis=0)

        gelu = gate * (0.5 * (1.0 + jnp.tanh(c0 * (gate + 0.044715 * (gate * gate * gate)))))
        hcar_scr[sub] = hblk
        return (h * gelu).astype(BF16)

    def out_proj(sub, yb):
        return ALPHA * x_ref[sub] + _dot(yb, wo_ref[...])

    def step(slot):
        zs = []
        for sub in range(LRU_NSUB):
            uc, r_pre, ig_pre = conv_gates(slot, sub)
            project_next(slot, sub)
            zs.append(out_proj(sub, recurrence(slot, sub, uc, r_pre, ig_pre)))
            if sub > 0:
                o_ref[sub - 1] = _layer_norm(zs[sub - 1], g_ref[...], b_ref[...])
        o_ref[LRU_NSUB - 1] = _layer_norm(zs[-1], g_ref[...], b_ref[...])

    for parity in range(2):
        pl.when(lax.rem(i, 2) == parity)(functools.partial(step, parity))


def _rglru_layer(x, win, cw, cb, wr, br, wi, bi, lam, wo, g, b):
    bsz, s, d = x.shape
    tm = LRU_SUB
    w = LRU_WIDTH
    last = s // tm - 1
    return pl.pallas_call(
        _rglru_kernel,
        grid=(bsz // LRU_NSUB, s // tm),
        in_specs=[pl.BlockSpec((LRU_NSUB, tm, d), lambda bi_, i: (bi_, i, 0)),
                  pl.BlockSpec((LRU_NSUB, tm, d), lambda bi_, i: (bi_, jnp.minimum(i + 1, last), 0)),
                  _resident((d, 2 * w)), _resident((LRU_CONV, w)), _resident((1, w)),
                  _resident((LRU_BLOCKS, LRU_BLOCK_W, LRU_BLOCK_W)), _resident((1, w)),
                  _resident((LRU_BLOCKS, LRU_BLOCK_W, LRU_BLOCK_W)), _resident((1, w)),
                  _resident((1, w)), _resident((w, d)), _resident((1, d)), _resident((1, d))],
        out_specs=pl.BlockSpec((LRU_NSUB, tm, d), lambda bi_, i: (bi_, i, 0)),
        out_shape=jax.ShapeDtypeStruct((bsz, s, d), F32),
        scratch_shapes=[pltpu.VMEM((LRU_NSUB, LRU_NLT, SUBLANES + LRU_SUB, LANES), F32),
                        pltpu.VMEM((LRU_NSUB, 3, LRU_NLT, LRU_PAD + LRU_SUB, LANES), F32),
                        pltpu.VMEM((LRU_NSUB, 3, LRU_NLT, LRU_PAD + LRU_SUB, LANES), F32),
                        pltpu.VMEM((LRU_NSUB, SUBLANES, w), F32),
                        pltpu.VMEM((2, LRU_NSUB, tm, 2 * w), F32)],
        compiler_params=_params(("arbitrary", "arbitrary")),
        name="rglru_layer",
    )(x, x, win, cw, cb, wr, br, wi, bi, lam, wo, g, b)


MLA_TM = 1024
MLA_TQ = 1024
MLA_HALF = MLA_TQ // 2
MLA_TK = 512
C_DOWN_PAD = 768
MLA_Q_SCALE = (C_NOPE + C_ROPE) ** -0.5 * math.log2(math.e)


def _mla_proj_kernel(x_ref, cos_ref, sin_ref, wd_ref, qn_ref, kvn_ref, wuq_ref, wukv_ref,
                     q_ref, k_ref, v_ref):
    tm = MLA_TM
    xb = x_ref[0].astype(BF16)
    c = _dot(xb, wd_ref[...])
    cq = c[:, :C_Q_RANK]
    ckv = c[:, C_Q_RANK:C_Q_RANK + C_KV_RANK]
    kr_slab = c[:, C_Q_RANK + C_KV_RANK:]
    cq = cq * lax.rsqrt(jnp.mean(cq * cq, axis=-1, keepdims=True) + RMS_EPS) * qn_ref[...]
    ckv = ckv * lax.rsqrt(jnp.mean(ckv * ckv, axis=-1, keepdims=True) + RMS_EPS) * kvn_ref[...]
    q = _dot(cq.astype(BF16), wuq_ref[...]) * MLA_Q_SCALE
    kv = _dot(ckv.astype(BF16), wukv_ref[...])

    cos = cos_ref[...]
    sin = sin_ref[...]
    lane = lax.broadcasted_iota(jnp.int32, (tm, LANES), 1)
    first_half = (lane & 32) == 0
    low = lane < 64
    nn = C_HEADS * C_NOPE

    kr = _rope_slab(kr_slab, cos, sin, first_half)
    kr_pair = (kr.astype(BF16), pltpu.roll(kr, 64, 1).astype(BF16))
    for pair in range(C_HEADS // 2):
        qr = _rope_slab(q[:, nn + pair * LANES:nn + (pair + 1) * LANES], cos, sin, first_half)
        for half in range(2):
            h = 2 * pair + half
            qr_h = jnp.where(low, qr, 0.0) if half == 0 else jnp.where(low, 0.0, qr)
            q_ref[0, h, :, 0:C_NOPE] = q[:, h * C_NOPE:(h + 1) * C_NOPE].astype(BF16)
            q_ref[0, h, :, C_NOPE:C_QK_PAD] = qr_h.astype(BF16)
            k_ref[0, h, :, 0:C_NOPE] = kv[:, h * C_NOPE:(h + 1) * C_NOPE].astype(BF16)
            k_ref[0, h, :, C_NOPE:C_QK_PAD] = kr_pair[half]
            v_ref[0, h] = kv[:, nn + h * C_V:nn + (h + 1) * C_V].astype(BF16)


def _mla_proj(x, cos, sin, wd, qn, kvn, wuq, wukv):
    bsz, s, d = x.shape
    tm = MLA_TM
    qk_shape = jax.ShapeDtypeStruct((bsz, C_HEADS, s, C_QK_PAD), BF16)
    v_shape = jax.ShapeDtypeStruct((bsz, C_HEADS, s, C_V), BF16)
    return pl.pallas_call(
        _mla_proj_kernel,
        grid=(bsz, s // tm),
        in_specs=[pl.BlockSpec((1, tm, d), lambda bi, i: (bi, i, 0)),
                  pl.BlockSpec((tm, LANES), lambda bi, i: (i, 0)),
                  pl.BlockSpec((tm, LANES), lambda bi, i: (i, 0)),
                  _resident(wd.shape), _resident(qn.shape), _resident(kvn.shape),
                  _resident(wuq.shape), _resident(wukv.shape)],
        out_specs=[pl.BlockSpec((1, C_HEADS, tm, C_QK_PAD), lambda bi, i: (bi, 0, i, 0)),
                   pl.BlockSpec((1, C_HEADS, tm, C_QK_PAD), lambda bi, i: (bi, 0, i, 0)),
                   pl.BlockSpec((1, C_HEADS, tm, C_V), lambda bi, i: (bi, 0, i, 0))],
        out_shape=[qk_shape, qk_shape, v_shape],
        compiler_params=_params(("parallel", "parallel")),
        name="mla_proj",
    )(x, cos, sin, wd, qn, kvn, wuq, wukv)


def _mla_attn_kernel(q_ref, qn_ref, k_ref, v_ref, o_ref, m_scr, acc_scr, s_scr):
    i = pl.program_id(2)
    tk, hq = MLA_TK, MLA_HALF

    m_scr[...] = jnp.full_like(m_scr, -jnp.inf)
    acc_scr[...] = jnp.zeros_like(acc_scr)
    ones = jnp.ones((tk, LANES), BF16)

    def scores(slot, hf, j, queries=q_ref):
        k0 = pl.multiple_of(j * tk, tk)
        s_scr[slot, hf] = _dot_nt(queries[0, 0, hf * hq:(hf + 1) * hq, :], k_ref[0, 0, pl.ds(k0, tk), :])

    def consume(slot, hf, j, diagonal):
        k0 = pl.multiple_of(j * tk, tk)
        vv = jnp.concatenate([v_ref[0, 0, pl.ds(k0, tk), :], ones], axis=1)
        s = s_scr[slot, hf]
        if diagonal:
            qpos = lax.broadcasted_iota(jnp.int32, (hq, tk), 0)
            kpos = lax.broadcasted_iota(jnp.int32, (hq, tk), 1)
            s = jnp.where(kpos <= qpos, s, NEG)
        m_prev = m_scr[hf]
        m_next = jnp.maximum(m_prev, jnp.max(s, axis=-1, keepdims=True))
        p = jnp.exp2(s - jnp.tile(m_next, (1, tk // LANES)))
        corr = jnp.exp2(m_prev - m_next)
        acc_scr[hf] = jnp.tile(corr, (1, 2)) * acc_scr[hf] + _dot(p.astype(BF16), vv)
        m_scr[hf] = m_next

    @pl.when(i == 0)
    def _():
        scores(0, 0, 0)
        scores(0, 1, 0)

    def chunk_pair(j):
        scores(1, 0, j + 1)
        scores(1, 1, j + 1)
        consume(0, 0, j, False)
        consume(0, 1, j, False)
        scores(0, 0, j + 2)
        scores(0, 1, j + 2)
        consume(1, 0, j + 1, False)
        consume(1, 1, j + 1, False)

    done = 0
    for bit in (2, 1, 0):
        npairs = 1 << bit

        @pl.when((i >> bit) & 1 == 1)
        def _(npairs=npairs, first=done):
            for p in range(npairs):
                chunk_pair(2 * (first + p))

        done = done + jnp.where((i >> bit) & 1 == 1, npairs, 0)

    scores(1, 1, 2 * i + 1)
    consume(0, 0, 2 * i, True)
    consume(0, 1, 2 * i, False)
    scores(0, 0, 0, queries=qn_ref)
    scores(0, 1, 0, queries=qn_ref)
    consume(1, 1, 2 * i + 1, True)
    for hf in range(2):
        acc = acc_scr[hf]
        o_ref[0, hf * hq:(hf + 1) * hq, :] = (acc[:, :C_V] / acc[:, C_V:]).astype(BF16)


def _mla_attn(q, k, v):
    bsz, nh, s, _ = q.shape
    tq = MLA_TQ
    last = s // tq - 1
    assert last < 8, "the kernel decomposes the tile index into three bits"
    return pl.pallas_call(
        _mla_attn_kernel,
        grid=(bsz, nh, s // tq),
        in_specs=[pl.BlockSpec((1, 1, tq, C_QK_PAD), lambda b, h, i: (b, h, i, 0)),
                  pl.BlockSpec((1, 1, tq, C_QK_PAD), lambda b, h, i: (b, h, jnp.minimum(i + 1, last), 0)),
                  pl.BlockSpec((1, 1, s, C_QK_PAD), lambda b, h, i: (b, h, 0, 0)),
                  pl.BlockSpec((1, 1, s, C_V), lambda b, h, i: (b, h, 0, 0))],
        out_specs=pl.BlockSpec((1, tq, C_V), lambda b, h, i: (b, i, h)),
        out_shape=jax.ShapeDtypeStruct((bsz, s, nh * C_V), BF16),
        scratch_shapes=[pltpu.VMEM((2, MLA_HALF, LANES), F32),
                        pltpu.VMEM((2, MLA_HALF, 2 * C_V), F32),
                        pltpu.VMEM((2, 2, MLA_HALF, MLA_TK), F32)],
        compiler_params=_params(("parallel", "parallel", "arbitrary")),
        name="mla_attn",
    )(q, q, k, v)


X_TM = 1024
X_LN_ROWS = 256


def _xattn_kernel(*refs, mixer_prologue):
    tm = X_TM
    if mixer_prologue:
        a_ref, wa_ref, ga_ref, ba_ref, x_ref, wq_ref, mkv_ref, wo_ref, g_ref, b_ref, o_ref = refs
        x = _layer_norm(ALPHA * x_ref[0] + _dot(a_ref[0], wa_ref[...]), ga_ref[...], ba_ref[...])
    else:
        x_ref, wq_ref, mkv_ref, wo_ref, g_ref, b_ref, o_ref = refs
        x = x_ref[0]
    q = (_dot(x.astype(BF16), wq_ref[...]) * (X_HEAD_DIM ** -0.5 * LOG2E)).astype(BF16)
    heads = [(h * X_HEAD_DIM, (h + 1) * X_HEAD_DIM) for h in range(X_HEADS)]
    scores = [_dot_nt(q[:, lo:hi], mkv_ref[0, :, lo:hi]) for lo, hi in heads]
    probs, inv_denoms = [], []
    for s in scores:
        p = jnp.exp2(s - jnp.max(s, axis=-1, keepdims=True))
        inv_denoms.append(1.0 / jnp.sum(p, axis=-1, keepdims=True))
        probs.append(p.astype(BF16))
    outs = [(_dot(probs[h], mkv_ref[0, :, D_MODEL + lo:D_MODEL + hi]) * inv_denoms[h]).astype(BF16)
            for h, (lo, hi) in enumerate(heads)]
    o = jnp.concatenate(outs, axis=-1)
    for r0 in range(0, tm, X_LN_ROWS):
        rows = slice(r0, r0 + X_LN_ROWS)
        y = _dot(o[rows], wo_ref[...])
        o_ref[0, rows, :] = _layer_norm(ALPHA * x[rows] + y, g_ref[...], b_ref[...])


def _xattn_layer(x, layer, wq, mkv, wo, g, b, mixer=None):
    bsz, s, d = x.shape
    tm = X_TM
    tile = pl.BlockSpec((1, tm, d), lambda bi, i: (bi, i, 0))
    in_specs = [tile, _resident((d, d), layer),
                pl.BlockSpec((1, MEM_LEN, 2 * d), lambda bi, i: (bi, 0, 0)),
                _resident((d, d), layer), _resident((1, d)), _resident((1, d))]
    args = (x, wq, mkv, wo, g, b)
    if mixer is not None:
        in_specs = [tile, _resident((d, d)), _resident((1, d)), _resident((1, d))] + in_specs
        args = tuple(mixer) + args
    return pl.pallas_call(
        functools.partial(_xattn_kernel, mixer_prologue=mixer is not None),
        grid=(bsz, s // tm),
        in_specs=in_specs,
        out_specs=tile,
        out_shape=jax.ShapeDtypeStruct((bsz, s, d), F32),
        compiler_params=_params(("parallel", "parallel")),
        name="xattn_layer",
    )(*args)


FFN_TM = 512
FFN_LN_ROWS = 256
FFN_DOWN_GROUPS = (6, FFN_NCHUNK)


def _ffn_kernel(x_ref, wup_ref, cw_ref, cb_ref, wdn_ref, g_ref, b_ref, o_ref, hprev_scr, act_scr,
                acc_scr, hstage_scr):
    i = pl.program_id(1)
    tm = FFN_TM

    @pl.when(i == 0)
    def _():
        hprev_scr[...] = jnp.zeros_like(hprev_scr)

    x = x_ref[0]
    xb = x.astype(BF16)

    def up_conv(col0, slot):
        cols = slice(col0, col0 + FFN_CHUNK)
        h = _dot(xb, wup_ref[:, cols])
        prev = hprev_scr[:, cols]
        hprev_scr[:, cols] = h[tm - SUBLANES:, :]
        cw = cw_ref[:, cols]
        cb = cb_ref[:, cols]
        outs = []
        for lt in range(FFN_CHUNK // LANES):
            ls = slice(lt * LANES, (lt + 1) * LANES)
            hs = hstage_scr.at[slot, lt]
            hs[0:SUBLANES, :] = prev[:, ls]
            hs[SUBLANES:SUBLANES + tm, :] = h[:, ls]
            y = hs[SUBLANES - 2:SUBLANES - 2 + tm, :] * cw[0:1, ls]
            y = y + hs[SUBLANES - 1:SUBLANES - 1 + tm, :] * cw[1:2, ls]
            y = y + h[:, ls] * cw[2:3, ls]
            outs.append(y + cb[:, ls])
        return jnp.concatenate(outs, axis=1)

    def down(rows, c0, c1):
        ks = slice(c0 * FFN_CHUNK, c1 * FFN_CHUNK)
        return _dot(act_scr[rows, ks], wdn_ref[ks, :])

    everything = slice(0, tm)
    for c in range(FFN_NCHUNK):
        gt = up_conv(c * FFN_CHUNK, 2 * (c % 2))
        val = up_conv(D_FF + c * FFN_CHUNK, 2 * (c % 2) + 1)
        act_scr[:, c * FFN_CHUNK:(c + 1) * FFN_CHUNK] = ((gt * _sigmoid(gt)) * val).astype(BF16)
        if c + 1 in FFN_DOWN_GROUPS[:-1]:
            gi = FFN_DOWN_GROUPS.index(c + 1)
            if gi == 0:
                acc_scr[...] = down(everything, 0, c + 1)
            else:
                acc_scr[...] += down(everything, FFN_DOWN_GROUPS[gi - 1], c + 1)
    for r0 in range(0, tm, FFN_LN_ROWS):
        rows = slice(r0, r0 + FFN_LN_ROWS)
        out = acc_scr[rows, :] + down(rows, FFN_DOWN_GROUPS[-2], FFN_DOWN_GROUPS[-1])
        o_ref[0, rows, :] = _layer_norm(ALPHA * x[rows] + out, g_ref[...], b_ref[...])


def _ffn_layer(x, layer, wup, cw, cb, wdn, g, b):
    bsz, s, d = x.shape
    tm = FFN_TM
    return pl.pallas_call(
        _ffn_kernel,
        grid=(bsz, s // tm),
        in_specs=[pl.BlockSpec((1, tm, d), lambda bi, i: (bi, i, 0)),
                  _resident((d, 2 * D_FF), layer), _resident((FFN_CONV, 2 * D_FF)), _resident((1, 2 * D_FF)),
                  _resident((D_FF, d), layer), _resident((1, d)), _resident((1, d))],
        out_specs=pl.BlockSpec((1, tm, d), lambda bi, i: (bi, i, 0)),
        out_shape=jax.ShapeDtypeStruct((bsz, s, d), F32),
        scratch_shapes=[pltpu.VMEM((SUBLANES, 2 * D_FF), F32), pltpu.VMEM((tm, D_FF), BF16),
                        pltpu.VMEM((tm, d), F32),
                        pltpu.VMEM((4, FFN_CHUNK // LANES, SUBLANES + tm, LANES), F32)],
        compiler_params=_params(("arbitrary", "arbitrary")),
        name="ffn_layer",
    )(x, wup, cw, cb, wdn, g, b)


def kernel(x, mem, a_w_qkv, a_sinks, a_w_o, b_w_in, b_conv_w, b_conv_b, b_w_rgate, b_b_rgate, b_w_igate, b_b_igate, b_lambda, b_w_o, c_w_down, c_q_norm, c_kv_norm, c_w_uq, c_w_ukv, c_w_o, mem_w_kv, x_w_q, x_w_o, f_w_up, f_conv_w, f_conv_b, f_w_down, ln_g, ln_b):
    bsz, s, d = x.shape
    cos, sin = _rope_tables(s)

    mkv = _matmul(mem.reshape(bsz * MEM_LEN, d), mem_w_kv.astype(BF16), 512, BF16)
    mkv = mkv.reshape(bsz, MEM_LEN, 2 * d)

    def row(v):
        return v.reshape(1, -1)

    xq_all, xo_all = x_w_q.astype(BF16), x_w_o.astype(BF16)
    fup_all, fdn_all = f_w_up.astype(BF16), f_w_down.astype(BF16)
    aqkv_all, ao_all = a_w_qkv.astype(BF16), a_w_o.astype(BF16)

    for i in range(DEPTH):
        kind, j = i % N_MIXERS, i // N_MIXERS
        g0, b0 = row(ln_g[i, 0]), row(ln_b[i, 0])
        mixer = None
        if kind == 0:
            x = _swa_layer(x, j, a_sinks[j], cos, sin, aqkv_all, ao_all, g0, b0)
        elif kind == 1:
            x = _rglru_layer(x, b_w_in[j].astype(BF16), b_conv_w[j], row(b_conv_b[j]),
                             b_w_rgate[j].astype(BF16), row(b_b_rgate[j]),
                             b_w_igate[j].astype(BF16), row(b_b_igate[j]),
                             row(b_lambda[j]), b_w_o[j].astype(BF16), g0, b0)
        else:
            wd = jnp.pad(c_w_down[j], ((0, 0), (0, C_DOWN_PAD - c_w_down.shape[-1]))).astype(BF16)
            wuq = c_w_uq[j].reshape(C_Q_RANK, C_HEADS, C_NOPE + C_ROPE)
            wuq = jnp.concatenate([wuq[:, :, :C_NOPE].reshape(C_Q_RANK, -1),
                                   wuq[:, :, C_NOPE:].reshape(C_Q_RANK, -1)], axis=-1).astype(BF16)
            wukv = c_w_ukv[j].reshape(C_KV_RANK, C_HEADS, C_NOPE + C_V)
            wukv = jnp.concatenate([wukv[:, :, :C_NOPE].reshape(C_KV_RANK, -1),
                                    wukv[:, :, C_NOPE:].reshape(C_KV_RANK, -1)], axis=-1).astype(BF16)
            q, k, v = _mla_proj(x, cos, sin, wd, row(c_q_norm[j]), row(c_kv_norm[j]), wuq, wukv)
            mixer = (_mla_attn(q, k, v), c_w_o[j].astype(BF16), g0, b0)
        x = _xattn_layer(x, i, xq_all, mkv, xo_all, row(ln_g[i, 1]), row(ln_b[i, 1]), mixer=mixer)
        x = _ffn_layer(x, i, fup_all, f_conv_w[i], row(f_conv_b[i]), fdn_all, row(ln_g[i, 2]), row(ln_b[i, 2]))
    return x
```

```python
import functools
import math

import jax
import jax.numpy as jnp
import numpy as np
from jax import lax
from jax.experimental import pallas as pl
from jax.experimental.pallas import tpu as pltpu

D_MODEL = 1024
DEPTH = 4
N_MIXERS = 3
MEM_LEN = 256
BLOCK = 128
ROPE_THETA = 10000.0
NEG = -1e30
LN_EPS = 1e-5
RMS_EPS = 1e-6

A_HEADS = 16
A_KV_HEADS = 4
A_HEAD_DIM = 64

LRU_WIDTH = D_MODEL
LRU_BLOCKS = 4
LRU_BLOCK_W = LRU_WIDTH // LRU_BLOCKS
LRU_CONV = 4
LRU_C = 8.0

C_HEADS = 8
C_NOPE = 128
C_ROPE = 64
C_V = 128
C_Q_RANK = 384
C_KV_RANK = 256
C_QK_PAD = 256

X_HEADS = 4
X_HEAD_DIM = D_MODEL // X_HEADS

D_FF = 2816
FFN_CONV = 3
FFN_CHUNK = 256
FFN_NCHUNK = D_FF // FFN_CHUNK

ALPHA = (2.0 * DEPTH) ** 0.25

LANES = 128
SUBLANES = 8
VMEM_LIMIT = 56 * 1024 * 1024

BF16 = jnp.bfloat16
F32 = jnp.float32

NT_DIMS = (((1,), (1,)), ((), ()))
LOG2E = math.log2(math.e)


def _dot(a, b):
    return jnp.dot(a, b, preferred_element_type=F32)


def _dot_nt(a, b):
    return lax.dot_general(a, b, NT_DIMS, preferred_element_type=F32)


def _layer_norm(z, g, b):
    mu = jnp.mean(z, axis=-1, keepdims=True)
    zc = z - mu
    var = jnp.mean(zc * zc, axis=-1, keepdims=True)
    return zc * lax.rsqrt(var + LN_EPS) * g + b


def _sigmoid(z):
    return 0.5 * jnp.tanh(0.5 * z) + 0.5


def _resident(shape, layer=None):
    nd = len(shape)
    if layer is None:
        return pl.BlockSpec(shape, lambda *_: (0,) * nd, pipeline_mode=pl.Buffered(1))
    return pl.BlockSpec((None,) + tuple(shape), lambda *_: (layer,) + (0,) * nd, pipeline_mode=pl.Buffered(1))


def _params(sem):
    return pltpu.CompilerParams(dimension_semantics=sem, vmem_limit_bytes=VMEM_LIMIT)


def _matmul_kernel(a_ref, w_ref, o_ref):
    o_ref[...] = _dot(a_ref[...].astype(BF16), w_ref[...]).astype(o_ref.dtype)


def _matmul(a, w, tm, out_dtype):
    t, k = a.shape
    n = w.shape[1]
    return pl.pallas_call(
        _matmul_kernel,
        grid=(t // tm,),
        in_specs=[pl.BlockSpec((tm, k), lambda i: (i, 0)), _resident((k, n))],
        out_specs=pl.BlockSpec((tm, n), lambda i: (i, 0)),
        out_shape=jax.ShapeDtypeStruct((t, n), out_dtype),
        compiler_params=_params(("parallel",)),
        name="matmul",
    )(a, w)


def _rope_slab(slab, cos, sin_signed, first_half):
    rot = jnp.where(first_half, pltpu.roll(slab, LANES - 32, 1), pltpu.roll(slab, 32, 1))
    return slab * cos + rot * sin_signed


def _rope_tables(seq):
    inv = 1.0 / (ROPE_THETA ** (jnp.arange(0, A_HEAD_DIM, 2, dtype=F32) / A_HEAD_DIM))
    ang = jnp.arange(seq, dtype=F32)[:, None] * inv[None, :]
    cos, sin = jnp.cos(ang), jnp.sin(ang)
    return (jnp.concatenate([cos, cos, cos, cos], axis=-1),
            jnp.concatenate([-sin, sin, -sin, sin], axis=-1))


SWA_TM = 512
SWA_NBLK = SWA_TM // BLOCK


def _swa_kernel(sinks_ref, x_ref, cos_ref, sin_ref, wqkv_ref, wo_ref, g_ref, b_ref, o_ref,
                kext_scr, vext_scr, q_scr, o_scr):
    i = pl.program_id(1)
    tm = SWA_TM

    @pl.when(i == 0)
    def _():
        kext_scr[:, 0:BLOCK, :] = jnp.zeros((2 * A_KV_HEADS, BLOCK, LANES), BF16)
        vext_scr[:, 0:BLOCK, :] = jnp.zeros((2 * A_KV_HEADS, BLOCK, LANES), BF16)

    @pl.when(i > 0)
    def _():
        kext_scr[:, 0:BLOCK, :] = kext_scr[:, tm:tm + BLOCK, :]
        vext_scr[:, 0:BLOCK, :] = vext_scr[:, tm:tm + BLOCK, :]

    x = x_ref[0]
    qkv = _dot(x.astype(BF16), wqkv_ref[...])
    cos = cos_ref[...]
    sin = sin_ref[...]
    lane = lax.broadcasted_iota(jnp.int32, (tm, LANES), 1)
    first_half = (lane & 32) == 0
    low = lane < 64

    nq = A_HEADS * A_HEAD_DIM
    nkv = A_KV_HEADS * A_HEAD_DIM
    for j in range(nq // LANES):
        slab = _rope_slab(qkv[:, j * LANES:(j + 1) * LANES], cos, sin, first_half)
        q_scr[:, j * LANES:(j + 1) * LANES] = (slab * (A_HEAD_DIM ** -0.5 * LOG2E)).astype(BF16)

    for m in range(nkv // LANES):
        kslab = _rope_slab(qkv[:, nq + m * LANES:nq + (m + 1) * LANES], cos, sin, first_half)
        vslab = qkv[:, nq + nkv + m * LANES:nq + nkv + (m + 1) * LANES]
        for slab, ext in ((kslab, kext_scr), (vslab, vext_scr)):
            swapped = pltpu.roll(slab, 64, 1)
            h0, h1 = 2 * m, 2 * m + 1
            ext[2 * h0, BLOCK:BLOCK + tm, :] = jnp.where(low, slab, 0.0).astype(BF16)
            ext[2 * h0 + 1, BLOCK:BLOCK + tm, :] = jnp.where(low, 0.0, swapped).astype(BF16)
            ext[2 * h1, BLOCK:BLOCK + tm, :] = jnp.where(low, swapped, 0.0).astype(BF16)
            ext[2 * h1 + 1, BLOCK:BLOCK + tm, :] = jnp.where(low, 0.0, slab).astype(BF16)

    qi = lax.broadcasted_iota(jnp.int32, (2 * BLOCK, 4 * BLOCK), 0) & (BLOCK - 1)
    kj = lax.broadcasted_iota(jnp.int32, (2 * BLOCK, 4 * BLOCK), 1) & (2 * BLOCK - 1)
    dist = qi + BLOCK - kj
    band = (dist >= 0) & (dist < BLOCK)
    band_first = band & ((kj >= BLOCK) | (i > 0))
    top = lax.broadcasted_iota(jnp.int32, (2 * BLOCK, 1), 0) < BLOCK
    low2 = lax.broadcasted_iota(jnp.int32, (2 * BLOCK, LANES), 1) < 64

    def window(ext, h, r0):
        return jnp.concatenate([ext[2 * h, r0:r0 + 2 * BLOCK, :], ext[2 * h + 1, r0:r0 + 2 * BLOCK, :]],
                               axis=0)

    for n in range(SWA_NBLK):
        mask = band_first if n == 0 else band
        r0 = n * BLOCK
        scores = []
        for h in range(A_KV_HEADS):
            qs = jnp.concatenate([q_scr[r0:r0 + BLOCK, (2 * h) * LANES:(2 * h + 1) * LANES],
                                  q_scr[r0:r0 + BLOCK, (2 * h + 1) * LANES:(2 * h + 2) * LANES]], axis=0)
            scores.append(_dot_nt(qs, window(kext_scr, h, r0)))
        probs, inv_denoms = [], []
        for h in range(A_KV_HEADS):
            s = jnp.where(mask, scores[h], NEG)
            ps, rs = [], []
            for half in range(2):
                sh = s[:, half * 2 * BLOCK:(half + 1) * 2 * BLOCK]
                sink = jnp.where(top, sinks_ref[4 * h + half] * LOG2E, sinks_ref[4 * h + 2 + half] * LOG2E)
                mx = jnp.maximum(jnp.max(sh, axis=-1, keepdims=True), sink)
                p = jnp.exp2(sh - mx)
                rs.append(1.0 / (jnp.sum(p, axis=-1, keepdims=True) + jnp.exp2(sink - mx)))
                ps.append(p.astype(BF16))
            probs.append(jnp.concatenate(ps, axis=1))
            inv_denoms.append(jnp.where(low2, rs[0], rs[1]))
        for h in range(A_KV_HEADS):
            o = _dot(probs[h], window(vext_scr, h, r0)) * inv_denoms[h]
            o_scr[r0:r0 + BLOCK, (2 * h) * LANES:(2 * h + 1) * LANES] = o[:BLOCK].astype(BF16)
            o_scr[r0:r0 + BLOCK, (2 * h + 1) * LANES:(2 * h + 2) * LANES] = o[BLOCK:].astype(BF16)

    for r0 in range(0, tm, tm // 2):
        rows = slice(r0, r0 + tm // 2)
        y = _dot(o_scr[rows, :], wo_ref[...])
        o_ref[0, rows, :] = _layer_norm(ALPHA * x[rows] + y, g_ref[...], b_ref[...])


def _swa_layer(x, layer, sinks, cos, sin, wqkv, wo, g, b):
    bsz, s, d = x.shape
    tm = SWA_TM
    nqkv = wqkv.shape[-1]
    return pl.pallas_call(
        _swa_kernel,
        grid=(bsz, s // tm),
        in_specs=[pl.BlockSpec(memory_space=pltpu.SMEM),
                  pl.BlockSpec((1, tm, d), lambda bi, i: (bi, i, 0)),
                  pl.BlockSpec((tm, LANES), lambda bi, i: (i, 0)),
                  pl.BlockSpec((tm, LANES), lambda bi, i: (i, 0)),
                  _resident((d, nqkv), layer), _resident((d, d), layer), _resident((1, d)), _resident((1, d))],
        out_specs=pl.BlockSpec((1, tm, d), lambda bi, i: (bi, i, 0)),
        out_shape=jax.ShapeDtypeStruct((bsz, s, d), F32),
        scratch_shapes=[pltpu.VMEM((2 * A_KV_HEADS, tm + BLOCK, LANES), BF16),
                        pltpu.VMEM((2 * A_KV_HEADS, tm + BLOCK, LANES), BF16),
                        pltpu.VMEM((tm, d), BF16),
                        pltpu.VMEM((tm, d), BF16)],
        compiler_params=_params(("arbitrary", "arbitrary")),
        name="swa_layer",
    )(sinks, x, cos, sin, wqkv, wo, g, b)


LRU_SUB = 256
LRU_NSUB = 2
LRU_NLT = LRU_WIDTH // LANES
LRU_PAD = 2 * SUBLANES


def _rglru_kernel(x_ref, xn_ref, win_ref, cw_ref, cb_ref, wr_ref, br_ref, wi_ref, bi_ref, lam_ref, wo_ref,
                  g_ref, b_ref, o_ref, ustage_scr, a_scr, b_scr, hcar_scr, gu_scr):
    i = pl.program_id(1)
    ts = LRU_SUB
    w = LRU_WIDTH
    pad = LRU_PAD
    sl = SUBLANES

    @pl.when(i == 0)
    def _():
        ustage_scr[:, :, 0:sl, :] = jnp.zeros((LRU_NSUB, LRU_NLT, sl, LANES), F32)
        a_scr[:, :, :, 0:pad, :] = jnp.ones((LRU_NSUB, 3, LRU_NLT, pad, LANES), F32)
        b_scr[:, :, :, 0:pad, :] = jnp.zeros((LRU_NSUB, 3, LRU_NLT, pad, LANES), F32)
        hcar_scr[...] = jnp.zeros_like(hcar_scr)

    @pl.when(i > 0)
    def _():
        ustage_scr[:, :, 0:sl, :] = ustage_scr[:, :, ts:ts + sl, :]
        a_scr[:, 0, :, sl:pad, :] = a_scr[:, 0, :, ts + sl:ts + pad, :]
        b_scr[:, 0, :, sl:pad, :] = b_scr[:, 0, :, ts + sl:ts + pad, :]

    cw = cw_ref[...]
    cb = cb_ref[...]
    z = -lam_ref[...]
    softplus = jnp.maximum(z, 0.0) + jnp.log1p(jnp.exp(-jnp.abs(z)))
    c0 = math.sqrt(2.0 / math.pi)

    @pl.when(i == 0)
    def _():
        for sub in range(LRU_NSUB):
            gu_scr[0, sub] = _dot(x_ref[sub].astype(BF16), win_ref[...])

    def project_next(slot, sub):
        gu_scr[1 - slot, sub] = _dot(xn_ref[sub].astype(BF16), win_ref[...])

    def conv_gates(slot, sub):
        u = gu_scr[slot, sub, :, w:2 * w]
        ucs = []
        for lt in range(LRU_NLT):
            ls = slice(lt * LANES, (lt + 1) * LANES)
            us = ustage_scr.at[sub, lt]
            us[sl:sl + ts, :] = u[:, ls]
            uc = us[sl - 3:sl - 3 + ts, :] * cw[0:1, ls]
            uc = uc + us[sl - 2:sl - 2 + ts, :] * cw[1:2, ls]
            uc = uc + us[sl - 1:sl - 1 + ts, :] * cw[2:3, ls]
            uc = uc + u[:, ls] * cw[3:4, ls]
            ucs.append(uc + cb[:, ls])
        uc = jnp.concatenate(ucs, axis=1)

        ucb = uc.astype(BF16)
        rs, igs = [], []
        for h in range(LRU_BLOCKS):
            blk = ucb[:, h * LRU_BLOCK_W:(h + 1) * LRU_BLOCK_W]
            rs.append(_dot(blk, wr_ref[h]))
            igs.append(_dot(blk, wi_ref[h]))
        return uc, jnp.concatenate(rs, axis=-1), jnp.concatenate(igs, axis=-1)

    def recurrence(slot, sub, uc, r_pre, ig_pre):
        hblk = hcar_scr[sub]
        gate = gu_scr[slot, sub, :, 0:w]
        r = _sigmoid(r_pre + br_ref[...])
        ig = _sigmoid(ig_pre + bi_ref[...])
        log_a = (-LRU_C * r) * softplus
        a0 = jnp.exp(log_a)
        th = jnp.tanh(log_a)
        b0 = (jnp.sqrt(-2.0 * th) * lax.rsqrt(1.0 - th)) * (ig * uc)

        a8, b8 = [], []
        for lt in range(LRU_NLT):
            ls = slice(lt * LANES, (lt + 1) * LANES)
            a_scr[sub, 0, lt, pad:pad + ts, :] = a0[:, ls]
            b_scr[sub, 0, lt, pad:pad + ts, :] = b0[:, ls]
            for k, d in enumerate((1, 2)):
                a_cur = a_scr[sub, k, lt, sl:pad + ts, :]
                b_cur = b_scr[sub, k, lt, sl:pad + ts, :]
                a_scr[sub, k + 1, lt, sl:pad + ts, :] = a_cur * a_scr[sub, k, lt, sl - d:pad + ts - d, :]
                b_scr[sub, k + 1, lt, sl:pad + ts, :] = (a_cur * b_scr[sub, k, lt, sl - d:pad + ts - d, :]
                                                        + b_cur)
            a_cur = a_scr[sub, 2, lt, pad:pad + ts, :]
            a8.append(a_cur * a_scr[sub, 2, lt, pad - 4:pad - 4 + ts, :])
            b8.append(a_cur * b_scr[sub, 2, lt, pad - 4:pad - 4 + ts, :] + b_scr[sub, 2, lt, pad:pad + ts, :])
        a8 = jnp.concatenate(a8, axis=1)
        b8 = jnp.concatenate(b8, axis=1)

        hs = []
        for j in range(ts // sl):
            hblk = a8[j * sl:(j + 1) * sl] * hblk + b8[j * sl:(j + 1) * sl]
            hs.append(hblk)
        h = jnp.concatenate(hs, axis=0)

        gelu = gate * (0.5 * (1.0 + jnp.tanh(c0 * (gate + 0.044715 * (gate * gate * gate)))))
        hcar_scr[sub] = hblk
        return (h * gelu).astype(BF16)

    def out_proj(sub, yb):
        return ALPHA * x_ref[sub] + _dot(yb, wo_ref[...])

    def step(slot):
        zs = []
        for sub in range(LRU_NSUB):
            uc, r_pre, ig_pre = conv_gates(slot, sub)
            project_next(slot, sub)
            zs.append(out_proj(sub, recurrence(slot, sub, uc, r_pre, ig_pre)))
            if sub > 0:
                o_ref[sub - 1] = _layer_norm(zs[sub - 1], g_ref[...], b_ref[...])
        o_ref[LRU_NSUB - 1] = _layer_norm(zs[-1], g_ref[...], b_ref[...])

    for parity in range(2):
        pl.when(lax.rem(i, 2) == parity)(functools.partial(step, parity))


def _rglru_layer(x, win, cw, cb, wr, br, wi, bi, lam, wo, g, b):
    bsz, s, d = x.shape
    tm = LRU_SUB
    w = LRU_WIDTH
    last = s // tm - 1
    return pl.pallas_call(
        _rglru_kernel,
        grid=(bsz // LRU_NSUB, s // tm),
        in_specs=[pl.BlockSpec((LRU_NSUB, tm, d), lambda bi_, i: (bi_, i, 0)),
                  pl.BlockSpec((LRU_NSUB, tm, d), lambda bi_, i: (bi_, jnp.minimum(i + 1, last), 0)),
                  _resident((d, 2 * w)), _resident((LRU_CONV, w)), _resident((1, w)),
                  _resident((LRU_BLOCKS, LRU_BLOCK_W, LRU_BLOCK_W)), _resident((1, w)),
                  _resident((LRU_BLOCKS, LRU_BLOCK_W, LRU_BLOCK_W)), _resident((1, w)),
                  _resident((1, w)), _resident((w, d)), _resident((1, d)), _resident((1, d))],
        out_specs=pl.BlockSpec((LRU_NSUB, tm, d), lambda bi_, i: (bi_, i, 0)),
        out_shape=jax.ShapeDtypeStruct((bsz, s, d), F32),
        scratch_shapes=[pltpu.VMEM((LRU_NSUB, LRU_NLT, SUBLANES + LRU_SUB, LANES), F32),
                        pltpu.VMEM((LRU_NSUB, 3, LRU_NLT, LRU_PAD + LRU_SUB, LANES), F32),
                        pltpu.VMEM((LRU_NSUB, 3, LRU_NLT, LRU_PAD + LRU_SUB, LANES), F32),
                        pltpu.VMEM((LRU_NSUB, SUBLANES, w), F32),
                        pltpu.VMEM((2, LRU_NSUB, tm, 2 * w), F32)],
        compiler_params=_params(("arbitrary", "arbitrary")),
        name="rglru_layer",
    )(x, x, win, cw, cb, wr, br, wi, bi, lam, wo, g, b)


MLA_TM = 1024
MLA_TQ = 1024
MLA_HALF = MLA_TQ // 2
MLA_TK = 512
C_DOWN_PAD = 768
MLA_Q_SCALE = (C_NOPE + C_ROPE) ** -0.5 * math.log2(math.e)


def _mla_proj_kernel(x_ref, cos_ref, sin_ref, wd_ref, qn_ref, kvn_ref, wuq_ref, wukv_ref,
                     q_ref, k_ref, v_ref):
    tm = MLA_TM
    xb = x_ref[0].astype(BF16)
    c = _dot(xb, wd_ref[...])
    cq = c[:, :C_Q_RANK]
    ckv = c[:, C_Q_RANK:C_Q_RANK + C_KV_RANK]
    kr_slab = c[:, C_Q_RANK + C_KV_RANK:]
    cq = cq * lax.rsqrt(jnp.mean(cq * cq, axis=-1, keepdims=True) + RMS_EPS) * qn_ref[...]
    ckv = ckv * lax.rsqrt(jnp.mean(ckv * ckv, axis=-1, keepdims=True) + RMS_EPS) * kvn_ref[...]
    q = _dot(cq.astype(BF16), wuq_ref[...]) * MLA_Q_SCALE
    kv = _dot(ckv.astype(BF16), wukv_ref[...])

    cos = cos_ref[...]
    sin = sin_ref[...]
    lane = lax.broadcasted_iota(jnp.int32, (tm, LANES), 1)
    first_half = (lane & 32) == 0
    low = lane < 64
    nn = C_HEADS * C_NOPE

    kr = _rope_slab(kr_slab, cos, sin, first_half)
    kr_pair = (kr.astype(BF16), pltpu.roll(kr, 64, 1).astype(BF16))
    for pair in range(C_HEADS // 2):
        qr = _rope_slab(q[:, nn + pair * LANES:nn + (pair + 1) * LANES], cos, sin, first_half)
        for half in range(2):
            h = 2 * pair + half
            qr_h = jnp.where(low, qr, 0.0) if half == 0 else jnp.where(low, 0.0, qr)
            q_ref[0, h, :, 0:C_NOPE] = q[:, h * C_NOPE:(h + 1) * C_NOPE].astype(BF16)
            q_ref[0, h, :, C_NOPE:C_QK_PAD] = qr_h.astype(BF16)
            k_ref[0, h, :, 0:C_NOPE] = kv[:, h * C_NOPE:(h + 1) * C_NOPE].astype(BF16)
            k_ref[0, h, :, C_NOPE:C_QK_PAD] = kr_pair[half]
            v_ref[0, h] = kv[:, nn + h * C_V:nn + (h + 1) * C_V].astype(BF16)


def _mla_proj(x, cos, sin, wd, qn, kvn, wuq, wukv):
    bsz, s, d = x.shape
    tm = MLA_TM
    qk_shape = jax.ShapeDtypeStruct((bsz, C_HEADS, s, C_QK_PAD), BF16)
    v_shape = jax.ShapeDtypeStruct((bsz, C_HEADS, s, C_V), BF16)
    return pl.pallas_call(
        _mla_proj_kernel,
        grid=(bsz, s // tm),
        in_specs=[pl.BlockSpec((1, tm, d), lambda bi, i: (bi, i, 0)),
                  pl.BlockSpec((tm, LANES), lambda bi, i: (i, 0)),
                  pl.BlockSpec((tm, LANES), lambda bi, i: (i, 0)),
                  _resident(wd.shape), _resident(qn.shape), _resident(kvn.shape),
                  _resident(wuq.shape), _resident(wukv.shape)],
        out_specs=[pl.BlockSpec((1, C_HEADS, tm, C_QK_PAD), lambda bi, i: (bi, 0, i, 0)),
                   pl.BlockSpec((1, C_HEADS, tm, C_QK_PAD), lambda bi, i: (bi, 0, i, 0)),
                   pl.BlockSpec((1, C_HEADS, tm, C_V), lambda bi, i: (bi, 0, i, 0))],
        out_shape=[qk_shape, qk_shape, v_shape],
        compiler_params=_params(("parallel", "parallel")),
        name="mla_proj",
    )(x, cos, sin, wd, qn, kvn, wuq, wukv)


def _mla_attn_kernel(q_ref, qn_ref, k_ref, v_ref, o_ref, m_scr, acc_scr, s_scr):
    i = pl.program_id(2)
    tk, hq = MLA_TK, MLA_HALF

    m_scr[...] = jnp.full_like(m_scr, -jnp.inf)
    acc_scr[...] = jnp.zeros_like(acc_scr)
    ones = jnp.ones((tk, LANES), BF16)

    def scores(slot, hf, j, queries=q_ref):
        k0 = pl.multiple_of(j * tk, tk)
        s_scr[slot, hf] = _dot_nt(queries[0, 0, hf * hq:(hf + 1) * hq, :], k_ref[0, 0, pl.ds(k0, tk), :])

    def consume(slot, hf, j, diagonal):
        k0 = pl.multiple_of(j * tk, tk)
        vv = jnp.concatenate([v_ref[0, 0, pl.ds(k0, tk), :], ones], axis=1)
        s = s_scr[slot, hf]
        if diagonal:
            qpos = lax.broadcasted_iota(jnp.int32, (hq, tk), 0)
            kpos = lax.broadcasted_iota(jnp.int32, (hq, tk), 1)
            s = jnp.where(kpos <= qpos, s, NEG)
        m_prev = m_scr[hf]
        m_next = jnp.maximum(m_prev, jnp.max(s, axis=-1, keepdims=True))
        p = jnp.exp2(s - jnp.tile(m_next, (1, tk // LANES)))
        corr = jnp.exp2(m_prev - m_next)
        acc_scr[hf] = jnp.tile(corr, (1, 2)) * acc_scr[hf] + _dot(p.astype(BF16), vv)
        m_scr[hf] = m_next

    @pl.when(i == 0)
    def _():
        scores(0, 0, 0)
        scores(0, 1, 0)

    def chunk_pair(j):
        scores(1, 0, j + 1)
        scores(1, 1, j + 1)
        consume(0, 0, j, False)
        consume(0, 1, j, False)
        scores(0, 0, j + 2)
        scores(0, 1, j + 2)
        consume(1, 0, j + 1, False)
        consume(1, 1, j + 1, False)

    done = 0
    for bit in (2, 1, 0):
        npairs = 1 << bit

        @pl.when((i >> bit) & 1 == 1)
        def _(npairs=npairs, first=done):
            for p in range(npairs):
                chunk_pair(2 * (first + p))

        done = done + jnp.where((i >> bit) & 1 == 1, npairs, 0)

    scores(1, 1, 2 * i + 1)
    consume(0, 0, 2 * i, True)
    consume(0, 1, 2 * i, False)
    scores(0, 0, 0, queries=qn_ref)
    scores(0, 1, 0, queries=qn_ref)
    consume(1, 1, 2 * i + 1, True)
    for hf in range(2):
        acc = acc_scr[hf]
        o_ref[0, hf * hq:(hf + 1) * hq, :] = (acc[:, :C_V] / acc[:, C_V:]).astype(BF16)


def _mla_attn(q, k, v):
    bsz, nh, s, _ = q.shape
    tq = MLA_TQ
    last = s // tq - 1
    assert last < 8, "the kernel decomposes the tile index into three bits"
    return pl.pallas_call(
        _mla_attn_kernel,
        grid=(bsz, nh, s // tq),
        in_specs=[pl.BlockSpec((1, 1, tq, C_QK_PAD), lambda b, h, i: (b, h, i, 0)),
                  pl.BlockSpec((1, 1, tq, C_QK_PAD), lambda b, h, i: (b, h, jnp.minimum(i + 1, last), 0)),
                  pl.BlockSpec((1, 1, s, C_QK_PAD), lambda b, h, i: (b, h, 0, 0)),
                  pl.BlockSpec((1, 1, s, C_V), lambda b, h, i: (b, h, 0, 0))],
        out_specs=pl.BlockSpec((1, tq, C_V), lambda b, h, i: (b, i, h)),
        out_shape=jax.ShapeDtypeStruct((bsz, s, nh * C_V), BF16),
        scratch_shapes=[pltpu.VMEM((2, MLA_HALF, LANES), F32),
                        pltpu.VMEM((2, MLA_HALF, 2 * C_V), F32),
                        pltpu.VMEM((2, 2, MLA_HALF, MLA_TK), F32)],
        compiler_params=_params(("parallel", "parallel", "arbitrary")),
        name="mla_attn",
    )(q, q, k, v)


X_TM = 1024
X_LN_ROWS = 256


def _xattn_kernel(*refs, mixer_prologue):
    tm = X_TM
    if mixer_prologue:
        a_ref, wa_ref, ga_ref, ba_ref, x_ref, wq_ref, mkv_ref, wo_ref, g_ref, b_ref, o_ref = refs
        x = _layer_norm(ALPHA * x_ref[0] + _dot(a_ref[0], wa_ref[...]), ga_ref[...], ba_ref[...])
    else:
        x_ref, wq_ref, mkv_ref, wo_ref, g_ref, b_ref, o_ref = refs
        x = x_ref[0]
    q = (_dot(x.astype(BF16), wq_ref[...]) * (X_HEAD_DIM ** -0.5 * LOG2E)).astype(BF16)
    heads = [(h * X_HEAD_DIM, (h + 1) * X_HEAD_DIM) for h in range(X_HEADS)]
    scores = [_dot_nt(q[:, lo:hi], mkv_ref[0, :, lo:hi]) for lo, hi in heads]
    probs, inv_denoms = [], []
    for s in scores:
        p = jnp.exp2(s - jnp.max(s, axis=-1, keepdims=True))
        inv_denoms.append(1.0 / jnp.sum(p, axis=-1, keepdims=True))
        probs.append(p.astype(BF16))
    outs = [(_dot(probs[h], mkv_ref[0, :, D_MODEL + lo:D_MODEL + hi]) * inv_denoms[h]).astype(BF16)
            for h, (lo, hi) in enumerate(heads)]
    o = jnp.concatenate(outs, axis=-1)
    for r0 in range(0, tm, X_LN_ROWS):
        rows = slice(r0, r0 + X_LN_ROWS)
        y = _dot(o[rows], wo_ref[...])
        o_ref[0, rows, :] = _layer_norm(ALPHA * x[rows] + y, g_ref[...], b_ref[...])


def _xattn_layer(x, layer, wq, mkv, wo, g, b, mixer=None):
    bsz, s, d = x.shape
    tm = X_TM
    tile = pl.BlockSpec((1, tm, d), lambda bi, i: (bi, i, 0))
    in_specs = [tile, _resident((d, d), layer),
                pl.BlockSpec((1, MEM_LEN, 2 * d), lambda bi, i: (bi, 0, 0)),
                _resident((d, d), layer), _resident((1, d)), _resident((1, d))]
    args = (x, wq, mkv, wo, g, b)
    if mixer is not None:
        in_specs = [tile, _resident((d, d)), _resident((1, d)), _resident((1, d))] + in_specs
        args = tuple(mixer) + args
    return pl.pallas_call(
        functools.partial(_xattn_kernel, mixer_prologue=mixer is not None),
        grid=(bsz, s // tm),
        in_specs=in_specs,
        out_specs=tile,
        out_shape=jax.ShapeDtypeStruct((bsz, s, d), F32),
        compiler_params=_params(("parallel", "parallel")),
        name="xattn_layer",
    )(*args)


FFN_TM = 1024
FFN_LN_ROWS = 256
FFN_DOWN_GROUPS = (6, FFN_NCHUNK)


def _ffn_kernel(x_ref, wup_ref, cw_ref, cb_ref, wdn_ref, g_ref, b_ref, o_ref, hprev_scr, act_scr,
                acc_scr, hstage_scr):
    i = pl.program_id(1)
    tm = FFN_TM

    @pl.when(i == 0)
    def _():
        hprev_scr[...] = jnp.zeros_like(hprev_scr)

    x = x_ref[0]
    xb = x.astype(BF16)

    def up_conv(col0, slot):
        cols = slice(col0, col0 + FFN_CHUNK)
        h = _dot(xb, wup_ref[:, cols])
        prev = hprev_scr[:, cols]
        hprev_scr[:, cols] = h[tm - SUBLANES:, :]
        cw = cw_ref[:, cols]
        cb = cb_ref[:, cols]
        outs = []
        for lt in range(FFN_CHUNK // LANES):
            ls = slice(lt * LANES, (lt + 1) * LANES)
            hs = hstage_scr.at[slot, lt]
            hs[0:SUBLANES, :] = prev[:, ls]
            hs[SUBLANES:SUBLANES + tm, :] = h[:, ls]
            y = hs[SUBLANES - 2:SUBLANES - 2 + tm, :] * cw[0:1, ls]
            y = y + hs[SUBLANES - 1:SUBLANES - 1 + tm, :] * cw[1:2, ls]
            y = y + h[:, ls] * cw[2:3, ls]
            outs.append(y + cb[:, ls])
        return jnp.concatenate(outs, axis=1)

    def down(rows, c0, c1):
        ks = slice(c0 * FFN_CHUNK, c1 * FFN_CHUNK)
        return _dot(act_scr[rows, ks], wdn_ref[ks, :])

    everything = slice(0, tm)
    for c in range(FFN_NCHUNK):
        gt = up_conv(c * FFN_CHUNK, 2 * (c % 2))
        val = up_conv(D_FF + c * FFN_CHUNK, 2 * (c % 2) + 1)
        act_scr[:, c * FFN_CHUNK:(c + 1) * FFN_CHUNK] = ((gt * _sigmoid(gt)) * val).astype(BF16)
        if c + 1 in FFN_DOWN_GROUPS[:-1]:
            gi = FFN_DOWN_GROUPS.index(c + 1)
            if gi == 0:
                acc_scr[...] = down(everything, 0, c + 1)
            else:
                acc_scr[...] += down(everything, FFN_DOWN_GROUPS[gi - 1], c + 1)
    for r0 in range(0, tm, FFN_LN_ROWS):
        rows = slice(r0, r0 + FFN_LN_ROWS)
        out = acc_scr[rows, :] + down(rows, FFN_DOWN_GROUPS[-2], FFN_DOWN_GROUPS[-1])
        o_ref[0, rows, :] = _layer_norm(ALPHA * x[rows] + out, g_ref[...], b_ref[...])


def _ffn_layer(x, layer, wup, cw, cb, wdn, g, b):
    bsz, s, d = x.shape
    tm = FFN_TM
    return pl.pallas_call(
        _ffn_kernel,
        grid=(bsz, s // tm),
        in_specs=[pl.BlockSpec((1, tm, d), lambda bi, i: (bi, i, 0)),
                  _resident((d, 2 * D_FF), layer), _resident((FFN_CONV, 2 * D_FF)), _resident((1, 2 * D_FF)),
                  _resident((D_FF, d), layer), _resident((1, d)), _resident((1, d))],
        out_specs=pl.BlockSpec((1, tm, d), lambda bi, i: (bi, i, 0)),
        out_shape=jax.ShapeDtypeStruct((bsz, s, d), F32),
        scratch_shapes=[pltpu.VMEM((SUBLANES, 2 * D_FF), F32), pltpu.VMEM((tm, D_FF), BF16),
                        pltpu.VMEM((tm, d), F32),
                        pltpu.VMEM((4, FFN_CHUNK // LANES, SUBLANES + tm, LANES), F32)],
        compiler_params=_params(("arbitrary", "arbitrary")),
        name="ffn_layer",
    )(x, wup, cw, cb, wdn, g, b)


def kernel(x, mem, a_w_qkv, a_sinks, a_w_o, b_w_in, b_conv_w, b_conv_b, b_w_rgate, b_b_rgate, b_w_igate, b_b_igate, b_lambda, b_w_o, c_w_down, c_q_norm, c_kv_norm, c_w_uq, c_w_ukv, c_w_o, mem_w_kv, x_w_q, x_w_o, f_w_up, f_conv_w, f_conv_b, f_w_down, ln_g, ln_b):
    bsz, s, d = x.shape
    cos, sin = _rope_tables(s)

    mkv = _matmul(mem.reshape(bsz * MEM_LEN, d), mem_w_kv.astype(BF16), 512, BF16)
    mkv = mkv.reshape(bsz, MEM_LEN, 2 * d)

    def row(v):
        return v.reshape(1, -1)

    xq_all, xo_all = x_w_q.astype(BF16), x_w_o.astype(BF16)
    fup_all, fdn_all = f_w_up.astype(BF16), f_w_down.astype(BF16)
    aqkv_all, ao_all = a_w_qkv.astype(BF16), a_w_o.astype(BF16)

    for i in range(DEPTH):
        kind, j = i % N_MIXERS, i // N_MIXERS
        g0, b0 = row(ln_g[i, 0]), row(ln_b[i, 0])
        mixer = None
        if kind == 0:
            x = _swa_layer(x, j, a_sinks[j], cos, sin, aqkv_all, ao_all, g0, b0)
        elif kind == 1:
            x = _rglru_layer(x, b_w_in[j].astype(BF16), b_conv_w[j], row(b_conv_b[j]),
                             b_w_rgate[j].astype(BF16), row(b_b_rgate[j]),
                             b_w_igate[j].astype(BF16), row(b_b_igate[j]),
                             row(b_lambda[j]), b_w_o[j].astype(BF16), g0, b0)
        else:
            wd = jnp.pad(c_w_down[j], ((0, 0), (0, C_DOWN_PAD - c_w_down.shape[-1]))).astype(BF16)
            wuq = c_w_uq[j].reshape(C_Q_RANK, C_HEADS, C_NOPE + C_ROPE)
            wuq = jnp.concatenate([wuq[:, :, :C_NOPE].reshape(C_Q_RANK, -1),
                                   wuq[:, :, C_NOPE:].reshape(C_Q_RANK, -1)], axis=-1).astype(BF16)
            wukv = c_w_ukv[j].reshape(C_KV_RANK, C_HEADS, C_NOPE + C_V)
            wukv = jnp.concatenate([wukv[:, :, :C_NOPE].reshape(C_KV_RANK, -1),
                                    wukv[:, :, C_NOPE:].reshape(C_KV_RANK, -1)], axis=-1).astype(BF16)
            q, k, v = _mla_proj(x, cos, sin, wd, row(c_q_norm[j]), row(c_kv_norm[j]), wuq, wukv)
            mixer = (_mla_attn(q, k, v), c_w_o[j].astype(BF16), g0, b0)
        x = _xattn_layer(x, i, xq_all, mkv, xo_all, row(ln_g[i, 1]), row(ln_b[i, 1]), mixer=mixer)
        x = _ffn_layer(x, i, fup_all, f_conv_w[i], row(f_conv_b[i]), fdn_all, row(ln_g[i, 2]), row(ln_b[i, 2]))
    return x
```

```python
import functools
import math

import jax
import jax.numpy as jnp
from jax import lax
from jax.experimental import pallas as pl
from jax.experimental.pallas import tpu as pltpu

D_MODEL = 1024
DEPTH = 4
N_MIXERS = 3
MEM_LEN = 256
BLOCK = 128
ROPE_THETA = 10000.0
NEG = -1e30
LN_EPS = 1e-5
RMS_EPS = 1e-6

A_HEADS = 16
A_KV_HEADS = 4
A_HEAD_DIM = 64

LRU_WIDTH = D_MODEL
LRU_BLOCKS = 4
LRU_BLOCK_W = LRU_WIDTH // LRU_BLOCKS
LRU_CONV = 4
LRU_C = 8.0

C_HEADS = 8
C_NOPE = 128
C_ROPE = 64
C_V = 128
C_Q_RANK = 384
C_KV_RANK = 256
C_QK_PAD = 256

X_HEADS = 4
X_HEAD_DIM = D_MODEL // X_HEADS

D_FF = 2816
FFN_CONV = 3
FFN_CHUNK = 256
FFN_NCHUNK = D_FF // FFN_CHUNK

ALPHA = (2.0 * DEPTH) ** 0.25

LANES = 128
SUBLANES = 8
VMEM_LIMIT = 56 * 1024 * 1024

BF16 = jnp.bfloat16
F32 = jnp.float32

NT_DIMS = (((1,), (1,)), ((), ()))
LOG2E = math.log2(math.e)


def _dot(a, b):
    return jnp.dot(a, b, preferred_element_type=F32)


def _dot_nt(a, b):
    return lax.dot_general(a, b, NT_DIMS, preferred_element_type=F32)


def _layer_norm(z, g, b):
    mu = jnp.mean(z, axis=-1, keepdims=True)
    zc = z - mu
    var = jnp.mean(zc * zc, axis=-1, keepdims=True)
    return zc * lax.rsqrt(var + LN_EPS) * g + b


def _sigmoid(z):
    return 0.5 * jnp.tanh(0.5 * z) + 0.5


def _resident(shape, layer=None):
    nd = len(shape)
    if layer is None:
        return pl.BlockSpec(shape, lambda *_: (0,) * nd, pipeline_mode=pl.Buffered(1))
    return pl.BlockSpec((None,) + tuple(shape), lambda *_: (layer,) + (0,) * nd, pipeline_mode=pl.Buffered(1))


def _params(sem):
    return pltpu.CompilerParams(dimension_semantics=sem, vmem_limit_bytes=VMEM_LIMIT)


def _matmul_kernel(a_ref, w_ref, o_ref):
    o_ref[...] = _dot(a_ref[...].astype(BF16), w_ref[...]).astype(o_ref.dtype)


def _matmul(a, w, tm, out_dtype):
    t, k = a.shape
    n = w.shape[1]
    return pl.pallas_call(
        _matmul_kernel,
        grid=(t // tm,),
        in_specs=[pl.BlockSpec((tm, k), lambda i: (i, 0)), _resident((k, n))],
        out_specs=pl.BlockSpec((tm, n), lambda i: (i, 0)),
        out_shape=jax.ShapeDtypeStruct((t, n), out_dtype),
        compiler_params=_params(("parallel",)),
        name="matmul",
    )(a, w)


def _rope_slab(slab, cos, sin_signed, first_half):
    rot = jnp.where(first_half, pltpu.roll(slab, LANES - 32, 1), pltpu.roll(slab, 32, 1))
    return slab * cos + rot * sin_signed


def _rope_tables(seq):
    inv = 1.0 / (ROPE_THETA ** (jnp.arange(0, A_HEAD_DIM, 2, dtype=F32) / A_HEAD_DIM))
    ang = jnp.arange(seq, dtype=F32)[:, None] * inv[None, :]
    cos, sin = jnp.cos(ang), jnp.sin(ang)
    return (jnp.concatenate([cos, cos, cos, cos], axis=-1),
            jnp.concatenate([-sin, sin, -sin, sin], axis=-1))


SWA_TM = 512
SWA_NBLK = SWA_TM // BLOCK


def _swa_kernel(sinks_ref, x_ref, cos_ref, sin_ref, wqkv_ref, wo_ref, g_ref, b_ref, o_ref,
                kext_scr, vext_scr, q_scr, o_scr):
    i = pl.program_id(1)
    tm = SWA_TM

    @pl.when(i == 0)
    def _():
        kext_scr[:, 0:BLOCK, :] = jnp.zeros((2 * A_KV_HEADS, BLOCK, LANES), BF16)
        vext_scr[:, 0:BLOCK, :] = jnp.zeros((2 * A_KV_HEADS, BLOCK, LANES), BF16)

    @pl.when(i > 0)
    def _():
        kext_scr[:, 0:BLOCK, :] = kext_scr[:, tm:tm + BLOCK, :]
        vext_scr[:, 0:BLOCK, :] = vext_scr[:, tm:tm + BLOCK, :]

    x = x_ref[0]
    qkv = _dot(x.astype(BF16), wqkv_ref[...])
    cos = cos_ref[...]
    sin = sin_ref[...]
    lane = lax.broadcasted_iota(jnp.int32, (tm, LANES), 1)
    first_half = (lane & 32) == 0
    low = lane < 64

    nq = A_HEADS * A_HEAD_DIM
    nkv = A_KV_HEADS * A_HEAD_DIM
    for j in range(nq // LANES):
        slab = _rope_slab(qkv[:, j * LANES:(j + 1) * LANES], cos, sin, first_half)
        q_scr[:, j * LANES:(j + 1) * LANES] = (slab * (A_HEAD_DIM ** -0.5 * LOG2E)).astype(BF16)

    for m in range(nkv // LANES):
        kslab = _rope_slab(qkv[:, nq + m * LANES:nq + (m + 1) * LANES], cos, sin, first_half)
        vslab = qkv[:, nq + nkv + m * LANES:nq + nkv + (m + 1) * LANES]
        for slab, ext in ((kslab, kext_scr), (vslab, vext_scr)):
            swapped = pltpu.roll(slab, 64, 1)
            h0, h1 = 2 * m, 2 * m + 1
            ext[2 * h0, BLOCK:BLOCK + tm, :] = jnp.where(low, slab, 0.0).astype(BF16)
            ext[2 * h0 + 1, BLOCK:BLOCK + tm, :] = jnp.where(low, 0.0, swapped).astype(BF16)
            ext[2 * h1, BLOCK:BLOCK + tm, :] = jnp.where(low, swapped, 0.0).astype(BF16)
            ext[2 * h1 + 1, BLOCK:BLOCK + tm, :] = jnp.where(low, 0.0, slab).astype(BF16)

    qi = lax.broadcasted_iota(jnp.int32, (2 * BLOCK, 4 * BLOCK), 0) & (BLOCK - 1)
    kj = lax.broadcasted_iota(jnp.int32, (2 * BLOCK, 4 * BLOCK), 1) & (2 * BLOCK - 1)
    dist = qi + BLOCK - kj
    band = (dist >= 0) & (dist < BLOCK)
    band_first = band & ((kj >= BLOCK) | (i > 0))
    top = lax.broadcasted_iota(jnp.int32, (2 * BLOCK, 1), 0) < BLOCK
    low2 = lax.broadcasted_iota(jnp.int32, (2 * BLOCK, LANES), 1) < 64

    def window(ext, h, r0):
        return jnp.concatenate([ext[2 * h, r0:r0 + 2 * BLOCK, :], ext[2 * h + 1, r0:r0 + 2 * BLOCK, :]],
                               axis=0)

    for n in range(SWA_NBLK):
        mask = band_first if n == 0 else band
        r0 = n * BLOCK
        scores = []
        for h in range(A_KV_HEADS):
            qs = jnp.concatenate([q_scr[r0:r0 + BLOCK, (2 * h) * LANES:(2 * h + 1) * LANES],
                                  q_scr[r0:r0 + BLOCK, (2 * h + 1) * LANES:(2 * h + 2) * LANES]], axis=0)
            scores.append(_dot_nt(qs, window(kext_scr, h, r0)))
        probs, inv_denoms = [], []
        for h in range(A_KV_HEADS):
            s = jnp.where(mask, scores[h], NEG)
            ps, rs = [], []
            for half in range(2):
                sh = s[:, half * 2 * BLOCK:(half + 1) * 2 * BLOCK]
                sink = jnp.where(top, sinks_ref[4 * h + half] * LOG2E, sinks_ref[4 * h + 2 + half] * LOG2E)
                mx = jnp.maximum(jnp.max(sh, axis=-1, keepdims=True), sink)
                p = jnp.exp2(sh - mx)
                rs.append(1.0 / (jnp.sum(p, axis=-1, keepdims=True) + jnp.exp2(sink - mx)))
                ps.append(p.astype(BF16))
            probs.append(jnp.concatenate(ps, axis=1))
            inv_denoms.append(jnp.where(low2, rs[0], rs[1]))
        for h in range(A_KV_HEADS):
            o = _dot(probs[h], window(vext_scr, h, r0)) * inv_denoms[h]
            o_scr[r0:r0 + BLOCK, (2 * h) * LANES:(2 * h + 1) * LANES] = o[:BLOCK].astype(BF16)
            o_scr[r0:r0 + BLOCK, (2 * h + 1) * LANES:(2 * h + 2) * LANES] = o[BLOCK:].astype(BF16)

    for r0 in range(0, tm, tm // 2):
        rows = slice(r0, r0 + tm // 2)
        y = _dot(o_scr[rows, :], wo_ref[...])
        o_ref[0, rows, :] = _layer_norm(ALPHA * x[rows] + y, g_ref[...], b_ref[...])


def _swa_layer(x, layer, sinks, cos, sin, wqkv, wo, g, b):
    bsz, s, d = x.shape
    tm = SWA_TM
    nqkv = wqkv.shape[-1]
    return pl.pallas_call(
        _swa_kernel,
        grid=(bsz, s // tm),
        in_specs=[pl.BlockSpec(memory_space=pltpu.SMEM),
                  pl.BlockSpec((1, tm, d), lambda bi, i: (bi, i, 0)),
                  pl.BlockSpec((tm, LANES), lambda bi, i: (i, 0)),
                  pl.BlockSpec((tm, LANES), lambda bi, i: (i, 0)),
                  _resident((d, nqkv), layer), _resident((d, d), layer), _resident((1, d)), _resident((1, d))],
        out_specs=pl.BlockSpec((1, tm, d), lambda bi, i: (bi, i, 0)),
        out_shape=jax.ShapeDtypeStruct((bsz, s, d), F32),
        scratch_shapes=[pltpu.VMEM((2 * A_KV_HEADS, tm + BLOCK, LANES), BF16),
                        pltpu.VMEM((2 * A_KV_HEADS, tm + BLOCK, LANES), BF16),
                        pltpu.VMEM((tm, d), BF16),
                        pltpu.VMEM((tm, d), BF16)],
        compiler_params=_params(("arbitrary", "arbitrary")),
        name="swa_layer",
    )(sinks, x, cos, sin, wqkv, wo, g, b)


LRU_SUB = 256
LRU_NSUB = 2
LRU_NLT = LRU_WIDTH // LANES
LRU_PAD = 2 * SUBLANES


def _rglru_kernel(x_ref, xn_ref, win_ref, cw_ref, cb_ref, wr_ref, br_ref, wi_ref, bi_ref, lam_ref, wo_ref,
                  g_ref, b_ref, o_ref, ustage_scr, a_scr, b_scr, hcar_scr, gu_scr):
    i = pl.program_id(1)
    ts = LRU_SUB
    w = LRU_WIDTH
    pad = LRU_PAD
    sl = SUBLANES

    @pl.when(i == 0)
    def _():
        ustage_scr[:, :, 0:sl, :] = jnp.zeros((LRU_NSUB, LRU_NLT, sl, LANES), F32)
        a_scr[:, :, :, 0:pad, :] = jnp.ones((LRU_NSUB, 3, LRU_NLT, pad, LANES), F32)
        b_scr[:, :, :, 0:pad, :] = jnp.zeros((LRU_NSUB, 3, LRU_NLT, pad, LANES), F32)
        hcar_scr[...] = jnp.zeros_like(hcar_scr)

    @pl.when(i > 0)
    def _():
        ustage_scr[:, :, 0:sl, :] = ustage_scr[:, :, ts:ts + sl, :]
        a_scr[:, 0, :, sl:pad, :] = a_scr[:, 0, :, ts + sl:ts + pad, :]
        b_scr[:, 0, :, sl:pad, :] = b_scr[:, 0, :, ts + sl:ts + pad, :]

    cw = cw_ref[...]
    cb = cb_ref[...]
    z = -lam_ref[...]
    softplus = jnp.maximum(z, 0.0) + jnp.log1p(jnp.exp(-jnp.abs(z)))
    c0 = math.sqrt(2.0 / math.pi)

    @pl.when(i == 0)
    def _():
        for sub in range(LRU_NSUB):
            gu_scr[0, sub] = _dot(x_ref[sub].astype(BF16), win_ref[...])

    def project_next(slot, sub):
        gu_scr[1 - slot, sub] = _dot(xn_ref[sub].astype(BF16), win_ref[...])

    def conv_gates(slot, sub):
        u = gu_scr[slot, sub, :, w:2 * w]
        ucs = []
        for lt in range(LRU_NLT):
            ls = slice(lt * LANES, (lt + 1) * LANES)
            us = ustage_scr.at[sub, lt]
            us[sl:sl + ts, :] = u[:, ls]
            uc = us[sl - 3:sl - 3 + ts, :] * cw[0:1, ls]
            uc = uc + us[sl - 2:sl - 2 + ts, :] * cw[1:2, ls]
            uc = uc + us[sl - 1:sl - 1 + ts, :] * cw[2:3, ls]
            uc = uc + u[:, ls] * cw[3:4, ls]
            ucs.append(uc + cb[:, ls])
        uc = jnp.concatenate(ucs, axis=1)

        ucb = uc.astype(BF16)
        rs, igs = [], []
        for h in range(LRU_BLOCKS):
            blk = ucb[:, h * LRU_BLOCK_W:(h + 1) * LRU_BLOCK_W]
            rs.append(_dot(blk, wr_ref[h]))
            igs.append(_dot(blk, wi_ref[h]))
        return uc, jnp.concatenate(rs, axis=-1), jnp.concatenate(igs, axis=-1)

    def recurrence(slot, sub, uc, r_pre, ig_pre):
        hblk = hcar_scr[sub]
        gate = gu_scr[slot, sub, :, 0:w]
        r = _sigmoid(r_pre + br_ref[...])
        ig = _sigmoid(ig_pre + bi_ref[...])
        log_a = (-LRU_C * r) * softplus
        a0 = jnp.exp(log_a)
        th = jnp.tanh(log_a)
        b0 = (jnp.sqrt(-2.0 * th) * lax.rsqrt(1.0 - th)) * (ig * uc)

        a8, b8 = [], []
        for lt in range(LRU_NLT):
            ls = slice(lt * LANES, (lt + 1) * LANES)
            a_scr[sub, 0, lt, pad:pad + ts, :] = a0[:, ls]
            b_scr[sub, 0, lt, pad:pad + ts, :] = b0[:, ls]
            for k, d in enumerate((1, 2)):
                a_cur = a_scr[sub, k, lt, sl:pad + ts, :]
                b_cur = b_scr[sub, k, lt, sl:pad + ts, :]
                a_scr[sub, k + 1, lt, sl:pad + ts, :] = a_cur * a_scr[sub, k, lt, sl - d:pad + ts - d, :]
                b_scr[sub, k + 1, lt, sl:pad + ts, :] = (a_cur * b_scr[sub, k, lt, sl - d:pad + ts - d, :]
                                                        + b_cur)
            a_cur = a_scr[sub, 2, lt, pad:pad + ts, :]
            a8.append(a_cur * a_scr[sub, 2, lt, pad - 4:pad - 4 + ts, :])
            b8.append(a_cur * b_scr[sub, 2, lt, pad - 4:pad - 4 + ts, :] + b_scr[sub, 2, lt, pad:pad + ts, :])
        a8 = jnp.concatenate(a8, axis=1)
        b8 = jnp.concatenate(b8, axis=1)

        hs = []
        for j in range(ts // sl):
            hblk = a8[j * sl:(j + 1) * sl] * hblk + b8[j * sl:(j + 1) * sl]
            hs.append(hblk)
        h = jnp.concatenate(hs, axis=0)

        gelu = gate * (0.5 * (1.0 + jnp.tanh(c0 * (gate + 0.044715 * (gate * gate * gate)))))
        hcar_scr[sub] = hblk
        return (h * gelu).astype(BF16)

    def out_proj(sub, yb):
        return ALPHA * x_ref[sub] + _dot(yb, wo_ref[...])

    def step(slot):
        zs = []
        for sub in range(LRU_NSUB):
            uc, r_pre, ig_pre = conv_gates(slot, sub)
            project_next(slot, sub)
            zs.append(out_proj(sub, recurrence(slot, sub, uc, r_pre, ig_pre)))
            if sub > 0:
                o_ref[sub - 1] = _layer_norm(zs[sub - 1], g_ref[...], b_ref[...])
        o_ref[LRU_NSUB - 1] = _layer_norm(zs[-1], g_ref[...], b_ref[...])

    for parity in range(2):
        pl.when(lax.rem(i, 2) == parity)(functools.partial(step, parity))


def _rglru_layer(x, win, cw, cb, wr, br, wi, bi, lam, wo, g, b):
    bsz, s, d = x.shape
    tm = LRU_SUB
    w = LRU_WIDTH
    last = s // tm - 1
    return pl.pallas_call(
        _rglru_kernel,
        grid=(bsz // LRU_NSUB, s // tm),
        in_specs=[pl.BlockSpec((LRU_NSUB, tm, d), lambda bi_, i: (bi_, i, 0)),
                  pl.BlockSpec((LRU_NSUB, tm, d), lambda bi_, i: (bi_, jnp.minimum(i + 1, last), 0)),
                  _resident((d, 2 * w)), _resident((LRU_CONV, w)), _resident((1, w)),
                  _resident((LRU_BLOCKS, LRU_BLOCK_W, LRU_BLOCK_W)), _resident((1, w)),
                  _resident((LRU_BLOCKS, LRU_BLOCK_W, LRU_BLOCK_W)), _resident((1, w)),
                  _resident((1, w)), _resident((w, d)), _resident((1, d)), _resident((1, d))],
        out_specs=pl.BlockSpec((LRU_NSUB, tm, d), lambda bi_, i: (bi_, i, 0)),
        out_shape=jax.ShapeDtypeStruct((bsz, s, d), F32),
        scratch_shapes=[pltpu.VMEM((LRU_NSUB, LRU_NLT, SUBLANES + LRU_SUB, LANES), F32),
                        pltpu.VMEM((LRU_NSUB, 3, LRU_NLT, LRU_PAD + LRU_SUB, LANES), F32),
                        pltpu.VMEM((LRU_NSUB, 3, LRU_NLT, LRU_PAD + LRU_SUB, LANES), F32),
                        pltpu.VMEM((LRU_NSUB, SUBLANES, w), F32),
                        pltpu.VMEM((2, LRU_NSUB, tm, 2 * w), F32)],
        compiler_params=_params(("arbitrary", "arbitrary")),
        name="rglru_layer",
    )(x, x, win, cw, cb, wr, br, wi, bi, lam, wo, g, b)


MLA_TM = 1024
MLA_TQ = 1024
MLA_HALF = MLA_TQ // 2
MLA_TK = 512
C_DOWN_PAD = 768
MLA_Q_SCALE = (C_NOPE + C_ROPE) ** -0.5 * math.log2(math.e)


def _mla_proj_kernel(x_ref, cos_ref, sin_ref, wd_ref, qn_ref, kvn_ref, wuq_ref, wukv_ref,
                     q_ref, k_ref, v_ref):
    tm = MLA_TM
    xb = x_ref[0].astype(BF16)
    c = _dot(xb, wd_ref[...])
    cq = c[:, :C_Q_RANK]
    ckv = c[:, C_Q_RANK:C_Q_RANK + C_KV_RANK]
    kr_slab = c[:, C_Q_RANK + C_KV_RANK:]
    cq = cq * lax.rsqrt(jnp.mean(cq * cq, axis=-1, keepdims=True) + RMS_EPS) * qn_ref[...]
    ckv = ckv * lax.rsqrt(jnp.mean(ckv * ckv, axis=-1, keepdims=True) + RMS_EPS) * kvn_ref[...]
    q = _dot(cq.astype(BF16), wuq_ref[...]) * MLA_Q_SCALE
    kv = _dot(ckv.astype(BF16), wukv_ref[...])

    cos = cos_ref[...]
    sin = sin_ref[...]
    lane = lax.broadcasted_iota(jnp.int32, (tm, LANES), 1)
    first_half = (lane & 32) == 0
    low = lane < 64
    nn = C_HEADS * C_NOPE

    kr = _rope_slab(kr_slab, cos, sin, first_half)
    kr_pair = (kr.astype(BF16), pltpu.roll(kr, 64, 1).astype(BF16))
    for pair in range(C_HEADS // 2):
        qr = _rope_slab(q[:, nn + pair * LANES:nn + (pair + 1) * LANES], cos, sin, first_half)
        for half in range(2):
            h = 2 * pair + half
            qr_h = jnp.where(low, qr, 0.0) if half == 0 else jnp.where(low, 0.0, qr)
            q_ref[0, h, :, 0:C_NOPE] = q[:, h * C_NOPE:(h + 1) * C_NOPE].astype(BF16)
            q_ref[0, h, :, C_NOPE:C_QK_PAD] = qr_h.astype(BF16)
            k_ref[0, h, :, 0:C_NOPE] = kv[:, h * C_NOPE:(h + 1) * C_NOPE].astype(BF16)
            k_ref[0, h, :, C_NOPE:C_QK_PAD] = kr_pair[half]
            v_ref[0, h] = kv[:, nn + h * C_V:nn + (h + 1) * C_V].astype(BF16)


def _mla_proj(x, cos, sin, wd, qn, kvn, wuq, wukv):
    bsz, s, d = x.shape
    tm = MLA_TM
    qk_shape = jax.ShapeDtypeStruct((bsz, C_HEADS, s, C_QK_PAD), BF16)
    v_shape = jax.ShapeDtypeStruct((bsz, C_HEADS, s, C_V), BF16)
    return pl.pallas_call(
        _mla_proj_kernel,
        grid=(bsz, s // tm),
        in_specs=[pl.BlockSpec((1, tm, d), lambda bi, i: (bi, i, 0)),
                  pl.BlockSpec((tm, LANES), lambda bi, i: (i, 0)),
                  pl.BlockSpec((tm, LANES), lambda bi, i: (i, 0)),
                  _resident(wd.shape), _resident(qn.shape), _resident(kvn.shape),
                  _resident(wuq.shape), _resident(wukv.shape)],
        out_specs=[pl.BlockSpec((1, C_HEADS, tm, C_QK_PAD), lambda bi, i: (bi, 0, i, 0)),
                   pl.BlockSpec((1, C_HEADS, tm, C_QK_PAD), lambda bi, i: (bi, 0, i, 0)),
                   pl.BlockSpec((1, C_HEADS, tm, C_V), lambda bi, i: (bi, 0, i, 0))],
        out_shape=[qk_shape, qk_shape, v_shape],
        compiler_params=_params(("parallel", "parallel")),
        name="mla_proj",
    )(x, cos, sin, wd, qn, kvn, wuq, wukv)


def _mla_attn_kernel(q_ref, qn_ref, k_ref, v_ref, o_ref, m_scr, acc_scr, s_scr):
    i = pl.program_id(2)
    tk, hq = MLA_TK, MLA_HALF

    m_scr[...] = jnp.full_like(m_scr, -jnp.inf)
    acc_scr[...] = jnp.zeros_like(acc_scr)
    ones = jnp.ones((tk, LANES), BF16)

    def scores(slot, hf, j, queries=q_ref):
        k0 = pl.multiple_of(j * tk, tk)
        s_scr[slot, hf] = _dot_nt(queries[0, 0, hf * hq:(hf + 1) * hq, :], k_ref[0, 0, pl.ds(k0, tk), :])

    def consume(slot, hf, j, diagonal):
        k0 = pl.multiple_of(j * tk, tk)
        vv = jnp.concatenate([v_ref[0, 0, pl.ds(k0, tk), :], ones], axis=1)
        s = s_scr[slot, hf]
        if diagonal:
            qpos = lax.broadcasted_iota(jnp.int32, (hq, tk), 0)
            kpos = lax.broadcasted_iota(jnp.int32, (hq, tk), 1)
            s = jnp.where(kpos <= qpos, s, NEG)
        m_prev = m_scr[hf]
        m_next = jnp.maximum(m_prev, jnp.max(s, axis=-1, keepdims=True))
        p = jnp.exp2(s - jnp.tile(m_next, (1, tk // LANES)))
        corr = jnp.exp2(m_prev - m_next)
        acc_scr[hf] = jnp.tile(corr, (1, 2)) * acc_scr[hf] + _dot(p.astype(BF16), vv)
        m_scr[hf] = m_next

    @pl.when(i == 0)
    def _():
        scores(0, 0, 0)
        scores(0, 1, 0)

    def chunk_pair(j):
        scores(1, 0, j + 1)
        scores(1, 1, j + 1)
        consume(0, 0, j, False)
        consume(0, 1, j, False)
        scores(0, 0, j + 2)
        scores(0, 1, j + 2)
        consume(1, 0, j + 1, False)
        consume(1, 1, j + 1, False)

    done = 0
    for bit in (2, 1):
        npairs = 1 << bit

        @pl.when((i >> bit) & 1 == 1)
        def _(npairs=npairs, first=done):
            for p in range(npairs):
                chunk_pair(2 * (first + p))

        done = done + jnp.where((i >> bit) & 1 == 1, npairs, 0)

    def tail():
        scores(1, 1, 2 * i + 1)
        consume(0, 0, 2 * i, True)
        consume(0, 1, 2 * i, False)
        scores(0, 0, 0, queries=qn_ref)
        scores(0, 1, 0, queries=qn_ref)
        consume(1, 1, 2 * i + 1, True)
        for hf in range(2):
            acc = acc_scr[hf]
            o_ref[0, hf * hq:(hf + 1) * hq, :] = (acc[:, :C_V] / acc[:, C_V:]).astype(BF16)

    @pl.when(i & 1 == 1)
    def _():
        chunk_pair(2 * done)
        tail()

    pl.when(i & 1 == 0)(tail)


def _mla_attn(q, k, v):
    bsz, nh, s, _ = q.shape
    tq = MLA_TQ
    last = s // tq - 1
    assert last < 8, "the kernel decomposes the tile index into three bits"
    return pl.pallas_call(
        _mla_attn_kernel,
        grid=(bsz, nh, s // tq),
        in_specs=[pl.BlockSpec((1, 1, tq, C_QK_PAD), lambda b, h, i: (b, h, i, 0)),
                  pl.BlockSpec((1, 1, tq, C_QK_PAD), lambda b, h, i: (b, h, jnp.minimum(i + 1, last), 0)),
                  pl.BlockSpec((1, 1, s, C_QK_PAD), lambda b, h, i: (b, h, 0, 0)),
                  pl.BlockSpec((1, 1, s, C_V), lambda b, h, i: (b, h, 0, 0))],
        out_specs=pl.BlockSpec((1, tq, C_V), lambda b, h, i: (b, i, h)),
        out_shape=jax.ShapeDtypeStruct((bsz, s, nh * C_V), BF16),
        scratch_shapes=[pltpu.VMEM((2, MLA_HALF, LANES), F32),
                        pltpu.VMEM((2, MLA_HALF, 2 * C_V), F32),
                        pltpu.VMEM((2, 2, MLA_HALF, MLA_TK), F32)],
        compiler_params=_params(("parallel", "parallel", "arbitrary")),
        name="mla_attn",
    )(q, q, k, v)


X_TM = 1024
X_LN_ROWS = 256


def _xattn_kernel(*refs, mixer_prologue):
    tm = X_TM
    if mixer_prologue:
        a_ref, wa_ref, ga_ref, ba_ref, x_ref, wq_ref, mkv_ref, wo_ref, g_ref, b_ref, o_ref = refs
        x = _layer_norm(ALPHA * x_ref[0] + _dot(a_ref[0], wa_ref[...]), ga_ref[...], ba_ref[...])
    else:
        x_ref, wq_ref, mkv_ref, wo_ref, g_ref, b_ref, o_ref = refs
        x = x_ref[0]
    q = (_dot(x.astype(BF16), wq_ref[...]) * (X_HEAD_DIM ** -0.5 * LOG2E)).astype(BF16)
    heads = [(h * X_HEAD_DIM, (h + 1) * X_HEAD_DIM) for h in range(X_HEADS)]
    scores = [_dot_nt(q[:, lo:hi], mkv_ref[0, :, lo:hi]) for lo, hi in heads]
    probs, inv_denoms = [], []
    for s in scores:
        p = jnp.exp2(s - jnp.max(s, axis=-1, keepdims=True))
        inv_denoms.append(1.0 / jnp.sum(p, axis=-1, keepdims=True))
        probs.append(p.astype(BF16))
    outs = [(_dot(probs[h], mkv_ref[0, :, D_MODEL + lo:D_MODEL + hi]) * inv_denoms[h]).astype(BF16)
            for h, (lo, hi) in enumerate(heads)]
    o = jnp.concatenate(outs, axis=-1)
    for r0 in range(0, tm, X_LN_ROWS):
        rows = slice(r0, r0 + X_LN_ROWS)
        y = _dot(o[rows], wo_ref[...])
        o_ref[0, rows, :] = _layer_norm(ALPHA * x[rows] + y, g_ref[...], b_ref[...])


def _xattn_layer(x, layer, wq, mkv, wo, g, b, mixer=None):
    bsz, s, d = x.shape
    tm = X_TM
    tile = pl.BlockSpec((1, tm, d), lambda bi, i: (bi, i, 0))
    in_specs = [tile, _resident((d, d), layer),
                pl.BlockSpec((1, MEM_LEN, 2 * d), lambda bi, i: (bi, 0, 0)),
                _resident((d, d), layer), _resident((1, d)), _resident((1, d))]
    args = (x, wq, mkv, wo, g, b)
    if mixer is not None:
        in_specs = [tile, _resident((d, d)), _resident((1, d)), _resident((1, d))] + in_specs
        args = tuple(mixer) + args
    return pl.pallas_call(
        functools.partial(_xattn_kernel, mixer_prologue=mixer is not None),
        grid=(bsz, s // tm),
        in_specs=in_specs,
        out_specs=tile,
        out_shape=jax.ShapeDtypeStruct((bsz, s, d), F32),
        compiler_params=_params(("parallel", "parallel")),
        name="xattn_layer",
    )(*args)


FFN_TM = 512
FFN_LN_ROWS = 256
FFN_DOWN_GROUPS = (6, FFN_NCHUNK)


def _ffn_kernel(x_ref, wup_ref, cw_ref, cb_ref, wdn_ref, g_ref, b_ref, o_ref, hprev_scr, act_scr,
                acc_scr, hstage_scr):
    i = pl.program_id(1)
    tm = FFN_TM

    @pl.when(i == 0)
    def _():
        hprev_scr[...] = jnp.zeros_like(hprev_scr)

    x = x_ref[0]
    xb = x.astype(BF16)

    def up_conv(col0, slot):
        cols = slice(col0, col0 + FFN_CHUNK)
        h = _dot(xb, wup_ref[:, cols])
        prev = hprev_scr[:, cols]
        hprev_scr[:, cols] = h[tm - SUBLANES:, :]
        cw = cw_ref[:, cols]
        cb = cb_ref[:, cols]
        outs = []
        for lt in range(FFN_CHUNK // LANES):
            ls = slice(lt * LANES, (lt + 1) * LANES)
            hs = hstage_scr.at[slot, lt]
            hs[0:SUBLANES, :] = prev[:, ls]
            hs[SUBLANES:SUBLANES + tm, :] = h[:, ls]
            y = hs[SUBLANES - 2:SUBLANES - 2 + tm, :] * cw[0:1, ls]
            y = y + hs[SUBLANES - 1:SUBLANES - 1 + tm, :] * cw[1:2, ls]
            y = y + h[:, ls] * cw[2:3, ls]
            outs.append(y + cb[:, ls])
        return jnp.concatenate(outs, axis=1)

    def down(rows, c0, c1):
        ks = slice(c0 * FFN_CHUNK, c1 * FFN_CHUNK)
        return _dot(act_scr[rows, ks], wdn_ref[ks, :])

    everything = slice(0, tm)
    for c in range(FFN_NCHUNK):
        gt = up_conv(c * FFN_CHUNK, 2 * (c % 2))
        val = up_conv(D_FF + c * FFN_CHUNK, 2 * (c % 2) + 1)
        act_scr[:, c * FFN_CHUNK:(c + 1) * FFN_CHUNK] = ((gt * _sigmoid(gt)) * val).astype(BF16)
        if c + 1 in FFN_DOWN_GROUPS[:-1]:
            gi = FFN_DOWN_GROUPS.index(c + 1)
            if gi == 0:
                acc_scr[...] = down(everything, 0, c + 1)
            else:
                acc_scr[...] += down(everything, FFN_DOWN_GROUPS[gi - 1], c + 1)
    for r0 in range(0, tm, FFN_LN_ROWS):
        rows = slice(r0, r0 + FFN_LN_ROWS)
        out = acc_scr[rows, :] + down(rows, FFN_DOWN_GROUPS[-2], FFN_DOWN_GROUPS[-1])
        o_ref[0, rows, :] = _layer_norm(ALPHA * x[rows] + out, g_ref[...], b_ref[...])


def _ffn_layer(x, layer, wup, cw, cb, wdn, g, b):
    bsz, s, d = x.shape
    tm = FFN_TM
    return pl.pallas_call(
        _ffn_kernel,
        grid=(bsz, s // tm),
        in_specs=[pl.BlockSpec((1, tm, d), lambda bi, i: (bi, i, 0)),
                  _resident((d, 2 * D_FF), layer), _resident((FFN_CONV, 2 * D_FF)), _resident((1, 2 * D_FF)),
                  _resident((D_FF, d), layer), _resident((1, d)), _resident((1, d))],
        out_specs=pl.BlockSpec((1, tm, d), lambda bi, i: (bi, i, 0)),
        out_shape=jax.ShapeDtypeStruct((bsz, s, d), F32),
        scratch_shapes=[pltpu.VMEM((SUBLANES, 2 * D_FF), F32), pltpu.VMEM((tm, D_FF), BF16),
                        pltpu.VMEM((tm, d), F32),
                        pltpu.VMEM((4, FFN_CHUNK // LANES, SUBLANES + tm, LANES), F32)],
        compiler_params=_params(("arbitrary", "arbitrary")),
        name="ffn_layer",
    )(x, wup, cw, cb, wdn, g, b)


def kernel(x, mem, a_w_qkv, a_sinks, a_w_o, b_w_in, b_conv_w, b_conv_b, b_w_rgate, b_b_rgate, b_w_igate, b_b_igate, b_lambda, b_w_o, c_w_down, c_q_norm, c_kv_norm, c_w_uq, c_w_ukv, c_w_o, mem_w_kv, x_w_q, x_w_o, f_w_up, f_conv_w, f_conv_b, f_w_down, ln_g, ln_b):
    bsz, s, d = x.shape
    cos, sin = _rope_tables(s)

    mkv = _matmul(mem.reshape(bsz * MEM_LEN, d), mem_w_kv.astype(BF16), 512, BF16)
    mkv = mkv.reshape(bsz, MEM_LEN, 2 * d)

    def row(v):
        return v.reshape(1, -1)

    xq_all, xo_all = x_w_q.astype(BF16), x_w_o.astype(BF16)
    fup_all, fdn_all = f_w_up.astype(BF16), f_w_down.astype(BF16)
    aqkv_all, ao_all = a_w_qkv.astype(BF16), a_w_o.astype(BF16)

    for i in range(DEPTH):
        kind, j = i % N_MIXERS, i // N_MIXERS
        g0, b0 = row(ln_g[i, 0]), row(ln_b[i, 0])
        mixer = None
        if kind == 0:
            x = _swa_layer(x, j, a_sinks[j], cos, sin, aqkv_all, ao_all, g0, b0)
        elif kind == 1:
            x = _rglru_layer(x, b_w_in[j].astype(BF16), b_conv_w[j], row(b_conv_b[j]),
                             b_w_rgate[j].astype(BF16), row(b_b_rgate[j]),
                             b_w_igate[j].astype(BF16), row(b_b_igate[j]),
                             row(b_lambda[j]), b_w_o[j].astype(BF16), g0, b0)
        else:
            wd = jnp.pad(c_w_down[j], ((0, 0), (0, C_DOWN_PAD - c_w_down.shape[-1]))).astype(BF16)
            wuq = c_w_uq[j].reshape(C_Q_RANK, C_HEADS, C_NOPE + C_ROPE)
            wuq = jnp.concatenate([wuq[:, :, :C_NOPE].reshape(C_Q_RANK, -1),
                                   wuq[:, :, C_NOPE:].reshape(C_Q_RANK, -1)], axis=-1).astype(BF16)
            wukv = c_w_ukv[j].reshape(C_KV_RANK, C_HEADS, C_NOPE + C_V)
            wukv = jnp.concatenate([wukv[:, :, :C_NOPE].reshape(C_KV_RANK, -1),
                                    wukv[:, :, C_NOPE:].reshape(C_KV_RANK, -1)], axis=-1).astype(BF16)
            q, k, v = _mla_proj(x, cos, sin, wd, row(c_q_norm[j]), row(c_kv_norm[j]), wuq, wukv)
            mixer = (_mla_attn(q, k, v), c_w_o[j].astype(BF16), g0, b0)
        x = _xattn_layer(x, i, xq_all, mkv, xo_all, row(ln_g[i, 1]), row(ln_b[i, 1]), mixer=mixer)
        x = _ffn_layer(x, i, fup_all, f_conv_w[i], row(f_conv_b[i]), fdn_all, row(ln_g[i, 2]), row(ln_b[i, 2]))
    return x
```

```python
import functools
import math

import jax
import jax.numpy as jnp
from jax import lax
from jax.experimental import pallas as pl
from jax.experimental.pallas import tpu as pltpu

D_MODEL = 1024
DEPTH = 4
N_MIXERS = 3
MEM_LEN = 256
BLOCK = 128
ROPE_THETA = 10000.0
NEG = -1e30
LN_EPS = 1e-5
RMS_EPS = 1e-6

A_HEADS = 16
A_KV_HEADS = 4
A_HEAD_DIM = 64

LRU_WIDTH = D_MODEL
LRU_BLOCKS = 4
LRU_BLOCK_W = LRU_WIDTH // LRU_BLOCKS
LRU_CONV = 4
LRU_C = 8.0

C_HEADS = 8
C_NOPE = 128
C_ROPE = 64
C_V = 128
C_Q_RANK = 384
C_KV_RANK = 256
C_QK_PAD = 256

X_HEADS = 4
X_HEAD_DIM = D_MODEL // X_HEADS

D_FF = 2816
FFN_CONV = 3
FFN_CHUNK = 256
FFN_NCHUNK = D_FF // FFN_CHUNK

ALPHA = (2.0 * DEPTH) ** 0.25

LANES = 128
SUBLANES = 8
VMEM_LIMIT = 56 * 1024 * 1024

BF16 = jnp.bfloat16
F32 = jnp.float32

NT_DIMS = (((1,), (1,)), ((), ()))
LOG2E = math.log2(math.e)


def _dot(a, b):
    return jnp.dot(a, b, preferred_element_type=F32)


def _dot_nt(a, b):
    return lax.dot_general(a, b, NT_DIMS, preferred_element_type=F32)


def _layer_norm(z, g, b):
    mu = jnp.mean(z, axis=-1, keepdims=True)
    zc = z - mu
    var = jnp.mean(zc * zc, axis=-1, keepdims=True)
    return zc * lax.rsqrt(var + LN_EPS) * g + b


def _sigmoid(z):
    return 0.5 * jnp.tanh(0.5 * z) + 0.5


def _resident(shape, layer=None):
    nd = len(shape)
    if layer is None:
        return pl.BlockSpec(shape, lambda *_: (0,) * nd, pipeline_mode=pl.Buffered(1))
    return pl.BlockSpec((None,) + tuple(shape), lambda *_: (layer,) + (0,) * nd, pipeline_mode=pl.Buffered(1))


def _params(sem):
    return pltpu.CompilerParams(dimension_semantics=sem, vmem_limit_bytes=VMEM_LIMIT)


def _matmul_kernel(a_ref, w_ref, o_ref):
    o_ref[...] = _dot(a_ref[...].astype(BF16), w_ref[...]).astype(o_ref.dtype)


def _matmul(a, w, tm, out_dtype):
    t, k = a.shape
    n = w.shape[1]
    return pl.pallas_call(
        _matmul_kernel,
        grid=(t // tm,),
        in_specs=[pl.BlockSpec((tm, k), lambda i: (i, 0)), _resident((k, n))],
        out_specs=pl.BlockSpec((tm, n), lambda i: (i, 0)),
        out_shape=jax.ShapeDtypeStruct((t, n), out_dtype),
        compiler_params=_params(("parallel",)),
        name="matmul",
    )(a, w)


def _rope_slab(slab, cos, sin_signed, first_half):
    rot = jnp.where(first_half, pltpu.roll(slab, LANES - 32, 1), pltpu.roll(slab, 32, 1))
    return slab * cos + rot * sin_signed


def _rope_tables(seq):
    inv = 1.0 / (ROPE_THETA ** (jnp.arange(0, A_HEAD_DIM, 2, dtype=F32) / A_HEAD_DIM))
    ang = jnp.arange(seq, dtype=F32)[:, None] * inv[None, :]
    cos, sin = jnp.cos(ang), jnp.sin(ang)
    return (jnp.concatenate([cos, cos, cos, cos], axis=-1),
            jnp.concatenate([-sin, sin, -sin, sin], axis=-1))


SWA_TM = 512
SWA_NBLK = SWA_TM // BLOCK


def _swa_kernel(sinks_ref, x_ref, cos_ref, sin_ref, wqkv_ref, wo_ref, g_ref, b_ref, o_ref,
                kext_scr, vext_scr, q_scr, o_scr):
    i = pl.program_id(1)
    tm = SWA_TM

    @pl.when(i == 0)
    def _():
        kext_scr[:, 0:BLOCK, :] = jnp.zeros((2 * A_KV_HEADS, BLOCK, LANES), BF16)
        vext_scr[:, 0:BLOCK, :] = jnp.zeros((2 * A_KV_HEADS, BLOCK, LANES), BF16)

    @pl.when(i > 0)
    def _():
        kext_scr[:, 0:BLOCK, :] = kext_scr[:, tm:tm + BLOCK, :]
        vext_scr[:, 0:BLOCK, :] = vext_scr[:, tm:tm + BLOCK, :]

    x = x_ref[0]
    qkv = _dot(x.astype(BF16), wqkv_ref[...])
    cos = cos_ref[...]
    sin = sin_ref[...]
    lane = lax.broadcasted_iota(jnp.int32, (tm, LANES), 1)
    first_half = (lane & 32) == 0
    low = lane < 64

    nq = A_HEADS * A_HEAD_DIM
    nkv = A_KV_HEADS * A_HEAD_DIM
    for j in range(nq // LANES):
        slab = _rope_slab(qkv[:, j * LANES:(j + 1) * LANES], cos, sin, first_half)
        q_scr[:, j * LANES:(j + 1) * LANES] = (slab * (A_HEAD_DIM ** -0.5 * LOG2E)).astype(BF16)

    for m in range(nkv // LANES):
        kslab = _rope_slab(qkv[:, nq + m * LANES:nq + (m + 1) * LANES], cos, sin, first_half)
        vslab = qkv[:, nq + nkv + m * LANES:nq + nkv + (m + 1) * LANES]
        for slab, ext in ((kslab, kext_scr), (vslab, vext_scr)):
            swapped = pltpu.roll(slab, 64, 1)
            h0, h1 = 2 * m, 2 * m + 1
            ext[2 * h0, BLOCK:BLOCK + tm, :] = jnp.where(low, slab, 0.0).astype(BF16)
            ext[2 * h0 + 1, BLOCK:BLOCK + tm, :] = jnp.where(low, 0.0, swapped).astype(BF16)
            ext[2 * h1, BLOCK:BLOCK + tm, :] = jnp.where(low, swapped, 0.0).astype(BF16)
            ext[2 * h1 + 1, BLOCK:BLOCK + tm, :] = jnp.where(low, 0.0, slab).astype(BF16)

    qi = lax.broadcasted_iota(jnp.int32, (2 * BLOCK, 4 * BLOCK), 0) & (BLOCK - 1)
    kj = lax.broadcasted_iota(jnp.int32, (2 * BLOCK, 4 * BLOCK), 1) & (2 * BLOCK - 1)
    dist = qi + BLOCK - kj
    band = (dist >= 0) & (dist < BLOCK)
    band_first = band & ((kj >= BLOCK) | (i > 0))
    top = lax.broadcasted_iota(jnp.int32, (2 * BLOCK, 1), 0) < BLOCK
    low2 = lax.broadcasted_iota(jnp.int32, (2 * BLOCK, LANES), 1) < 64

    def window(ext, h, r0):
        return jnp.concatenate([ext[2 * h, r0:r0 + 2 * BLOCK, :], ext[2 * h + 1, r0:r0 + 2 * BLOCK, :]],
                               axis=0)

    for n in range(SWA_NBLK):
        mask = band_first if n == 0 else band
        r0 = n * BLOCK
        scores = []
        for h in range(A_KV_HEADS):
            qs = jnp.concatenate([q_scr[r0:r0 + BLOCK, (2 * h) * LANES:(2 * h + 1) * LANES],
                                  q_scr[r0:r0 + BLOCK, (2 * h + 1) * LANES:(2 * h + 2) * LANES]], axis=0)
            scores.append(_dot_nt(qs, window(kext_scr, h, r0)))
        probs, inv_denoms = [], []
        for h in range(A_KV_HEADS):
            s = jnp.where(mask, scores[h], NEG)
            ps, rs = [], []
            for half in range(2):
                sh = s[:, half * 2 * BLOCK:(half + 1) * 2 * BLOCK]
                sink = jnp.where(top, sinks_ref[4 * h + half] * LOG2E, sinks_ref[4 * h + 2 + half] * LOG2E)
                mx = jnp.maximum(jnp.max(sh, axis=-1, keepdims=True), sink)
                p = jnp.exp2(sh - mx)
                rs.append(1.0 / (jnp.sum(p, axis=-1, keepdims=True) + jnp.exp2(sink - mx)))
                ps.append(p.astype(BF16))
            probs.append(jnp.concatenate(ps, axis=1))
            inv_denoms.append(jnp.where(low2, rs[0], rs[1]))
        for h in range(A_KV_HEADS):
            o = _dot(probs[h], window(vext_scr, h, r0)) * inv_denoms[h]
            o_scr[r0:r0 + BLOCK, (2 * h) * LANES:(2 * h + 1) * LANES] = o[:BLOCK].astype(BF16)
            o_scr[r0:r0 + BLOCK, (2 * h + 1) * LANES:(2 * h + 2) * LANES] = o[BLOCK:].astype(BF16)

    for r0 in range(0, tm, tm // 2):
        rows = slice(r0, r0 + tm // 2)
        y = _dot(o_scr[rows, :], wo_ref[...])
        o_ref[0, rows, :] = _layer_norm(ALPHA * x[rows] + y, g_ref[...], b_ref[...])


def _swa_layer(x, layer, sinks, cos, sin, wqkv, wo, g, b):
    bsz, s, d = x.shape
    tm = SWA_TM
    nqkv = wqkv.shape[-1]
    return pl.pallas_call(
        _swa_kernel,
        grid=(bsz, s // tm),
        in_specs=[pl.BlockSpec(memory_space=pltpu.SMEM),
                  pl.BlockSpec((1, tm, d), lambda bi, i: (bi, i, 0)),
                  pl.BlockSpec((tm, LANES), lambda bi, i: (i, 0)),
                  pl.BlockSpec((tm, LANES), lambda bi, i: (i, 0)),
                  _resident((d, nqkv), layer), _resident((d, d), layer), _resident((1, d)), _resident((1, d))],
        out_specs=pl.BlockSpec((1, tm, d), lambda bi, i: (bi, i, 0)),
        out_shape=jax.ShapeDtypeStruct((bsz, s, d), F32),
        scratch_shapes=[pltpu.VMEM((2 * A_KV_HEADS, tm + BLOCK, LANES), BF16),
                        pltpu.VMEM((2 * A_KV_HEADS, tm + BLOCK, LANES), BF16),
                        pltpu.VMEM((tm, d), BF16),
                        pltpu.VMEM((tm, d), BF16)],
        compiler_params=_params(("arbitrary", "arbitrary")),
        name="swa_layer",
    )(sinks, x, cos, sin, wqkv, wo, g, b)


LRU_SUB = 256
LRU_NSUB = 2
LRU_NLT = LRU_WIDTH // LANES
LRU_PAD = 2 * SUBLANES


def _rglru_kernel(x_ref, xn_ref, win_ref, cw_ref, cb_ref, wr_ref, br_ref, wi_ref, bi_ref, lam_ref, wo_ref,
                  g_ref, b_ref, o_ref, ustage_scr, a_scr, b_scr, hcar_scr, gu_scr):
    i = pl.program_id(1)
    ts = LRU_SUB
    w = LRU_WIDTH
    pad = LRU_PAD
    sl = SUBLANES

    @pl.when(i == 0)
    def _():
        ustage_scr[:, :, 0:sl, :] = jnp.zeros((LRU_NSUB, LRU_NLT, sl, LANES), F32)
        a_scr[:, :, :, 0:pad, :] = jnp.ones((LRU_NSUB, 3, LRU_NLT, pad, LANES), F32)
        b_scr[:, :, :, 0:pad, :] = jnp.zeros((LRU_NSUB, 3, LRU_NLT, pad, LANES), F32)
        hcar_scr[...] = jnp.zeros_like(hcar_scr)

    @pl.when(i > 0)
    def _():
        ustage_scr[:, :, 0:sl, :] = ustage_scr[:, :, ts:ts + sl, :]
        a_scr[:, 0, :, sl:pad, :] = a_scr[:, 0, :, ts + sl:ts + pad, :]
        b_scr[:, 0, :, sl:pad, :] = b_scr[:, 0, :, ts + sl:ts + pad, :]

    cw = cw_ref[...]
    cb = cb_ref[...]
    z = -lam_ref[...]
    softplus = jnp.maximum(z, 0.0) + jnp.log1p(jnp.exp(-jnp.abs(z)))
    c0 = math.sqrt(2.0 / math.pi)

    @pl.when(i == 0)
    def _():
        for sub in range(LRU_NSUB):
            gu_scr[0, sub] = _dot(x_ref[sub].astype(BF16), win_ref[...])

    def project_next(slot, sub):
        gu_scr[1 - slot, sub] = _dot(xn_ref[sub].astype(BF16), win_ref[...])

    def conv_gates(slot, sub):
        u = gu_scr[slot, sub, :, w:2 * w]
        ucs = []
        for lt in range(LRU_NLT):
            ls = slice(lt * LANES, (lt + 1) * LANES)
            us = ustage_scr.at[sub, lt]
            us[sl:sl + ts, :] = u[:, ls]
            uc = us[sl - 3:sl - 3 + ts, :] * cw[0:1, ls]
            uc = uc + us[sl - 2:sl - 2 + ts, :] * cw[1:2, ls]
            uc = uc + us[sl - 1:sl - 1 + ts, :] * cw[2:3, ls]
            uc = uc + u[:, ls] * cw[3:4, ls]
            ucs.append(uc + cb[:, ls])
        uc = jnp.concatenate(ucs, axis=1)

        ucb = uc.astype(BF16)
        rs, igs = [], []
        for h in range(LRU_BLOCKS):
            blk = ucb[:, h * LRU_BLOCK_W:(h + 1) * LRU_BLOCK_W]
            rs.append(_dot(blk, wr_ref[h]))
            igs.append(_dot(blk, wi_ref[h]))
        return uc, jnp.concatenate(rs, axis=-1), jnp.concatenate(igs, axis=-1)

    def recurrence(slot, sub, uc, r_pre, ig_pre):
        hblk = hcar_scr[sub]
        gate = gu_scr[slot, sub, :, 0:w]
        r = _sigmoid(r_pre + br_ref[...])
        ig = _sigmoid(ig_pre + bi_ref[...])
        log_a = (-LRU_C * r) * softplus
        a0 = jnp.exp(log_a)
        th = jnp.tanh(log_a)
        b0 = (jnp.sqrt(-2.0 * th) * lax.rsqrt(1.0 - th)) * (ig * uc)

        a8, b8 = [], []
        for lt in range(LRU_NLT):
            ls = slice(lt * LANES, (lt + 1) * LANES)
            a_scr[sub, 0, lt, pad:pad + ts, :] = a0[:, ls]
            b_scr[sub, 0, lt, pad:pad + ts, :] = b0[:, ls]
            for k, d in enumerate((1, 2)):
                a_cur = a_scr[sub, k, lt, sl:pad + ts, :]
                b_cur = b_scr[sub, k, lt, sl:pad + ts, :]
                a_scr[sub, k + 1, lt, sl:pad + ts, :] = a_cur * a_scr[sub, k, lt, sl - d:pad + ts - d, :]
                b_scr[sub, k + 1, lt, sl:pad + ts, :] = (a_cur * b_scr[sub, k, lt, sl - d:pad + ts - d, :]
                                                        + b_cur)
            a_cur = a_scr[sub, 2, lt, pad:pad + ts, :]
            a8.append(a_cur * a_scr[sub, 2, lt, pad - 4:pad - 4 + ts, :])
            b8.append(a_cur * b_scr[sub, 2, lt, pad - 4:pad - 4 + ts, :] + b_scr[sub, 2, lt, pad:pad + ts, :])
        a8 = jnp.concatenate(a8, axis=1)
        b8 = jnp.concatenate(b8, axis=1)

        hs = []
        for j in range(ts // sl):
            hblk = a8[j * sl:(j + 1) * sl] * hblk + b8[j * sl:(j + 1) * sl]
            hs.append(hblk)
        h = jnp.concatenate(hs, axis=0)

        gelu = gate * (0.5 * (1.0 + jnp.tanh(c0 * (gate + 0.044715 * (gate * gate * gate)))))
        hcar_scr[sub] = hblk
        return (h * gelu).astype(BF16)

    def out_proj(sub, yb):
        return ALPHA * x_ref[sub] + _dot(yb, wo_ref[...])

    def step(slot):
        zs = []
        for sub in range(LRU_NSUB):
            uc, r_pre, ig_pre = conv_gates(slot, sub)
            project_next(slot, sub)
            zs.append(out_proj(sub, recurrence(slot, sub, uc, r_pre, ig_pre)))
            if sub > 0:
                o_ref[sub - 1] = _layer_norm(zs[sub - 1], g_ref[...], b_ref[...])
        o_ref[LRU_NSUB - 1] = _layer_norm(zs[-1], g_ref[...], b_ref[...])

    for parity in range(2):
        pl.when(lax.rem(i, 2) == parity)(functools.partial(step, parity))


def _rglru_layer(x, win, cw, cb, wr, br, wi, bi, lam, wo, g, b):
    bsz, s, d = x.shape
    tm = LRU_SUB
    w = LRU_WIDTH
    last = s // tm - 1
    return pl.pallas_call(
        _rglru_kernel,
        grid=(bsz // LRU_NSUB, s // tm),
        in_specs=[pl.BlockSpec((LRU_NSUB, tm, d), lambda bi_, i: (bi_, i, 0)),
                  pl.BlockSpec((LRU_NSUB, tm, d), lambda bi_, i: (bi_, jnp.minimum(i + 1, last), 0)),
                  _resident((d, 2 * w)), _resident((LRU_CONV, w)), _resident((1, w)),
                  _resident((LRU_BLOCKS, LRU_BLOCK_W, LRU_BLOCK_W)), _resident((1, w)),
                  _resident((LRU_BLOCKS, LRU_BLOCK_W, LRU_BLOCK_W)), _resident((1, w)),
                  _resident((1, w)), _resident((w, d)), _resident((1, d)), _resident((1, d))],
        out_specs=pl.BlockSpec((LRU_NSUB, tm, d), lambda bi_, i: (bi_, i, 0)),
        out_shape=jax.ShapeDtypeStruct((bsz, s, d), F32),
        scratch_shapes=[pltpu.VMEM((LRU_NSUB, LRU_NLT, SUBLANES + LRU_SUB, LANES), F32),
                        pltpu.VMEM((LRU_NSUB, 3, LRU_NLT, LRU_PAD + LRU_SUB, LANES), F32),
                        pltpu.VMEM((LRU_NSUB, 3, LRU_NLT, LRU_PAD + LRU_SUB, LANES), F32),
                        pltpu.VMEM((LRU_NSUB, SUBLANES, w), F32),
                        pltpu.VMEM((2, LRU_NSUB, tm, 2 * w), F32)],
        compiler_params=_params(("arbitrary", "arbitrary")),
        name="rglru_layer",
    )(x, x, win, cw, cb, wr, br, wi, bi, lam, wo, g, b)


MLA_TM = 1024
MLA_TQ = 1024
MLA_HALF = MLA_TQ // 2
MLA_TK = 512
C_DOWN_PAD = 768
MLA_Q_SCALE = (C_NOPE + C_ROPE) ** -0.5 * math.log2(math.e)


def _mla_proj_kernel(x_ref, cos_ref, sin_ref, wd_ref, qn_ref, kvn_ref, wuq_ref, wukv_ref,
                     q_ref, k_ref, v_ref):
    tm = MLA_TM
    xb = x_ref[0].astype(BF16)
    c = _dot(xb, wd_ref[...])
    cq = c[:, :C_Q_RANK]
    ckv = c[:, C_Q_RANK:C_Q_RANK + C_KV_RANK]
    kr_slab = c[:, C_Q_RANK + C_KV_RANK:]
    cq = cq * lax.rsqrt(jnp.mean(cq * cq, axis=-1, keepdims=True) + RMS_EPS) * qn_ref[...]
    ckv = ckv * lax.rsqrt(jnp.mean(ckv * ckv, axis=-1, keepdims=True) + RMS_EPS) * kvn_ref[...]
    q = _dot(cq.astype(BF16), wuq_ref[...]) * MLA_Q_SCALE
    kv = _dot(ckv.astype(BF16), wukv_ref[...])

    cos = cos_ref[...]
    sin = sin_ref[...]
    lane = lax.broadcasted_iota(jnp.int32, (tm, LANES), 1)
    first_half = (lane & 32) == 0
    low = lane < 64
    nn = C_HEADS * C_NOPE

    kr = _rope_slab(kr_slab, cos, sin, first_half)
    kr_pair = (kr.astype(BF16), pltpu.roll(kr, 64, 1).astype(BF16))
    for pair in range(C_HEADS // 2):
        qr = _rope_slab(q[:, nn + pair * LANES:nn + (pair + 1) * LANES], cos, sin, first_half)
        for half in range(2):
            h = 2 * pair + half
            qr_h = jnp.where(low, qr, 0.0) if half == 0 else jnp.where(low, 0.0, qr)
            q_ref[0, h, :, 0:C_NOPE] = q[:, h * C_NOPE:(h + 1) * C_NOPE].astype(BF16)
            q_ref[0, h, :, C_NOPE:C_QK_PAD] = qr_h.astype(BF16)
            k_ref[0, h, :, 0:C_NOPE] = kv[:, h * C_NOPE:(h + 1) * C_NOPE].astype(BF16)
            k_ref[0, h, :, C_NOPE:C_QK_PAD] = kr_pair[half]
            v_ref[0, h] = kv[:, nn + h * C_V:nn + (h + 1) * C_V].astype(BF16)


def _mla_proj(x, cos, sin, wd, qn, kvn, wuq, wukv):
    bsz, s, d = x.shape
    tm = MLA_TM
    qk_shape = jax.ShapeDtypeStruct((bsz, C_HEADS, s, C_QK_PAD), BF16)
    v_shape = jax.ShapeDtypeStruct((bsz, C_HEADS, s, C_V), BF16)
    return pl.pallas_call(
        _mla_proj_kernel,
        grid=(bsz, s // tm),
        in_specs=[pl.BlockSpec((1, tm, d), lambda bi, i: (bi, i, 0)),
                  pl.BlockSpec((tm, LANES), lambda bi, i: (i, 0)),
                  pl.BlockSpec((tm, LANES), lambda bi, i: (i, 0)),
                  _resident(wd.shape), _resident(qn.shape), _resident(kvn.shape),
                  _resident(wuq.shape), _resident(wukv.shape)],
        out_specs=[pl.BlockSpec((1, C_HEADS, tm, C_QK_PAD), lambda bi, i: (bi, 0, i, 0)),
                   pl.BlockSpec((1, C_HEADS, tm, C_QK_PAD), lambda bi, i: (bi, 0, i, 0)),
                   pl.BlockSpec((1, C_HEADS, tm, C_V), lambda bi, i: (bi, 0, i, 0))],
        out_shape=[qk_shape, qk_shape, v_shape],
        compiler_params=_params(("parallel", "parallel")),
        name="mla_proj",
    )(x, cos, sin, wd, qn, kvn, wuq, wukv)


def _mla_attn_kernel(q_ref, qn_ref, k_ref, v_ref, o_ref, m_scr, acc_scr, s_scr):
    i = pl.program_id(2)
    tk, hq = MLA_TK, MLA_HALF

    m_scr[...] = jnp.full_like(m_scr, -jnp.inf)
    acc_scr[...] = jnp.zeros_like(acc_scr)
    ones = jnp.ones((tk, LANES), BF16)

    def scores(slot, hf, j, queries=q_ref):
        k0 = pl.multiple_of(j * tk, tk)
        s_scr[slot, hf] = _dot_nt(queries[0, 0, hf * hq:(hf + 1) * hq, :], k_ref[0, 0, pl.ds(k0, tk), :])

    def consume(slot, hf, j, diagonal):
        k0 = pl.multiple_of(j * tk, tk)
        vv = jnp.concatenate([v_ref[0, 0, pl.ds(k0, tk), :], ones], axis=1)
        s = s_scr[slot, hf]
        if diagonal:
            qpos = lax.broadcasted_iota(jnp.int32, (hq, tk), 0)
            kpos = lax.broadcasted_iota(jnp.int32, (hq, tk), 1)
            s = jnp.where(kpos <= qpos, s, NEG)
        m_prev = m_scr[hf]
        m_next = jnp.maximum(m_prev, jnp.max(s, axis=-1, keepdims=True))
        p = jnp.exp2(s - jnp.tile(m_next, (1, tk // LANES)))
        corr = jnp.exp2(m_prev - m_next)
        acc_scr[hf] = jnp.tile(corr, (1, 2)) * acc_scr[hf] + _dot(p.astype(BF16), vv)
        m_scr[hf] = m_next

    @pl.when(i == 0)
    def _():
        scores(0, 0, 0)
        scores(0, 1, 0)

    def chunk_pair(j):
        scores(1, 0, j + 1)
        scores(1, 1, j + 1)
        consume(0, 0, j, False)
        consume(0, 1, j, False)
        scores(0, 0, j + 2)
        scores(0, 1, j + 2)
        consume(1, 0, j + 1, False)
        consume(1, 1, j + 1, False)

    @pl.when(i >= 4)
    def _():
        for p in range(4):
            chunk_pair(2 * p)

    done = jnp.where(i >= 4, 4, 0)

    def tail():
        scores(1, 1, 2 * i + 1)
        consume(0, 0, 2 * i, True)
        consume(0, 1, 2 * i, False)
        scores(0, 0, 0, queries=qn_ref)
        scores(0, 1, 0, queries=qn_ref)
        consume(1, 1, 2 * i + 1, True)
        for hf in range(2):
            acc = acc_scr[hf]
            o_ref[0, hf * hq:(hf + 1) * hq, :] = (acc[:, :C_V] / acc[:, C_V:]).astype(BF16)

    for rest in range(4):
        @pl.when(i % 4 == rest)
        def _(rest=rest):
            for p in range(rest):
                chunk_pair(2 * (done + p))
            tail()


def _mla_attn(q, k, v):
    bsz, nh, s, _ = q.shape
    tq = MLA_TQ
    last = s // tq - 1
    assert last < 8, "the kernel decomposes the tile index into three bits"
    return pl.pallas_call(
        _mla_attn_kernel,
        grid=(bsz, nh, s // tq),
        in_specs=[pl.BlockSpec((1, 1, tq, C_QK_PAD), lambda b, h, i: (b, h, i, 0)),
                  pl.BlockSpec((1, 1, tq, C_QK_PAD), lambda b, h, i: (b, h, jnp.minimum(i + 1, last), 0)),
                  pl.BlockSpec((1, 1, s, C_QK_PAD), lambda b, h, i: (b, h, 0, 0)),
                  pl.BlockSpec((1, 1, s, C_V), lambda b, h, i: (b, h, 0, 0))],
        out_specs=pl.BlockSpec((1, tq, C_V), lambda b, h, i: (b, i, h)),
        out_shape=jax.ShapeDtypeStruct((bsz, s, nh * C_V), BF16),
        scratch_shapes=[pltpu.VMEM((2, MLA_HALF, LANES), F32),
                        pltpu.VMEM((2, MLA_HALF, 2 * C_V), F32),
                        pltpu.VMEM((2, 2, MLA_HALF, MLA_TK), F32)],
        compiler_params=_params(("parallel", "parallel", "arbitrary")),
        name="mla_attn",
    )(q, q, k, v)


X_TM = 1024
X_LN_ROWS = 256


def _xattn_kernel(*refs, mixer_prologue):
    tm = X_TM
    if mixer_prologue:
        a_ref, wa_ref, ga_ref, ba_ref, x_ref, wq_ref, mkv_ref, wo_ref, g_ref, b_ref, o_ref = refs
        x = _layer_norm(ALPHA * x_ref[0] + _dot(a_ref[0], wa_ref[...]), ga_ref[...], ba_ref[...])
    else:
        x_ref, wq_ref, mkv_ref, wo_ref, g_ref, b_ref, o_ref = refs
        x = x_ref[0]
    q = (_dot(x.astype(BF16), wq_ref[...]) * (X_HEAD_DIM ** -0.5 * LOG2E)).astype(BF16)
    heads = [(h * X_HEAD_DIM, (h + 1) * X_HEAD_DIM) for h in range(X_HEADS)]
    scores = [_dot_nt(q[:, lo:hi], mkv_ref[0, :, lo:hi]) for lo, hi in heads]
    probs, inv_denoms = [], []
    for s in scores:
        p = jnp.exp2(s - jnp.max(s, axis=-1, keepdims=True))
        inv_denoms.append(1.0 / jnp.sum(p, axis=-1, keepdims=True))
        probs.append(p.astype(BF16))
    outs = [(_dot(probs[h], mkv_ref[0, :, D_MODEL + lo:D_MODEL + hi]) * inv_denoms[h]).astype(BF16)
            for h, (lo, hi) in enumerate(heads)]
    o = jnp.concatenate(outs, axis=-1)
    for r0 in range(0, tm, X_LN_ROWS):
        rows = slice(r0, r0 + X_LN_ROWS)
        y = _dot(o[rows], wo_ref[...])
        o_ref[0, rows, :] = _layer_norm(ALPHA * x[rows] + y, g_ref[...], b_ref[...])


def _xattn_layer(x, layer, wq, mkv, wo, g, b, mixer=None):
    bsz, s, d = x.shape
    tm = X_TM
    tile = pl.BlockSpec((1, tm, d), lambda bi, i: (bi, i, 0))
    in_specs = [tile, _resident((d, d), layer),
                pl.BlockSpec((1, MEM_LEN, 2 * d), lambda bi, i: (bi, 0, 0)),
                _resident((d, d), layer), _resident((1, d)), _resident((1, d))]
    args = (x, wq, mkv, wo, g, b)
    if mixer is not None:
        in_specs = [tile, _resident((d, d)), _resident((1, d)), _resident((1, d))] + in_specs
        args = tuple(mixer) + args
    return pl.pallas_call(
        functools.partial(_xattn_kernel, mixer_prologue=mixer is not None),
        grid=(bsz, s // tm),
        in_specs=in_specs,
        out_specs=tile,
        out_shape=jax.ShapeDtypeStruct((bsz, s, d), F32),
        compiler_params=_params(("parallel", "parallel")),
        name="xattn_layer",
    )(*args)


FFN_TM = 512
FFN_LN_ROWS = 256
FFN_DOWN_GROUPS = (6, FFN_NCHUNK)


def _ffn_kernel(x_ref, wup_ref, cw_ref, cb_ref, wdn_ref, g_ref, b_ref, o_ref, hprev_scr, act_scr,
                acc_scr, hstage_scr):
    i = pl.program_id(1)
    tm = FFN_TM

    @pl.when(i == 0)
    def _():
        hprev_scr[...] = jnp.zeros_like(hprev_scr)

    x = x_ref[0]
    xb = x.astype(BF16)

    def up_conv(col0, slot):
        cols = slice(col0, col0 + FFN_CHUNK)
        h = _dot(xb, wup_ref[:, cols])
        prev = hprev_scr[:, cols]
        hprev_scr[:, cols] = h[tm - SUBLANES:, :]
        cw = cw_ref[:, cols]
        cb = cb_ref[:, cols]
        outs = []
        for lt in range(FFN_CHUNK // LANES):
            ls = slice(lt * LANES, (lt + 1) * LANES)
            hs = hstage_scr.at[slot, lt]
            hs[0:SUBLANES, :] = prev[:, ls]
            hs[SUBLANES:SUBLANES + tm, :] = h[:, ls]
            y = hs[SUBLANES - 2:SUBLANES - 2 + tm, :] * cw[0:1, ls]
            y = y + hs[SUBLANES - 1:SUBLANES - 1 + tm, :] * cw[1:2, ls]
            y = y + h[:, ls] * cw[2:3, ls]
            outs.append(y + cb[:, ls])
        return jnp.concatenate(outs, axis=1)

    def down(rows, c0, c1):
        ks = slice(c0 * FFN_CHUNK, c1 * FFN_CHUNK)
        return _dot(act_scr[rows, ks], wdn_ref[ks, :])

    everything = slice(0, tm)
    for c in range(FFN_NCHUNK):
        gt = up_conv(c * FFN_CHUNK, 2 * (c % 2))
        val = up_conv(D_FF + c * FFN_CHUNK, 2 * (c % 2) + 1)
        act_scr[:, c * FFN_CHUNK:(c + 1) * FFN_CHUNK] = ((gt * _sigmoid(gt)) * val).astype(BF16)
        if c + 1 in FFN_DOWN_GROUPS[:-1]:
            gi = FFN_DOWN_GROUPS.index(c + 1)
            if gi == 0:
                acc_scr[...] = down(everything, 0, c + 1)
            else:
                acc_scr[...] += down(everything, FFN_DOWN_GROUPS[gi - 1], c + 1)
    for r0 in range(0, tm, FFN_LN_ROWS):
        rows = slice(r0, r0 + FFN_LN_ROWS)
        out = acc_scr[rows, :] + down(rows, FFN_DOWN_GROUPS[-2], FFN_DOWN_GROUPS[-1])
        o_ref[0, rows, :] = _layer_norm(ALPHA * x[rows] + out, g_ref[...], b_ref[...])


def _ffn_layer(x, layer, wup, cw, cb, wdn, g, b):
    bsz, s, d = x.shape
    tm = FFN_TM
    return pl.pallas_call(
        _ffn_kernel,
        grid=(bsz, s // tm),
        in_specs=[pl.BlockSpec((1, tm, d), lambda bi, i: (bi, i, 0)),
                  _resident((d, 2 * D_FF), layer), _resident((FFN_CONV, 2 * D_FF)), _resident((1, 2 * D_FF)),
                  _resident((D_FF, d), layer), _resident((1, d)), _resident((1, d))],
        out_specs=pl.BlockSpec((1, tm, d), lambda bi, i: (bi, i, 0)),
        out_shape=jax.ShapeDtypeStruct((bsz, s, d), F32),
        scratch_shapes=[pltpu.VMEM((SUBLANES, 2 * D_FF), F32), pltpu.VMEM((tm, D_FF), BF16),
                        pltpu.VMEM((tm, d), F32),
                        pltpu.VMEM((4, FFN_CHUNK // LANES, SUBLANES + tm, LANES), F32)],
        compiler_params=_params(("arbitrary", "arbitrary")),
        name="ffn_layer",
    )(x, wup, cw, cb, wdn, g, b)


def kernel(x, mem, a_w_qkv, a_sinks, a_w_o, b_w_in, b_conv_w, b_conv_b, b_w_rgate, b_b_rgate, b_w_igate, b_b_igate, b_lambda, b_w_o, c_w_down, c_q_norm, c_kv_norm, c_w_uq, c_w_ukv, c_w_o, mem_w_kv, x_w_q, x_w_o, f_w_up, f_conv_w, f_conv_b, f_w_down, ln_g, ln_b):
    bsz, s, d = x.shape
    cos, sin = _rope_tables(s)

    mkv = _matmul(mem.reshape(bsz * MEM_LEN, d), mem_w_kv.astype(BF16), 512, BF16)
    mkv = mkv.reshape(bsz, MEM_LEN, 2 * d)

    def row(v):
        return v.reshape(1, -1)

    xq_all, xo_all = x_w_q.astype(BF16), x_w_o.astype(BF16)
    fup_all, fdn_all = f_w_up.astype(BF16), f_w_down.astype(BF16)
    aqkv_all, ao_all = a_w_qkv.astype(BF16), a_w_o.astype(BF16)

    for i in range(DEPTH):
        kind, j = i % N_MIXERS, i // N_MIXERS
        g0, b0 = row(ln_g[i, 0]), row(ln_b[i, 0])
        mixer = None
        if kind == 0:
            x = _swa_layer(x, j, a_sinks[j], cos, sin, aqkv_all, ao_all, g0, b0)
        elif kind == 1:
            x = _rglru_layer(x, b_w_in[j].astype(BF16), b_conv_w[j], row(b_conv_b[j]),
                             b_w_rgate[j].astype(BF16), row(b_b_rgate[j]),
                             b_w_igate[j].astype(BF16), row(b_b_igate[j]),
                             row(b_lambda[j]), b_w_o[j].astype(BF16), g0, b0)
        else:
            wd = jnp.pad(c_w_down[j], ((0, 0), (0, C_DOWN_PAD - c_w_down.shape[-1]))).astype(BF16)
            wuq = c_w_uq[j].reshape(C_Q_RANK, C_HEADS, C_NOPE + C_ROPE)
            wuq = jnp.concatenate([wuq[:, :, :C_NOPE].reshape(C_Q_RANK, -1),
                                   wuq[:, :, C_NOPE:].reshape(C_Q_RANK, -1)], axis=-1).astype(BF16)
            wukv = c_w_ukv[j].reshape(C_KV_RANK, C_HEADS, C_NOPE + C_V)
            wukv = jnp.concatenate([wukv[:, :, :C_NOPE].reshape(C_KV_RANK, -1),
                                    wukv[:, :, C_NOPE:].reshape(C_KV_RANK, -1)], axis=-1).astype(BF16)
            q, k, v = _mla_proj(x, cos, sin, wd, row(c_q_norm[j]), row(c_kv_norm[j]), wuq, wukv)
            mixer = (_mla_attn(q, k, v), c_w_o[j].astype(BF16), g0, b0)
        x = _xattn_layer(x, i, xq_all, mkv, xo_all, row(ln_g[i, 1]), row(ln_b[i, 1]), mixer=mixer)
        x = _ffn_layer(x, i, fup_all, f_conv_w[i], row(f_conv_b[i]), fdn_all, row(ln_g[i, 2]), row(ln_b[i, 2]))
    return x
```

```python
import functools
import math

import jax
import jax.numpy as jnp
from jax import lax
from jax.experimental import pallas as pl
from jax.experimental.pallas import tpu as pltpu

D_MODEL = 1024
DEPTH = 4
N_MIXERS = 3
MEM_LEN = 256
BLOCK = 128
ROPE_THETA = 10000.0
NEG = -1e30
LN_EPS = 1e-5
RMS_EPS = 1e-6

A_HEADS = 16
A_KV_HEADS = 4
A_HEAD_DIM = 64

LRU_WIDTH = D_MODEL
LRU_BLOCKS = 4
LRU_BLOCK_W = LRU_WIDTH // LRU_BLOCKS
LRU_CONV = 4
LRU_C = 8.0

C_HEADS = 8
C_NOPE = 128
C_ROPE = 64
C_V = 128
C_Q_RANK = 384
C_KV_RANK = 256
C_QK_PAD = 256

X_HEADS = 4
X_HEAD_DIM = D_MODEL // X_HEADS

D_FF = 2816
FFN_CONV = 3
FFN_CHUNK = 256
FFN_NCHUNK = D_FF // FFN_CHUNK

ALPHA = (2.0 * DEPTH) ** 0.25

LANES = 128
SUBLANES = 8
VMEM_LIMIT = 56 * 1024 * 1024

BF16 = jnp.bfloat16
F32 = jnp.float32

NT_DIMS = (((1,), (1,)), ((), ()))
LOG2E = math.log2(math.e)


def _dot(a, b):
    return jnp.dot(a, b, preferred_element_type=F32)


def _dot_nt(a, b):
    return lax.dot_general(a, b, NT_DIMS, preferred_element_type=F32)


def _layer_norm(z, g, b):
    mu = jnp.mean(z, axis=-1, keepdims=True)
    zc = z - mu
    var = jnp.mean(zc * zc, axis=-1, keepdims=True)
    return zc * lax.rsqrt(var + LN_EPS) * g + b


def _sigmoid(z):
    return 0.5 * jnp.tanh(0.5 * z) + 0.5


def _resident(shape, layer=None):
    nd = len(shape)
    if layer is None:
        return pl.BlockSpec(shape, lambda *_: (0,) * nd, pipeline_mode=pl.Buffered(1))
    return pl.BlockSpec((None,) + tuple(shape), lambda *_: (layer,) + (0,) * nd, pipeline_mode=pl.Buffered(1))


def _params(sem):
    return pltpu.CompilerParams(dimension_semantics=sem, vmem_limit_bytes=VMEM_LIMIT)


def _matmul_kernel(a_ref, w_ref, o_ref):
    o_ref[...] = _dot(a_ref[...].astype(BF16), w_ref[...]).astype(o_ref.dtype)


def _matmul(a, w, tm, out_dtype):
    t, k = a.shape
    n = w.shape[1]
    return pl.pallas_call(
        _matmul_kernel,
        grid=(t // tm,),
        in_specs=[pl.BlockSpec((tm, k), lambda i: (i, 0)), _resident((k, n))],
        out_specs=pl.BlockSpec((tm, n), lambda i: (i, 0)),
        out_shape=jax.ShapeDtypeStruct((t, n), out_dtype),
        compiler_params=_params(("parallel",)),
        name="matmul",
    )(a, w)


def _rope_slab(slab, cos, sin_signed, first_half):
    rot = jnp.where(first_half, pltpu.roll(slab, LANES - 32, 1), pltpu.roll(slab, 32, 1))
    return slab * cos + rot * sin_signed


def _rope_tables(seq):
    inv = 1.0 / (ROPE_THETA ** (jnp.arange(0, A_HEAD_DIM, 2, dtype=F32) / A_HEAD_DIM))
    ang = jnp.arange(seq, dtype=F32)[:, None] * inv[None, :]
    cos, sin = jnp.cos(ang), jnp.sin(ang)
    return (jnp.concatenate([cos, cos, cos, cos], axis=-1),
            jnp.concatenate([-sin, sin, -sin, sin], axis=-1))


SWA_TM = 512
SWA_QCOLS = 512
SWA_BREADTH = 2
SWA_NBLK = SWA_TM // BLOCK


def _swa_kernel(sinks_ref, x_ref, cos_ref, sin_ref, wqkv_ref, wo_ref, g_ref, b_ref, o_ref,
                kext_scr, vext_scr, q_scr, o_scr):
    i = pl.program_id(1)
    tm = SWA_TM

    @pl.when(i == 0)
    def _():
        kext_scr[:, 0:BLOCK, :] = jnp.zeros((2 * A_KV_HEADS, BLOCK, LANES), BF16)
        vext_scr[:, 0:BLOCK, :] = jnp.zeros((2 * A_KV_HEADS, BLOCK, LANES), BF16)

    @pl.when(i > 0)
    def _():
        kext_scr[:, 0:BLOCK, :] = kext_scr[:, tm:tm + BLOCK, :]
        vext_scr[:, 0:BLOCK, :] = vext_scr[:, tm:tm + BLOCK, :]

    x = x_ref[0]
    xb = x.astype(BF16)
    cos = cos_ref[...]
    sin = sin_ref[...]
    lane = lax.broadcasted_iota(jnp.int32, (tm, LANES), 1)
    first_half = (lane & 32) == 0
    low = lane < 64

    nq = A_HEADS * A_HEAD_DIM
    nkv = A_KV_HEADS * A_HEAD_DIM

    def project(col0, ncols):
        return _dot(xb, wqkv_ref[:, col0:col0 + ncols])

    for c0 in range(0, nq, SWA_QCOLS):
        qg = project(c0, SWA_QCOLS)
        for j in range(SWA_QCOLS // LANES):
            slab = _rope_slab(qg[:, j * LANES:(j + 1) * LANES], cos, sin, first_half)
            q_scr[:, c0 + j * LANES:c0 + (j + 1) * LANES] = (slab * (A_HEAD_DIM ** -0.5 * LOG2E)).astype(BF16)

    kg = project(nq, nkv)
    vg = project(nq + nkv, nkv)
    for m in range(nkv // LANES):
        kslab = _rope_slab(kg[:, m * LANES:(m + 1) * LANES], cos, sin, first_half)
        vslab = vg[:, m * LANES:(m + 1) * LANES]
        for slab, ext in ((kslab, kext_scr), (vslab, vext_scr)):
            swapped = pltpu.roll(slab, 64, 1)
            h0, h1 = 2 * m, 2 * m + 1
            ext[2 * h0, BLOCK:BLOCK + tm, :] = jnp.where(low, slab, 0.0).astype(BF16)
            ext[2 * h0 + 1, BLOCK:BLOCK + tm, :] = jnp.where(low, 0.0, swapped).astype(BF16)
            ext[2 * h1, BLOCK:BLOCK + tm, :] = jnp.where(low, swapped, 0.0).astype(BF16)
            ext[2 * h1 + 1, BLOCK:BLOCK + tm, :] = jnp.where(low, 0.0, slab).astype(BF16)

    qi = lax.broadcasted_iota(jnp.int32, (2 * BLOCK, 4 * BLOCK), 0) & (BLOCK - 1)
    kj = lax.broadcasted_iota(jnp.int32, (2 * BLOCK, 4 * BLOCK), 1) & (2 * BLOCK - 1)
    dist = qi + BLOCK - kj
    band = (dist >= 0) & (dist < BLOCK)
    band_first = band & ((kj >= BLOCK) | (i > 0))
    top = lax.broadcasted_iota(jnp.int32, (2 * BLOCK, 1), 0) < BLOCK
    low2 = lax.broadcasted_iota(jnp.int32, (2 * BLOCK, LANES), 1) < 64

    def window(ext, h, r0):
        return jnp.concatenate([ext[2 * h, r0:r0 + 2 * BLOCK, :], ext[2 * h + 1, r0:r0 + 2 * BLOCK, :]],
                               axis=0)

    for n, h_first in [(n, h) for n in range(SWA_NBLK) for h in range(0, A_KV_HEADS, SWA_BREADTH)]:
        mask = band_first if n == 0 else band
        r0 = n * BLOCK
        group = range(h_first, h_first + SWA_BREADTH)
        scores = {}
        for h in group:
            qs = jnp.concatenate([q_scr[r0:r0 + BLOCK, (2 * h) * LANES:(2 * h + 1) * LANES],
                                  q_scr[r0:r0 + BLOCK, (2 * h + 1) * LANES:(2 * h + 2) * LANES]], axis=0)
            scores[h] = _dot_nt(qs, window(kext_scr, h, r0))
        probs, inv_denoms = {}, {}
        for h in group:
            s = jnp.where(mask, scores[h], NEG)
            ps, rs = [], []
            for half in range(2):
                sh = s[:, half * 2 * BLOCK:(half + 1) * 2 * BLOCK]
                sink = jnp.where(top, sinks_ref[4 * h + half] * LOG2E, sinks_ref[4 * h + 2 + half] * LOG2E)
                mx = jnp.maximum(jnp.max(sh, axis=-1, keepdims=True), sink)
                p = jnp.exp2(sh - mx)
                rs.append(1.0 / (jnp.sum(p, axis=-1, keepdims=True) + jnp.exp2(sink - mx)))
                ps.append(p.astype(BF16))
            probs[h] = jnp.concatenate(ps, axis=1)
            inv_denoms[h] = jnp.where(low2, rs[0], rs[1])
        for h in group:
            o = _dot(probs[h], window(vext_scr, h, r0)) * inv_denoms[h]
            o_scr[r0:r0 + BLOCK, (2 * h) * LANES:(2 * h + 1) * LANES] = o[:BLOCK].astype(BF16)
            o_scr[r0:r0 + BLOCK, (2 * h + 1) * LANES:(2 * h + 2) * LANES] = o[BLOCK:].astype(BF16)

    for r0 in range(0, tm, tm // 2):
        rows = slice(r0, r0 + tm // 2)
        y = _dot(o_scr[rows, :], wo_ref[...])
        o_ref[0, rows, :] = _layer_norm(ALPHA * x[rows] + y, g_ref[...], b_ref[...])


def _swa_layer(x, layer, sinks, cos, sin, wqkv, wo, g, b):
    bsz, s, d = x.shape
    tm = SWA_TM
    nqkv = wqkv.shape[-1]
    return pl.pallas_call(
        _swa_kernel,
        grid=(bsz, s // tm),
        in_specs=[pl.BlockSpec(memory_space=pltpu.SMEM),
                  pl.BlockSpec((1, tm, d), lambda bi, i: (bi, i, 0)),
                  pl.BlockSpec((tm, LANES), lambda bi, i: (i, 0)),
                  pl.BlockSpec((tm, LANES), lambda bi, i: (i, 0)),
                  _resident((d, nqkv), layer), _resident((d, d), layer), _resident((1, d)), _resident((1, d))],
        out_specs=pl.BlockSpec((1, tm, d), lambda bi, i: (bi, i, 0)),
        out_shape=jax.ShapeDtypeStruct((bsz, s, d), F32),
        scratch_shapes=[pltpu.VMEM((2 * A_KV_HEADS, tm + BLOCK, LANES), BF16),
                        pltpu.VMEM((2 * A_KV_HEADS, tm + BLOCK, LANES), BF16),
                        pltpu.VMEM((tm, d), BF16),
                        pltpu.VMEM((tm, d), BF16)],
        compiler_params=_params(("arbitrary", "arbitrary")),
        name="swa_layer",
    )(sinks, x, cos, sin, wqkv, wo, g, b)


LRU_SUB = 256
LRU_NSUB = 2
LRU_NLT = LRU_WIDTH // LANES
LRU_PAD = 2 * SUBLANES


def _rglru_kernel(x_ref, xn_ref, win_ref, cw_ref, cb_ref, wr_ref, br_ref, wi_ref, bi_ref, lam_ref, wo_ref,
                  g_ref, b_ref, o_ref, ustage_scr, a_scr, b_scr, hcar_scr, gu_scr):
    i = pl.program_id(1)
    ts = LRU_SUB
    w = LRU_WIDTH
    pad = LRU_PAD
    sl = SUBLANES

    @pl.when(i == 0)
    def _():
        ustage_scr[:, :, 0:sl, :] = jnp.zeros((LRU_NSUB, LRU_NLT, sl, LANES), F32)
        a_scr[:, :, :, 0:pad, :] = jnp.ones((LRU_NSUB, 3, LRU_NLT, pad, LANES), F32)
        b_scr[:, :, :, 0:pad, :] = jnp.zeros((LRU_NSUB, 3, LRU_NLT, pad, LANES), F32)
        hcar_scr[...] = jnp.zeros_like(hcar_scr)

    @pl.when(i > 0)
    def _():
        ustage_scr[:, :, 0:sl, :] = ustage_scr[:, :, ts:ts + sl, :]
        a_scr[:, 0, :, sl:pad, :] = a_scr[:, 0, :, ts + sl:ts + pad, :]
        b_scr[:, 0, :, sl:pad, :] = b_scr[:, 0, :, ts + sl:ts + pad, :]

    cw = cw_ref[...]
    cb = cb_ref[...]
    z = -lam_ref[...]
    softplus = jnp.maximum(z, 0.0) + jnp.log1p(jnp.exp(-jnp.abs(z)))
    c0 = math.sqrt(2.0 / math.pi)

    @pl.when(i == 0)
    def _():
        for sub in range(LRU_NSUB):
            gu_scr[0, sub] = _dot(x_ref[sub].astype(BF16), win_ref[...])

    def project_next(slot, sub):
        gu_scr[1 - slot, sub] = _dot(xn_ref[sub].astype(BF16), win_ref[...])

    def conv_gates(slot, sub):
        u = gu_scr[slot, sub, :, w:2 * w]
        ucs = []
        for lt in range(LRU_NLT):
            ls = slice(lt * LANES, (lt + 1) * LANES)
            us = ustage_scr.at[sub, lt]
            us[sl:sl + ts, :] = u[:, ls]
            uc = us[sl - 3:sl - 3 + ts, :] * cw[0:1, ls]
            uc = uc + us[sl - 2:sl - 2 + ts, :] * cw[1:2, ls]
            uc = uc + us[sl - 1:sl - 1 + ts, :] * cw[2:3, ls]
            uc = uc + u[:, ls] * cw[3:4, ls]
            ucs.append(uc + cb[:, ls])
        uc = jnp.concatenate(ucs, axis=1)

        ucb = uc.astype(BF16)
        rs, igs = [], []
        for h in range(LRU_BLOCKS):
            blk = ucb[:, h * LRU_BLOCK_W:(h + 1) * LRU_BLOCK_W]
            rs.append(_dot(blk, wr_ref[h]))
            igs.append(_dot(blk, wi_ref[h]))
        return uc, jnp.concatenate(rs, axis=-1), jnp.concatenate(igs, axis=-1)

    def recurrence(slot, sub, uc, r_pre, ig_pre):
        hblk = hcar_scr[sub]
        gate = gu_scr[slot, sub, :, 0:w]
        r = _sigmoid(r_pre + br_ref[...])
        ig = _sigmoid(ig_pre + bi_ref[...])
        log_a = (-LRU_C * r) * softplus
        a0 = jnp.exp(log_a)
        th = jnp.tanh(log_a)
        b0 = (jnp.sqrt(-2.0 * th) * lax.rsqrt(1.0 - th)) * (ig * uc)

        a8, b8 = [], []
        for lt in range(LRU_NLT):
            ls = slice(lt * LANES, (lt + 1) * LANES)
            a_scr[sub, 0, lt, pad:pad + ts, :] = a0[:, ls]
            b_scr[sub, 0, lt, pad:pad + ts, :] = b0[:, ls]
            for k, d in enumerate((1, 2)):
                a_cur = a_scr[sub, k, lt, sl:pad + ts, :]
                b_cur = b_scr[sub, k, lt, sl:pad + ts, :]
                a_scr[sub, k + 1, lt, sl:pad + ts, :] = a_cur * a_scr[sub, k, lt, sl - d:pad + ts - d, :]
                b_scr[sub, k + 1, lt, sl:pad + ts, :] = (a_cur * b_scr[sub, k, lt, sl - d:pad + ts - d, :]
                                                        + b_cur)
            a_cur = a_scr[sub, 2, lt, pad:pad + ts, :]
            a8.append(a_cur * a_scr[sub, 2, lt, pad - 4:pad - 4 + ts, :])
            b8.append(a_cur * b_scr[sub, 2, lt, pad - 4:pad - 4 + ts, :] + b_scr[sub, 2, lt, pad:pad + ts, :])
        a8 = jnp.concatenate(a8, axis=1)
        b8 = jnp.concatenate(b8, axis=1)

        hs = []
        for j in range(ts // sl):
            hblk = a8[j * sl:(j + 1) * sl] * hblk + b8[j * sl:(j + 1) * sl]
            hs.append(hblk)
        h = jnp.concatenate(hs, axis=0)

        gelu = gate * (0.5 * (1.0 + jnp.tanh(c0 * (gate + 0.044715 * (gate * gate * gate)))))
        hcar_scr[sub] = hblk
        return (h * gelu).astype(BF16)

    def out_proj(sub, yb):
        return ALPHA * x_ref[sub] + _dot(yb, wo_ref[...])

    def step(slot):
        zs = []
        for sub in range(LRU_NSUB):
            uc, r_pre, ig_pre = conv_gates(slot, sub)
            project_next(slot, sub)
            zs.append(out_proj(sub, recurrence(slot, sub, uc, r_pre, ig_pre)))
            if sub > 0:
                o_ref[sub - 1] = _layer_norm(zs[sub - 1], g_ref[...], b_ref[...])
        o_ref[LRU_NSUB - 1] = _layer_norm(zs[-1], g_ref[...], b_ref[...])

    for parity in range(2):
        pl.when(lax.rem(i, 2) == parity)(functools.partial(step, parity))


def _rglru_layer(x, win, cw, cb, wr, br, wi, bi, lam, wo, g, b):
    bsz, s, d = x.shape
    tm = LRU_SUB
    w = LRU_WIDTH
    last = s // tm - 1
    return pl.pallas_call(
        _rglru_kernel,
        grid=(bsz // LRU_NSUB, s // tm),
        in_specs=[pl.BlockSpec((LRU_NSUB, tm, d), lambda bi_, i: (bi_, i, 0)),
                  pl.BlockSpec((LRU_NSUB, tm, d), lambda bi_, i: (bi_, jnp.minimum(i + 1, last), 0)),
                  _resident((d, 2 * w)), _resident((LRU_CONV, w)), _resident((1, w)),
                  _resident((LRU_BLOCKS, LRU_BLOCK_W, LRU_BLOCK_W)), _resident((1, w)),
                  _resident((LRU_BLOCKS, LRU_BLOCK_W, LRU_BLOCK_W)), _resident((1, w)),
                  _resident((1, w)), _resident((w, d)), _resident((1, d)), _resident((1, d))],
        out_specs=pl.BlockSpec((LRU_NSUB, tm, d), lambda bi_, i: (bi_, i, 0)),
        out_shape=jax.ShapeDtypeStruct((bsz, s, d), F32),
        scratch_shapes=[pltpu.VMEM((LRU_NSUB, LRU_NLT, SUBLANES + LRU_SUB, LANES), F32),
                        pltpu.VMEM((LRU_NSUB, 3, LRU_NLT, LRU_PAD + LRU_SUB, LANES), F32),
                        pltpu.VMEM((LRU_NSUB, 3, LRU_NLT, LRU_PAD + LRU_SUB, LANES), F32),
                        pltpu.VMEM((LRU_NSUB, SUBLANES, w), F32),
                        pltpu.VMEM((2, LRU_NSUB, tm, 2 * w), F32)],
        compiler_params=_params(("arbitrary", "arbitrary")),
        name="rglru_layer",
    )(x, x, win, cw, cb, wr, br, wi, bi, lam, wo, g, b)


MLA_TM = 1024
MLA_TQ = 1024
MLA_HALF = MLA_TQ // 2
MLA_TK = 512
C_DOWN_PAD = 768
MLA_Q_SCALE = (C_NOPE + C_ROPE) ** -0.5 * math.log2(math.e)


def _mla_proj_kernel(x_ref, cos_ref, sin_ref, wd_ref, qn_ref, kvn_ref, wuq_ref, wukv_ref,
                     q_ref, k_ref, v_ref):
    tm = MLA_TM
    xb = x_ref[0].astype(BF16)
    c = _dot(xb, wd_ref[...])
    cq = c[:, :C_Q_RANK]
    ckv = c[:, C_Q_RANK:C_Q_RANK + C_KV_RANK]
    kr_slab = c[:, C_Q_RANK + C_KV_RANK:]
    cq = cq * lax.rsqrt(jnp.mean(cq * cq, axis=-1, keepdims=True) + RMS_EPS) * qn_ref[...]
    ckv = ckv * lax.rsqrt(jnp.mean(ckv * ckv, axis=-1, keepdims=True) + RMS_EPS) * kvn_ref[...]
    q = _dot(cq.astype(BF16), wuq_ref[...]) * MLA_Q_SCALE
    kv = _dot(ckv.astype(BF16), wukv_ref[...])

    cos = cos_ref[...]
    sin = sin_ref[...]
    lane = lax.broadcasted_iota(jnp.int32, (tm, LANES), 1)
    first_half = (lane & 32) == 0
    low = lane < 64
    nn = C_HEADS * C_NOPE

    kr = _rope_slab(kr_slab, cos, sin, first_half)
    kr_pair = (kr.astype(BF16), pltpu.roll(kr, 64, 1).astype(BF16))
    for pair in range(C_HEADS // 2):
        qr = _rope_slab(q[:, nn + pair * LANES:nn + (pair + 1) * LANES], cos, sin, first_half)
        for half in range(2):
            h = 2 * pair + half
            qr_h = jnp.where(low, qr, 0.0) if half == 0 else jnp.where(low, 0.0, qr)
            q_ref[0, h, :, 0:C_NOPE] = q[:, h * C_NOPE:(h + 1) * C_NOPE].astype(BF16)
            q_ref[0, h, :, C_NOPE:C_QK_PAD] = qr_h.astype(BF16)
            k_ref[0, h, :, 0:C_NOPE] = kv[:, h * C_NOPE:(h + 1) * C_NOPE].astype(BF16)
            k_ref[0, h, :, C_NOPE:C_QK_PAD] = kr_pair[half]
            v_ref[0, h] = kv[:, nn + h * C_V:nn + (h + 1) * C_V].astype(BF16)


def _mla_proj(x, cos, sin, wd, qn, kvn, wuq, wukv):
    bsz, s, d = x.shape
    tm = MLA_TM
    qk_shape = jax.ShapeDtypeStruct((bsz, C_HEADS, s, C_QK_PAD), BF16)
    v_shape = jax.ShapeDtypeStruct((bsz, C_HEADS, s, C_V), BF16)
    return pl.pallas_call(
        _mla_proj_kernel,
        grid=(bsz, s // tm),
        in_specs=[pl.BlockSpec((1, tm, d), lambda bi, i: (bi, i, 0)),
                  pl.BlockSpec((tm, LANES), lambda bi, i: (i, 0)),
                  pl.BlockSpec((tm, LANES), lambda bi, i: (i, 0)),
                  _resident(wd.shape), _resident(qn.shape), _resident(kvn.shape),
                  _resident(wuq.shape), _resident(wukv.shape)],
        out_specs=[pl.BlockSpec((1, C_HEADS, tm, C_QK_PAD), lambda bi, i: (bi, 0, i, 0)),
                   pl.BlockSpec((1, C_HEADS, tm, C_QK_PAD), lambda bi, i: (bi, 0, i, 0)),
                   pl.BlockSpec((1, C_HEADS, tm, C_V), lambda bi, i: (bi, 0, i, 0))],
        out_shape=[qk_shape, qk_shape, v_shape],
        compiler_params=_params(("parallel", "parallel")),
        name="mla_proj",
    )(x, cos, sin, wd, qn, kvn, wuq, wukv)


def _mla_attn_kernel(q_ref, qn_ref, k_ref, v_ref, o_ref, m_scr, acc_scr, s_scr):
    i = pl.program_id(2)
    tk, hq = MLA_TK, MLA_HALF

    m_scr[...] = jnp.full_like(m_scr, -jnp.inf)
    acc_scr[...] = jnp.zeros_like(acc_scr)
    ones = jnp.ones((tk, LANES), BF16)

    def scores(slot, hf, j, queries=q_ref):
        k0 = pl.multiple_of(j * tk, tk)
        s_scr[slot, hf] = _dot_nt(queries[0, 0, hf * hq:(hf + 1) * hq, :], k_ref[0, 0, pl.ds(k0, tk), :])

    def consume(slot, hf, j, diagonal):
        k0 = pl.multiple_of(j * tk, tk)
        vv = jnp.concatenate([v_ref[0, 0, pl.ds(k0, tk), :], ones], axis=1)
        s = s_scr[slot, hf]
        if diagonal:
            qpos = lax.broadcasted_iota(jnp.int32, (hq, tk), 0)
            kpos = lax.broadcasted_iota(jnp.int32, (hq, tk), 1)
            s = jnp.where(kpos <= qpos, s, NEG)
        m_prev = m_scr[hf]
        m_next = jnp.maximum(m_prev, jnp.max(s, axis=-1, keepdims=True))
        p = jnp.exp2(s - jnp.tile(m_next, (1, tk // LANES)))
        corr = jnp.exp2(m_prev - m_next)
        acc_scr[hf] = jnp.tile(corr, (1, 2)) * acc_scr[hf] + _dot(p.astype(BF16), vv)
        m_scr[hf] = m_next

    @pl.when(i == 0)
    def _():
        scores(0, 0, 0)
        scores(0, 1, 0)

    def chunk_pair(j):
        scores(1, 0, j + 1)
        scores(1, 1, j + 1)
        consume(0, 0, j, False)
        consume(0, 1, j, False)
        scores(0, 0, j + 2)
        scores(0, 1, j + 2)
        consume(1, 0, j + 1, False)
        consume(1, 1, j + 1, False)

    @pl.when(i >= 4)
    def _():
        for p in range(4):
            chunk_pair(2 * p)

    done = jnp.where(i >= 4, 4, 0)

    def tail():
        scores(1, 1, 2 * i + 1)
        consume(0, 0, 2 * i, True)
        consume(0, 1, 2 * i, False)
        scores(0, 0, 0, queries=qn_ref)
        scores(0, 1, 0, queries=qn_ref)
        consume(1, 1, 2 * i + 1, True)
        for hf in range(2):
            acc = acc_scr[hf]
            o_ref[0, hf * hq:(hf + 1) * hq, :] = (acc[:, :C_V] / acc[:, C_V:]).astype(BF16)

    for rest in range(4):
        @pl.when(i % 4 == rest)
        def _(rest=rest):
            for p in range(rest):
                chunk_pair(2 * (done + p))
            tail()


def _mla_attn(q, k, v):
    bsz, nh, s, _ = q.shape
    tq = MLA_TQ
    last = s // tq - 1
    assert last < 8, "the kernel decomposes the tile index into three bits"
    return pl.pallas_call(
        _mla_attn_kernel,
        grid=(bsz, nh, s // tq),
        in_specs=[pl.BlockSpec((1, 1, tq, C_QK_PAD), lambda b, h, i: (b, h, i, 0)),
                  pl.BlockSpec((1, 1, tq, C_QK_PAD), lambda b, h, i: (b, h, jnp.minimum(i + 1, last), 0)),
                  pl.BlockSpec((1, 1, s, C_QK_PAD), lambda b, h, i: (b, h, 0, 0)),
                  pl.BlockSpec((1, 1, s, C_V), lambda b, h, i: (b, h, 0, 0))],
        out_specs=pl.BlockSpec((1, tq, C_V), lambda b, h, i: (b, i, h)),
        out_shape=jax.ShapeDtypeStruct((bsz, s, nh * C_V), BF16),
        scratch_shapes=[pltpu.VMEM((2, MLA_HALF, LANES), F32),
                        pltpu.VMEM((2, MLA_HALF, 2 * C_V), F32),
                        pltpu.VMEM((2, 2, MLA_HALF, MLA_TK), F32)],
        compiler_params=_params(("parallel", "parallel", "arbitrary")),
        name="mla_attn",
    )(q, q, k, v)


X_TM = 1024
X_LN_ROWS = 256


def _xattn_kernel(*refs, mixer_prologue):
    tm = X_TM
    if mixer_prologue:
        a_ref, wa_ref, ga_ref, ba_ref, x_ref, wq_ref, mkv_ref, wo_ref, g_ref, b_ref, o_ref = refs
        x = _layer_norm(ALPHA * x_ref[0] + _dot(a_ref[0], wa_ref[...]), ga_ref[...], ba_ref[...])
    else:
        x_ref, wq_ref, mkv_ref, wo_ref, g_ref, b_ref, o_ref = refs
        x = x_ref[0]
    q = (_dot(x.astype(BF16), wq_ref[...]) * (X_HEAD_DIM ** -0.5 * LOG2E)).astype(BF16)
    heads = [(h * X_HEAD_DIM, (h + 1) * X_HEAD_DIM) for h in range(X_HEADS)]
    scores = [_dot_nt(q[:, lo:hi], mkv_ref[0, :, lo:hi]) for lo, hi in heads]
    probs, inv_denoms = [], []
    for s in scores:
        p = jnp.exp2(s - jnp.max(s, axis=-1, keepdims=True))
        inv_denoms.append(1.0 / jnp.sum(p, axis=-1, keepdims=True))
        probs.append(p.astype(BF16))
    outs = [(_dot(probs[h], mkv_ref[0, :, D_MODEL + lo:D_MODEL + hi]) * inv_denoms[h]).astype(BF16)
            for h, (lo, hi) in enumerate(heads)]
    o = jnp.concatenate(outs, axis=-1)
    for r0 in range(0, tm, X_LN_ROWS):
        rows = slice(r0, r0 + X_LN_ROWS)
        y = _dot(o[rows], wo_ref[...])
        o_ref[0, rows, :] = _layer_norm(ALPHA * x[rows] + y, g_ref[...], b_ref[...])


def _xattn_layer(x, layer, wq, mkv, wo, g, b, mixer=None):
    bsz, s, d = x.shape
    tm = X_TM
    tile = pl.BlockSpec((1, tm, d), lambda bi, i: (bi, i, 0))
    in_specs = [tile, _resident((d, d), layer),
                pl.BlockSpec((1, MEM_LEN, 2 * d), lambda bi, i: (bi, 0, 0)),
                _resident((d, d), layer), _resident((1, d)), _resident((1, d))]
    args = (x, wq, mkv, wo, g, b)
    if mixer is not None:
        in_specs = [tile, _resident((d, d)), _resident((1, d)), _resident((1, d))] + in_specs
        args = tuple(mixer) + args
    return pl.pallas_call(
        functools.partial(_xattn_kernel, mixer_prologue=mixer is not None),
        grid=(bsz, s // tm),
        in_specs=in_specs,
        out_specs=tile,
        out_shape=jax.ShapeDtypeStruct((bsz, s, d), F32),
        compiler_params=_params(("parallel", "parallel")),
        name="xattn_layer",
    )(*args)


FFN_TM = 512
FFN_LN_ROWS = 256
FFN_DOWN_GROUPS = (6, FFN_NCHUNK)


def _ffn_kernel(x_ref, wup_ref, cw_ref, cb_ref, wdn_ref, g_ref, b_ref, o_ref, hprev_scr, act_scr,
                acc_scr, hstage_scr):
    i = pl.program_id(1)
    tm = FFN_TM

    @pl.when(i == 0)
    def _():
        hprev_scr[...] = jnp.zeros_like(hprev_scr)

    x = x_ref[0]
    xb = x.astype(BF16)

    def up_conv(col0, slot):
        cols = slice(col0, col0 + FFN_CHUNK)
        h = _dot(xb, wup_ref[:, cols])
        prev = hprev_scr[:, cols]
        hprev_scr[:, cols] = h[tm - SUBLANES:, :]
        cw = cw_ref[:, cols]
        cb = cb_ref[:, cols]
        outs = []
        for lt in range(FFN_CHUNK // LANES):
            ls = slice(lt * LANES, (lt + 1) * LANES)
            hs = hstage_scr.at[slot, lt]
            hs[0:SUBLANES, :] = prev[:, ls]
            hs[SUBLANES:SUBLANES + tm, :] = h[:, ls]
            y = hs[SUBLANES - 2:SUBLANES - 2 + tm, :] * cw[0:1, ls]
            y = y + hs[SUBLANES - 1:SUBLANES - 1 + tm, :] * cw[1:2, ls]
            y = y + h[:, ls] * cw[2:3, ls]
            outs.append(y + cb[:, ls])
        return jnp.concatenate(outs, axis=1)

    def down(rows, c0, c1):
        ks = slice(c0 * FFN_CHUNK, c1 * FFN_CHUNK)
        return _dot(act_scr[rows, ks], wdn_ref[ks, :])

    everything = slice(0, tm)
    for c in range(FFN_NCHUNK):
        gt = up_conv(c * FFN_CHUNK, 2 * (c % 2))
        val = up_conv(D_FF + c * FFN_CHUNK, 2 * (c % 2) + 1)
        act_scr[:, c * FFN_CHUNK:(c + 1) * FFN_CHUNK] = ((gt * _sigmoid(gt)) * val).astype(BF16)
        if c + 1 in FFN_DOWN_GROUPS[:-1]:
            gi = FFN_DOWN_GROUPS.index(c + 1)
            if gi == 0:
                acc_scr[...] = down(everything, 0, c + 1)
            else:
                acc_scr[...] += down(everything, FFN_DOWN_GROUPS[gi - 1], c + 1)
    for r0 in range(0, tm, FFN_LN_ROWS):
        rows = slice(r0, r0 + FFN_LN_ROWS)
        out = acc_scr[rows, :] + down(rows, FFN_DOWN_GROUPS[-2], FFN_DOWN_GROUPS[-1])
        o_ref[0, rows, :] = _layer_norm(ALPHA * x[rows] + out, g_ref[...], b_ref[...])


def _ffn_layer(x, layer, wup, cw, cb, wdn, g, b):
    bsz, s, d = x.shape
    tm = FFN_TM
    return pl.pallas_call(
        _ffn_kernel,
        grid=(bsz, s // tm),
        in_specs=[pl.BlockSpec((1, tm, d), lambda bi, i: (bi, i, 0)),
                  _resident((d, 2 * D_FF), layer), _resident((FFN_CONV, 2 * D_FF)), _resident((1, 2 * D_FF)),
                  _resident((D_FF, d), layer), _resident((1, d)), _resident((1, d))],
        out_specs=pl.BlockSpec((1, tm, d), lambda bi, i: (bi, i, 0)),
        out_shape=jax.ShapeDtypeStruct((bsz, s, d), F32),
        scratch_shapes=[pltpu.VMEM((SUBLANES, 2 * D_FF), F32), pltpu.VMEM((tm, D_FF), BF16),
                        pltpu.VMEM((tm, d), F32),
                        pltpu.VMEM((4, FFN_CHUNK // LANES, SUBLANES + tm, LANES), F32)],
        compiler_params=_params(("arbitrary", "arbitrary")),
        name="ffn_layer",
    )(x, wup, cw, cb, wdn, g, b)


def kernel(x, mem, a_w_qkv, a_sinks, a_w_o, b_w_in, b_conv_w, b_conv_b, b_w_rgate, b_b_rgate, b_w_igate, b_b_igate, b_lambda, b_w_o, c_w_down, c_q_norm, c_kv_norm, c_w_uq, c_w_ukv, c_w_o, mem_w_kv, x_w_q, x_w_o, f_w_up, f_conv_w, f_conv_b, f_w_down, ln_g, ln_b):
    bsz, s, d = x.shape
    cos, sin = _rope_tables(s)

    mkv = _matmul(mem.reshape(bsz * MEM_LEN, d), mem_w_kv.astype(BF16), 512, BF16)
    mkv = mkv.reshape(bsz, MEM_LEN, 2 * d)

    def row(v):
        return v.reshape(1, -1)

    xq_all, xo_all = x_w_q.astype(BF16), x_w_o.astype(BF16)
    fup_all, fdn_all = f_w_up.astype(BF16), f_w_down.astype(BF16)
    aqkv_all, ao_all = a_w_qkv.astype(BF16), a_w_o.astype(BF16)

    for i in range(DEPTH):
        kind, j = i % N_MIXERS, i // N_MIXERS
        g0, b0 = row(ln_g[i, 0]), row(ln_b[i, 0])
        mixer = None
        if kind == 0:
            x = _swa_layer(x, j, a_sinks[j], cos, sin, aqkv_all, ao_all, g0, b0)
        elif kind == 1:
            x = _rglru_layer(x, b_w_in[j].astype(BF16), b_conv_w[j], row(b_conv_b[j]),
                             b_w_rgate[j].astype(BF16), row(b_b_rgate[j]),
                             b_w_igate[j].astype(BF16), row(b_b_igate[j]),
                             row(b_lambda[j]), b_w_o[j].astype(BF16), g0, b0)
        else:
            wd = jnp.pad(c_w_down[j], ((0, 0), (0, C_DOWN_PAD - c_w_down.shape[-1]))).astype(BF16)
            wuq = c_w_uq[j].reshape(C_Q_RANK, C_HEADS, C_NOPE + C_ROPE)
            wuq = jnp.concatenate([wuq[:, :, :C_NOPE].reshape(C_Q_RANK, -1),
                                   wuq[:, :, C_NOPE:].reshape(C_Q_RANK, -1)], axis=-1).astype(BF16)
            wukv = c_w_ukv[j].reshape(C_KV_RANK, C_HEADS, C_NOPE + C_V)
            wukv = jnp.concatenate([wukv[:, :, :C_NOPE].reshape(C_KV_RANK, -1),
                                    wukv[:, :, C_NOPE:].reshape(C_KV_RANK, -1)], axis=-1).astype(BF16)
            q, k, v = _mla_proj(x, cos, sin, wd, row(c_q_norm[j]), row(c_kv_norm[j]), wuq, wukv)
            mixer = (_mla_attn(q, k, v), c_w_o[j].astype(BF16), g0, b0)
        x = _xattn_layer(x, i, xq_all, mkv, xo_all, row(ln_g[i, 1]), row(ln_b[i, 1]), mixer=mixer)
        x = _ffn_layer(x, i, fup_all, f_conv_w[i], row(f_conv_b[i]), fdn_all, row(ln_g[i, 2]), row(ln_b[i, 2]))
    return x
```

```python
import functools
import math

import jax
import jax.numpy as jnp
from jax import lax
from jax.experimental import pallas as pl
from jax.experimental.pallas import tpu as pltpu

D_MODEL = 1024
DEPTH = 4
N_MIXERS = 3
MEM_LEN = 256
BLOCK = 128
ROPE_THETA = 10000.0
NEG = -1e30
LN_EPS = 1e-5
RMS_EPS = 1e-6

A_HEADS = 16
A_KV_HEADS = 4
A_HEAD_DIM = 64

LRU_WIDTH = D_MODEL
LRU_BLOCKS = 4
LRU_BLOCK_W = LRU_WIDTH // LRU_BLOCKS
LRU_CONV = 4
LRU_C = 8.0

C_HEADS = 8
C_NOPE = 128
C_ROPE = 64
C_V = 128
C_Q_RANK = 384
C_KV_RANK = 256
C_QK_PAD = 256

X_HEADS = 4
X_HEAD_DIM = D_MODEL // X_HEADS

D_FF = 2816
FFN_CONV = 3
FFN_CHUNK = 256
FFN_NCHUNK = D_FF // FFN_CHUNK

ALPHA = (2.0 * DEPTH) ** 0.25

LANES = 128
SUBLANES = 8
VMEM_LIMIT = 56 * 1024 * 1024

BF16 = jnp.bfloat16
F32 = jnp.float32

NT_DIMS = (((1,), (1,)), ((), ()))
LOG2E = math.log2(math.e)


def _dot(a, b):
    return jnp.dot(a, b, preferred_element_type=F32)


def _dot_nt(a, b):
    return lax.dot_general(a, b, NT_DIMS, preferred_element_type=F32)


def _layer_norm(z, g, b):
    mu = jnp.mean(z, axis=-1, keepdims=True)
    zc = z - mu
    var = jnp.mean(zc * zc, axis=-1, keepdims=True)
    return zc * lax.rsqrt(var + LN_EPS) * g + b


def _sigmoid(z):
    return 0.5 * jnp.tanh(0.5 * z) + 0.5


def _resident(shape, layer=None):
    nd = len(shape)
    if layer is None:
        return pl.BlockSpec(shape, lambda *_: (0,) * nd, pipeline_mode=pl.Buffered(1))
    return pl.BlockSpec((None,) + tuple(shape), lambda *_: (layer,) + (0,) * nd, pipeline_mode=pl.Buffered(1))


def _params(sem):
    return pltpu.CompilerParams(dimension_semantics=sem, vmem_limit_bytes=VMEM_LIMIT)


def _matmul_kernel(a_ref, w_ref, o_ref):
    o_ref[...] = _dot(a_ref[...].astype(BF16), w_ref[...]).astype(o_ref.dtype)


def _matmul(a, w, tm, out_dtype):
    t, k = a.shape
    n = w.shape[1]
    return pl.pallas_call(
        _matmul_kernel,
        grid=(t // tm,),
        in_specs=[pl.BlockSpec((tm, k), lambda i: (i, 0)), _resident((k, n))],
        out_specs=pl.BlockSpec((tm, n), lambda i: (i, 0)),
        out_shape=jax.ShapeDtypeStruct((t, n), out_dtype),
        compiler_params=_params(("parallel",)),
        name="matmul",
    )(a, w)


def _rope_slab(slab, cos, sin_signed, first_half):
    rot = jnp.where(first_half, pltpu.roll(slab, LANES - 32, 1), pltpu.roll(slab, 32, 1))
    return slab * cos + rot * sin_signed


def _rope_tables(seq):
    inv = 1.0 / (ROPE_THETA ** (jnp.arange(0, A_HEAD_DIM, 2, dtype=F32) / A_HEAD_DIM))
    ang = jnp.arange(seq, dtype=F32)[:, None] * inv[None, :]
    cos, sin = jnp.cos(ang), jnp.sin(ang)
    return (jnp.concatenate([cos, cos, cos, cos], axis=-1),
            jnp.concatenate([-sin, sin, -sin, sin], axis=-1))


SWA_TM = 512
SWA_NBLK = SWA_TM // BLOCK


def _swa_kernel(sinks_ref, x_ref, cos_ref, sin_ref, wqkv_ref, wo_ref, g_ref, b_ref, o_ref,
                kext_scr, vext_scr, q_scr, o_scr):
    i = pl.program_id(1)
    tm = SWA_TM

    @pl.when(i == 0)
    def _():
        kext_scr[:, 0:BLOCK, :] = jnp.zeros((2 * A_KV_HEADS, BLOCK, LANES), BF16)
        vext_scr[:, 0:BLOCK, :] = jnp.zeros((2 * A_KV_HEADS, BLOCK, LANES), BF16)

    x = x_ref[0]
    qkv = _dot(x.astype(BF16), wqkv_ref[...])
    cos = cos_ref[...]
    sin = sin_ref[...]
    lane = lax.broadcasted_iota(jnp.int32, (tm, LANES), 1)
    first_half = (lane & 32) == 0
    low = lane < 64

    nq = A_HEADS * A_HEAD_DIM
    nkv = A_KV_HEADS * A_HEAD_DIM
    for j in range(nq // LANES):
        slab = _rope_slab(qkv[:, j * LANES:(j + 1) * LANES], cos, sin, first_half)
        q_scr[:, j * LANES:(j + 1) * LANES] = (slab * (A_HEAD_DIM ** -0.5 * LOG2E)).astype(BF16)

    for m in range(nkv // LANES):
        kslab = _rope_slab(qkv[:, nq + m * LANES:nq + (m + 1) * LANES], cos, sin, first_half)
        vslab = qkv[:, nq + nkv + m * LANES:nq + nkv + (m + 1) * LANES]
        for slab, ext in ((kslab, kext_scr), (vslab, vext_scr)):
            swapped = pltpu.roll(slab, 64, 1)
            h0, h1 = 2 * m, 2 * m + 1
            ext[2 * h0, BLOCK:BLOCK + tm, :] = jnp.where(low, slab, 0.0).astype(BF16)
            ext[2 * h0 + 1, BLOCK:BLOCK + tm, :] = jnp.where(low, 0.0, swapped).astype(BF16)
            ext[2 * h1, BLOCK:BLOCK + tm, :] = jnp.where(low, swapped, 0.0).astype(BF16)
            ext[2 * h1 + 1, BLOCK:BLOCK + tm, :] = jnp.where(low, 0.0, slab).astype(BF16)

    qi = lax.broadcasted_iota(jnp.int32, (2 * BLOCK, 4 * BLOCK), 0) & (BLOCK - 1)
    kj = lax.broadcasted_iota(jnp.int32, (2 * BLOCK, 4 * BLOCK), 1) & (2 * BLOCK - 1)
    dist = qi + BLOCK - kj
    band = (dist >= 0) & (dist < BLOCK)
    band_first = band & ((kj >= BLOCK) | (i > 0))
    top = lax.broadcasted_iota(jnp.int32, (2 * BLOCK, 1), 0) < BLOCK
    low2 = lax.broadcasted_iota(jnp.int32, (2 * BLOCK, LANES), 1) < 64

    def window(ext, h, r0):
        return jnp.concatenate([ext[2 * h, r0:r0 + 2 * BLOCK, :], ext[2 * h + 1, r0:r0 + 2 * BLOCK, :]],
                               axis=0)

    for n in range(SWA_NBLK):
        mask = band_first if n == 0 else band
        r0 = n * BLOCK
        scores = []
        for h in range(A_KV_HEADS):
            qs = jnp.concatenate([q_scr[r0:r0 + BLOCK, (2 * h) * LANES:(2 * h + 1) * LANES],
                                  q_scr[r0:r0 + BLOCK, (2 * h + 1) * LANES:(2 * h + 2) * LANES]], axis=0)
            scores.append(_dot_nt(qs, window(kext_scr, h, r0)))
        probs, inv_denoms = [], []
        for h in range(A_KV_HEADS):
            s = jnp.where(mask, scores[h], NEG)
            ps, rs = [], []
            for half in range(2):
                sh = s[:, half * 2 * BLOCK:(half + 1) * 2 * BLOCK]
                sink = jnp.where(top, sinks_ref[4 * h + half] * LOG2E, sinks_ref[4 * h + 2 + half] * LOG2E)
                mx = jnp.maximum(jnp.max(sh, axis=-1, keepdims=True), sink)
                p = jnp.exp2(sh - mx)
                rs.append(1.0 / (jnp.sum(p, axis=-1, keepdims=True) + jnp.exp2(sink - mx)))
                ps.append(p.astype(BF16))
            probs.append(jnp.concatenate(ps, axis=1))
            inv_denoms.append(jnp.where(low2, rs[0], rs[1]))
        for h in range(A_KV_HEADS):
            o = _dot(probs[h], window(vext_scr, h, r0)) * inv_denoms[h]
            o_scr[r0:r0 + BLOCK, (2 * h) * LANES:(2 * h + 1) * LANES] = o[:BLOCK].astype(BF16)
            o_scr[r0:r0 + BLOCK, (2 * h + 1) * LANES:(2 * h + 2) * LANES] = o[BLOCK:].astype(BF16)

    kext_scr[:, 0:BLOCK, :] = kext_scr[:, tm:tm + BLOCK, :]
    vext_scr[:, 0:BLOCK, :] = vext_scr[:, tm:tm + BLOCK, :]

    for r0 in range(0, tm, tm // 2):
        rows = slice(r0, r0 + tm // 2)
        y = _dot(o_scr[rows, :], wo_ref[...])
        o_ref[0, rows, :] = _layer_norm(ALPHA * x[rows] + y, g_ref[...], b_ref[...])


def _swa_layer(x, layer, sinks, cos, sin, wqkv, wo, g, b):
    bsz, s, d = x.shape
    tm = SWA_TM
    nqkv = wqkv.shape[-1]
    return pl.pallas_call(
        _swa_kernel,
        grid=(bsz, s // tm),
        in_specs=[pl.BlockSpec(memory_space=pltpu.SMEM),
                  pl.BlockSpec((1, tm, d), lambda bi, i: (bi, i, 0)),
                  pl.BlockSpec((tm, LANES), lambda bi, i: (i, 0)),
                  pl.BlockSpec((tm, LANES), lambda bi, i: (i, 0)),
                  _resident((d, nqkv), layer), _resident((d, d), layer), _resident((1, d)), _resident((1, d))],
        out_specs=pl.BlockSpec((1, tm, d), lambda bi, i: (bi, i, 0)),
        out_shape=jax.ShapeDtypeStruct((bsz, s, d), F32),
        scratch_shapes=[pltpu.VMEM((2 * A_KV_HEADS, tm + BLOCK, LANES), BF16),
                        pltpu.VMEM((2 * A_KV_HEADS, tm + BLOCK, LANES), BF16),
                        pltpu.VMEM((tm, d), BF16),
                        pltpu.VMEM((tm, d), BF16)],
        compiler_params=_params(("arbitrary", "arbitrary")),
        name="swa_layer",
    )(sinks, x, cos, sin, wqkv, wo, g, b)


LRU_SUB = 256
LRU_NSUB = 2
LRU_NLT = LRU_WIDTH // LANES
LRU_PAD = 2 * SUBLANES


def _rglru_kernel(x_ref, xn_ref, win_ref, cw_ref, cb_ref, wr_ref, br_ref, wi_ref, bi_ref, lam_ref, wo_ref,
                  g_ref, b_ref, o_ref, ustage_scr, a_scr, b_scr, hcar_scr, gu_scr):
    i = pl.program_id(1)
    ts = LRU_SUB
    w = LRU_WIDTH
    pad = LRU_PAD
    sl = SUBLANES

    @pl.when(i == 0)
    def _():
        ustage_scr[:, :, 0:sl, :] = jnp.zeros((LRU_NSUB, LRU_NLT, sl, LANES), F32)
        a_scr[:, :, :, 0:pad, :] = jnp.ones((LRU_NSUB, 3, LRU_NLT, pad, LANES), F32)
        b_scr[:, :, :, 0:pad, :] = jnp.zeros((LRU_NSUB, 3, LRU_NLT, pad, LANES), F32)
        hcar_scr[...] = jnp.zeros_like(hcar_scr)

    def hand_over():
        ustage_scr[:, :, 0:sl, :] = ustage_scr[:, :, ts:ts + sl, :]
        a_scr[:, 0, :, sl:pad, :] = a_scr[:, 0, :, ts + sl:ts + pad, :]
        b_scr[:, 0, :, sl:pad, :] = b_scr[:, 0, :, ts + sl:ts + pad, :]

    cw = cw_ref[...]
    cb = cb_ref[...]
    z = -lam_ref[...]
    softplus = jnp.maximum(z, 0.0) + jnp.log1p(jnp.exp(-jnp.abs(z)))
    c0 = math.sqrt(2.0 / math.pi)

    @pl.when(i == 0)
    def _():
        for sub in range(LRU_NSUB):
            gu_scr[0, sub] = _dot(x_ref[sub].astype(BF16), win_ref[...])

    def project_next(slot, sub):
        gu_scr[1 - slot, sub] = _dot(xn_ref[sub].astype(BF16), win_ref[...])

    def conv_gates(slot, sub):
        u = gu_scr[slot, sub, :, w:2 * w]
        ucs = []
        for lt in range(LRU_NLT):
            ls = slice(lt * LANES, (lt + 1) * LANES)
            us = ustage_scr.at[sub, lt]
            us[sl:sl + ts, :] = u[:, ls]
            uc = us[sl - 3:sl - 3 + ts, :] * cw[0:1, ls]
            uc = uc + us[sl - 2:sl - 2 + ts, :] * cw[1:2, ls]
            uc = uc + us[sl - 1:sl - 1 + ts, :] * cw[2:3, ls]
            uc = uc + u[:, ls] * cw[3:4, ls]
            ucs.append(uc + cb[:, ls])
        uc = jnp.concatenate(ucs, axis=1)

        ucb = uc.astype(BF16)
        rs, igs = [], []
        for h in range(LRU_BLOCKS):
            blk = ucb[:, h * LRU_BLOCK_W:(h + 1) * LRU_BLOCK_W]
            rs.append(_dot(blk, wr_ref[h]))
            igs.append(_dot(blk, wi_ref[h]))
        return uc, jnp.concatenate(rs, axis=-1), jnp.concatenate(igs, axis=-1)

    def recurrence(slot, sub, uc, r_pre, ig_pre):
        hblk = hcar_scr[sub]
        gate = gu_scr[slot, sub, :, 0:w]
        r = _sigmoid(r_pre + br_ref[...])
        ig = _sigmoid(ig_pre + bi_ref[...])
        log_a = (-LRU_C * r) * softplus
        a0 = jnp.exp(log_a)
        th = jnp.tanh(log_a)
        b0 = (jnp.sqrt(-2.0 * th) * lax.rsqrt(1.0 - th)) * (ig * uc)

        a8, b8 = [], []
        for lt in range(LRU_NLT):
            ls = slice(lt * LANES, (lt + 1) * LANES)
            a_scr[sub, 0, lt, pad:pad + ts, :] = a0[:, ls]
            b_scr[sub, 0, lt, pad:pad + ts, :] = b0[:, ls]
            for k, d in enumerate((1, 2)):
                a_cur = a_scr[sub, k, lt, sl:pad + ts, :]
                b_cur = b_scr[sub, k, lt, sl:pad + ts, :]
                a_scr[sub, k + 1, lt, sl:pad + ts, :] = a_cur * a_scr[sub, k, lt, sl - d:pad + ts - d, :]
                b_scr[sub, k + 1, lt, sl:pad + ts, :] = (a_cur * b_scr[sub, k, lt, sl - d:pad + ts - d, :]
                                                        + b_cur)
            a_cur = a_scr[sub, 2, lt, pad:pad + ts, :]
            a8.append(a_cur * a_scr[sub, 2, lt, pad - 4:pad - 4 + ts, :])
            b8.append(a_cur * b_scr[sub, 2, lt, pad - 4:pad - 4 + ts, :] + b_scr[sub, 2, lt, pad:pad + ts, :])
        a8 = jnp.concatenate(a8, axis=1)
        b8 = jnp.concatenate(b8, axis=1)

        hs = []
        for j in range(ts // sl):
            hblk = a8[j * sl:(j + 1) * sl] * hblk + b8[j * sl:(j + 1) * sl]
            hs.append(hblk)
        h = jnp.concatenate(hs, axis=0)

        gelu = gate * (0.5 * (1.0 + jnp.tanh(c0 * (gate + 0.044715 * (gate * gate * gate)))))
        hcar_scr[sub] = hblk
        return (h * gelu).astype(BF16)

    def out_proj(sub, yb):
        return ALPHA * x_ref[sub] + _dot(yb, wo_ref[...])

    def step(slot):
        zs = []
        for sub in range(LRU_NSUB):
            uc, r_pre, ig_pre = conv_gates(slot, sub)
            project_next(slot, sub)
            zs.append(out_proj(sub, recurrence(slot, sub, uc, r_pre, ig_pre)))
            if sub > 0:
                o_ref[sub - 1] = _layer_norm(zs[sub - 1], g_ref[...], b_ref[...])
        hand_over()
        o_ref[LRU_NSUB - 1] = _layer_norm(zs[-1], g_ref[...], b_ref[...])

    for parity in range(2):
        pl.when(lax.rem(i, 2) == parity)(functools.partial(step, parity))


def _rglru_layer(x, win, cw, cb, wr, br, wi, bi, lam, wo, g, b):
    bsz, s, d = x.shape
    tm = LRU_SUB
    w = LRU_WIDTH
    last = s // tm - 1
    return pl.pallas_call(
        _rglru_kernel,
        grid=(bsz // LRU_NSUB, s // tm),
        in_specs=[pl.BlockSpec((LRU_NSUB, tm, d), lambda bi_, i: (bi_, i, 0)),
                  pl.BlockSpec((LRU_NSUB, tm, d), lambda bi_, i: (bi_, jnp.minimum(i + 1, last), 0)),
                  _resident((d, 2 * w)), _resident((LRU_CONV, w)), _resident((1, w)),
                  _resident((LRU_BLOCKS, LRU_BLOCK_W, LRU_BLOCK_W)), _resident((1, w)),
                  _resident((LRU_BLOCKS, LRU_BLOCK_W, LRU_BLOCK_W)), _resident((1, w)),
                  _resident((1, w)), _resident((w, d)), _resident((1, d)), _resident((1, d))],
        out_specs=pl.BlockSpec((LRU_NSUB, tm, d), lambda bi_, i: (bi_, i, 0)),
        out_shape=jax.ShapeDtypeStruct((bsz, s, d), F32),
        scratch_shapes=[pltpu.VMEM((LRU_NSUB, LRU_NLT, SUBLANES + LRU_SUB, LANES), F32),
                        pltpu.VMEM((LRU_NSUB, 3, LRU_NLT, LRU_PAD + LRU_SUB, LANES), F32),
                        pltpu.VMEM((LRU_NSUB, 3, LRU_NLT, LRU_PAD + LRU_SUB, LANES), F32),
                        pltpu.VMEM((LRU_NSUB, SUBLANES, w), F32),
                        pltpu.VMEM((2, LRU_NSUB, tm, 2 * w), F32)],
        compiler_params=_params(("arbitrary", "arbitrary")),
        name="rglru_layer",
    )(x, x, win, cw, cb, wr, br, wi, bi, lam, wo, g, b)


MLA_TM = 1024
MLA_TQ = 1024
MLA_HALF = MLA_TQ // 2
MLA_TK = 512
C_DOWN_PAD = 768
MLA_Q_SCALE = (C_NOPE + C_ROPE) ** -0.5 * math.log2(math.e)


def _mla_proj_kernel(x_ref, cos_ref, sin_ref, wd_ref, qn_ref, kvn_ref, wuq_ref, wukv_ref,
                     q_ref, k_ref, v_ref):
    tm = MLA_TM
    xb = x_ref[0].astype(BF16)
    c = _dot(xb, wd_ref[...])
    cq = c[:, :C_Q_RANK]
    ckv = c[:, C_Q_RANK:C_Q_RANK + C_KV_RANK]
    kr_slab = c[:, C_Q_RANK + C_KV_RANK:]
    cq = cq * lax.rsqrt(jnp.mean(cq * cq, axis=-1, keepdims=True) + RMS_EPS) * qn_ref[...]
    ckv = ckv * lax.rsqrt(jnp.mean(ckv * ckv, axis=-1, keepdims=True) + RMS_EPS) * kvn_ref[...]
    q = _dot(cq.astype(BF16), wuq_ref[...]) * MLA_Q_SCALE
    kv = _dot(ckv.astype(BF16), wukv_ref[...])

    cos = cos_ref[...]
    sin = sin_ref[...]
    lane = lax.broadcasted_iota(jnp.int32, (tm, LANES), 1)
    first_half = (lane & 32) == 0
    low = lane < 64
    nn = C_HEADS * C_NOPE

    kr = _rope_slab(kr_slab, cos, sin, first_half)
    kr_pair = (kr.astype(BF16), pltpu.roll(kr, 64, 1).astype(BF16))
    for pair in range(C_HEADS // 2):
        qr = _rope_slab(q[:, nn + pair * LANES:nn + (pair + 1) * LANES], cos, sin, first_half)
        for half in range(2):
            h = 2 * pair + half
            qr_h = jnp.where(low, qr, 0.0) if half == 0 else jnp.where(low, 0.0, qr)
            q_ref[0, h, :, 0:C_NOPE] = q[:, h * C_NOPE:(h + 1) * C_NOPE].astype(BF16)
            q_ref[0, h, :, C_NOPE:C_QK_PAD] = qr_h.astype(BF16)
            k_ref[0, h, :, 0:C_NOPE] = kv[:, h * C_NOPE:(h + 1) * C_NOPE].astype(BF16)
            k_ref[0, h, :, C_NOPE:C_QK_PAD] = kr_pair[half]
            v_ref[0, h] = kv[:, nn + h * C_V:nn + (h + 1) * C_V].astype(BF16)


def _mla_proj(x, cos, sin, wd, qn, kvn, wuq, wukv):
    bsz, s, d = x.shape
    tm = MLA_TM
    qk_shape = jax.ShapeDtypeStruct((bsz, C_HEADS, s, C_QK_PAD), BF16)
    v_shape = jax.ShapeDtypeStruct((bsz, C_HEADS, s, C_V), BF16)
    return pl.pallas_call(
        _mla_proj_kernel,
        grid=(bsz, s // tm),
        in_specs=[pl.BlockSpec((1, tm, d), lambda bi, i: (bi, i, 0)),
                  pl.BlockSpec((tm, LANES), lambda bi, i: (i, 0)),
                  pl.BlockSpec((tm, LANES), lambda bi, i: (i, 0)),
                  _resident(wd.shape), _resident(qn.shape), _resident(kvn.shape),
                  _resident(wuq.shape), _resident(wukv.shape)],
        out_specs=[pl.BlockSpec((1, C_HEADS, tm, C_QK_PAD), lambda bi, i: (bi, 0, i, 0)),
                   pl.BlockSpec((1, C_HEADS, tm, C_QK_PAD), lambda bi, i: (bi, 0, i, 0)),
                   pl.BlockSpec((1, C_HEADS, tm, C_V), lambda bi, i: (bi, 0, i, 0))],
        out_shape=[qk_shape, qk_shape, v_shape],
        compiler_params=_params(("parallel", "parallel")),
        name="mla_proj",
    )(x, cos, sin, wd, qn, kvn, wuq, wukv)


def _mla_attn_kernel(q_ref, qn_ref, k_ref, v_ref, o_ref, m_scr, acc_scr, s_scr):
    i = pl.program_id(2)
    tk, hq = MLA_TK, MLA_HALF

    m_scr[...] = jnp.full_like(m_scr, -jnp.inf)
    acc_scr[...] = jnp.zeros_like(acc_scr)
    ones = jnp.ones((tk, LANES), BF16)

    def scores(slot, hf, j, queries=q_ref):
        k0 = pl.multiple_of(j * tk, tk)
        s_scr[slot, hf] = _dot_nt(queries[0, 0, hf * hq:(hf + 1) * hq, :], k_ref[0, 0, pl.ds(k0, tk), :])

    def consume(slot, hf, j, diagonal):
        k0 = pl.multiple_of(j * tk, tk)
        vv = jnp.concatenate([v_ref[0, 0, pl.ds(k0, tk), :], ones], axis=1)
        s = s_scr[slot, hf]
        if diagonal:
            qpos = lax.broadcasted_iota(jnp.int32, (hq, tk), 0)
            kpos = lax.broadcasted_iota(jnp.int32, (hq, tk), 1)
            s = jnp.where(kpos <= qpos, s, NEG)
        m_prev = m_scr[hf]
        m_next = jnp.maximum(m_prev, jnp.max(s, axis=-1, keepdims=True))
        p = jnp.exp2(s - jnp.tile(m_next, (1, tk // LANES)))
        corr = jnp.exp2(m_prev - m_next)
        acc_scr[hf] = jnp.tile(corr, (1, 2)) * acc_scr[hf] + _dot(p.astype(BF16), vv)
        m_scr[hf] = m_next

    @pl.when(i == 0)
    def _():
        scores(0, 0, 0)
        scores(0, 1, 0)

    def chunk_pair(j):
        scores(1, 0, j + 1)
        scores(1, 1, j + 1)
        consume(0, 0, j, False)
        consume(0, 1, j, False)
        scores(0, 0, j + 2)
        scores(0, 1, j + 2)
        consume(1, 0, j + 1, False)
        consume(1, 1, j + 1, False)

    @pl.when(i >= 4)
    def _():
        for p in range(4):
            chunk_pair(2 * p)

    done = jnp.where(i >= 4, 4, 0)

    def tail():
        scores(1, 1, 2 * i + 1)
        consume(0, 0, 2 * i, True)
        consume(0, 1, 2 * i, False)
        scores(0, 0, 0, queries=qn_ref)
        scores(0, 1, 0, queries=qn_ref)
        consume(1, 1, 2 * i + 1, True)
        for hf in range(2):
            acc = acc_scr[hf]
            o_ref[0, hf * hq:(hf + 1) * hq, :] = (acc[:, :C_V] / acc[:, C_V:]).astype(BF16)

    for rest in range(4):
        @pl.when(i % 4 == rest)
        def _(rest=rest):
            for p in range(rest):
                chunk_pair(2 * (done + p))
            tail()


def _mla_attn(q, k, v):
    bsz, nh, s, _ = q.shape
    tq = MLA_TQ
    last = s // tq - 1
    assert last < 8, "the kernel decomposes the tile index into three bits"
    return pl.pallas_call(
        _mla_attn_kernel,
        grid=(bsz, nh, s // tq),
        in_specs=[pl.BlockSpec((1, 1, tq, C_QK_PAD), lambda b, h, i: (b, h, i, 0)),
                  pl.BlockSpec((1, 1, tq, C_QK_PAD), lambda b, h, i: (b, h, jnp.minimum(i + 1, last), 0)),
                  pl.BlockSpec((1, 1, s, C_QK_PAD), lambda b, h, i: (b, h, 0, 0)),
                  pl.BlockSpec((1, 1, s, C_V), lambda b, h, i: (b, h, 0, 0))],
        out_specs=pl.BlockSpec((1, tq, C_V), lambda b, h, i: (b, i, h)),
        out_shape=jax.ShapeDtypeStruct((bsz, s, nh * C_V), BF16),
        scratch_shapes=[pltpu.VMEM((2, MLA_HALF, LANES), F32),
                        pltpu.VMEM((2, MLA_HALF, 2 * C_V), F32),
                        pltpu.VMEM((2, 2, MLA_HALF, MLA_TK), F32)],
        compiler_params=_params(("parallel", "parallel", "arbitrary")),
        name="mla_attn",
    )(q, q, k, v)


X_TM = 1024
X_LN_ROWS = 256


def _xattn_kernel(*refs, mixer_prologue):
    tm = X_TM
    if mixer_prologue:
        a_ref, wa_ref, ga_ref, ba_ref, x_ref, wq_ref, mkv_ref, wo_ref, g_ref, b_ref, o_ref = refs
        x = _layer_norm(ALPHA * x_ref[0] + _dot(a_ref[0], wa_ref[...]), ga_ref[...], ba_ref[...])
    else:
        x_ref, wq_ref, mkv_ref, wo_ref, g_ref, b_ref, o_ref = refs
        x = x_ref[0]
    q = (_dot(x.astype(BF16), wq_ref[...]) * (X_HEAD_DIM ** -0.5 * LOG2E)).astype(BF16)
    heads = [(h * X_HEAD_DIM, (h + 1) * X_HEAD_DIM) for h in range(X_HEADS)]
    scores = [_dot_nt(q[:, lo:hi], mkv_ref[0, :, lo:hi]) for lo, hi in heads]
    probs, inv_denoms = [], []
    for s in scores:
        p = jnp.exp2(s - jnp.max(s, axis=-1, keepdims=True))
        inv_denoms.append(1.0 / jnp.sum(p, axis=-1, keepdims=True))
        probs.append(p.astype(BF16))
    outs = [(_dot(probs[h], mkv_ref[0, :, D_MODEL + lo:D_MODEL + hi]) * inv_denoms[h]).astype(BF16)
            for h, (lo, hi) in enumerate(heads)]
    o = jnp.concatenate(outs, axis=-1)
    for r0 in range(0, tm, X_LN_ROWS):
        rows = slice(r0, r0 + X_LN_ROWS)
        y = _dot(o[rows], wo_ref[...])
        o_ref[0, rows, :] = _layer_norm(ALPHA * x[rows] + y, g_ref[...], b_ref[...])


def _xattn_layer(x, layer, wq, mkv, wo, g, b, mixer=None):
    bsz, s, d = x.shape
    tm = X_TM
    tile = pl.BlockSpec((1, tm, d), lambda bi, i: (bi, i, 0))
    in_specs = [tile, _resident((d, d), layer),
                pl.BlockSpec((1, MEM_LEN, 2 * d), lambda bi, i: (bi, 0, 0)),
                _resident((d, d), layer), _resident((1, d)), _resident((1, d))]
    args = (x, wq, mkv, wo, g, b)
    if mixer is not None:
        in_specs = [tile, _resident((d, d)), _resident((1, d)), _resident((1, d))] + in_specs
        args = tuple(mixer) + args
    return pl.pallas_call(
        functools.partial(_xattn_kernel, mixer_prologue=mixer is not None),
        grid=(bsz, s // tm),
        in_specs=in_specs,
        out_specs=tile,
        out_shape=jax.ShapeDtypeStruct((bsz, s, d), F32),
        compiler_params=_params(("parallel", "parallel")),
        name="xattn_layer",
    )(*args)


FFN_TM = 512
FFN_LN_ROWS = 256
FFN_DOWN_GROUPS = (6, FFN_NCHUNK)


def _ffn_kernel(x_ref, wup_ref, cw_ref, cb_ref, wdn_ref, g_ref, b_ref, o_ref, hprev_scr, act_scr,
                acc_scr, hstage_scr):
    i = pl.program_id(1)
    tm = FFN_TM

    @pl.when(i == 0)
    def _():
        hprev_scr[...] = jnp.zeros_like(hprev_scr)

    x = x_ref[0]
    xb = x.astype(BF16)

    def up_conv(col0, slot):
        cols = slice(col0, col0 + FFN_CHUNK)
        h = _dot(xb, wup_ref[:, cols])
        prev = hprev_scr[:, cols]
        hprev_scr[:, cols] = h[tm - SUBLANES:, :]
        cw = cw_ref[:, cols]
        cb = cb_ref[:, cols]
        outs = []
        for lt in range(FFN_CHUNK // LANES):
            ls = slice(lt * LANES, (lt + 1) * LANES)
            hs = hstage_scr.at[slot, lt]
            hs[0:SUBLANES, :] = prev[:, ls]
            hs[SUBLANES:SUBLANES + tm, :] = h[:, ls]
            y = hs[SUBLANES - 2:SUBLANES - 2 + tm, :] * cw[0:1, ls]
            y = y + hs[SUBLANES - 1:SUBLANES - 1 + tm, :] * cw[1:2, ls]
            y = y + h[:, ls] * cw[2:3, ls]
            outs.append(y + cb[:, ls])
        return jnp.concatenate(outs, axis=1)

    def down(rows, c0, c1):
        ks = slice(c0 * FFN_CHUNK, c1 * FFN_CHUNK)
        return _dot(act_scr[rows, ks], wdn_ref[ks, :])

    everything = slice(0, tm)
    for c in range(FFN_NCHUNK):
        gt = up_conv(c * FFN_CHUNK, 2 * (c % 2))
        val = up_conv(D_FF + c * FFN_CHUNK, 2 * (c % 2) + 1)
        act_scr[:, c * FFN_CHUNK:(c + 1) * FFN_CHUNK] = ((gt * _sigmoid(gt)) * val).astype(BF16)
        if c + 1 in FFN_DOWN_GROUPS[:-1]:
            gi = FFN_DOWN_GROUPS.index(c + 1)
            if gi == 0:
                acc_scr[...] = down(everything, 0, c + 1)
            else:
                acc_scr[...] += down(everything, FFN_DOWN_GROUPS[gi - 1], c + 1)
    for r0 in range(0, tm, FFN_LN_ROWS):
        rows = slice(r0, r0 + FFN_LN_ROWS)
        out = acc_scr[rows, :] + down(rows, FFN_DOWN_GROUPS[-2], FFN_DOWN_GROUPS[-1])
        o_ref[0, rows, :] = _layer_norm(ALPHA * x[rows] + out, g_ref[...], b_ref[...])


def _ffn_layer(x, layer, wup, cw, cb, wdn, g, b):
    bsz, s, d = x.shape
    tm = FFN_TM
    return pl.pallas_call(
        _ffn_kernel,
        grid=(bsz, s // tm),
        in_specs=[pl.BlockSpec((1, tm, d), lambda bi, i: (bi, i, 0)),
                  _resident((d, 2 * D_FF), layer), _resident((FFN_CONV, 2 * D_FF)), _resident((1, 2 * D_FF)),
                  _resident((D_FF, d), layer), _resident((1, d)), _resident((1, d))],
        out_specs=pl.BlockSpec((1, tm, d), lambda bi, i: (bi, i, 0)),
        out_shape=jax.ShapeDtypeStruct((bsz, s, d), F32),
        scratch_shapes=[pltpu.VMEM((SUBLANES, 2 * D_FF), F32), pltpu.VMEM((tm, D_FF), BF16),
                        pltpu.VMEM((tm, d), F32),
                        pltpu.VMEM((4, FFN_CHUNK // LANES, SUBLANES + tm, LANES), F32)],
        compiler_params=_params(("arbitrary", "arbitrary")),
        name="ffn_layer",
    )(x, wup, cw, cb, wdn, g, b)


def kernel(x, mem, a_w_qkv, a_sinks, a_w_o, b_w_in, b_conv_w, b_conv_b, b_w_rgate, b_b_rgate, b_w_igate, b_b_igate, b_lambda, b_w_o, c_w_down, c_q_norm, c_kv_norm, c_w_uq, c_w_ukv, c_w_o, mem_w_kv, x_w_q, x_w_o, f_w_up, f_conv_w, f_conv_b, f_w_down, ln_g, ln_b):
    bsz, s, d = x.shape
    cos, sin = _rope_tables(s)

    mkv = _matmul(mem.reshape(bsz * MEM_LEN, d), mem_w_kv.astype(BF16), 512, BF16)
    mkv = mkv.reshape(bsz, MEM_LEN, 2 * d)

    def row(v):
        return v.reshape(1, -1)

    xq_all, xo_all = x_w_q.astype(BF16), x_w_o.astype(BF16)
    fup_all, fdn_all = f_w_up.astype(BF16), f_w_down.astype(BF16)
    aqkv_all, ao_all = a_w_qkv.astype(BF16), a_w_o.astype(BF16)

    for i in range(DEPTH):
        kind, j = i % N_MIXERS, i // N_MIXERS
        g0, b0 = row(ln_g[i, 0]), row(ln_b[i, 0])
        mixer = None
        if kind == 0:
            x = _swa_layer(x, j, a_sinks[j], cos, sin, aqkv_all, ao_all, g0, b0)
        elif kind == 1:
            x = _rglru_layer(x, b_w_in[j].astype(BF16), b_conv_w[j], row(b_conv_b[j]),
                             b_w_rgate[j].astype(BF16), row(b_b_rgate[j]),
                             b_w_igate[j].astype(BF16), row(b_b_igate[j]),
                             row(b_lambda[j]), b_w_o[j].astype(BF16), g0, b0)
        else:
            wd = jnp.pad(c_w_down[j], ((0, 0), (0, C_DOWN_PAD - c_w_down.shape[-1]))).astype(BF16)
            wuq = c_w_uq[j].reshape(C_Q_RANK, C_HEADS, C_NOPE + C_ROPE)
            wuq = jnp.concatenate([wuq[:, :, :C_NOPE].reshape(C_Q_RANK, -1),
                                   wuq[:, :, C_NOPE:].reshape(C_Q_RANK, -1)], axis=-1).astype(BF16)
            wukv = c_w_ukv[j].reshape(C_KV_RANK, C_HEADS, C_NOPE + C_V)
            wukv = jnp.concatenate([wukv[:, :, :C_NOPE].reshape(C_KV_RANK, -1),
                                    wukv[:, :, C_NOPE:].reshape(C_KV_RANK, -1)], axis=-1).astype(BF16)
            q, k, v = _mla_proj(x, cos, sin, wd, row(c_q_norm[j]), row(c_kv_norm[j]), wuq, wukv)
            mixer = (_mla_attn(q, k, v), c_w_o[j].astype(BF16), g0, b0)
        x = _xattn_layer(x, i, xq_all, mkv, xo_all, row(ln_g[i, 1]), row(ln_b[i, 1]), mixer=mixer)
        x = _ffn_layer(x, i, fup_all, f_conv_w[i], row(f_conv_b[i]), fdn_all, row(ln_g[i, 2]), row(ln_b[i, 2]))
    return x
```
